```python
import math
import jax, jax.numpy as jnp
from jax import lax
import numpy as np

D_MODEL = 2048
BATCH = 4
SEQ = 2048
DEPTH = 1
DEC_BATCH = 32
DEC_SEQ = 1
PAST_LEN = 16384
PAGE_SIZE = 128

HEAD_DIM = 64
N_HEADS = D_MODEL // 128
N_KV_HEADS = 4
GQA_GROUP = N_HEADS // N_KV_HEADS
WINDOW = 128
ATTN_BLOCK = WINDOW
ATTN_W = N_HEADS * HEAD_DIM
KV_W = N_KV_HEADS * HEAD_DIM
CHUNK = 128
GM_GROUPS = 4
GM_W = D_MODEL // 2
GM_GROUP_W = GM_W // GM_GROUPS
D_FF = (11 * D_MODEL) // 4
CONV_W = 3
PLE_DIM = 256
IN_W = ATTN_W + 2 * KV_W + 2 * GM_W
SPLITS = (ATTN_W, ATTN_W + KV_W, ATTN_W + 2 * KV_W, ATTN_W + 2 * KV_W + GM_W)
EPS = 1e-6
MASK_VALUE = -1e30

kernel_name = "hybrid_swa_sgu_convffn_decode_step"


def rmsnorm(x, w):
    xf = x.astype(jnp.float32)
    y = xf * lax.rsqrt(jnp.mean(xf * xf, axis=-1, keepdims=True) + EPS)
    return (y * w.astype(jnp.float32)).astype(x.dtype)


def layernorm(x, w):
    xf = x.astype(jnp.float32)
    mu = jnp.mean(xf, axis=-1, keepdims=True)
    xc = xf - mu
    y = xc * lax.rsqrt(jnp.mean(xc * xc, axis=-1, keepdims=True) + EPS)
    return (y * w.astype(jnp.float32)).astype(x.dtype)


def alibi_slopes():
    h = jnp.arange(1, N_HEADS + 1, dtype=jnp.float32)
    return jnp.exp2(-8.0 * h / N_HEADS)


def sink_attention(q, k, v, dist, valid, sinks):
    s = jnp.einsum('...ikgd,...jkd->...kgij', q, k).astype(jnp.float32) * (HEAD_DIM ** -0.5)
    slopes = alibi_slopes().reshape(N_KV_HEADS, GQA_GROUP, 1, 1)
    s = s - slopes * dist.astype(jnp.float32)
    s = jnp.where(valid, s, MASK_VALUE)
    sink_col = jnp.broadcast_to(sinks.astype(jnp.float32).reshape(N_KV_HEADS, GQA_GROUP, 1, 1), s.shape[:-1] + (1,))
    probs = jax.nn.softmax(jnp.concatenate([s, sink_col], axis=-1), axis=-1)[..., :-1]
    return jnp.einsum('...kgij,...jkd->...ikgd', probs.astype(v.dtype), v)


def attn_prompt(q, k, v, sinks):
    B, L = q.shape[:2]
    nb = L // ATTN_BLOCK
    qb = q.reshape(B, nb, ATTN_BLOCK, N_KV_HEADS, GQA_GROUP, HEAD_DIM)
    pad = jnp.zeros((B, ATTN_BLOCK, N_KV_HEADS, HEAD_DIM), k.dtype)
    kb = jnp.concatenate([pad, k], axis=1).reshape(B, nb + 1, ATTN_BLOCK, N_KV_HEADS, HEAD_DIM)
    vb = jnp.concatenate([pad, v], axis=1).reshape(B, nb + 1, ATTN_BLOCK, N_KV_HEADS, HEAD_DIM)
    kk = jnp.concatenate([kb[:, :-1], kb[:, 1:]], axis=2)
    vv = jnp.concatenate([vb[:, :-1], vb[:, 1:]], axis=2)
    i = jnp.arange(ATTN_BLOCK)[:, None]
    j = jnp.arange(2 * ATTN_BLOCK)[None, :]
    dist = ATTN_BLOCK + i - j
    c = jnp.arange(nb)[:, None, None]
    valid = (dist >= 0) & (dist <= WINDOW) & (c * ATTN_BLOCK + j - ATTN_BLOCK >= 0)
    out = sink_attention(qb, kk, vv, dist, valid[:, None, None], sinks)
    return out.reshape(B, L, ATTN_W), k[:, -WINDOW:], v[:, -WINDOW:]


def attn_sample(q, k, v, k_buf, v_buf, sinks):
    B, T = q.shape[:2]
    kk = jnp.concatenate([k_buf, k], axis=1)
    vv = jnp.concatenate([v_buf, v], axis=1)
    i = jnp.arange(T)[:, None]
    j = jnp.arange(WINDOW + T)[None, :]
    dist = WINDOW + i - j
    valid = (dist >= 0) & (dist <= WINDOW)
    qg = q.reshape(B, T, N_KV_HEADS, GQA_GROUP, HEAD_DIM)
    out = sink_attention(qg, kk, vv, dist, valid, sinks)
    return out.reshape(B, T, ATTN_W), kk[:, T:], vv[:, T:]


def spatial_mix(vg, w_s, b_s):
    Lc = vg.shape[2]
    mask = jnp.tril(jnp.ones((Lc, Lc), dtype=bool))
    w = jnp.where(mask, w_s[:, :Lc, :Lc], 0).astype(vg.dtype)
    bias = jnp.transpose(b_s[:, :Lc])[:, :, None].astype(vg.dtype)
    return jnp.einsum('gts,bcsgd->bctgd', w, vg) + bias


def layer(x, p, k_buf, v_buf, conv_buf, lw, is_prompt):
    B, L, _ = x.shape
    xn = rmsnorm(x, lw['attn_norm_w'])
    proj = xn @ lw['w_in']
    q, k, v, gu, gv = jnp.split(proj, SPLITS, axis=-1)
    q = rmsnorm(q.reshape(B, L, N_HEADS, HEAD_DIM), lw['q_norm_w'])
    k = rmsnorm(k.reshape(B, L, N_KV_HEADS, HEAD_DIM), lw['k_norm_w'])
    v = v.reshape(B, L, N_KV_HEADS, HEAD_DIM)
    if is_prompt:
        attn, k_new, v_new = attn_prompt(q, k, v, lw['attn_sinks'])
        n_chunks, chunk_len = L // CHUNK, CHUNK
    else:
        attn, k_new, v_new = attn_sample(q, k, v, k_buf, v_buf, lw['attn_sinks'])
        n_chunks, chunk_len = 1, L
    gu = jax.nn.gelu(gu)
    gv = layernorm(jax.nn.gelu(gv), lw['sgu_norm_w'])
    vg = gv.reshape(B, n_chunks, chunk_len, GM_GROUPS, GM_GROUP_W)
    sgu = gu * spatial_mix(vg, lw['sgu_w'], lw['sgu_b']).reshape(B, L, GM_W)
    sgu_state = gv[:, -chunk_len:]
    branch_attn = attn @ lw['w_br_attn']
    branch_sgu = sgu @ lw['w_br_gm']
    gates = jax.nn.sigmoid(xn @ lw['w_gate'] + lw['b_gate'])
    merged = gates[..., D_MODEL:] * branch_attn + gates[..., :D_MODEL] * branch_sgu
    x = x + merged @ lw['w_out']
    h = rmsnorm(x, lw['ffn_norm_w']) @ lw['w_up']
    if conv_buf is None:
        conv_buf = jnp.zeros((B, CONV_W - 1, 2 * D_FF), h.dtype)
    hp = jnp.concatenate([conv_buf, h], axis=1)
    cw = lw['conv_w']
    hc = cw[0] * hp[:, 0:L] + cw[1] * hp[:, 1:L + 1] + cw[2] * hp[:, 2:L + 2] + lw['conv_b']
    hg, hu = jnp.split(hc, 2, axis=-1)
    x = x + (jax.nn.silu(hg) * hu) @ lw['w_down']
    conv_state = hp[:, -(CONV_W - 1):]
    ple_gate = jax.nn.sigmoid(rmsnorm(x, lw['ple_norm_w']) @ lw['w_ple_gate'])
    x = x + ple_gate * (p @ lw['w_ple_proj'])
    return x, k_new, v_new, sgu_state, conv_state


def setup_inputs(seed: int = 0) -> dict:
    key = jax.random.key(seed)
    ks = iter(jax.random.split(key, 40))

    def nrm(shape, scale=1.0):
        return jax.random.normal(next(ks), shape, jnp.float32) * scale

    def gain(shape):
        return 1.0 + nrm(shape, 0.05)

    return {
        'x_prompt': nrm((BATCH, SEQ, D_MODEL)),
        'x_sample': nrm((DEC_BATCH, DEC_SEQ, D_MODEL)),
        'p_prompt': nrm((DEPTH, BATCH, SEQ, PLE_DIM)),
        'p_sample': nrm((DEPTH, DEC_BATCH, DEC_SEQ, PLE_DIM)),
        'state_attn_k': nrm((DEPTH, DEC_BATCH, WINDOW, N_KV_HEADS, HEAD_DIM)),
        'state_attn_v': nrm((DEPTH, DEC_BATCH, WINDOW, N_KV_HEADS, HEAD_DIM)),
        'state_conv': nrm((DEPTH, DEC_BATCH, CONV_W - 1, 2 * D_FF)),
        'attn_norm_w': gain((DEPTH, D_MODEL)),
        'w_in': nrm((DEPTH, D_MODEL, IN_W), D_MODEL ** -0.5),
        'q_norm_w': gain((DEPTH, HEAD_DIM)),
        'k_norm_w': gain((DEPTH, HEAD_DIM)),
        'attn_sinks': nrm((DEPTH, N_HEADS), 0.5),
        'sgu_norm_w': gain((DEPTH, GM_W)),
        'sgu_w': nrm((DEPTH, GM_GROUPS, CHUNK, CHUNK), 0.5 * CHUNK ** -0.5),
        'sgu_b': 1.0 + nrm((DEPTH, GM_GROUPS, CHUNK), 0.02),
        'w_br_attn': nrm((DEPTH, ATTN_W, D_MODEL), ATTN_W ** -0.5),
        'w_br_gm': nrm((DEPTH, GM_W, D_MODEL), GM_W ** -0.5),
        'w_gate': nrm((DEPTH, D_MODEL, 2 * D_MODEL), D_MODEL ** -0.5),
        'b_gate': nrm((DEPTH, 2 * D_MODEL), 0.02),
        'w_out': nrm((DEPTH, D_MODEL, D_MODEL), D_MODEL ** -0.5),
        'ffn_norm_w': gain((DEPTH, D_MODEL)),
        'w_up': nrm((DEPTH, D_MODEL, 2 * D_FF), D_MODEL ** -0.5),
        'conv_w': nrm((DEPTH, CONV_W, 2 * D_FF), CONV_W ** -0.5),
        'conv_b': nrm((DEPTH, 2 * D_FF), 0.02),
        'w_down': nrm((DEPTH, D_FF, D_MODEL), D_FF ** -0.5),
        'ple_norm_w': gain((DEPTH, D_MODEL)),
        'w_ple_gate': nrm((DEPTH, D_MODEL, D_MODEL), D_MODEL ** -0.5),
        'w_ple_proj': nrm((DEPTH, PLE_DIM, D_MODEL), PLE_DIM ** -0.5),
    }


def reference(x_prompt, x_sample, p_prompt, p_sample, state_attn_k, state_attn_v, state_conv,
              attn_norm_w, w_in, q_norm_w, k_norm_w, attn_sinks, sgu_norm_w, sgu_w, sgu_b,
              w_br_attn, w_br_gm, w_gate, b_gate, w_out, ffn_norm_w, w_up, conv_w, conv_b,
              w_down, ple_norm_w, w_ple_gate, w_ple_proj):
    xp, xs = x_prompt, x_sample
    kp_l, vp_l, ks_l, vs_l, gp_l, gs_l, cp_l, cs_l = [], [], [], [], [], [], [], []
    for i in range(DEPTH):
        lw = {
            'attn_norm_w': attn_norm_w[i], 'w_in': w_in[i], 'q_norm_w': q_norm_w[i],
            'k_norm_w': k_norm_w[i], 'attn_sinks': attn_sinks[i], 'sgu_norm_w': sgu_norm_w[i],
            'sgu_w': sgu_w[i], 'sgu_b': sgu_b[i], 'w_br_attn': w_br_attn[i], 'w_br_gm': w_br_gm[i],
            'w_gate': w_gate[i], 'b_gate': b_gate[i], 'w_out': w_out[i], 'ffn_norm_w': ffn_norm_w[i],
            'w_up': w_up[i], 'conv_w': conv_w[i], 'conv_b': conv_b[i], 'w_down': w_down[i],
            'ple_norm_w': ple_norm_w[i], 'w_ple_gate': w_ple_gate[i], 'w_ple_proj': w_ple_proj[i],
        }
        xp, kp, vp, gp, cp = layer(xp, p_prompt[i], None, None, None, lw, True)
        xs, ksn, vsn, gs, cs = layer(xs, p_sample[i], state_attn_k[i], state_attn_v[i], state_conv[i], lw, False)
        kp_l.append(kp); vp_l.append(vp); ks_l.append(ksn); vs_l.append(vsn)
        gp_l.append(gp); gs_l.append(gs); cp_l.append(cp); cs_l.append(cs)
    attn_k_prompt = jnp.stack(kp_l)
    attn_v_prompt = jnp.stack(vp_l)
    attn_k_sample = jnp.stack(ks_l)
    attn_v_sample = jnp.stack(vs_l)
    sgu_v_prompt = jnp.stack(gp_l)
    sgu_v_sample = jnp.stack(gs_l)
    conv_prompt = jnp.stack(cp_l)
    conv_sample = jnp.stack(cs_l)
    return (xp, xs, attn_k_prompt, attn_v_prompt, attn_k_sample, attn_v_sample, sgu_v_prompt, sgu_v_sample, conv_prompt, conv_sample)
```

```python
import functools

import jax
import jax.numpy as jnp
from jax import lax
from jax.experimental import pallas as pl
from jax.experimental.pallas import tpu as pltpu

F32 = jnp.float32
BF16 = jnp.bfloat16

HEAD_DIM = 64
N_HEADS = 16
N_KV_HEADS = 4
GQA_GROUP = N_HEADS // N_KV_HEADS
WINDOW = 128
CHUNK = 128
GM_GROUPS = 4
EPS = 1e-6
MASK_VALUE = -1e30
LANES = 128
HEADS_PER_VREG = LANES // HEAD_DIM
MIB = 1 << 20


def _alibi_slope(h):
    return float(2.0 ** (-8.0 * (h + 1) / N_HEADS))


def _params(semantics, vmem_mib):
    return pltpu.CompilerParams(dimension_semantics=semantics, vmem_limit_bytes=vmem_mib * MIB)


def _rmsnorm_rows(x, w):
    ms = jnp.mean(x * x, axis=-1, keepdims=True)
    return x * lax.rsqrt(ms + EPS) * w


def _norm_mm_kernel(*refs, has_bias, act):
    if has_bias:
        x_ref, nw_ref, w_ref, b_ref, o_ref, xn_ref = refs
    else:
        x_ref, nw_ref, w_ref, o_ref, xn_ref = refs

    @pl.when(pl.program_id(1) == 0)
    def _():
        xn_ref[...] = _rmsnorm_rows(x_ref[...], nw_ref[...]).astype(BF16)

    acc = jnp.dot(xn_ref[...], w_ref[...], preferred_element_type=F32)
    if has_bias:
        acc = acc + b_ref[...]
    if act == "sigmoid":
        acc = jax.nn.sigmoid(acc)
    o_ref[...] = acc.astype(o_ref.dtype)


def _norm_mm(x, nw, w, bias, *, act, out_dtype, tm, tn):
    m, k = x.shape
    n = w.shape[1]
    in_specs = [
        pl.BlockSpec((tm, k), lambda i, j: (i, 0)),
        pl.BlockSpec((1, k), lambda i, j: (0, 0)),
        pl.BlockSpec((k, tn), lambda i, j: (0, j)),
    ]
    args = [x, nw.reshape(1, k), w]
    if bias is not None:
        in_specs.append(pl.BlockSpec((1, tn), lambda i, j: (0, j)))
        args.append(bias.reshape(1, n))
    return pl.pallas_call(
        functools.partial(_norm_mm_kernel, has_bias=bias is not None, act=act),
        grid=(m // tm, n // tn),
        in_specs=in_specs,
        out_specs=pl.BlockSpec((tm, tn), lambda i, j: (i, j)),
        out_shape=jax.ShapeDtypeStruct((m, n), out_dtype),
        scratch_shapes=[pltpu.VMEM((tm, k), BF16)],
        compiler_params=_params(("parallel", "arbitrary"), 48),
    )(*args)


def _head_group_matrix():
    r = lax.broadcasted_iota(jnp.int32, (LANES, LANES), 0)
    c = lax.broadcasted_iota(jnp.int32, (LANES, LANES), 1)
    return ((r >> 6) == (c >> 6)).astype(BF16)


def _head_rmsnorm(xcol, w, gmat):
    ss = jnp.dot((xcol * xcol).astype(BF16), gmat, preferred_element_type=F32)
    return xcol * lax.rsqrt(ss * (1.0 / HEAD_DIM) + EPS) * w


def _attn_prompt_kernel(sink_ref, q_ref, kc_ref, kp_ref, vc_ref, vp_ref, qw_ref, kw_ref,
                        o_ref, kn_ref, vn_ref):
    c = pl.program_id(1)
    blk = WINDOW
    gmat = _head_group_matrix()
    lane = lax.broadcasted_iota(jnp.int32, (1, LANES), 1)
    low = lane < HEAD_DIM

    kw = kw_ref[...]
    qw = qw_ref[...]
    kc = kc_ref[...].astype(F32)
    kp = kp_ref[...].astype(F32)
    vc = vc_ref[...].astype(F32)
    vp = vp_ref[...].astype(F32)
    ncol = kc.shape[1] // LANES
    kcn = [_head_rmsnorm(kc[:, p * LANES:(p + 1) * LANES], kw, gmat) for p in range(ncol)]
    kpn = [_head_rmsnorm(kp[:, p * LANES:(p + 1) * LANES], kw, gmat) for p in range(ncol)]
    kn_ref[0] = jnp.concatenate(kcn, axis=1)
    vn_ref[0] = vc

    kcat = [jnp.concatenate([kpn[p], kcn[p]], axis=0) for p in range(ncol)]
    vcat = [jnp.concatenate([vp[:, p * LANES:(p + 1) * LANES], vc[:, p * LANES:(p + 1) * LANES]], axis=0)
            for p in range(ncol)]
    krol = [pltpu.roll(kcat[p], HEAD_DIM, 1) for p in range(ncol)]
    vrol = [pltpu.roll(vcat[p], HEAD_DIM, 1) for p in range(ncol)]

    qi = lax.broadcasted_iota(jnp.int32, (blk, 2 * blk), 0)
    kj = lax.broadcasted_iota(jnp.int32, (blk, 2 * blk), 1)
    dist = blk + qi - kj
    valid = (dist >= 0) & (dist <= WINDOW) & ((kj >= blk) | (c > 0))
    distf = dist.astype(F32)

    q = q_ref[...].astype(F32)
    qn = [_head_rmsnorm(q[:, p * LANES:(p + 1) * LANES], qw, gmat) * (HEAD_DIM ** -0.5)
          for p in range(N_HEADS // HEADS_PER_VREG)]

    for g in range(N_KV_HEADS):
        p, half = divmod(g, HEADS_PER_VREG)
        if half == 0:
            kd = jnp.where(low, kcat[p], krol[p])
            vd = jnp.where(low, vcat[p], vrol[p])
        else:
            kd = jnp.where(low, krol[p], kcat[p])
            vd = jnp.where(low, vrol[p], vcat[p])
        kd = kd.astype(BF16)
        vd = vd.astype(BF16)
        qs = []
        for hl in range(GQA_GROUP):
            h = g * GQA_GROUP + hl
            pc, hh = divmod(h, HEADS_PER_VREG)
            keep = low if hh == 0 else jnp.logical_not(low)
            qs.append(jnp.where(keep, qn[pc], 0.0).astype(BF16))
        qstack = jnp.concatenate(qs, axis=0)
        s_all = lax.dot_general(qstack, kd, (((1,), (1,)), ((), ())), preferred_element_type=F32)
        ps = []
        for hl in range(GQA_GROUP):
            h = g * GQA_GROUP + hl
            s = s_all[hl * blk:(hl + 1) * blk] - _alibi_slope(h) * distf
            s = jnp.where(valid, s, MASK_VALUE)
            sink = sink_ref[h]
            mx = jnp.maximum(jnp.max(s, axis=-1, keepdims=True), sink)
            e = jnp.exp(s - mx)
            den = jnp.sum(e, axis=-1, keepdims=True) + jnp.exp(sink - mx)
            ps.append((e / den).astype(BF16))
        pstack = jnp.concatenate(ps, axis=0)
        o_all = jnp.dot(pstack, vd, preferred_element_type=F32)
        for pair in range(GQA_GROUP // HEADS_PER_VREG):
            pc = g * (GQA_GROUP // HEADS_PER_VREG) + pair
            even = o_all[(2 * pair) * blk:(2 * pair + 1) * blk]
            odd = o_all[(2 * pair + 1) * blk:(2 * pair + 2) * blk]
            o_ref[:, pc * LANES:(pc + 1) * LANES] = jnp.where(low, even, odd).astype(o_ref.dtype)


def _attn_prompt(proj, qw2, kw2, sinks, *, batch, seq):
    nb = seq // WINDOW
    attn_w = N_HEADS * HEAD_DIM
    kv_w = N_KV_HEADS * HEAD_DIM
    kcol = attn_w // kv_w
    vcol = kcol + 1

    def cur(col):
        return lambda b, c: (b * nb + c, col)

    def prev(col):
        return lambda b, c: (jnp.maximum(b * nb + c - 1, 0), col)

    return pl.pallas_call(
        _attn_prompt_kernel,
        grid=(batch, nb),
        in_specs=[
            pl.BlockSpec(memory_space=pltpu.SMEM),
            pl.BlockSpec((WINDOW, attn_w), cur(0)),
            pl.BlockSpec((WINDOW, kv_w), cur(kcol)),
            pl.BlockSpec((WINDOW, kv_w), prev(kcol)),
            pl.BlockSpec((WINDOW, kv_w), cur(vcol)),
            pl.BlockSpec((WINDOW, kv_w), prev(vcol)),
            pl.BlockSpec((1, LANES), lambda b, c: (0, 0)),
            pl.BlockSpec((1, LANES), lambda b, c: (0, 0)),
        ],
        out_specs=[
            pl.BlockSpec((WINDOW, attn_w), lambda b, c: (b * nb + c, 0)),
            pl.BlockSpec((1, WINDOW, kv_w), lambda b, c: (b, 0, 0)),
            pl.BlockSpec((1, WINDOW, kv_w), lambda b, c: (b, 0, 0)),
        ],
        out_shape=[
            jax.ShapeDtypeStruct((batch * seq, attn_w), BF16),
            jax.ShapeDtypeStruct((batch, WINDOW, kv_w), F32),
            jax.ShapeDtypeStruct((batch, WINDOW, kv_w), F32),
        ],
        compiler_params=_params(("parallel", "arbitrary"), 32),
    )(sinks, proj, proj, proj, proj, proj, qw2, kw2)


def _layernorm_rows(x, w):
    mu = jnp.mean(x, axis=-1, keepdims=True)
    xc = x - mu
    return xc * lax.rsqrt(jnp.mean(xc * xc, axis=-1, keepdims=True) + EPS) * w


def _sgu_prompt_kernel(gu0_ref, gu1_ref, gv0_ref, gv1_ref, nw_ref, ws_ref, bs_ref, o_ref, st_ref):
    gv = jnp.concatenate([gv0_ref[...], gv1_ref[...]], axis=1).astype(F32)
    vn = _layernorm_rows(jax.nn.gelu(gv), nw_ref[...])
    st_ref[0] = vn
    vb = vn.astype(BF16)
    r = lax.broadcasted_iota(jnp.int32, (CHUNK, CHUNK), 0)
    c = lax.broadcasted_iota(jnp.int32, (CHUNK, CHUNK), 1)
    causal = r >= c
    gw = vb.shape[1] // GM_GROUPS
    half = gu0_ref.shape[1]
    for g in range(GM_GROUPS):
        w = jnp.where(causal, ws_ref[g], 0.0).astype(BF16)
        mix = jnp.dot(w, vb[:, g * gw:(g + 1) * gw], preferred_element_type=F32) + bs_ref[:, g:g + 1]
        src = gu0_ref if g * gw < half else gu1_ref
        off = g * gw - (0 if g * gw < half else half)
        u = jax.nn.gelu(src[:, off:off + gw].astype(F32))
        o_ref[:, g * gw:(g + 1) * gw] = (u * mix).astype(o_ref.dtype)


def _sgu_prompt(proj, nw, ws, bs_t, *, batch, seq, gu_off, gm_w):
    nc = seq // CHUNK
    half = gm_w // 2
    b0 = gu_off // half

    def col(k):
        return lambda b, c: (b * nc + c, b0 + k)

    return pl.pallas_call(
        _sgu_prompt_kernel,
        grid=(batch, nc),
        in_specs=[
            pl.BlockSpec((CHUNK, half), col(0)),
            pl.BlockSpec((CHUNK, half), col(1)),
            pl.BlockSpec((CHUNK, half), col(2)),
            pl.BlockSpec((CHUNK, half), col(3)),
            pl.BlockSpec((1, gm_w), lambda b, c: (0, 0)),
            pl.BlockSpec((GM_GROUPS, CHUNK, CHUNK), lambda b, c: (0, 0, 0)),
            pl.BlockSpec((CHUNK, GM_GROUPS), lambda b, c: (0, 0)),
        ],
        out_specs=[
            pl.BlockSpec((CHUNK, gm_w), lambda b, c: (b * nc + c, 0)),
            pl.BlockSpec((1, CHUNK, gm_w), lambda b, c: (b, 0, 0)),
        ],
        out_shape=[
            jax.ShapeDtypeStruct((batch * seq, gm_w), BF16),
            jax.ShapeDtypeStruct((batch, CHUNK, gm_w), F32),
        ],
        compiler_params=_params(("parallel", "arbitrary"), 32),
    )(proj, proj, proj, proj, nw.reshape(1, gm_w), ws, bs_t)


def _split_dot(x, m):
    hi = x.astype(BF16)
    lo = (x - hi.astype(F32)).astype(BF16)
    return jnp.dot(hi, m, preferred_element_type=F32) + jnp.dot(lo, m, preferred_element_type=F32)


def _attn_sample_kernel(q_ref, kn_ref, vn_ref, kb_ref, vb_ref, qw_ref, kw_ref, sink_ref,
                        o_ref, ko_ref, vo_ref):
    nsamp = q_ref.shape[0]
    attn_w = q_ref.shape[1]
    kv_w = kn_ref.shape[1]
    gmat = _head_group_matrix()
    q = q_ref[...].astype(F32)
    kn = kn_ref[...].astype(F32)
    qn = jnp.concatenate(
        [_head_rmsnorm(q[:, p * LANES:(p + 1) * LANES], qw_ref[...], gmat) * (HEAD_DIM ** -0.5)
         for p in range(attn_w // LANES)], axis=1)
    knn = jnp.concatenate(
        [_head_rmsnorm(kn[:, p * LANES:(p + 1) * LANES], kw_ref[...], gmat)
         for p in range(kv_w // LANES)], axis=1)
    vnn = vn_ref[...].astype(F32)

    ec = lax.broadcasted_iota(jnp.int32, (kv_w, attn_w), 0)
    el = lax.broadcasted_iota(jnp.int32, (kv_w, attn_w), 1)
    expand = (((ec >> 6) == (el >> 8)) & ((ec & 63) == (el & 63))).astype(BF16)
    gl = lax.broadcasted_iota(jnp.int32, (attn_w, LANES), 0)
    gh = lax.broadcasted_iota(jnp.int32, (attn_w, LANES), 1)
    hsum = ((gl >> 6) == gh).astype(BF16)
    tl = lax.broadcasted_iota(jnp.int32, (LANES, attn_w), 1)
    th = lax.broadcasted_iota(jnp.int32, (LANES, attn_w), 0)
    hexp = ((tl >> 6) == th).astype(BF16)

    nkeys = WINDOW + 8
    row = lax.broadcasted_iota(jnp.int32, (nkeys, LANES), 0)
    head = lax.broadcasted_iota(jnp.int32, (nkeys, LANES), 1)
    slope = jnp.exp2(-8.0 * (head + 1).astype(F32) / N_HEADS)
    dist = (WINDOW - row).astype(F32)
    key_ok = row <= WINDOW
    srow = lax.broadcasted_iota(jnp.int32, (WINDOW, kv_w), 0)
    sinks = sink_ref[...]

    for s in range(nsamp):
        kb = kb_ref[s]
        vb = vb_ref[s]
        knew = knn[s:s + 1]
        vnew = vnn[s:s + 1]
        ko_ref[s] = jnp.where(srow == WINDOW - 1, knew, pltpu.roll(kb, WINDOW - 1, 0))
        vo_ref[s] = jnp.where(srow == WINDOW - 1, vnew, pltpu.roll(vb, WINDOW - 1, 0))
        kk = jnp.concatenate([kb, jnp.broadcast_to(knew, (8, kv_w))], axis=0)
        vv = jnp.concatenate([vb, jnp.broadcast_to(vnew, (8, kv_w))], axis=0)
        kexp = _split_dot(kk, expand)
        vexp = _split_dot(vv, expand)
        sc = _split_dot(kexp * qn[s:s + 1], hsum)
        sc = sc - slope * dist
        sc = jnp.where(key_ok, sc, MASK_VALUE)
        mx = jnp.maximum(jnp.max(sc, axis=0, keepdims=True), sinks)
        e = jnp.exp(sc - mx)
        den = jnp.sum(e, axis=0, keepdims=True) + jnp.exp(sinks - mx)
        pexp = _split_dot(e / den, hexp)
        o_ref[s:s + 1, :] = jnp.sum(pexp * vexp, axis=0, keepdims=True)


def _attn_sample(q, knew, vnew, kbuf, vbuf, qw2, kw2, sinks_row, *, group):
    nb, attn_w = q.shape
    kv_w = knew.shape[1]
    row = lambda width: pl.BlockSpec((group, width), lambda i: (i, 0))
    buf = pl.BlockSpec((group, WINDOW, kv_w), lambda i: (i, 0, 0))
    vec = pl.BlockSpec((1, LANES), lambda i: (0, 0))
    return pl.pallas_call(
        _attn_sample_kernel,
        grid=(nb // group,),
        in_specs=[row(attn_w), row(kv_w), row(kv_w), buf, buf, vec, vec, vec],
        out_specs=[row(attn_w), buf, buf],
        out_shape=[
            jax.ShapeDtypeStruct((nb, attn_w), F32),
            jax.ShapeDtypeStruct((nb, WINDOW, kv_w), F32),
            jax.ShapeDtypeStruct((nb, WINDOW, kv_w), F32),
        ],
        compiler_params=_params(("parallel",), 32),
    )(q, knew, vnew, kbuf, vbuf, qw2, kw2, sinks_row)


def _sgu_sample_kernel(gu_ref, gv_ref, nw_ref, w0_ref, b0_ref, o_ref, st_ref):
    vn = _layernorm_rows(jax.nn.gelu(gv_ref[...].astype(F32)), nw_ref[...])
    st_ref[...] = vn
    mix = w0_ref[...] * vn + b0_ref[...]
    o_ref[...] = jax.nn.gelu(gu_ref[...].astype(F32)) * mix


def _sgu_sample(gu, gv, nw, w0_row, b0_row):
    nb, gm_w = gu.shape
    full = pl.BlockSpec((nb, gm_w), lambda i: (0, 0))
    vec = pl.BlockSpec((1, gm_w), lambda i: (0, 0))
    return pl.pallas_call(
        _sgu_sample_kernel,
        grid=(1,),
        in_specs=[full, full, vec, vec, vec],
        out_specs=[full, full],
        out_shape=[jax.ShapeDtypeStruct((nb, gm_w), F32), jax.ShapeDtypeStruct((nb, gm_w), F32)],
    )(gu, gv, nw.reshape(1, gm_w), w0_row, b0_row)


def _merge_kernel(a_ref, s_ref, wa_ref, wg_ref, ga_ref, gb_ref, o_ref):
    a = jnp.dot(a_ref[...].astype(BF16), wa_ref[...], preferred_element_type=F32)
    m = jnp.dot(s_ref[...].astype(BF16), wg_ref[...], preferred_element_type=F32)
    o_ref[...] = (gb_ref[...].astype(F32) * a + ga_ref[...].astype(F32) * m).astype(o_ref.dtype)


def _merge(attn, sgu, wa, wg, gates, *, tm, tn):
    m, ka = attn.shape
    kg = sgu.shape[1]
    n = wa.shape[1]
    nj = n // tn
    return pl.pallas_call(
        _merge_kernel,
        grid=(m // tm, nj),
        in_specs=[
            pl.BlockSpec((tm, ka), lambda i, j: (i, 0)),
            pl.BlockSpec((tm, kg), lambda i, j: (i, 0)),
            pl.BlockSpec((ka, tn), lambda i, j: (0, j)),
            pl.BlockSpec((kg, tn), lambda i, j: (0, j)),
            pl.BlockSpec((tm, tn), lambda i, j: (i, j)),
            pl.BlockSpec((tm, tn), lambda i, j: (i, j + nj)),
        ],
        out_specs=pl.BlockSpec((tm, tn), lambda i, j: (i, j)),
        out_shape=jax.ShapeDtypeStruct((m, n), BF16),
        compiler_params=_params(("parallel", "arbitrary"), 32),
    )(attn, sgu, wa, wg, gates, gates)


def _mm_res_kernel(a_ref, w_ref, r_ref, o_ref):
    o_ref[...] = r_ref[...] + jnp.dot(a_ref[...], w_ref[...], preferred_element_type=F32)


def _mm_res(a, w, res, *, tm, tn):
    m, k = a.shape
    n = w.shape[1]
    return pl.pallas_call(
        _mm_res_kernel,
        grid=(m // tm, n // tn),
        in_specs=[
            pl.BlockSpec((tm, k), lambda i, j: (i, 0)),
            pl.BlockSpec((k, tn), lambda i, j: (0, j)),
            pl.BlockSpec((tm, tn), lambda i, j: (i, j)),
        ],
        out_specs=pl.BlockSpec((tm, tn), lambda i, j: (i, j)),
        out_shape=jax.ShapeDtypeStruct((m, n), F32),
        compiler_params=_params(("parallel", "arbitrary"), 32),
    )(a, w, res)


HALO = 16
PAD = 8


def _ffn_down_prompt_kernel(hg_ref, hu_ref, pg_ref, pu_ref, cwg_ref, cwu_ref, cbg_ref, cbu_ref,
                            wd_ref, x_ref, o_ref, sg_ref, su_ref, *, tiles_per_seq):
    i = pl.program_id(0)
    j = pl.program_id(1)
    tm = hg_ref.shape[0]
    first = (i % tiles_per_seq) == 0

    @pl.when(j == 0)
    def _():
        o_ref[...] = x_ref[...]

    def conv(h_ref, p_ref, cw_ref, cb_ref, s_ref):
        s_ref[PAD:PAD + tm, :] = h_ref[...].astype(F32)
        halo = p_ref[...].astype(F32)[HALO - PAD:, :]
        s_ref[0:PAD, :] = jnp.where(first, 0.0, halo)
        cw = cw_ref[...]
        return (cw[0:1] * s_ref[PAD - 2:PAD - 2 + tm, :] + cw[1:2] * s_ref[PAD - 1:PAD - 1 + tm, :]
                + cw[2:3] * s_ref[PAD:PAD + tm, :] + cb_ref[...])

    hcg = conv(hg_ref, pg_ref, cwg_ref, cbg_ref, sg_ref)
    hcu = conv(hu_ref, pu_ref, cwu_ref, cbu_ref, su_ref)
    act = (jax.nn.silu(hcg) * hcu).astype(BF16)
    o_ref[...] += jnp.dot(act, wd_ref[...], preferred_element_type=F32)


def _ffn_down_prompt(h, cw, cb, wd, x, *, seq, tm, tf):
    m, two_f = h.shape
    d_ff = two_f // 2
    nf = d_ff // tf
    d = wd.shape[1]
    hb = tm // HALO

    def halo(off):
        return lambda i, j: (jnp.maximum(i * hb - 1, 0), j + off)

    return pl.pallas_call(
        functools.partial(_ffn_down_prompt_kernel, tiles_per_seq=seq // tm),
        grid=(m // tm, nf),
        in_specs=[
            pl.BlockSpec((tm, tf), lambda i, j: (i, j)),
            pl.BlockSpec((tm, tf), lambda i, j: (i, j + nf)),
            pl.BlockSpec((HALO, tf), halo(0)),
            pl.BlockSpec((HALO, tf), halo(nf)),
            pl.BlockSpec((3, tf), lambda i, j: (0, j)),
            pl.BlockSpec((3, tf), lambda i, j: (0, j + nf)),
            pl.BlockSpec((1, tf), lambda i, j: (0, j)),
            pl.BlockSpec((1, tf), lambda i, j: (0, j + nf)),
            pl.BlockSpec((tf, d), lambda i, j: (j, 0)),
            pl.BlockSpec((tm, d), lambda i, j: (i, 0)),
        ],
        out_specs=pl.BlockSpec((tm, d), lambda i, j: (i, 0)),
        out_shape=jax.ShapeDtypeStruct((m, d), F32),
        scratch_shapes=[pltpu.VMEM((tm + PAD, tf), F32), pltpu.VMEM((tm + PAD, tf), F32)],
        compiler_params=_params(("parallel", "arbitrary"), 48),
    )(h, h, h, h, cw, cw, cb.reshape(1, two_f), cb.reshape(1, two_f), wd, x)


def _ffn_down_sample_kernel(hg_ref, hu_ref, b0g_ref, b0u_ref, b1g_ref, b1u_ref, cwg_ref, cwu_ref,
                            cbg_ref, cbu_ref, wd_ref, x_ref, o_ref):
    @pl.when(pl.program_id(0) == 0)
    def _():
        o_ref[...] = x_ref[...]

    def conv(h_ref, b0_ref, b1_ref, cw_ref, cb_ref):
        cw = cw_ref[...]
        return cw[0:1] * b0_ref[...] + cw[1:2] * b1_ref[...] + cw[2:3] * h_ref[...] + cb_ref[...]

    hcg = conv(hg_ref, b0g_ref, b1g_ref, cwg_ref, cbg_ref)
    hcu = conv(hu_ref, b0u_ref, b1u_ref, cwu_ref, cbu_ref)
    act = (jax.nn.silu(hcg) * hcu).astype(BF16)
    o_ref[...] += jnp.dot(act, wd_ref[...], preferred_element_type=F32)


def _ffn_down_sample(h, state2, cw, cb, wd, x, *, tf):
    nb, two_f = h.shape
    d_ff = two_f // 2
    nf = d_ff // tf
    d = wd.shape[1]
    colblk = lambda off: pl.BlockSpec((nb, tf), lambda j: (0, j + off))
    return pl.pallas_call(
        _ffn_down_sample_kernel,
        grid=(nf,),
        in_specs=[
            colblk(0), colblk(nf),
            colblk(0), colblk(nf), colblk(2 * nf), colblk(3 * nf),
            pl.BlockSpec((3, tf), lambda j: (0, j)),
            pl.BlockSpec((3, tf), lambda j: (0, j + nf)),
            pl.BlockSpec((1, tf), lambda j: (0, j)),
            pl.BlockSpec((1, tf), lambda j: (0, j + nf)),
            pl.BlockSpec((tf, d), lambda j: (j, 0)),
            pl.BlockSpec((nb, d), lambda j: (0, 0)),
        ],
        out_specs=pl.BlockSpec((nb, d), lambda j: (0, 0)),
        out_shape=jax.ShapeDtypeStruct((nb, d), F32),
        compiler_params=_params(("arbitrary",), 32),
    )(h, h, state2, state2, state2, state2, cw, cw, cb.reshape(1, two_f), cb.reshape(1, two_f), wd, x)


def _ple_kernel(x_ref, nw_ref, w_ref, p_ref, wp_ref, r_ref, o_ref, xn_ref):
    @pl.when(pl.program_id(1) == 0)
    def _():
        xn_ref[...] = _rmsnorm_rows(x_ref[...], nw_ref[...]).astype(BF16)

    gate = jax.nn.sigmoid(jnp.dot(xn_ref[...], w_ref[...], preferred_element_type=F32))
    emb = jnp.dot(p_ref[...].astype(BF16), wp_ref[...], preferred_element_type=F32)
    o_ref[...] = r_ref[...] + gate * emb


def _ple(x, nw, w, p, wp, *, tm, tn):
    m, k = x.shape
    n = w.shape[1]
    kp = p.shape[1]
    return pl.pallas_call(
        _ple_kernel,
        grid=(m // tm, n // tn),
        in_specs=[
            pl.BlockSpec((tm, k), lambda i, j: (i, 0)),
            pl.BlockSpec((1, k), lambda i, j: (0, 0)),
            pl.BlockSpec((k, tn), lambda i, j: (0, j)),
            pl.BlockSpec((tm, kp), lambda i, j: (i, 0)),
            pl.BlockSpec((kp, tn), lambda i, j: (0, j)),
            pl.BlockSpec((tm, tn), lambda i, j: (i, j)),
        ],
        out_specs=pl.BlockSpec((tm, tn), lambda i, j: (i, j)),
        out_shape=jax.ShapeDtypeStruct((m, n), F32),
        scratch_shapes=[pltpu.VMEM((tm, k), BF16)],
        compiler_params=_params(("parallel", "arbitrary"), 48),
    )(x, nw.reshape(1, k), w, p, wp, x)


def _layer_weights(i, attn_norm_w, w_in, q_norm_w, k_norm_w, attn_sinks, sgu_norm_w, sgu_w, sgu_b,
                   w_br_attn, w_br_gm, w_gate, b_gate, w_out, ffn_norm_w, w_up, conv_w, conv_b,
                   w_down, ple_norm_w, w_ple_gate, w_ple_proj):
    bf = lambda a: a[i].astype(BF16)
    sinks = attn_sinks[i]
    return dict(
        attn_norm_w=attn_norm_w[i], w_in=bf(w_in),
        qw2=jnp.tile(q_norm_w[i], HEADS_PER_VREG).reshape(1, LANES),
        kw2=jnp.tile(k_norm_w[i], HEADS_PER_VREG).reshape(1, LANES),
        sinks=sinks,
        sinks_row=jnp.pad(sinks, (0, LANES - N_HEADS)).reshape(1, LANES),
        sgu_norm_w=sgu_norm_w[i], sgu_w=sgu_w[i], sgu_b_t=jnp.transpose(sgu_b[i]),
        sgu_w0_row=jnp.repeat(sgu_w[i][:, 0, 0], w_br_gm.shape[1] // GM_GROUPS).reshape(1, -1),
        sgu_b0_row=jnp.repeat(sgu_b[i][:, 0], w_br_gm.shape[1] // GM_GROUPS).reshape(1, -1),
        w_br_attn=bf(w_br_attn), w_br_gm=bf(w_br_gm), w_gate=bf(w_gate), b_gate=b_gate[i],
        w_out=bf(w_out), ffn_norm_w=ffn_norm_w[i], w_up=bf(w_up), conv_w=conv_w[i], conv_b=conv_b[i],
        w_down=bf(w_down), ple_norm_w=ple_norm_w[i], w_ple_gate=bf(w_ple_gate), w_ple_proj=bf(w_ple_proj),
    )


def _tail(x, attn, sgu, gates, lw, *, tm, tn, h_dtype):
    merged = _merge(attn, sgu, lw["w_br_attn"], lw["w_br_gm"], gates, tm=tm, tn=tn)
    x1 = _mm_res(merged, lw["w_out"], x, tm=tm, tn=tn)
    h = _norm_mm(x1, lw["ffn_norm_w"], lw["w_up"], None, act=None, out_dtype=h_dtype, tm=tm, tn=tn)
    return x1, h


def _prompt_layer(x, p, lw, *, tm=512, tn=512, tf=512):
    batch, seq, d = x.shape
    x2d = x.reshape(batch * seq, d)
    attn_w = N_HEADS * HEAD_DIM
    kv_w = N_KV_HEADS * HEAD_DIM
    gm_w = lw["w_br_gm"].shape[0]
    proj = _norm_mm(x2d, lw["attn_norm_w"], lw["w_in"], None, act=None, out_dtype=BF16, tm=tm, tn=tn)
    gates = _norm_mm(x2d, lw["attn_norm_w"], lw["w_gate"], lw["b_gate"], act="sigmoid",
                     out_dtype=BF16, tm=tm, tn=tn)
    attn, k_new, v_new = _attn_prompt(proj, lw["qw2"], lw["kw2"], lw["sinks"], batch=batch, seq=seq)
    sgu, sgu_state = _sgu_prompt(proj, lw["sgu_norm_w"], lw["sgu_w"], lw["sgu_b_t"], batch=batch, seq=seq,
                                 gu_off=attn_w + 2 * kv_w, gm_w=gm_w)
    x1, h = _tail(x2d, attn, sgu, gates, lw, tm=tm, tn=tn, h_dtype=BF16)
    x2 = _ffn_down_prompt(h, lw["conv_w"], lw["conv_b"], lw["w_down"], x1, seq=seq, tm=tm, tf=tf)
    x3 = _ple(x2, lw["ple_norm_w"], lw["w_ple_gate"], p.reshape(batch * seq, -1), lw["w_ple_proj"], tm=tm, tn=tn)
    conv_state = h.reshape(batch, seq, -1)[:, -2:, :].astype(F32)
    return (x3.reshape(batch, seq, d), k_new.reshape(batch, WINDOW, N_KV_HEADS, HEAD_DIM),
            v_new.reshape(batch, WINDOW, N_KV_HEADS, HEAD_DIM), sgu_state, conv_state)


def _sample_layer(x, p, k_buf, v_buf, conv_buf, lw, *, tn=512, tf=512):
    nb, t, d = x.shape
    x2d = x.reshape(nb * t, d)
    attn_w = N_HEADS * HEAD_DIM
    kv_w = N_KV_HEADS * HEAD_DIM
    gm_w = lw["w_br_gm"].shape[0]
    proj = _norm_mm(x2d, lw["attn_norm_w"], lw["w_in"], None, act=None, out_dtype=F32, tm=nb, tn=tn)
    gates = _norm_mm(x2d, lw["attn_norm_w"], lw["w_gate"], lw["b_gate"], act="sigmoid",
                     out_dtype=F32, tm=nb, tn=tn)
    q = proj[:, :attn_w]
    k = proj[:, attn_w:attn_w + kv_w]
    v = proj[:, attn_w + kv_w:attn_w + 2 * kv_w]
    gu = proj[:, attn_w + 2 * kv_w:attn_w + 2 * kv_w + gm_w]
    gv = proj[:, attn_w + 2 * kv_w + gm_w:]
    attn, k_new, v_new = _attn_sample(q, k, v, k_buf.reshape(nb, WINDOW, kv_w), v_buf.reshape(nb, WINDOW, kv_w),
                                      lw["qw2"], lw["kw2"], lw["sinks_row"], group=8)
    sgu, sgu_state = _sgu_sample(gu, gv, lw["sgu_norm_w"], lw["sgu_w0_row"], lw["sgu_b0_row"])
    x1, h = _tail(x2d, attn, sgu, gates, lw, tm=nb, tn=tn, h_dtype=F32)
    x2 = _ffn_down_sample(h, conv_buf.reshape(nb, -1), lw["conv_w"], lw["conv_b"], lw["w_down"], x1, tf=tf)
    x3 = _ple(x2, lw["ple_norm_w"], lw["w_ple_gate"], p.reshape(nb * t, -1), lw["w_ple_proj"], tm=nb, tn=tn)
    conv_state = jnp.stack([conv_buf[:, 1, :], h], axis=1)
    return (x3.reshape(nb, t, d), k_new.reshape(nb, WINDOW, N_KV_HEADS, HEAD_DIM),
            v_new.reshape(nb, WINDOW, N_KV_HEADS, HEAD_DIM), sgu_state.reshape(nb, t, gm_w), conv_state)


def kernel(x_prompt, x_sample, p_prompt, p_sample, state_attn_k, state_attn_v, state_conv, attn_norm_w, w_in, q_norm_w, k_norm_w, attn_sinks, sgu_norm_w, sgu_w, sgu_b, w_br_attn, w_br_gm, w_gate, b_gate, w_out, ffn_norm_w, w_up, conv_w, conv_b, w_down, ple_norm_w, w_ple_gate, w_ple_proj):
    depth = w_in.shape[0]
    xp, xs = x_prompt, x_sample
    outs = [[] for _ in range(8)]
    for i in range(depth):
        lw = _layer_weights(i, attn_norm_w, w_in, q_norm_w, k_norm_w, attn_sinks, sgu_norm_w, sgu_w, sgu_b,
                            w_br_attn, w_br_gm, w_gate, b_gate, w_out, ffn_norm_w, w_up, conv_w, conv_b,
                            w_down, ple_norm_w, w_ple_gate, w_ple_proj)
        xp, kp, vp, gp, cp = _prompt_layer(xp, p_prompt[i], lw)
        xs, ks, vs, gs, cs = _sample_layer(xs, p_sample[i], state_attn_k[i], state_attn_v[i], state_conv[i], lw)
        for lst, val in zip(outs, (kp, vp, ks, vs, gp, gs, cp, cs)):
            lst.append(val)
    return (xp, xs) + tuple(jnp.stack(lst) for lst in outs)
```

```python
import functools

import jax
import jax.numpy as jnp
from jax import lax
from jax.experimental import pallas as pl
from jax.experimental.pallas import tpu as pltpu

F32 = jnp.float32
BF16 = jnp.bfloat16

HEAD_DIM = 64
N_HEADS = 16
N_KV_HEADS = 4
GQA_GROUP = N_HEADS // N_KV_HEADS
WINDOW = 128
CHUNK = 128
GM_GROUPS = 4
EPS = 1e-6
MASK_VALUE = -1e30
LANES = 128
SUBLANES = 8
HEADS_PER_VREG = LANES // HEAD_DIM
MIB = 1 << 20


def _alibi_slope(h):
    return float(2.0 ** (-8.0 * (h + 1) / N_HEADS))


def _params(semantics, vmem_mib):
    return pltpu.CompilerParams(dimension_semantics=semantics, vmem_limit_bytes=vmem_mib * MIB)


def _rmsnorm_rows(x, w):
    ms = jnp.mean(x * x, axis=-1, keepdims=True)
    return x * lax.rsqrt(ms + EPS) * w


NORM_ROWS = 256


def _rmsnorm_to(x_ref, nw_ref, xn_ref, add_ref=None):
    tm = x_ref.shape[0]
    rows = min(NORM_ROWS, tm)

    def body(r, carry):
        sl = pl.ds(pl.multiple_of(r * rows, rows), rows)
        x = x_ref[sl, :]
        if add_ref is not None:
            x = x + add_ref[sl, :].astype(F32)
        xn_ref[sl, :] = _rmsnorm_rows(x, nw_ref[...]).astype(BF16)
        return carry

    lax.fori_loop(0, tm // rows, body, 0)


def _norm_mm_kernel(*refs, has_bias, act):
    if has_bias:
        x_ref, nw_ref, w_ref, b_ref, o_ref, xn_ref = refs
    else:
        x_ref, nw_ref, w_ref, o_ref, xn_ref = refs

    @pl.when(pl.program_id(1) == 0)
    def _():
        _rmsnorm_to(x_ref, nw_ref, xn_ref)

    acc = jnp.dot(xn_ref[...], w_ref[...].astype(BF16), preferred_element_type=F32)
    if has_bias:
        acc = acc + b_ref[...]
    if act == "sigmoid":
        acc = jax.nn.sigmoid(acc)
    o_ref[...] = acc.astype(o_ref.dtype)


def _norm_mm(x, nw, w, bias, *, act, out_dtype, tm, tn, name):
    m, k = x.shape
    n = w.shape[1]
    in_specs = [
        pl.BlockSpec((tm, k), lambda i, j: (i, 0)),
        pl.BlockSpec((1, k), lambda i, j: (0, 0)),
        pl.BlockSpec((k, tn), lambda i, j: (0, j)),
    ]
    args = [x, nw.reshape(1, k), w]
    if bias is not None:
        in_specs.append(pl.BlockSpec((1, tn), lambda i, j: (0, j)))
        args.append(bias.reshape(1, n))
    return pl.pallas_call(
        functools.partial(_norm_mm_kernel, has_bias=bias is not None, act=act),
        grid=(m // tm, n // tn),
        in_specs=in_specs,
        out_specs=pl.BlockSpec((tm, tn), lambda i, j: (i, j)),
        out_shape=jax.ShapeDtypeStruct((m, n), out_dtype),
        scratch_shapes=[pltpu.VMEM((tm, k), BF16)],
        compiler_params=_params(("parallel", "arbitrary"), 48),
        name=name,
    )(*args)


def _in_proj_kernel(x_ref, nw_ref, win_ref, wg_ref, bg_ref, proj_ref, gates_ref, xn_ref, *, n_in):
    j = pl.program_id(1)

    @pl.when(j == 0)
    def _():
        _rmsnorm_to(x_ref, nw_ref, xn_ref)

    @pl.when(j < n_in)
    def _():
        acc = jnp.dot(xn_ref[...], win_ref[...].astype(BF16), preferred_element_type=F32)
        proj_ref[...] = acc.astype(proj_ref.dtype)

    @pl.when(j >= n_in)
    def _():
        acc = jnp.dot(xn_ref[...], wg_ref[...].astype(BF16), preferred_element_type=F32)
        gates_ref[...] = jax.nn.sigmoid(acc + bg_ref[...]).astype(gates_ref.dtype)


def _in_proj(x, nw, w_in, w_gate, b_gate, *, tm, tn):
    m, k = x.shape
    n_in = w_in.shape[1] // tn
    n_g = w_gate.shape[1] // tn
    in_col = lambda i, j: (0, jnp.minimum(j, n_in - 1))
    g_col = lambda i, j: (0, jnp.maximum(j - n_in, 0))
    return pl.pallas_call(
        functools.partial(_in_proj_kernel, n_in=n_in),
        grid=(m // tm, n_in + n_g),
        in_specs=[
            pl.BlockSpec((tm, k), lambda i, j: (i, 0)),
            pl.BlockSpec((1, k), lambda i, j: (0, 0)),
            pl.BlockSpec((k, tn), in_col),
            pl.BlockSpec((k, tn), g_col),
            pl.BlockSpec((1, tn), g_col),
        ],
        out_specs=[
            pl.BlockSpec((tm, tn), lambda i, j: (i, jnp.minimum(j, n_in - 1))),
            pl.BlockSpec((tm, tn), lambda i, j: (i, jnp.maximum(j - n_in, 0))),
        ],
        out_shape=[
            jax.ShapeDtypeStruct((m, w_in.shape[1]), BF16),
            jax.ShapeDtypeStruct((m, w_gate.shape[1]), BF16),
        ],
        scratch_shapes=[pltpu.VMEM((tm, k), BF16)],
        compiler_params=_params(("arbitrary", "arbitrary"), 56),
        name="in_proj",
    )(x, nw.reshape(1, k), w_in, w_gate, b_gate.reshape(1, -1))


def _head_group_matrix():
    r = lax.broadcasted_iota(jnp.int32, (LANES, LANES), 0)
    c = lax.broadcasted_iota(jnp.int32, (LANES, LANES), 1)
    return ((r >> 6) == (c >> 6)).astype(BF16)


def _head_rmsnorm(xcol, w, gmat):
    ss = jnp.dot((xcol * xcol).astype(BF16), gmat, preferred_element_type=F32)
    return xcol * lax.rsqrt(ss * (1.0 / HEAD_DIM) + EPS) * w


def _attn_prompt_kernel(sink_ref, q_ref, kc_ref, kp_ref, vc_ref, vp_ref, qw_ref, kw_ref,
                        o_ref, kn_ref, vn_ref):
    c = pl.program_id(1)
    blk = WINDOW
    gmat = _head_group_matrix()
    lane = lax.broadcasted_iota(jnp.int32, (1, LANES), 1)
    low = lane < HEAD_DIM

    kw = kw_ref[...]
    qw = qw_ref[...]
    kc = kc_ref[...].astype(F32)
    kp = kp_ref[...].astype(F32)
    vc = vc_ref[...].astype(F32)
    vp = vp_ref[...].astype(F32)
    ncol = kc.shape[1] // LANES
    kcn = [_head_rmsnorm(kc[:, p * LANES:(p + 1) * LANES], kw, gmat) for p in range(ncol)]
    kpn = [_head_rmsnorm(kp[:, p * LANES:(p + 1) * LANES], kw, gmat) for p in range(ncol)]
    kn_ref[0] = jnp.concatenate(kcn, axis=1)
    vn_ref[0] = vc

    kcat = [jnp.concatenate([kpn[p], kcn[p]], axis=0) for p in range(ncol)]
    vcat = [jnp.concatenate([vp[:, p * LANES:(p + 1) * LANES], vc[:, p * LANES:(p + 1) * LANES]], axis=0)
            for p in range(ncol)]
    krol = [pltpu.roll(kcat[p], HEAD_DIM, 1) for p in range(ncol)]
    vrol = [pltpu.roll(vcat[p], HEAD_DIM, 1) for p in range(ncol)]

    qi = lax.broadcasted_iota(jnp.int32, (blk, 2 * blk), 0)
    kj = lax.broadcasted_iota(jnp.int32, (blk, 2 * blk), 1)
    dist = blk + qi - kj
    valid = (dist >= 0) & (dist <= WINDOW) & ((kj >= blk) | (c > 0))
    distf = dist.astype(F32)

    q = q_ref[...].astype(F32)
    qn = [_head_rmsnorm(q[:, p * LANES:(p + 1) * LANES], qw, gmat) * (HEAD_DIM ** -0.5)
          for p in range(N_HEADS // HEADS_PER_VREG)]

    for g in range(N_KV_HEADS):
        p, half = divmod(g, HEADS_PER_VREG)
        if half == 0:
            kd = jnp.where(low, kcat[p], krol[p])
            vd = jnp.where(low, vcat[p], vrol[p])
        else:
            kd = jnp.where(low, krol[p], kcat[p])
            vd = jnp.where(low, vrol[p], vcat[p])
        kd = kd.astype(BF16)
        vd = vd.astype(BF16)
        qs = []
        for hl in range(GQA_GROUP):
            h = g * GQA_GROUP + hl
            pc, hh = divmod(h, HEADS_PER_VREG)
            keep = low if hh == 0 else jnp.logical_not(low)
            qs.append(jnp.where(keep, qn[pc], 0.0).astype(BF16))
        qstack = jnp.concatenate(qs, axis=0)
        s_all = lax.dot_general(qstack, kd, (((1,), (1,)), ((), ())), preferred_element_type=F32)
        ps = []
        for hl in range(GQA_GROUP):
            h = g * GQA_GROUP + hl
            s = s_all[hl * blk:(hl + 1) * blk] - _alibi_slope(h) * distf
            s = jnp.where(valid, s, MASK_VALUE)
            sink = sink_ref[h]
            mx = jnp.maximum(jnp.max(s, axis=-1, keepdims=True), sink)
            e = jnp.exp(s - mx)
            den = jnp.sum(e, axis=-1, keepdims=True) + jnp.exp(sink - mx)
            ps.append((e / den).astype(BF16))
        pstack = jnp.concatenate(ps, axis=0)
        o_all = jnp.dot(pstack, vd, preferred_element_type=F32)
        for pair in range(GQA_GROUP // HEADS_PER_VREG):
            pc = g * (GQA_GROUP // HEADS_PER_VREG) + pair
            even = o_all[(2 * pair) * blk:(2 * pair + 1) * blk]
            odd = o_all[(2 * pair + 1) * blk:(2 * pair + 2) * blk]
            o_ref[:, pc * LANES:(pc + 1) * LANES] = jnp.where(low, even, odd).astype(o_ref.dtype)


def _attn_prompt(proj, qw2, kw2, sinks, *, batch, seq):
    nb = seq // WINDOW
    attn_w = N_HEADS * HEAD_DIM
    kv_w = N_KV_HEADS * HEAD_DIM
    kcol = attn_w // kv_w
    vcol = kcol + 1

    def cur(col):
        return lambda b, c: (b * nb + c, col)

    def prev(col):
        return lambda b, c: (jnp.maximum(b * nb + c - 1, 0), col)

    return pl.pallas_call(
        _attn_prompt_kernel,
        grid=(batch, nb),
        in_specs=[
            pl.BlockSpec(memory_space=pltpu.SMEM),
            pl.BlockSpec((WINDOW, attn_w), cur(0)),
            pl.BlockSpec((WINDOW, kv_w), cur(kcol)),
            pl.BlockSpec((WINDOW, kv_w), prev(kcol)),
            pl.BlockSpec((WINDOW, kv_w), cur(vcol)),
            pl.BlockSpec((WINDOW, kv_w), prev(vcol)),
            pl.BlockSpec((1, LANES), lambda b, c: (0, 0)),
            pl.BlockSpec((1, LANES), lambda b, c: (0, 0)),
        ],
        out_specs=[
            pl.BlockSpec((WINDOW, attn_w), lambda b, c: (b * nb + c, 0)),
            pl.BlockSpec((1, WINDOW, kv_w), lambda b, c: (b, 0, 0)),
            pl.BlockSpec((1, WINDOW, kv_w), lambda b, c: (b, 0, 0)),
        ],
        out_shape=[
            jax.ShapeDtypeStruct((batch * seq, attn_w), BF16),
            jax.ShapeDtypeStruct((batch, WINDOW, kv_w), F32),
            jax.ShapeDtypeStruct((batch, WINDOW, kv_w), F32),
        ],
        compiler_params=_params(("parallel", "arbitrary"), 32),
        name="attn_prompt",
    )(sinks, proj, proj, proj, proj, proj, qw2, kw2)


def _layernorm_rows(x, w):
    mu = jnp.mean(x, axis=-1, keepdims=True)
    xc = x - mu
    return xc * lax.rsqrt(jnp.mean(xc * xc, axis=-1, keepdims=True) + EPS) * w


def _sgu_prompt_kernel(gu0_ref, gu1_ref, gv0_ref, gv1_ref, nw_ref, ws_ref, bs_ref, o_ref, st_ref):
    gv = jnp.concatenate([gv0_ref[...], gv1_ref[...]], axis=1).astype(F32)
    vn = _layernorm_rows(jax.nn.gelu(gv), nw_ref[...])
    st_ref[0] = vn
    vb = vn.astype(BF16)
    r = lax.broadcasted_iota(jnp.int32, (CHUNK, CHUNK), 0)
    c = lax.broadcasted_iota(jnp.int32, (CHUNK, CHUNK), 1)
    causal = r >= c
    gw = vb.shape[1] // GM_GROUPS
    half = gu0_ref.shape[1]
    for g in range(GM_GROUPS):
        w = jnp.where(causal, ws_ref[g], 0.0).astype(BF16)
        mix = jnp.dot(w, vb[:, g * gw:(g + 1) * gw], preferred_element_type=F32) + bs_ref[:, g:g + 1]
        src = gu0_ref if g * gw < half else gu1_ref
        off = g * gw - (0 if g * gw < half else half)
        u = jax.nn.gelu(src[:, off:off + gw].astype(F32))
        o_ref[:, g * gw:(g + 1) * gw] = (u * mix).astype(o_ref.dtype)


def _sgu_prompt(proj, nw, ws, bs_t, *, batch, seq, gu_off, gm_w):
    nc = seq // CHUNK
    half = gm_w // 2
    b0 = gu_off // half

    def col(k):
        return lambda b, c: (b * nc + c, b0 + k)

    return pl.pallas_call(
        _sgu_prompt_kernel,
        grid=(batch, nc),
        in_specs=[
            pl.BlockSpec((CHUNK, half), col(0)),
            pl.BlockSpec((CHUNK, half), col(1)),
            pl.BlockSpec((CHUNK, half), col(2)),
            pl.BlockSpec((CHUNK, half), col(3)),
            pl.BlockSpec((1, gm_w), lambda b, c: (0, 0)),
            pl.BlockSpec((GM_GROUPS, CHUNK, CHUNK), lambda b, c: (0, 0, 0)),
            pl.BlockSpec((CHUNK, GM_GROUPS), lambda b, c: (0, 0)),
        ],
        out_specs=[
            pl.BlockSpec((CHUNK, gm_w), lambda b, c: (b * nc + c, 0)),
            pl.BlockSpec((1, CHUNK, gm_w), lambda b, c: (b, 0, 0)),
        ],
        out_shape=[
            jax.ShapeDtypeStruct((batch * seq, gm_w), BF16),
            jax.ShapeDtypeStruct((batch, CHUNK, gm_w), F32),
        ],
        compiler_params=_params(("parallel", "arbitrary"), 32),
        name="sgu_prompt",
    )(proj, proj, proj, proj, nw.reshape(1, gm_w), ws, bs_t)


def _split_dot(x, m):
    hi = x.astype(BF16)
    lo = (x - hi.astype(F32)).astype(BF16)
    return jnp.dot(hi, m, preferred_element_type=F32) + jnp.dot(lo, m, preferred_element_type=F32)


def _attn_sample_kernel(q_ref, kn_ref, vn_ref, kb_ref, vb_ref, qw_ref, kw_ref, sink_ref,
                        o_ref, ko_ref, vo_ref):
    nsamp = q_ref.shape[0]
    attn_w = q_ref.shape[1]
    kv_w = kn_ref.shape[1]
    gmat = _head_group_matrix()
    q = q_ref[...].astype(F32)
    kn = kn_ref[...].astype(F32)
    qn = jnp.concatenate(
        [_head_rmsnorm(q[:, p * LANES:(p + 1) * LANES], qw_ref[...], gmat) * (HEAD_DIM ** -0.5)
         for p in range(attn_w // LANES)], axis=1)
    knn = jnp.concatenate(
        [_head_rmsnorm(kn[:, p * LANES:(p + 1) * LANES], kw_ref[...], gmat)
         for p in range(kv_w // LANES)], axis=1)
    vnn = vn_ref[...].astype(F32)

    ec = lax.broadcasted_iota(jnp.int32, (kv_w, attn_w), 0)
    el = lax.broadcasted_iota(jnp.int32, (kv_w, attn_w), 1)
    expand = (((ec >> 6) == (el >> 8)) & ((ec & 63) == (el & 63))).astype(BF16)
    gl = lax.broadcasted_iota(jnp.int32, (attn_w, LANES), 0)
    gh = lax.broadcasted_iota(jnp.int32, (attn_w, LANES), 1)
    hsum = ((gl >> 6) == gh).astype(BF16)
    tl = lax.broadcasted_iota(jnp.int32, (LANES, attn_w), 1)
    th = lax.broadcasted_iota(jnp.int32, (LANES, attn_w), 0)
    hexp = ((tl >> 6) == th).astype(BF16)

    nkeys = WINDOW + SUBLANES
    row = lax.broadcasted_iota(jnp.int32, (nkeys, LANES), 0)
    head = lax.broadcasted_iota(jnp.int32, (nkeys, LANES), 1)
    slope = jnp.exp2(-8.0 * (head + 1).astype(F32) / N_HEADS)
    dist = (WINDOW - row).astype(F32)
    key_ok = row <= WINDOW
    srow = lax.broadcasted_iota(jnp.int32, (WINDOW, kv_w), 0)
    sinks = sink_ref[...]

    for s in range(nsamp):
        kb = kb_ref[s]
        vb = vb_ref[s]
        knew = knn[s:s + 1]
        vnew = vnn[s:s + 1]
        ko_ref[s] = jnp.where(srow == WINDOW - 1, knew, pltpu.roll(kb, WINDOW - 1, 0))
        vo_ref[s] = jnp.where(srow == WINDOW - 1, vnew, pltpu.roll(vb, WINDOW - 1, 0))
        kk = jnp.concatenate([kb, jnp.broadcast_to(knew, (SUBLANES, kv_w))], axis=0)
        vv = jnp.concatenate([vb, jnp.broadcast_to(vnew, (SUBLANES, kv_w))], axis=0)
        kexp = _split_dot(kk, expand)
        vexp = _split_dot(vv, expand)
        sc = _split_dot(kexp * qn[s:s + 1], hsum)
        sc = sc - slope * dist
        sc = jnp.where(key_ok, sc, MASK_VALUE)
        mx = jnp.maximum(jnp.max(sc, axis=0, keepdims=True), sinks)
        e = jnp.exp(sc - mx)
        den = jnp.sum(e, axis=0, keepdims=True) + jnp.exp(sinks - mx)
        pexp = _split_dot(e / den, hexp)
        o_ref[s:s + 1, :] = jnp.sum(pexp * vexp, axis=0, keepdims=True)


def _attn_sample(q, knew, vnew, kbuf, vbuf, qw2, kw2, sinks_row, *, group):
    nb, attn_w = q.shape
    kv_w = knew.shape[1]
    row = lambda width: pl.BlockSpec((group, width), lambda i: (i, 0))
    buf = pl.BlockSpec((group, WINDOW, kv_w), lambda i: (i, 0, 0))
    vec = pl.BlockSpec((1, LANES), lambda i: (0, 0))
    return pl.pallas_call(
        _attn_sample_kernel,
        grid=(nb // group,),
        in_specs=[row(attn_w), row(kv_w), row(kv_w), buf, buf, vec, vec, vec],
        out_specs=[row(attn_w), buf, buf],
        out_shape=[
            jax.ShapeDtypeStruct((nb, attn_w), F32),
            jax.ShapeDtypeStruct((nb, WINDOW, kv_w), F32),
            jax.ShapeDtypeStruct((nb, WINDOW, kv_w), F32),
        ],
        compiler_params=_params(("parallel",), 32),
        name="attn_sample",
    )(q, knew, vnew, kbuf, vbuf, qw2, kw2, sinks_row)


def _sgu_sample_kernel(gu_ref, gv_ref, nw_ref, w0_ref, b0_ref, o_ref, st_ref):
    vn = _layernorm_rows(jax.nn.gelu(gv_ref[...].astype(F32)), nw_ref[...])
    st_ref[...] = vn
    mix = w0_ref[...] * vn + b0_ref[...]
    o_ref[...] = jax.nn.gelu(gu_ref[...].astype(F32)) * mix


def _sgu_sample(gu, gv, nw, w0_row, b0_row):
    nb, gm_w = gu.shape
    full = pl.BlockSpec((nb, gm_w), lambda i: (0, 0))
    vec = pl.BlockSpec((1, gm_w), lambda i: (0, 0))
    return pl.pallas_call(
        _sgu_sample_kernel,
        grid=(1,),
        in_specs=[full, full, vec, vec, vec],
        out_specs=[full, full],
        out_shape=[jax.ShapeDtypeStruct((nb, gm_w), F32), jax.ShapeDtypeStruct((nb, gm_w), F32)],
        name="sgu_sample",
    )(gu, gv, nw.reshape(1, gm_w), w0_row, b0_row)


def _merge_out_kernel(a_ref, s_ref, ga_ref, gb_ref, x_ref, wa_ref, wg_ref, wo_ref, nw_ref,
                      x1_ref, xn1_ref):
    a = jnp.dot(a_ref[...].astype(BF16), wa_ref[...], preferred_element_type=F32)
    m = jnp.dot(s_ref[...].astype(BF16), wg_ref[...], preferred_element_type=F32)
    merged = (gb_ref[...].astype(F32) * a + ga_ref[...].astype(F32) * m).astype(BF16)
    x1 = x_ref[...] + jnp.dot(merged, wo_ref[...], preferred_element_type=F32)
    x1_ref[...] = x1
    xn1_ref[...] = _rmsnorm_rows(x1, nw_ref[...]).astype(BF16)


def _merge_out(attn, sgu, gates, x, wa, wg, wo, nw, *, tm):
    m, ka = attn.shape
    kg = sgu.shape[1]
    d = wo.shape[1]
    rows = lambda width, col=0: pl.BlockSpec((tm, width), lambda i: (i, col))
    resident = lambda shape: pl.BlockSpec(shape, lambda i: (0, 0), pipeline_mode=pl.Buffered(1))
    return pl.pallas_call(
        _merge_out_kernel,
        grid=(m // tm,),
        in_specs=[
            rows(ka), rows(kg), rows(d, 0), rows(d, 1), rows(d),
            resident((ka, d)), resident((kg, d)), resident((d, d)),
            pl.BlockSpec((1, d), lambda i: (0, 0)),
        ],
        out_specs=[rows(d), rows(d)],
        out_shape=[jax.ShapeDtypeStruct((m, d), F32), jax.ShapeDtypeStruct((m, d), BF16)],
        compiler_params=_params(("parallel",), 48),
        name="merge_out",
    )(attn, sgu, gates, gates, x, wa, wg, wo, nw.reshape(1, d))


PAD = SUBLANES


def _ffn_step(t, xn_ref, wug_ref, wuu_ref, wd_ref, cwg_ref, cwu_ref, cbg_ref, cbu_ref,
              csg_ref, csu_ref, h_new, h_old, carry_ref, acc_ref, *, nf, tiles_per_seq):
    tm = xn_ref.shape[0]
    tf = wug_ref.shape[1]
    xn = xn_ref[...]
    h_new[PAD:PAD + tm, 0:tf] = jnp.dot(xn, wug_ref[...].astype(BF16), preferred_element_type=F32)
    h_new[PAD:PAD + tm, tf:2 * tf] = jnp.dot(xn, wuu_ref[...].astype(BF16), preferred_element_type=F32)

    tp = jnp.maximum(t - 1, 0)
    ip = tp // nf
    jp = tp % nf
    first = (ip % tiles_per_seq) == 0
    h_old[0:PAD, :] = jnp.where(first, 0.0, carry_ref[jp])
    carry_ref[jp] = h_old[tm:tm + PAD, :]
    cw = jnp.concatenate([cwg_ref[...], cwu_ref[...]], axis=1)
    cb = jnp.concatenate([cbg_ref[...], cbu_ref[...]], axis=1)
    hc = (cw[0:1] * h_old[PAD - 2:PAD - 2 + tm, :] + cw[1:2] * h_old[PAD - 1:PAD - 1 + tm, :]
          + cw[2:3] * h_old[PAD:PAD + tm, :] + cb)
    act = (jax.nn.silu(hc[:, 0:tf]) * hc[:, tf:2 * tf]).astype(BF16)
    contrib = jnp.dot(act, wd_ref[...].astype(BF16), preferred_element_type=F32)
    acc_ref[...] = jnp.where(jp == 0, 0.0, acc_ref[...]) + contrib
    csg_ref[0] = h_old[PAD + tm - 2:PAD + tm, 0:tf]
    csu_ref[0] = h_old[PAD + tm - 2:PAD + tm, tf:2 * tf]


def _ffn_kernel(xn_ref, wug_ref, wuu_ref, wd_ref, cwg_ref, cwu_ref, cbg_ref, cbu_ref,
                o_ref, csg_ref, csu_ref, ha_ref, hb_ref, carry_ref, acc_ref, *, nf, tiles_per_seq):
    t = pl.program_id(0)
    step = functools.partial(_ffn_step, t, xn_ref, wug_ref, wuu_ref, wd_ref, cwg_ref, cwu_ref,
                             cbg_ref, cbu_ref, csg_ref, csu_ref, carry_ref=carry_ref, acc_ref=acc_ref,
                             nf=nf, tiles_per_seq=tiles_per_seq)

    @pl.when(t == 0)
    def _():
        hb_ref[...] = jnp.zeros_like(hb_ref)
        carry_ref[...] = jnp.zeros_like(carry_ref)
        acc_ref[...] = jnp.zeros_like(acc_ref)

    @pl.when(t % 2 == 0)
    def _():
        step(h_new=ha_ref, h_old=hb_ref)

    @pl.when(t % 2 == 1)
    def _():
        step(h_new=hb_ref, h_old=ha_ref)

    @pl.when(jnp.maximum(t - 1, 0) % nf == nf - 1)
    def _():
        o_ref[...] = acc_ref[...].astype(o_ref.dtype)


def _ffn_prompt(xn, w_up, w_down, cw, cb, *, batch, seq, tm, tf):
    m, k = xn.shape
    d_ff, d = w_down.shape
    nf = d_ff // tf
    nm = m // tm
    tiles_per_seq = seq // tm
    prev = lambda t: jnp.maximum(t - 1, 0)
    up_col = lambda off: (lambda t: (0, off + t % nf))
    dn_col = lambda off: (lambda t: (0, off + prev(t) % nf))
    state = lambda t: (prev(t) // nf, 0, prev(t) % nf)
    delta, csg, csu = pl.pallas_call(
        functools.partial(_ffn_kernel, nf=nf, tiles_per_seq=tiles_per_seq),
        grid=(nm * nf + 1,),
        in_specs=[
            pl.BlockSpec((tm, k), lambda t: (jnp.minimum(t // nf, nm - 1), 0)),
            pl.BlockSpec((k, tf), up_col(0)),
            pl.BlockSpec((k, tf), up_col(nf)),
            pl.BlockSpec((tf, d), lambda t: (prev(t) % nf, 0)),
            pl.BlockSpec((3, tf), dn_col(0)),
            pl.BlockSpec((3, tf), dn_col(nf)),
            pl.BlockSpec((1, tf), dn_col(0)),
            pl.BlockSpec((1, tf), dn_col(nf)),
        ],
        out_specs=[
            pl.BlockSpec((tm, d), lambda t: (prev(t) // nf, 0)),
            pl.BlockSpec((1, 2, tf), state),
            pl.BlockSpec((1, 2, tf), state),
        ],
        out_shape=[
            jax.ShapeDtypeStruct((m, d), BF16),
            jax.ShapeDtypeStruct((nm, 2, d_ff), F32),
            jax.ShapeDtypeStruct((nm, 2, d_ff), F32),
        ],
        scratch_shapes=[
            pltpu.VMEM((tm + PAD, 2 * tf), F32),
            pltpu.VMEM((tm + PAD, 2 * tf), F32),
            pltpu.VMEM((nf, PAD, 2 * tf), F32),
            pltpu.VMEM((tm, d), F32),
        ],
        compiler_params=_params(("arbitrary",), 58),
        name="ffn_prompt",
    )(xn, w_up, w_up, w_down, cw, cw, cb.reshape(1, -1), cb.reshape(1, -1))
    tails = jnp.concatenate([csg, csu], axis=-1).reshape(batch, tiles_per_seq, 2, 2 * d_ff)
    return delta, tails[:, -1]


def _ffn_down_sample_kernel(hg_ref, hu_ref, b0g_ref, b0u_ref, b1g_ref, b1u_ref, cwg_ref, cwu_ref,
                            cbg_ref, cbu_ref, wd_ref, x_ref, o_ref):
    @pl.when(pl.program_id(0) == 0)
    def _():
        o_ref[...] = x_ref[...]

    def conv(h_ref, b0_ref, b1_ref, cw_ref, cb_ref):
        cw = cw_ref[...]
        return cw[0:1] * b0_ref[...] + cw[1:2] * b1_ref[...] + cw[2:3] * h_ref[...] + cb_ref[...]

    hcg = conv(hg_ref, b0g_ref, b1g_ref, cwg_ref, cbg_ref)
    hcu = conv(hu_ref, b0u_ref, b1u_ref, cwu_ref, cbu_ref)
    act = (jax.nn.silu(hcg) * hcu).astype(BF16)
    o_ref[...] += jnp.dot(act, wd_ref[...].astype(BF16), preferred_element_type=F32)


def _ffn_down_sample(h, state2, cw, cb, wd, x, *, tf):
    nb, two_f = h.shape
    d_ff = two_f // 2
    nf = d_ff // tf
    d = wd.shape[1]
    colblk = lambda off: pl.BlockSpec((nb, tf), lambda j: (0, j + off))
    return pl.pallas_call(
        _ffn_down_sample_kernel,
        grid=(nf,),
        in_specs=[
            colblk(0), colblk(nf),
            colblk(0), colblk(nf), colblk(2 * nf), colblk(3 * nf),
            pl.BlockSpec((3, tf), lambda j: (0, j)),
            pl.BlockSpec((3, tf), lambda j: (0, j + nf)),
            pl.BlockSpec((1, tf), lambda j: (0, j)),
            pl.BlockSpec((1, tf), lambda j: (0, j + nf)),
            pl.BlockSpec((tf, d), lambda j: (j, 0)),
            pl.BlockSpec((nb, d), lambda j: (0, 0)),
        ],
        out_specs=pl.BlockSpec((nb, d), lambda j: (0, 0)),
        out_shape=jax.ShapeDtypeStruct((nb, d), F32),
        compiler_params=_params(("arbitrary",), 32),
        name="ffn_down_sample",
    )(h, h, state2, state2, state2, state2, cw, cw, cb.reshape(1, two_f), cb.reshape(1, two_f), wd, x)


def _ple_kernel(*refs, has_delta):
    if has_delta:
        x_ref, d_ref, nw_ref, w_ref, p_ref, wp_ref, o_ref, xn_ref = refs
    else:
        x_ref, nw_ref, w_ref, p_ref, wp_ref, o_ref, xn_ref = refs
        d_ref = None
    j = pl.program_id(1)
    tn = o_ref.shape[1]

    @pl.when(j == 0)
    def _():
        _rmsnorm_to(x_ref, nw_ref, xn_ref, add_ref=d_ref)

    gate = jax.nn.sigmoid(jnp.dot(xn_ref[...], w_ref[...].astype(BF16), preferred_element_type=F32))
    emb = jnp.dot(p_ref[...].astype(BF16), wp_ref[...].astype(BF16), preferred_element_type=F32)
    o_ref[...] = gate * emb
    for jj in range(x_ref.shape[1] // tn):
        @pl.when(j == jj)
        def _():
            res = x_ref[:, jj * tn:(jj + 1) * tn]
            if has_delta:
                res = res + d_ref[:, jj * tn:(jj + 1) * tn].astype(F32)
            o_ref[...] += res


def _ple(x, delta, nw, w, p, wp, *, tm, tn, name):
    m, k = x.shape
    n = w.shape[1]
    kp = p.shape[1]
    row_full = pl.BlockSpec((tm, k), lambda i, j: (i, 0))
    in_specs = [row_full] + ([row_full] if delta is not None else []) + [
        pl.BlockSpec((1, k), lambda i, j: (0, 0)),
        pl.BlockSpec((k, tn), lambda i, j: (0, j)),
        pl.BlockSpec((tm, kp), lambda i, j: (i, 0)),
        pl.BlockSpec((kp, tn), lambda i, j: (0, j)),
    ]
    args = [x] + ([delta] if delta is not None else []) + [nw.reshape(1, k), w, p, wp]
    return pl.pallas_call(
        functools.partial(_ple_kernel, has_delta=delta is not None),
        grid=(m // tm, n // tn),
        in_specs=in_specs,
        out_specs=pl.BlockSpec((tm, tn), lambda i, j: (i, j)),
        out_shape=jax.ShapeDtypeStruct((m, n), F32),
        scratch_shapes=[pltpu.VMEM((tm, k), BF16)],
        compiler_params=_params(("parallel", "arbitrary"), 56),
        name=name,
    )(*args)


def _layer_weights(i, attn_norm_w, w_in, q_norm_w, k_norm_w, attn_sinks, sgu_norm_w, sgu_w, sgu_b,
                   w_br_attn, w_br_gm, w_gate, b_gate, w_out, ffn_norm_w, w_up, conv_w, conv_b,
                   w_down, ple_norm_w, w_ple_gate, w_ple_proj):
    sinks = attn_sinks[i]
    gm_group_w = w_br_gm.shape[1] // GM_GROUPS
    return dict(
        attn_norm_w=attn_norm_w[i], w_in=w_in[i],
        qw2=jnp.tile(q_norm_w[i], HEADS_PER_VREG).reshape(1, LANES),
        kw2=jnp.tile(k_norm_w[i], HEADS_PER_VREG).reshape(1, LANES),
        sinks=sinks,
        sinks_row=jnp.pad(sinks, (0, LANES - N_HEADS)).reshape(1, LANES),
        sgu_norm_w=sgu_norm_w[i], sgu_w=sgu_w[i], sgu_b_t=jnp.transpose(sgu_b[i]),
        sgu_w0_row=jnp.repeat(sgu_w[i][:, 0, 0], gm_group_w).reshape(1, -1),
        sgu_b0_row=jnp.repeat(sgu_b[i][:, 0], gm_group_w).reshape(1, -1),
        w_br_attn=w_br_attn[i].astype(BF16), w_br_gm=w_br_gm[i].astype(BF16),
        w_gate=w_gate[i], b_gate=b_gate[i], w_out=w_out[i].astype(BF16),
        ffn_norm_w=ffn_norm_w[i], w_up=w_up[i], conv_w=conv_w[i], conv_b=conv_b[i],
        w_down=w_down[i], ple_norm_w=ple_norm_w[i], w_ple_gate=w_ple_gate[i], w_ple_proj=w_ple_proj[i],
    )


def _prompt_layer(x, p, lw):
    batch, seq, d = x.shape
    x2d = x.reshape(batch * seq, d)
    attn_w = N_HEADS * HEAD_DIM
    kv_w = N_KV_HEADS * HEAD_DIM
    gm_w = lw["w_br_gm"].shape[0]
    proj, gates = _in_proj(x2d, lw["attn_norm_w"], lw["w_in"], lw["w_gate"], lw["b_gate"], tm=1024, tn=512)
    attn, k_new, v_new = _attn_prompt(proj, lw["qw2"], lw["kw2"], lw["sinks"], batch=batch, seq=seq)
    sgu, sgu_state = _sgu_prompt(proj, lw["sgu_norm_w"], lw["sgu_w"], lw["sgu_b_t"], batch=batch, seq=seq,
                                 gu_off=attn_w + 2 * kv_w, gm_w=gm_w)
    x1, xn1 = _merge_out(attn, sgu, gates, x2d, lw["w_br_attn"], lw["w_br_gm"], lw["w_out"],
                         lw["ffn_norm_w"], tm=256)
    delta, conv_state = _ffn_prompt(xn1, lw["w_up"], lw["w_down"], lw["conv_w"], lw["conv_b"],
                                    batch=batch, seq=seq, tm=1024, tf=256)
    x3 = _ple(x1, delta, lw["ple_norm_w"], lw["w_ple_gate"], p.reshape(batch * seq, -1), lw["w_ple_proj"],
              tm=1024, tn=512, name="ple_prompt")
    return (x3.reshape(batch, seq, d), k_new.reshape(batch, WINDOW, N_KV_HEADS, HEAD_DIM),
            v_new.reshape(batch, WINDOW, N_KV_HEADS, HEAD_DIM), sgu_state, conv_state)


def _sample_layer(x, p, k_buf, v_buf, conv_buf, lw, *, tn=512, tf=512):
    nb, t, d = x.shape
    x2d = x.reshape(nb * t, d)
    attn_w = N_HEADS * HEAD_DIM
    kv_w = N_KV_HEADS * HEAD_DIM
    gm_w = lw["w_br_gm"].shape[0]
    proj = _norm_mm(x2d, lw["attn_norm_w"], lw["w_in"], None, act=None, out_dtype=F32, tm=nb, tn=tn,
                    name="proj_sample")
    gates = _norm_mm(x2d, lw["attn_norm_w"], lw["w_gate"], lw["b_gate"], act="sigmoid",
                     out_dtype=F32, tm=nb, tn=tn, name="gates_sample")
    q = proj[:, :attn_w]
    k = proj[:, attn_w:attn_w + kv_w]
    v = proj[:, attn_w + kv_w:attn_w + 2 * kv_w]
    gu = proj[:, attn_w + 2 * kv_w:attn_w + 2 * kv_w + gm_w]
    gv = proj[:, attn_w + 2 * kv_w + gm_w:]
    attn, k_new, v_new = _attn_sample(q, k, v, k_buf.reshape(nb, WINDOW, kv_w), v_buf.reshape(nb, WINDOW, kv_w),
                                      lw["qw2"], lw["kw2"], lw["sinks_row"], group=8)
    sgu, sgu_state = _sgu_sample(gu, gv, lw["sgu_norm_w"], lw["sgu_w0_row"], lw["sgu_b0_row"])
    x1, _ = _merge_out(attn, sgu, gates, x2d, lw["w_br_attn"], lw["w_br_gm"], lw["w_out"],
                       lw["ffn_norm_w"], tm=nb)
    h = _norm_mm(x1, lw["ffn_norm_w"], lw["w_up"], None, act=None, out_dtype=F32, tm=nb, tn=tn,
                 name="ffn_up_sample")
    x2 = _ffn_down_sample(h, conv_buf.reshape(nb, -1), lw["conv_w"], lw["conv_b"], lw["w_down"], x1, tf=tf)
    x3 = _ple(x2, None, lw["ple_norm_w"], lw["w_ple_gate"], p.reshape(nb * t, -1), lw["w_ple_proj"],
              tm=nb, tn=tn, name="ple_sample")
    conv_state = jnp.stack([conv_buf[:, 1, :], h], axis=1)
    return (x3.reshape(nb, t, d), k_new.reshape(nb, WINDOW, N_KV_HEADS, HEAD_DIM),
            v_new.reshape(nb, WINDOW, N_KV_HEADS, HEAD_DIM), sgu_state.reshape(nb, t, gm_w), conv_state)


def kernel(x_prompt, x_sample, p_prompt, p_sample, state_attn_k, state_attn_v, state_conv, attn_norm_w, w_in, q_norm_w, k_norm_w, attn_sinks, sgu_norm_w, sgu_w, sgu_b, w_br_attn, w_br_gm, w_gate, b_gate, w_out, ffn_norm_w, w_up, conv_w, conv_b, w_down, ple_norm_w, w_ple_gate, w_ple_proj):
    depth = w_in.shape[0]
    xp, xs = x_prompt, x_sample
    outs = [[] for _ in range(8)]
    for i in range(depth):
        lw = _layer_weights(i, attn_norm_w, w_in, q_norm_w, k_norm_w, attn_sinks, sgu_norm_w, sgu_w, sgu_b,
                            w_br_attn, w_br_gm, w_gate, b_gate, w_out, ffn_norm_w, w_up, conv_w, conv_b,
                            w_down, ple_norm_w, w_ple_gate, w_ple_proj)
        xp, kp, vp, gp, cp = _prompt_layer(xp, p_prompt[i], lw)
        xs, ks, vs, gs, cs = _sample_layer(xs, p_sample[i], state_attn_k[i], state_attn_v[i], state_conv[i], lw)
        for lst, val in zip(outs, (kp, vp, ks, vs, gp, gs, cp, cs)):
            lst.append(val)
    return (xp, xs) + tuple(jnp.stack(lst) for lst in outs)
```

```python
import functools

import jax
import jax.numpy as jnp
from jax import lax
from jax.experimental import pallas as pl
from jax.experimental.pallas import tpu as pltpu

F32 = jnp.float32
BF16 = jnp.bfloat16

HEAD_DIM = 64
N_HEADS = 16
N_KV_HEADS = 4
GQA_GROUP = N_HEADS // N_KV_HEADS
WINDOW = 128
CHUNK = 128
GM_GROUPS = 4
EPS = 1e-6
MASK_VALUE = -1e30
LANES = 128
SUBLANES = 8
HEADS_PER_VREG = LANES // HEAD_DIM
MIB = 1 << 20


def _alibi_slope(h):
    return float(2.0 ** (-8.0 * (h + 1) / N_HEADS))


def _params(semantics, vmem_mib):
    return pltpu.CompilerParams(dimension_semantics=semantics, vmem_limit_bytes=vmem_mib * MIB)


def _rmsnorm_rows(x, w):
    ms = jnp.mean(x * x, axis=-1, keepdims=True)
    return x * lax.rsqrt(ms + EPS) * w


NORM_ROWS = 256


def _rmsnorm_to(x_ref, nw_ref, xn_ref, add_ref=None):
    tm = x_ref.shape[0]
    rows = min(NORM_ROWS, tm)

    def body(r, carry):
        sl = pl.ds(pl.multiple_of(r * rows, rows), rows)
        x = x_ref[sl, :]
        if add_ref is not None:
            x = x + add_ref[sl, :].astype(F32)
        xn_ref[sl, :] = _rmsnorm_rows(x, nw_ref[...]).astype(BF16)
        return carry

    lax.fori_loop(0, tm // rows, body, 0)


def _norm_mm_kernel(*refs, has_bias, act):
    if has_bias:
        x_ref, nw_ref, w_ref, b_ref, o_ref, wb_ref, xn_ref = refs
    else:
        x_ref, nw_ref, w_ref, o_ref, wb_ref, xn_ref = refs

    @pl.when(pl.program_id(0) == 0)
    def _():
        _rmsnorm_to(x_ref, nw_ref, xn_ref)

    wb = w_ref[...].astype(BF16)
    wb_ref[...] = wb
    acc = jnp.dot(xn_ref[...], wb, preferred_element_type=F32)
    if has_bias:
        acc = acc + b_ref[...]
    if act == "sigmoid":
        acc = jax.nn.sigmoid(acc)
    o_ref[...] = acc.astype(o_ref.dtype)


def _norm_mm(x, nw, w, bias, *, act, tn, name):
    m, k = x.shape
    n = w.shape[1]
    in_specs = [
        pl.BlockSpec((m, k), lambda j: (0, 0)),
        pl.BlockSpec((1, k), lambda j: (0, 0)),
        pl.BlockSpec((k, tn), lambda j: (0, j)),
    ]
    args = [x, nw.reshape(1, k), w]
    if bias is not None:
        in_specs.append(pl.BlockSpec((1, tn), lambda j: (0, j)))
        args.append(bias.reshape(1, n))
    return pl.pallas_call(
        functools.partial(_norm_mm_kernel, has_bias=bias is not None, act=act),
        grid=(n // tn,),
        in_specs=in_specs,
        out_specs=[pl.BlockSpec((m, tn), lambda j: (0, j)), pl.BlockSpec((k, tn), lambda j: (0, j))],
        out_shape=[jax.ShapeDtypeStruct((m, n), F32), jax.ShapeDtypeStruct((k, n), BF16)],
        scratch_shapes=[pltpu.VMEM((m, k), BF16)],
        compiler_params=_params(("arbitrary",), 56),
        name=name,
    )(*args)


def _in_proj_kernel(x_ref, nw_ref, win_ref, wg_ref, bg_ref, proj_ref, gates_ref, xn_ref, *, n_in):
    j = pl.program_id(1)

    @pl.when(j == 0)
    def _():
        _rmsnorm_to(x_ref, nw_ref, xn_ref)

    @pl.when(j < n_in)
    def _():
        acc = jnp.dot(xn_ref[...], win_ref[...].astype(BF16), preferred_element_type=F32)
        proj_ref[...] = acc.astype(proj_ref.dtype)

    @pl.when(j >= n_in)
    def _():
        acc = jnp.dot(xn_ref[...], wg_ref[...].astype(BF16), preferred_element_type=F32)
        gates_ref[...] = jax.nn.sigmoid(acc + bg_ref[...]).astype(gates_ref.dtype)


def _in_proj(x, nw, w_in, w_gate, b_gate, *, tm, tn):
    m, k = x.shape
    n_in = w_in.shape[1] // tn
    n_g = w_gate.shape[1] // tn
    in_col = lambda i, j: (0, jnp.minimum(j, n_in - 1))
    g_col = lambda i, j: (0, jnp.maximum(j - n_in, 0))
    return pl.pallas_call(
        functools.partial(_in_proj_kernel, n_in=n_in),
        grid=(m // tm, n_in + n_g),
        in_specs=[
            pl.BlockSpec((tm, k), lambda i, j: (i, 0)),
            pl.BlockSpec((1, k), lambda i, j: (0, 0)),
            pl.BlockSpec((k, tn), in_col),
            pl.BlockSpec((k, tn), g_col),
            pl.BlockSpec((1, tn), g_col),
        ],
        out_specs=[
            pl.BlockSpec((tm, tn), lambda i, j: (i, jnp.minimum(j, n_in - 1))),
            pl.BlockSpec((tm, tn), lambda i, j: (i, jnp.maximum(j - n_in, 0))),
        ],
        out_shape=[
            jax.ShapeDtypeStruct((m, w_in.shape[1]), BF16),
            jax.ShapeDtypeStruct((m, w_gate.shape[1]), BF16),
        ],
        scratch_shapes=[pltpu.VMEM((tm, k), BF16)],
        compiler_params=_params(("arbitrary", "arbitrary"), 56),
        name="in_proj",
    )(x, nw.reshape(1, k), w_in, w_gate, b_gate.reshape(1, -1))


def _head_group_matrix():
    r = lax.broadcasted_iota(jnp.int32, (LANES, LANES), 0)
    c = lax.broadcasted_iota(jnp.int32, (LANES, LANES), 1)
    return ((r >> 6) == (c >> 6)).astype(BF16)


def _head_rmsnorm(xcol, w, gmat):
    ss = jnp.dot((xcol * xcol).astype(BF16), gmat, preferred_element_type=F32)
    return xcol * lax.rsqrt(ss * (1.0 / HEAD_DIM) + EPS) * w


def _attn_prompt_kernel(sink_ref, q_ref, kc_ref, kp_ref, vc_ref, vp_ref, qw_ref, kw_ref,
                        o_ref, kn_ref, vn_ref):
    c = pl.program_id(1)
    blk = WINDOW
    gmat = _head_group_matrix()
    lane = lax.broadcasted_iota(jnp.int32, (1, LANES), 1)
    low = lane < HEAD_DIM

    kw = kw_ref[...]
    qw = qw_ref[...]
    kc = kc_ref[...].astype(F32)
    kp = kp_ref[...].astype(F32)
    vc = vc_ref[...].astype(F32)
    vp = vp_ref[...].astype(F32)
    ncol = kc.shape[1] // LANES
    kcn = [_head_rmsnorm(kc[:, p * LANES:(p + 1) * LANES], kw, gmat) for p in range(ncol)]
    kpn = [_head_rmsnorm(kp[:, p * LANES:(p + 1) * LANES], kw, gmat) for p in range(ncol)]
    kn_ref[0] = jnp.concatenate(kcn, axis=1)
    vn_ref[0] = vc

    kcat = [jnp.concatenate([kpn[p], kcn[p]], axis=0) for p in range(ncol)]
    vcat = [jnp.concatenate([vp[:, p * LANES:(p + 1) * LANES], vc[:, p * LANES:(p + 1) * LANES]], axis=0)
            for p in range(ncol)]
    krol = [pltpu.roll(kcat[p], HEAD_DIM, 1) for p in range(ncol)]
    vrol = [pltpu.roll(vcat[p], HEAD_DIM, 1) for p in range(ncol)]

    qi = lax.broadcasted_iota(jnp.int32, (blk, 2 * blk), 0)
    kj = lax.broadcasted_iota(jnp.int32, (blk, 2 * blk), 1)
    dist = blk + qi - kj
    valid = (dist >= 0) & (dist <= WINDOW) & ((kj >= blk) | (c > 0))
    distf = dist.astype(F32)

    q = q_ref[...].astype(F32)
    qn = [_head_rmsnorm(q[:, p * LANES:(p + 1) * LANES], qw, gmat) * (HEAD_DIM ** -0.5)
          for p in range(N_HEADS // HEADS_PER_VREG)]

    for g in range(N_KV_HEADS):
        p, half = divmod(g, HEADS_PER_VREG)
        if half == 0:
            kd = jnp.where(low, kcat[p], krol[p])
            vd = jnp.where(low, vcat[p], vrol[p])
        else:
            kd = jnp.where(low, krol[p], kcat[p])
            vd = jnp.where(low, vrol[p], vcat[p])
        kd = kd.astype(BF16)
        vd = vd.astype(BF16)
        qs = []
        for hl in range(GQA_GROUP):
            h = g * GQA_GROUP + hl
            pc, hh = divmod(h, HEADS_PER_VREG)
            keep = low if hh == 0 else jnp.logical_not(low)
            qs.append(jnp.where(keep, qn[pc], 0.0).astype(BF16))
        qstack = jnp.concatenate(qs, axis=0)
        s_all = lax.dot_general(qstack, kd, (((1,), (1,)), ((), ())), preferred_element_type=F32)
        ps = []
        for hl in range(GQA_GROUP):
            h = g * GQA_GROUP + hl
            s = s_all[hl * blk:(hl + 1) * blk] - _alibi_slope(h) * distf
            s = jnp.where(valid, s, MASK_VALUE)
            sink = sink_ref[h]
            mx = jnp.maximum(jnp.max(s, axis=-1, keepdims=True), sink)
            e = jnp.exp(s - mx)
            den = jnp.sum(e, axis=-1, keepdims=True) + jnp.exp(sink - mx)
            ps.append((e / den).astype(BF16))
        pstack = jnp.concatenate(ps, axis=0)
        o_all = jnp.dot(pstack, vd, preferred_element_type=F32)
        for pair in range(GQA_GROUP // HEADS_PER_VREG):
            pc = g * (GQA_GROUP // HEADS_PER_VREG) + pair
            even = o_all[(2 * pair) * blk:(2 * pair + 1) * blk]
            odd = o_all[(2 * pair + 1) * blk:(2 * pair + 2) * blk]
            o_ref[:, pc * LANES:(pc + 1) * LANES] = jnp.where(low, even, odd).astype(o_ref.dtype)


def _attn_prompt(proj, qw2, kw2, sinks, *, batch, seq):
    nb = seq // WINDOW
    attn_w = N_HEADS * HEAD_DIM
    kv_w = N_KV_HEADS * HEAD_DIM
    kcol = attn_w // kv_w
    vcol = kcol + 1

    def cur(col):
        return lambda b, c: (b * nb + c, col)

    def prev(col):
        return lambda b, c: (jnp.maximum(b * nb + c - 1, 0), col)

    return pl.pallas_call(
        _attn_prompt_kernel,
        grid=(batch, nb),
        in_specs=[
            pl.BlockSpec(memory_space=pltpu.SMEM),
            pl.BlockSpec((WINDOW, attn_w), cur(0)),
            pl.BlockSpec((WINDOW, kv_w), cur(kcol)),
            pl.BlockSpec((WINDOW, kv_w), prev(kcol)),
            pl.BlockSpec((WINDOW, kv_w), cur(vcol)),
            pl.BlockSpec((WINDOW, kv_w), prev(vcol)),
            pl.BlockSpec((1, LANES), lambda b, c: (0, 0)),
            pl.BlockSpec((1, LANES), lambda b, c: (0, 0)),
        ],
        out_specs=[
            pl.BlockSpec((WINDOW, attn_w), lambda b, c: (b * nb + c, 0)),
            pl.BlockSpec((1, WINDOW, kv_w), lambda b, c: (b, 0, 0)),
            pl.BlockSpec((1, WINDOW, kv_w), lambda b, c: (b, 0, 0)),
        ],
        out_shape=[
            jax.ShapeDtypeStruct((batch * seq, attn_w), BF16),
            jax.ShapeDtypeStruct((batch, WINDOW, kv_w), F32),
            jax.ShapeDtypeStruct((batch, WINDOW, kv_w), F32),
        ],
        compiler_params=_params(("parallel", "arbitrary"), 32),
        name="attn_prompt",
    )(sinks, proj, proj, proj, proj, proj, qw2, kw2)


def _layernorm_rows(x, w):
    mu = jnp.mean(x, axis=-1, keepdims=True)
    xc = x - mu
    return xc * lax.rsqrt(jnp.mean(xc * xc, axis=-1, keepdims=True) + EPS) * w


def _sgu_prompt_kernel(gu0_ref, gu1_ref, gv0_ref, gv1_ref, nw_ref, ws_ref, bs_ref, o_ref, st_ref):
    gv = jnp.concatenate([gv0_ref[...], gv1_ref[...]], axis=1).astype(F32)
    vn = _layernorm_rows(jax.nn.gelu(gv), nw_ref[...])
    st_ref[0] = vn
    vb = vn.astype(BF16)
    r = lax.broadcasted_iota(jnp.int32, (CHUNK, CHUNK), 0)
    c = lax.broadcasted_iota(jnp.int32, (CHUNK, CHUNK), 1)
    causal = r >= c
    gw = vb.shape[1] // GM_GROUPS
    half = gu0_ref.shape[1]
    for g in range(GM_GROUPS):
        w = jnp.where(causal, ws_ref[g], 0.0).astype(BF16)
        mix = jnp.dot(w, vb[:, g * gw:(g + 1) * gw], preferred_element_type=F32) + bs_ref[:, g:g + 1]
        src = gu0_ref if g * gw < half else gu1_ref
        off = g * gw - (0 if g * gw < half else half)
        u = jax.nn.gelu(src[:, off:off + gw].astype(F32))
        o_ref[:, g * gw:(g + 1) * gw] = (u * mix).astype(o_ref.dtype)


def _sgu_prompt(proj, nw, ws, bs_t, *, batch, seq, gu_off, gm_w):
    nc = seq // CHUNK
    half = gm_w // 2
    b0 = gu_off // half

    def col(k):
        return lambda b, c: (b * nc + c, b0 + k)

    return pl.pallas_call(
        _sgu_prompt_kernel,
        grid=(batch, nc),
        in_specs=[
            pl.BlockSpec((CHUNK, half), col(0)),
            pl.BlockSpec((CHUNK, half), col(1)),
            pl.BlockSpec((CHUNK, half), col(2)),
            pl.BlockSpec((CHUNK, half), col(3)),
            pl.BlockSpec((1, gm_w), lambda b, c: (0, 0)),
            pl.BlockSpec((GM_GROUPS, CHUNK, CHUNK), lambda b, c: (0, 0, 0)),
            pl.BlockSpec((CHUNK, GM_GROUPS), lambda b, c: (0, 0)),
        ],
        out_specs=[
            pl.BlockSpec((CHUNK, gm_w), lambda b, c: (b * nc + c, 0)),
            pl.BlockSpec((1, CHUNK, gm_w), lambda b, c: (b, 0, 0)),
        ],
        out_shape=[
            jax.ShapeDtypeStruct((batch * seq, gm_w), BF16),
            jax.ShapeDtypeStruct((batch, CHUNK, gm_w), F32),
        ],
        compiler_params=_params(("parallel", "arbitrary"), 32),
        name="sgu_prompt",
    )(proj, proj, proj, proj, nw.reshape(1, gm_w), ws, bs_t)


def _split_dot(x, m):
    hi = x.astype(BF16)
    lo = (x - hi.astype(F32)).astype(BF16)
    return jnp.dot(hi, m, preferred_element_type=F32) + jnp.dot(lo, m, preferred_element_type=F32)


def _attn_sample_kernel(q_ref, kn_ref, vn_ref, kb_ref, vb_ref, qw_ref, kw_ref, sink_ref,
                        o_ref, ko_ref, vo_ref):
    nsamp = q_ref.shape[0]
    attn_w = q_ref.shape[1]
    kv_w = kn_ref.shape[1]
    gmat = _head_group_matrix()
    q = q_ref[...].astype(F32)
    kn = kn_ref[...].astype(F32)
    qn = jnp.concatenate(
        [_head_rmsnorm(q[:, p * LANES:(p + 1) * LANES], qw_ref[...], gmat) * (HEAD_DIM ** -0.5)
         for p in range(attn_w // LANES)], axis=1)
    knn = jnp.concatenate(
        [_head_rmsnorm(kn[:, p * LANES:(p + 1) * LANES], kw_ref[...], gmat)
         for p in range(kv_w // LANES)], axis=1)
    vnn = vn_ref[...].astype(F32)

    ec = lax.broadcasted_iota(jnp.int32, (kv_w, attn_w), 0)
    el = lax.broadcasted_iota(jnp.int32, (kv_w, attn_w), 1)
    expand = (((ec >> 6) == (el >> 8)) & ((ec & 63) == (el & 63))).astype(BF16)
    gl = lax.broadcasted_iota(jnp.int32, (attn_w, LANES), 0)
    gh = lax.broadcasted_iota(jnp.int32, (attn_w, LANES), 1)
    hsum = ((gl >> 6) == gh).astype(BF16)
    tl = lax.broadcasted_iota(jnp.int32, (LANES, attn_w), 1)
    th = lax.broadcasted_iota(jnp.int32, (LANES, attn_w), 0)
    hexp = ((tl >> 6) == th).astype(BF16)

    nkeys = WINDOW + SUBLANES
    row = lax.broadcasted_iota(jnp.int32, (nkeys, LANES), 0)
    head = lax.broadcasted_iota(jnp.int32, (nkeys, LANES), 1)
    slope = jnp.exp2(-8.0 * (head + 1).astype(F32) / N_HEADS)
    dist = (WINDOW - row).astype(F32)
    key_ok = row <= WINDOW
    srow = lax.broadcasted_iota(jnp.int32, (WINDOW, kv_w), 0)
    sinks = sink_ref[...]

    for s in range(nsamp):
        kb = kb_ref[s]
        vb = vb_ref[s]
        knew = knn[s:s + 1]
        vnew = vnn[s:s + 1]
        ko_ref[s] = jnp.where(srow == WINDOW - 1, knew, pltpu.roll(kb, WINDOW - 1, 0))
        vo_ref[s] = jnp.where(srow == WINDOW - 1, vnew, pltpu.roll(vb, WINDOW - 1, 0))
        kk = jnp.concatenate([kb, jnp.broadcast_to(knew, (SUBLANES, kv_w))], axis=0)
        vv = jnp.concatenate([vb, jnp.broadcast_to(vnew, (SUBLANES, kv_w))], axis=0)
        kexp = _split_dot(kk, expand)
        vexp = _split_dot(vv, expand)
        sc = _split_dot(kexp * qn[s:s + 1], hsum)
        sc = sc - slope * dist
        sc = jnp.where(key_ok, sc, MASK_VALUE)
        mx = jnp.maximum(jnp.max(sc, axis=0, keepdims=True), sinks)
        e = jnp.exp(sc - mx)
        den = jnp.sum(e, axis=0, keepdims=True) + jnp.exp(sinks - mx)
        pexp = _split_dot(e / den, hexp)
        o_ref[s:s + 1, :] = jnp.sum(pexp * vexp, axis=0, keepdims=True)


def _attn_sample(q, knew, vnew, kbuf, vbuf, qw2, kw2, sinks_row, *, group):
    nb, attn_w = q.shape
    kv_w = knew.shape[1]
    row = lambda width: pl.BlockSpec((group, width), lambda i: (i, 0))
    buf = pl.BlockSpec((group, WINDOW, kv_w), lambda i: (i, 0, 0))
    vec = pl.BlockSpec((1, LANES), lambda i: (0, 0))
    return pl.pallas_call(
        _attn_sample_kernel,
        grid=(nb // group,),
        in_specs=[row(attn_w), row(kv_w), row(kv_w), buf, buf, vec, vec, vec],
        out_specs=[row(attn_w), buf, buf],
        out_shape=[
            jax.ShapeDtypeStruct((nb, attn_w), F32),
            jax.ShapeDtypeStruct((nb, WINDOW, kv_w), F32),
            jax.ShapeDtypeStruct((nb, WINDOW, kv_w), F32),
        ],
        compiler_params=_params(("parallel",), 32),
        name="attn_sample",
    )(q, knew, vnew, kbuf, vbuf, qw2, kw2, sinks_row)


def _sgu_sample_kernel(gu_ref, gv_ref, nw_ref, w0_ref, b0_ref, o_ref, st_ref):
    vn = _layernorm_rows(jax.nn.gelu(gv_ref[...].astype(F32)), nw_ref[...])
    st_ref[...] = vn
    mix = w0_ref[...] * vn + b0_ref[...]
    o_ref[...] = jax.nn.gelu(gu_ref[...].astype(F32)) * mix


def _sgu_sample(gu, gv, nw, w0_row, b0_row):
    nb, gm_w = gu.shape
    full = pl.BlockSpec((nb, gm_w), lambda i: (0, 0))
    vec = pl.BlockSpec((1, gm_w), lambda i: (0, 0))
    return pl.pallas_call(
        _sgu_sample_kernel,
        grid=(1,),
        in_specs=[full, full, vec, vec, vec],
        out_specs=[full, full],
        out_shape=[jax.ShapeDtypeStruct((nb, gm_w), F32), jax.ShapeDtypeStruct((nb, gm_w), F32)],
        name="sgu_sample",
    )(gu, gv, nw.reshape(1, gm_w), w0_row, b0_row)


def _merge_out_kernel(a_ref, s_ref, ga_ref, gb_ref, x_ref, wa_ref, wg_ref, wo_ref, nw_ref,
                      x1_ref, xn1_ref):
    a = jnp.dot(a_ref[...].astype(BF16), wa_ref[...], preferred_element_type=F32)
    m = jnp.dot(s_ref[...].astype(BF16), wg_ref[...], preferred_element_type=F32)
    merged = (gb_ref[...].astype(F32) * a + ga_ref[...].astype(F32) * m).astype(BF16)
    x1 = x_ref[...] + jnp.dot(merged, wo_ref[...], preferred_element_type=F32)
    x1_ref[...] = x1
    xn1_ref[...] = _rmsnorm_rows(x1, nw_ref[...]).astype(BF16)


def _merge_out(attn, sgu, gates, x, wa, wg, wo, nw, *, tm):
    m, ka = attn.shape
    kg = sgu.shape[1]
    d = wo.shape[1]
    rows = lambda width, col=0: pl.BlockSpec((tm, width), lambda i: (i, col))
    resident = lambda shape: pl.BlockSpec(shape, lambda i: (0, 0), pipeline_mode=pl.Buffered(1))
    return pl.pallas_call(
        _merge_out_kernel,
        grid=(m // tm,),
        in_specs=[
            rows(ka), rows(kg), rows(d, 0), rows(d, 1), rows(d),
            resident((ka, d)), resident((kg, d)), resident((d, d)),
            pl.BlockSpec((1, d), lambda i: (0, 0)),
        ],
        out_specs=[rows(d), rows(d)],
        out_shape=[jax.ShapeDtypeStruct((m, d), F32), jax.ShapeDtypeStruct((m, d), BF16)],
        compiler_params=_params(("parallel",), 48),
        name="merge_out",
    )(attn, sgu, gates, gates, x, wa, wg, wo, nw.reshape(1, d))


PAD = SUBLANES


def _ffn_step(t, xn_ref, wug_ref, wuu_ref, wd_ref, cwg_ref, cwu_ref, cbg_ref, cbu_ref,
              csg_ref, csu_ref, h_new, h_old, carry_ref, acc_ref, *, nf, tiles_per_seq):
    tm = xn_ref.shape[0]
    tf = wug_ref.shape[1]
    xn = xn_ref[...]
    h_new[PAD:PAD + tm, 0:tf] = jnp.dot(xn, wug_ref[...].astype(BF16), preferred_element_type=F32)
    h_new[PAD:PAD + tm, tf:2 * tf] = jnp.dot(xn, wuu_ref[...].astype(BF16), preferred_element_type=F32)

    tp = jnp.maximum(t - 1, 0)
    ip = tp // nf
    jp = tp % nf
    first = (ip % tiles_per_seq) == 0
    h_old[0:PAD, :] = jnp.where(first, 0.0, carry_ref[jp])
    carry_ref[jp] = h_old[tm:tm + PAD, :]
    cw = jnp.concatenate([cwg_ref[...], cwu_ref[...]], axis=1)
    cb = jnp.concatenate([cbg_ref[...], cbu_ref[...]], axis=1)
    hc = (cw[0:1] * h_old[PAD - 2:PAD - 2 + tm, :] + cw[1:2] * h_old[PAD - 1:PAD - 1 + tm, :]
          + cw[2:3] * h_old[PAD:PAD + tm, :] + cb)
    act = (jax.nn.silu(hc[:, 0:tf]) * hc[:, tf:2 * tf]).astype(BF16)
    contrib = jnp.dot(act, wd_ref[...].astype(BF16), preferred_element_type=F32)
    acc_ref[...] = jnp.where(jp == 0, 0.0, acc_ref[...]) + contrib
    csg_ref[0] = h_old[PAD + tm - 2:PAD + tm, 0:tf]
    csu_ref[0] = h_old[PAD + tm - 2:PAD + tm, tf:2 * tf]


def _ffn_kernel(xn_ref, wug_ref, wuu_ref, wd_ref, cwg_ref, cwu_ref, cbg_ref, cbu_ref,
                o_ref, csg_ref, csu_ref, ha_ref, hb_ref, carry_ref, acc_ref, *, nf, tiles_per_seq):
    t = pl.program_id(0)
    step = functools.partial(_ffn_step, t, xn_ref, wug_ref, wuu_ref, wd_ref, cwg_ref, cwu_ref,
                             cbg_ref, cbu_ref, csg_ref, csu_ref, carry_ref=carry_ref, acc_ref=acc_ref,
                             nf=nf, tiles_per_seq=tiles_per_seq)

    @pl.when(t == 0)
    def _():
        hb_ref[...] = jnp.zeros_like(hb_ref)
        carry_ref[...] = jnp.zeros_like(carry_ref)
        acc_ref[...] = jnp.zeros_like(acc_ref)

    @pl.when(t % 2 == 0)
    def _():
        step(h_new=ha_ref, h_old=hb_ref)

    @pl.when(t % 2 == 1)
    def _():
        step(h_new=hb_ref, h_old=ha_ref)

    @pl.when(jnp.maximum(t - 1, 0) % nf == nf - 1)
    def _():
        o_ref[...] = acc_ref[...].astype(o_ref.dtype)


def _ffn_prompt(xn, w_up, w_down, cw, cb, *, batch, seq, tm, tf):
    m, k = xn.shape
    d_ff, d = w_down.shape
    nf = d_ff // tf
    nm = m // tm
    tiles_per_seq = seq // tm
    prev = lambda t: jnp.maximum(t - 1, 0)
    up_col = lambda off: (lambda t: (0, off + t % nf))
    dn_col = lambda off: (lambda t: (0, off + prev(t) % nf))
    state = lambda t: (prev(t) // nf, 0, prev(t) % nf)
    delta, csg, csu = pl.pallas_call(
        functools.partial(_ffn_kernel, nf=nf, tiles_per_seq=tiles_per_seq),
        grid=(nm * nf + 1,),
        in_specs=[
            pl.BlockSpec((tm, k), lambda t: (jnp.minimum(t // nf, nm - 1), 0)),
            pl.BlockSpec((k, tf), up_col(0)),
            pl.BlockSpec((k, tf), up_col(nf)),
            pl.BlockSpec((tf, d), lambda t: (prev(t) % nf, 0)),
            pl.BlockSpec((3, tf), dn_col(0)),
            pl.BlockSpec((3, tf), dn_col(nf)),
            pl.BlockSpec((1, tf), dn_col(0)),
            pl.BlockSpec((1, tf), dn_col(nf)),
        ],
        out_specs=[
            pl.BlockSpec((tm, d), lambda t: (prev(t) // nf, 0)),
            pl.BlockSpec((1, 2, tf), state),
            pl.BlockSpec((1, 2, tf), state),
        ],
        out_shape=[
            jax.ShapeDtypeStruct((m, d), BF16),
            jax.ShapeDtypeStruct((nm, 2, d_ff), F32),
            jax.ShapeDtypeStruct((nm, 2, d_ff), F32),
        ],
        scratch_shapes=[
            pltpu.VMEM((tm + PAD, 2 * tf), F32),
            pltpu.VMEM((tm + PAD, 2 * tf), F32),
            pltpu.VMEM((nf, PAD, 2 * tf), F32),
            pltpu.VMEM((tm, d), F32),
        ],
        compiler_params=_params(("arbitrary",), 58),
        name="ffn_prompt",
    )(xn, w_up, w_up, w_down, cw, cw, cb.reshape(1, -1), cb.reshape(1, -1))
    tails = jnp.concatenate([csg, csu], axis=-1).reshape(batch, tiles_per_seq, 2, 2 * d_ff)
    return delta, tails[:, -1]


def _ffn_down_sample_kernel(hg_ref, hu_ref, b0g_ref, b0u_ref, b1g_ref, b1u_ref, cwg_ref, cwu_ref,
                            cbg_ref, cbu_ref, wd_ref, x_ref, o_ref, wdb_ref):
    @pl.when(pl.program_id(0) == 0)
    def _():
        o_ref[...] = x_ref[...]

    def conv(h_ref, b0_ref, b1_ref, cw_ref, cb_ref):
        cw = cw_ref[...]
        return cw[0:1] * b0_ref[...] + cw[1:2] * b1_ref[...] + cw[2:3] * h_ref[...] + cb_ref[...]

    hcg = conv(hg_ref, b0g_ref, b1g_ref, cwg_ref, cbg_ref)
    hcu = conv(hu_ref, b0u_ref, b1u_ref, cwu_ref, cbu_ref)
    act = (jax.nn.silu(hcg) * hcu).astype(BF16)
    wdb = wd_ref[...].astype(BF16)
    wdb_ref[...] = wdb
    o_ref[...] += jnp.dot(act, wdb, preferred_element_type=F32)


def _ffn_down_sample(h, state2, cw, cb, wd, x, *, tf):
    nb, two_f = h.shape
    d_ff = two_f // 2
    nf = d_ff // tf
    d = wd.shape[1]
    colblk = lambda off: pl.BlockSpec((nb, tf), lambda j: (0, j + off))
    return pl.pallas_call(
        _ffn_down_sample_kernel,
        grid=(nf,),
        in_specs=[
            colblk(0), colblk(nf),
            colblk(0), colblk(nf), colblk(2 * nf), colblk(3 * nf),
            pl.BlockSpec((3, tf), lambda j: (0, j)),
            pl.BlockSpec((3, tf), lambda j: (0, j + nf)),
            pl.BlockSpec((1, tf), lambda j: (0, j)),
            pl.BlockSpec((1, tf), lambda j: (0, j + nf)),
            pl.BlockSpec((tf, d), lambda j: (j, 0)),
            pl.BlockSpec((nb, d), lambda j: (0, 0)),
        ],
        out_specs=[pl.BlockSpec((nb, d), lambda j: (0, 0)), pl.BlockSpec((tf, d), lambda j: (j, 0))],
        out_shape=[jax.ShapeDtypeStruct((nb, d), F32), jax.ShapeDtypeStruct((d_ff, d), BF16)],
        compiler_params=_params(("arbitrary",), 32),
        name="ffn_down_sample",
    )(h, h, state2, state2, state2, state2, cw, cw, cb.reshape(1, two_f), cb.reshape(1, two_f), wd, x)


def _ple_kernel(*refs, has_delta, emit_w):
    refs = list(refs)
    x_ref = refs.pop(0)
    d_ref = refs.pop(0) if has_delta else None
    nw_ref, w_ref, p_ref, wp_ref, o_ref = refs[:5]
    xn_ref = refs[-1]
    j = pl.program_id(1)
    tn = o_ref.shape[1]

    @pl.when(j == 0)
    def _():
        _rmsnorm_to(x_ref, nw_ref, xn_ref, add_ref=d_ref)

    wb = w_ref[...].astype(BF16)
    wpb = wp_ref[...].astype(BF16)
    if emit_w:
        wb_ref, wpb_ref = refs[5:7]
        wb_ref[...] = wb
        wpb_ref[...] = wpb
    gate = jax.nn.sigmoid(jnp.dot(xn_ref[...], wb, preferred_element_type=F32))
    emb = jnp.dot(p_ref[...].astype(BF16), wpb, preferred_element_type=F32)
    o_ref[...] = gate * emb
    for jj in range(x_ref.shape[1] // tn):
        @pl.when(j == jj)
        def _():
            res = x_ref[:, jj * tn:(jj + 1) * tn]
            if has_delta:
                res = res + d_ref[:, jj * tn:(jj + 1) * tn].astype(F32)
            o_ref[...] += res


def _ple(x, delta, nw, w, p, wp, *, tm, tn, name, emit_w=False):
    m, k = x.shape
    n = w.shape[1]
    kp = p.shape[1]
    assert not emit_w or m == tm
    out_specs = [pl.BlockSpec((tm, tn), lambda i, j: (i, j))]
    out_shape = [jax.ShapeDtypeStruct((m, n), F32)]
    if emit_w:
        out_specs += [pl.BlockSpec((k, tn), lambda i, j: (0, j)), pl.BlockSpec((kp, tn), lambda i, j: (0, j))]
        out_shape += [jax.ShapeDtypeStruct((k, n), BF16), jax.ShapeDtypeStruct((kp, n), BF16)]
    row_full = pl.BlockSpec((tm, k), lambda i, j: (i, 0))
    in_specs = [row_full] + ([row_full] if delta is not None else []) + [
        pl.BlockSpec((1, k), lambda i, j: (0, 0)),
        pl.BlockSpec((k, tn), lambda i, j: (0, j)),
        pl.BlockSpec((tm, kp), lambda i, j: (i, 0)),
        pl.BlockSpec((kp, tn), lambda i, j: (0, j)),
    ]
    args = [x] + ([delta] if delta is not None else []) + [nw.reshape(1, k), w, p, wp]
    outs = pl.pallas_call(
        functools.partial(_ple_kernel, has_delta=delta is not None, emit_w=emit_w),
        grid=(m // tm, n // tn),
        in_specs=in_specs,
        out_specs=out_specs,
        out_shape=out_shape,
        scratch_shapes=[pltpu.VMEM((tm, k), BF16)],
        compiler_params=_params(("parallel", "arbitrary"), 56),
        name=name,
    )(*args)
    return outs if emit_w else outs[0]


def _layer_weights(i, attn_norm_w, w_in, q_norm_w, k_norm_w, attn_sinks, sgu_norm_w, sgu_w, sgu_b,
                   w_br_attn, w_br_gm, w_gate, b_gate, w_out, ffn_norm_w, w_up, conv_w, conv_b,
                   w_down, ple_norm_w, w_ple_gate, w_ple_proj):
    sinks = attn_sinks[i]
    gm_group_w = w_br_gm.shape[1] // GM_GROUPS
    return dict(
        attn_norm_w=attn_norm_w[i], w_in=w_in[i],
        qw2=jnp.tile(q_norm_w[i], HEADS_PER_VREG).reshape(1, LANES),
        kw2=jnp.tile(k_norm_w[i], HEADS_PER_VREG).reshape(1, LANES),
        sinks=sinks,
        sinks_row=jnp.pad(sinks, (0, LANES - N_HEADS)).reshape(1, LANES),
        sgu_norm_w=sgu_norm_w[i], sgu_w=sgu_w[i], sgu_b_t=jnp.transpose(sgu_b[i]),
        sgu_w0_row=jnp.repeat(sgu_w[i][:, 0, 0], gm_group_w).reshape(1, -1),
        sgu_b0_row=jnp.repeat(sgu_b[i][:, 0], gm_group_w).reshape(1, -1),
        w_br_attn=w_br_attn[i].astype(BF16), w_br_gm=w_br_gm[i].astype(BF16),
        w_gate=w_gate[i], b_gate=b_gate[i], w_out=w_out[i].astype(BF16),
        ffn_norm_w=ffn_norm_w[i], w_up=w_up[i], conv_w=conv_w[i], conv_b=conv_b[i],
        w_down=w_down[i], ple_norm_w=ple_norm_w[i], w_ple_gate=w_ple_gate[i], w_ple_proj=w_ple_proj[i],
    )


def _prompt_layer(x, p, lw, wb):
    batch, seq, d = x.shape
    x2d = x.reshape(batch * seq, d)
    attn_w = N_HEADS * HEAD_DIM
    kv_w = N_KV_HEADS * HEAD_DIM
    gm_w = lw["w_br_gm"].shape[0]
    proj, gates = _in_proj(x2d, lw["attn_norm_w"], wb["w_in"], wb["w_gate"], lw["b_gate"], tm=1024, tn=512)
    attn, k_new, v_new = _attn_prompt(proj, lw["qw2"], lw["kw2"], lw["sinks"], batch=batch, seq=seq)
    sgu, sgu_state = _sgu_prompt(proj, lw["sgu_norm_w"], lw["sgu_w"], lw["sgu_b_t"], batch=batch, seq=seq,
                                 gu_off=attn_w + 2 * kv_w, gm_w=gm_w)
    x1, xn1 = _merge_out(attn, sgu, gates, x2d, lw["w_br_attn"], lw["w_br_gm"], lw["w_out"],
                         lw["ffn_norm_w"], tm=256)
    delta, conv_state = _ffn_prompt(xn1, wb["w_up"], wb["w_down"], lw["conv_w"], lw["conv_b"],
                                    batch=batch, seq=seq, tm=1024, tf=512)
    x3 = _ple(x1, delta, lw["ple_norm_w"], wb["w_ple_gate"], p.reshape(batch * seq, -1), wb["w_ple_proj"],
              tm=1024, tn=512, name="ple_prompt")
    return (x3.reshape(batch, seq, d), k_new.reshape(batch, WINDOW, N_KV_HEADS, HEAD_DIM),
            v_new.reshape(batch, WINDOW, N_KV_HEADS, HEAD_DIM), sgu_state, conv_state)


def _sample_layer(x, p, k_buf, v_buf, conv_buf, lw):
    nb, t, d = x.shape
    x2d = x.reshape(nb * t, d)
    attn_w = N_HEADS * HEAD_DIM
    kv_w = N_KV_HEADS * HEAD_DIM
    gm_w = lw["w_br_gm"].shape[0]
    wb = {}
    proj, wb["w_in"] = _norm_mm(x2d, lw["attn_norm_w"], lw["w_in"], None, act=None,
                                tn=lw["w_in"].shape[1] // 2, name="proj_sample")
    gates, wb["w_gate"] = _norm_mm(x2d, lw["attn_norm_w"], lw["w_gate"], lw["b_gate"], act="sigmoid",
                                   tn=1024, name="gates_sample")
    q = proj[:, :attn_w]
    k = proj[:, attn_w:attn_w + kv_w]
    v = proj[:, attn_w + kv_w:attn_w + 2 * kv_w]
    gu = proj[:, attn_w + 2 * kv_w:attn_w + 2 * kv_w + gm_w]
    gv = proj[:, attn_w + 2 * kv_w + gm_w:]
    attn, k_new, v_new = _attn_sample(q, k, v, k_buf.reshape(nb, WINDOW, kv_w), v_buf.reshape(nb, WINDOW, kv_w),
                                      lw["qw2"], lw["kw2"], lw["sinks_row"], group=8)
    sgu, sgu_state = _sgu_sample(gu, gv, lw["sgu_norm_w"], lw["sgu_w0_row"], lw["sgu_b0_row"])
    x1, _ = _merge_out(attn, sgu, gates, x2d, lw["w_br_attn"], lw["w_br_gm"], lw["w_out"],
                       lw["ffn_norm_w"], tm=nb)
    h, wb["w_up"] = _norm_mm(x1, lw["ffn_norm_w"], lw["w_up"], None, act=None, tn=1024, name="ffn_up_sample")
    x2, wb["w_down"] = _ffn_down_sample(h, conv_buf.reshape(nb, -1), lw["conv_w"], lw["conv_b"], lw["w_down"],
                                        x1, tf=512)
    x3, wb["w_ple_gate"], wb["w_ple_proj"] = _ple(
        x2, None, lw["ple_norm_w"], lw["w_ple_gate"], p.reshape(nb * t, -1), lw["w_ple_proj"],
        tm=nb, tn=1024, name="ple_sample", emit_w=True)
    conv_state = jnp.stack([conv_buf[:, 1, :], h], axis=1)
    return (x3.reshape(nb, t, d), k_new.reshape(nb, WINDOW, N_KV_HEADS, HEAD_DIM),
            v_new.reshape(nb, WINDOW, N_KV_HEADS, HEAD_DIM), sgu_state.reshape(nb, t, gm_w), conv_state), wb


def kernel(x_prompt, x_sample, p_prompt, p_sample, state_attn_k, state_attn_v, state_conv, attn_norm_w, w_in, q_norm_w, k_norm_w, attn_sinks, sgu_norm_w, sgu_w, sgu_b, w_br_attn, w_br_gm, w_gate, b_gate, w_out, ffn_norm_w, w_up, conv_w, conv_b, w_down, ple_norm_w, w_ple_gate, w_ple_proj):
    depth = w_in.shape[0]
    xp, xs = x_prompt, x_sample
    outs = [[] for _ in range(8)]
    for i in range(depth):
        lw = _layer_weights(i, attn_norm_w, w_in, q_norm_w, k_norm_w, attn_sinks, sgu_norm_w, sgu_w, sgu_b,
                            w_br_attn, w_br_gm, w_gate, b_gate, w_out, ffn_norm_w, w_up, conv_w, conv_b,
                            w_down, ple_norm_w, w_ple_gate, w_ple_proj)
        (xs, ks, vs, gs, cs), wb = _sample_layer(xs, p_sample[i], state_attn_k[i], state_attn_v[i],
                                                 state_conv[i], lw)
        xp, kp, vp, gp, cp = _prompt_layer(xp, p_prompt[i], lw, wb)
        for lst, val in zip(outs, (kp, vp, ks, vs, gp, gs, cp, cs)):
            lst.append(val)
    return (xp, xs) + tuple(jnp.stack(lst) for lst in outs)
```

```python
import functools

import jax
import jax.numpy as jnp
from jax import lax
from jax.experimental import pallas as pl
from jax.experimental.pallas import tpu as pltpu

F32 = jnp.float32
BF16 = jnp.bfloat16

HEAD_DIM = 64
N_HEADS = 16
N_KV_HEADS = 4
GQA_GROUP = N_HEADS // N_KV_HEADS
WINDOW = 128
CHUNK = 128
GM_GROUPS = 4
EPS = 1e-6
MASK_VALUE = -1e30
LANES = 128
SUBLANES = 8
HEADS_PER_VREG = LANES // HEAD_DIM
MIB = 1 << 20


def _alibi_slope(h):
    return float(2.0 ** (-8.0 * (h + 1) / N_HEADS))


def _params(semantics, vmem_mib, flags=None):
    return pltpu.CompilerParams(dimension_semantics=semantics, vmem_limit_bytes=vmem_mib * MIB, flags=flags)


def _rmsnorm_rows(x, w):
    ms = jnp.mean(x * x, axis=-1, keepdims=True)
    return x * lax.rsqrt(ms + EPS) * w


NORM_ROWS = 256


def _rmsnorm_to(x_ref, nw_ref, xn_ref, add_ref=None):
    tm = x_ref.shape[0]
    rows = min(NORM_ROWS, tm)

    def body(r, carry):
        sl = pl.ds(pl.multiple_of(r * rows, rows), rows)
        x = x_ref[sl, :]
        if add_ref is not None:
            x = x + add_ref[sl, :].astype(F32)
        xn_ref[sl, :] = _rmsnorm_rows(x, nw_ref[...]).astype(BF16)
        return carry

    lax.fori_loop(0, tm // rows, body, 0)


def _norm_mm_kernel(*refs, has_bias, act):
    if has_bias:
        x_ref, nw_ref, w_ref, b_ref, o_ref, wb_ref, xn_ref = refs
    else:
        x_ref, nw_ref, w_ref, o_ref, wb_ref, xn_ref = refs

    @pl.when(pl.program_id(0) == 0)
    def _():
        _rmsnorm_to(x_ref, nw_ref, xn_ref)

    wb = w_ref[...].astype(BF16)
    wb_ref[...] = wb
    acc = jnp.dot(xn_ref[...], wb, preferred_element_type=F32)
    if has_bias:
        acc = acc + b_ref[...]
    if act == "sigmoid":
        acc = jax.nn.sigmoid(acc)
    o_ref[...] = acc.astype(o_ref.dtype)


def _norm_mm(x, nw, w, bias, *, act, tn, name):
    m, k = x.shape
    n = w.shape[1]
    in_specs = [
        pl.BlockSpec((m, k), lambda j: (0, 0)),
        pl.BlockSpec((1, k), lambda j: (0, 0)),
        pl.BlockSpec((k, tn), lambda j: (0, j)),
    ]
    args = [x, nw.reshape(1, k), w]
    if bias is not None:
        in_specs.append(pl.BlockSpec((1, tn), lambda j: (0, j)))
        args.append(bias.reshape(1, n))
    return pl.pallas_call(
        functools.partial(_norm_mm_kernel, has_bias=bias is not None, act=act),
        grid=(n // tn,),
        in_specs=in_specs,
        out_specs=[pl.BlockSpec((m, tn), lambda j: (0, j)), pl.BlockSpec((k, tn), lambda j: (0, j))],
        out_shape=[jax.ShapeDtypeStruct((m, n), F32), jax.ShapeDtypeStruct((k, n), BF16)],
        scratch_shapes=[pltpu.VMEM((m, k), BF16)],
        compiler_params=_params(("arbitrary",), 56),
        name=name,
    )(*args)


def _in_proj_kernel(x_ref, nw_ref, win_ref, wg_ref, bg_ref, proj_ref, gates_ref, xn_ref, *, n_in):
    j = pl.program_id(1)

    @pl.when(j == 0)
    def _():
        _rmsnorm_to(x_ref, nw_ref, xn_ref)

    @pl.when(j < n_in)
    def _():
        acc = jnp.dot(xn_ref[...], win_ref[...].astype(BF16), preferred_element_type=F32)
        proj_ref[...] = acc.astype(proj_ref.dtype)

    @pl.when(j >= n_in)
    def _():
        acc = jnp.dot(xn_ref[...], wg_ref[...].astype(BF16), preferred_element_type=F32)
        gates_ref[...] = jax.nn.sigmoid(acc + bg_ref[...]).astype(gates_ref.dtype)


def _in_proj(x, nw, w_in, w_gate, b_gate, *, tm, tn):
    m, k = x.shape
    n_in = w_in.shape[1] // tn
    n_g = w_gate.shape[1] // tn
    in_col = lambda i, j: (0, jnp.minimum(j, n_in - 1))
    g_col = lambda i, j: (0, jnp.maximum(j - n_in, 0))
    return pl.pallas_call(
        functools.partial(_in_proj_kernel, n_in=n_in),
        grid=(m // tm, n_in + n_g),
        in_specs=[
            pl.BlockSpec((tm, k), lambda i, j: (i, 0)),
            pl.BlockSpec((1, k), lambda i, j: (0, 0)),
            pl.BlockSpec((k, tn), in_col),
            pl.BlockSpec((k, tn), g_col),
            pl.BlockSpec((1, tn), g_col),
        ],
        out_specs=[
            pl.BlockSpec((tm, tn), lambda i, j: (i, jnp.minimum(j, n_in - 1))),
            pl.BlockSpec((tm, tn), lambda i, j: (i, jnp.maximum(j - n_in, 0))),
        ],
        out_shape=[
            jax.ShapeDtypeStruct((m, w_in.shape[1]), BF16),
            jax.ShapeDtypeStruct((m, w_gate.shape[1]), BF16),
        ],
        scratch_shapes=[pltpu.VMEM((tm, k), BF16)],
        compiler_params=_params(("arbitrary", "arbitrary"), 56),
        name="in_proj",
    )(x, nw.reshape(1, k), w_in, w_gate, b_gate.reshape(1, -1))


def _head_group_matrix():
    r = lax.broadcasted_iota(jnp.int32, (LANES, LANES), 0)
    c = lax.broadcasted_iota(jnp.int32, (LANES, LANES), 1)
    return ((r >> 6) == (c >> 6)).astype(BF16)


def _head_rmsnorm(xcol, w, gmat):
    ss = jnp.dot((xcol * xcol).astype(BF16), gmat, preferred_element_type=F32)
    return xcol * lax.rsqrt(ss * (1.0 / HEAD_DIM) + EPS) * w


def _attn_prompt_kernel(sink_ref, q_ref, kc_ref, kp_ref, vc_ref, vp_ref, qw_ref, kw_ref,
                        o_ref, kn_ref, vn_ref):
    c = pl.program_id(1)
    blk = WINDOW
    gmat = _head_group_matrix()
    lane = lax.broadcasted_iota(jnp.int32, (1, LANES), 1)
    low = lane < HEAD_DIM

    kw = kw_ref[...]
    qw = qw_ref[...]
    kc = kc_ref[...].astype(F32)
    kp = kp_ref[...].astype(F32)
    vc = vc_ref[...].astype(F32)
    vp = vp_ref[...].astype(F32)
    ncol = kc.shape[1] // LANES
    kcn = [_head_rmsnorm(kc[:, p * LANES:(p + 1) * LANES], kw, gmat) for p in range(ncol)]
    kpn = [_head_rmsnorm(kp[:, p * LANES:(p + 1) * LANES], kw, gmat) for p in range(ncol)]
    kn_ref[0] = jnp.concatenate(kcn, axis=1)
    vn_ref[0] = vc

    kcat = [jnp.concatenate([kpn[p], kcn[p]], axis=0) for p in range(ncol)]
    vcat = [jnp.concatenate([vp[:, p * LANES:(p + 1) * LANES], vc[:, p * LANES:(p + 1) * LANES]], axis=0)
            for p in range(ncol)]
    krol = [pltpu.roll(kcat[p], HEAD_DIM, 1) for p in range(ncol)]
    vrol = [pltpu.roll(vcat[p], HEAD_DIM, 1) for p in range(ncol)]

    qi = lax.broadcasted_iota(jnp.int32, (blk, 2 * blk), 0)
    kj = lax.broadcasted_iota(jnp.int32, (blk, 2 * blk), 1)
    dist = blk + qi - kj
    valid = (dist >= 0) & (dist <= WINDOW) & ((kj >= blk) | (c > 0))
    distf = dist.astype(F32)

    q = q_ref[...].astype(F32)
    qn = [_head_rmsnorm(q[:, p * LANES:(p + 1) * LANES], qw, gmat) * (HEAD_DIM ** -0.5)
          for p in range(N_HEADS // HEADS_PER_VREG)]

    for g in range(N_KV_HEADS):
        p, half = divmod(g, HEADS_PER_VREG)
        if half == 0:
            kd = jnp.where(low, kcat[p], krol[p])
            vd = jnp.where(low, vcat[p], vrol[p])
        else:
            kd = jnp.where(low, krol[p], kcat[p])
            vd = jnp.where(low, vrol[p], vcat[p])
        kd = kd.astype(BF16)
        vd = vd.astype(BF16)
        qs = []
        for hl in range(GQA_GROUP):
            h = g * GQA_GROUP + hl
            pc, hh = divmod(h, HEADS_PER_VREG)
            keep = low if hh == 0 else jnp.logical_not(low)
            qs.append(jnp.where(keep, qn[pc], 0.0).astype(BF16))
        qstack = jnp.concatenate(qs, axis=0)
        s_all = lax.dot_general(qstack, kd, (((1,), (1,)), ((), ())), preferred_element_type=F32)
        ps = []
        for hl in range(GQA_GROUP):
            h = g * GQA_GROUP + hl
            s = s_all[hl * blk:(hl + 1) * blk] - _alibi_slope(h) * distf
            s = jnp.where(valid, s, MASK_VALUE)
            sink = sink_ref[h]
            mx = jnp.maximum(jnp.max(s, axis=-1, keepdims=True), sink)
            e = jnp.exp(s - mx)
            den = jnp.sum(e, axis=-1, keepdims=True) + jnp.exp(sink - mx)
            ps.append((e / den).astype(BF16))
        pstack = jnp.concatenate(ps, axis=0)
        o_all = jnp.dot(pstack, vd, preferred_element_type=F32)
        for pair in range(GQA_GROUP // HEADS_PER_VREG):
            pc = g * (GQA_GROUP // HEADS_PER_VREG) + pair
            even = o_all[(2 * pair) * blk:(2 * pair + 1) * blk]
            odd = o_all[(2 * pair + 1) * blk:(2 * pair + 2) * blk]
            o_ref[:, pc * LANES:(pc + 1) * LANES] = jnp.where(low, even, odd).astype(o_ref.dtype)


def _attn_prompt(proj, qw2, kw2, sinks, *, batch, seq):
    nb = seq // WINDOW
    attn_w = N_HEADS * HEAD_DIM
    kv_w = N_KV_HEADS * HEAD_DIM
    kcol = attn_w // kv_w
    vcol = kcol + 1

    def cur(col):
        return lambda b, c: (b * nb + c, col)

    def prev(col):
        return lambda b, c: (jnp.maximum(b * nb + c - 1, 0), col)

    return pl.pallas_call(
        _attn_prompt_kernel,
        grid=(batch, nb),
        in_specs=[
            pl.BlockSpec(memory_space=pltpu.SMEM),
            pl.BlockSpec((WINDOW, attn_w), cur(0)),
            pl.BlockSpec((WINDOW, kv_w), cur(kcol)),
            pl.BlockSpec((WINDOW, kv_w), prev(kcol)),
            pl.BlockSpec((WINDOW, kv_w), cur(vcol)),
            pl.BlockSpec((WINDOW, kv_w), prev(vcol)),
            pl.BlockSpec((1, LANES), lambda b, c: (0, 0)),
            pl.BlockSpec((1, LANES), lambda b, c: (0, 0)),
        ],
        out_specs=[
            pl.BlockSpec((WINDOW, attn_w), lambda b, c: (b * nb + c, 0)),
            pl.BlockSpec((1, WINDOW, kv_w), lambda b, c: (b, 0, 0)),
            pl.BlockSpec((1, WINDOW, kv_w), lambda b, c: (b, 0, 0)),
        ],
        out_shape=[
            jax.ShapeDtypeStruct((batch * seq, attn_w), BF16),
            jax.ShapeDtypeStruct((batch, WINDOW, kv_w), F32),
            jax.ShapeDtypeStruct((batch, WINDOW, kv_w), F32),
        ],
        compiler_params=_params(("parallel", "arbitrary"), 32),
        name="attn_prompt",
    )(sinks, proj, proj, proj, proj, proj, qw2, kw2)


def _layernorm_rows(x, w):
    mu = jnp.mean(x, axis=-1, keepdims=True)
    xc = x - mu
    return xc * lax.rsqrt(jnp.mean(xc * xc, axis=-1, keepdims=True) + EPS) * w


def _sgu_prompt_kernel(gu0_ref, gu1_ref, gv0_ref, gv1_ref, nw_ref, ws_ref, bs_ref, o_ref, st_ref):
    gv = jnp.concatenate([gv0_ref[...], gv1_ref[...]], axis=1).astype(F32)
    vn = _layernorm_rows(jax.nn.gelu(gv), nw_ref[...])
    st_ref[0] = vn
    vb = vn.astype(BF16)
    r = lax.broadcasted_iota(jnp.int32, (CHUNK, CHUNK), 0)
    c = lax.broadcasted_iota(jnp.int32, (CHUNK, CHUNK), 1)
    causal = r >= c
    gw = vb.shape[1] // GM_GROUPS
    half = gu0_ref.shape[1]
    for g in range(GM_GROUPS):
        w = jnp.where(causal, ws_ref[g], 0.0).astype(BF16)
        mix = jnp.dot(w, vb[:, g * gw:(g + 1) * gw], preferred_element_type=F32) + bs_ref[:, g:g + 1]
        src = gu0_ref if g * gw < half else gu1_ref
        off = g * gw - (0 if g * gw < half else half)
        u = jax.nn.gelu(src[:, off:off + gw].astype(F32))
        o_ref[:, g * gw:(g + 1) * gw] = (u * mix).astype(o_ref.dtype)


def _sgu_prompt(proj, nw, ws, bs_t, *, batch, seq, gu_off, gm_w):
    nc = seq // CHUNK
    half = gm_w // 2
    b0 = gu_off // half

    def col(k):
        return lambda b, c: (b * nc + c, b0 + k)

    return pl.pallas_call(
        _sgu_prompt_kernel,
        grid=(batch, nc),
        in_specs=[
            pl.BlockSpec((CHUNK, half), col(0)),
            pl.BlockSpec((CHUNK, half), col(1)),
            pl.BlockSpec((CHUNK, half), col(2)),
            pl.BlockSpec((CHUNK, half), col(3)),
            pl.BlockSpec((1, gm_w), lambda b, c: (0, 0)),
            pl.BlockSpec((GM_GROUPS, CHUNK, CHUNK), lambda b, c: (0, 0, 0)),
            pl.BlockSpec((CHUNK, GM_GROUPS), lambda b, c: (0, 0)),
        ],
        out_specs=[
            pl.BlockSpec((CHUNK, gm_w), lambda b, c: (b * nc + c, 0)),
            pl.BlockSpec((1, CHUNK, gm_w), lambda b, c: (b, 0, 0)),
        ],
        out_shape=[
            jax.ShapeDtypeStruct((batch * seq, gm_w), BF16),
            jax.ShapeDtypeStruct((batch, CHUNK, gm_w), F32),
        ],
        compiler_params=_params(("parallel", "arbitrary"), 32),
        name="sgu_prompt",
    )(proj, proj, proj, proj, nw.reshape(1, gm_w), ws, bs_t)


def _split_dot(x, m):
    hi = x.astype(BF16)
    lo = (x - hi.astype(F32)).astype(BF16)
    return jnp.dot(hi, m, preferred_element_type=F32) + jnp.dot(lo, m, preferred_element_type=F32)


def _attn_sample_kernel(q_ref, kn_ref, vn_ref, kb_ref, vb_ref, qw_ref, kw_ref, sink_ref,
                        wa_ref, wg_ref, wo_ref, o_ref, ko_ref, vo_ref, wab_ref, wgb_ref, wob_ref):
    wab_ref[...] = wa_ref[...].astype(BF16)
    wgb_ref[...] = wg_ref[...].astype(BF16)
    wob_ref[...] = wo_ref[...].astype(BF16)
    nsamp = q_ref.shape[0]
    attn_w = q_ref.shape[1]
    kv_w = kn_ref.shape[1]
    gmat = _head_group_matrix()
    q = q_ref[...].astype(F32)
    kn = kn_ref[...].astype(F32)
    qn = jnp.concatenate(
        [_head_rmsnorm(q[:, p * LANES:(p + 1) * LANES], qw_ref[...], gmat) * (HEAD_DIM ** -0.5)
         for p in range(attn_w // LANES)], axis=1)
    knn = jnp.concatenate(
        [_head_rmsnorm(kn[:, p * LANES:(p + 1) * LANES], kw_ref[...], gmat)
         for p in range(kv_w // LANES)], axis=1)
    vnn = vn_ref[...].astype(F32)

    ec = lax.broadcasted_iota(jnp.int32, (kv_w, attn_w), 0)
    el = lax.broadcasted_iota(jnp.int32, (kv_w, attn_w), 1)
    expand = (((ec >> 6) == (el >> 8)) & ((ec & 63) == (el & 63))).astype(BF16)
    gl = lax.broadcasted_iota(jnp.int32, (attn_w, LANES), 0)
    gh = lax.broadcasted_iota(jnp.int32, (attn_w, LANES), 1)
    hsum = ((gl >> 6) == gh).astype(BF16)
    tl = lax.broadcasted_iota(jnp.int32, (LANES, attn_w), 1)
    th = lax.broadcasted_iota(jnp.int32, (LANES, attn_w), 0)
    hexp = ((tl >> 6) == th).astype(BF16)

    nkeys = WINDOW + SUBLANES
    row = lax.broadcasted_iota(jnp.int32, (nkeys, LANES), 0)
    head = lax.broadcasted_iota(jnp.int32, (nkeys, LANES), 1)
    slope = jnp.exp2(-8.0 * (head + 1).astype(F32) / N_HEADS)
    dist = (WINDOW - row).astype(F32)
    key_ok = row <= WINDOW
    srow = lax.broadcasted_iota(jnp.int32, (WINDOW, kv_w), 0)
    sinks = sink_ref[...]

    for s in range(nsamp):
        kb = kb_ref[s]
        vb = vb_ref[s]
        knew = knn[s:s + 1]
        vnew = vnn[s:s + 1]
        ko_ref[s] = jnp.where(srow == WINDOW - 1, knew, pltpu.roll(kb, WINDOW - 1, 0))
        vo_ref[s] = jnp.where(srow == WINDOW - 1, vnew, pltpu.roll(vb, WINDOW - 1, 0))
        kk = jnp.concatenate([kb, jnp.broadcast_to(knew, (SUBLANES, kv_w))], axis=0)
        vv = jnp.concatenate([vb, jnp.broadcast_to(vnew, (SUBLANES, kv_w))], axis=0)
        kexp = _split_dot(kk, expand)
        vexp = _split_dot(vv, expand)
        sc = _split_dot(kexp * qn[s:s + 1], hsum)
        sc = sc - slope * dist
        sc = jnp.where(key_ok, sc, MASK_VALUE)
        mx = jnp.maximum(jnp.max(sc, axis=0, keepdims=True), sinks)
        e = jnp.exp(sc - mx)
        den = jnp.sum(e, axis=0, keepdims=True) + jnp.exp(sinks - mx)
        pexp = _split_dot(e / den, hexp)
        o_ref[s:s + 1, :] = jnp.sum(pexp * vexp, axis=0, keepdims=True)


def _attn_sample(q, knew, vnew, kbuf, vbuf, qw2, kw2, sinks_row, wa, wg, wo, *, group):
    nb, attn_w = q.shape
    kv_w = knew.shape[1]
    steps = nb // group
    row = lambda width: pl.BlockSpec((group, width), lambda i: (i, 0))
    buf = pl.BlockSpec((group, WINDOW, kv_w), lambda i: (i, 0, 0))
    vec = pl.BlockSpec((1, LANES), lambda i: (0, 0))
    slab = lambda w: pl.BlockSpec((w.shape[0] // steps, w.shape[1]), lambda i: (i, 0))
    return pl.pallas_call(
        _attn_sample_kernel,
        grid=(steps,),
        in_specs=[row(attn_w), row(kv_w), row(kv_w), buf, buf, vec, vec, vec, slab(wa), slab(wg), slab(wo)],
        out_specs=[row(attn_w), buf, buf, slab(wa), slab(wg), slab(wo)],
        out_shape=[
            jax.ShapeDtypeStruct((nb, attn_w), F32),
            jax.ShapeDtypeStruct((nb, WINDOW, kv_w), F32),
            jax.ShapeDtypeStruct((nb, WINDOW, kv_w), F32),
            jax.ShapeDtypeStruct(wa.shape, BF16),
            jax.ShapeDtypeStruct(wg.shape, BF16),
            jax.ShapeDtypeStruct(wo.shape, BF16),
        ],
        compiler_params=_params(("parallel",), 48),
        name="attn_sample",
    )(q, knew, vnew, kbuf, vbuf, qw2, kw2, sinks_row, wa, wg, wo)


def _sgu_sample_kernel(gu_ref, gv_ref, nw_ref, w0_ref, b0_ref, o_ref, st_ref):
    vn = _layernorm_rows(jax.nn.gelu(gv_ref[...].astype(F32)), nw_ref[...])
    st_ref[...] = vn
    mix = w0_ref[...] * vn + b0_ref[...]
    o_ref[...] = jax.nn.gelu(gu_ref[...].astype(F32)) * mix


def _sgu_sample(gu, gv, nw, w0_row, b0_row):
    nb, gm_w = gu.shape
    full = pl.BlockSpec((nb, gm_w), lambda i: (0, 0))
    vec = pl.BlockSpec((1, gm_w), lambda i: (0, 0))
    return pl.pallas_call(
        _sgu_sample_kernel,
        grid=(1,),
        in_specs=[full, full, vec, vec, vec],
        out_specs=[full, full],
        out_shape=[jax.ShapeDtypeStruct((nb, gm_w), F32), jax.ShapeDtypeStruct((nb, gm_w), F32)],
        name="sgu_sample",
    )(gu, gv, nw.reshape(1, gm_w), w0_row, b0_row)


def _merge_out_kernel(a_ref, s_ref, ga_ref, gb_ref, x_ref, wa_ref, wg_ref, wo_ref, nw_ref,
                      x1_ref, xn1_ref):
    a = jnp.dot(a_ref[...].astype(BF16), wa_ref[...], preferred_element_type=F32)
    m = jnp.dot(s_ref[...].astype(BF16), wg_ref[...], preferred_element_type=F32)
    merged = (gb_ref[...].astype(F32) * a + ga_ref[...].astype(F32) * m).astype(BF16)
    x1 = x_ref[...] + jnp.dot(merged, wo_ref[...], preferred_element_type=F32)
    x1_ref[...] = x1
    xn1_ref[...] = _rmsnorm_rows(x1, nw_ref[...]).astype(BF16)


def _merge_out(attn, sgu, gates, x, wa, wg, wo, nw, *, tm):
    m, ka = attn.shape
    kg = sgu.shape[1]
    d = wo.shape[1]
    rows = lambda width, col=0: pl.BlockSpec((tm, width), lambda i: (i, col))
    resident = lambda shape: pl.BlockSpec(shape, lambda i: (0, 0), pipeline_mode=pl.Buffered(1))
    return pl.pallas_call(
        _merge_out_kernel,
        grid=(m // tm,),
        in_specs=[
            rows(ka), rows(kg), rows(d, 0), rows(d, 1), rows(d),
            resident((ka, d)), resident((kg, d)), resident((d, d)),
            pl.BlockSpec((1, d), lambda i: (0, 0)),
        ],
        out_specs=[rows(d), rows(d)],
        out_shape=[jax.ShapeDtypeStruct((m, d), F32), jax.ShapeDtypeStruct((m, d), BF16)],
        compiler_params=_params(("parallel",), 48),
        name="merge_out",
    )(attn, sgu, gates, gates, x, wa, wg, wo, nw.reshape(1, d))


PAD = SUBLANES
FFN_ROW_BLOCKS = 4


def _ffn_step(t, xn_ref, wug_ref, wuu_ref, wd_ref, cwg_ref, cwu_ref, cbg_ref, cbu_ref,
              csg_ref, csu_ref, h_new, h_old, carry_ref, acc_ref, *, nf, tiles_per_seq):
    tm = xn_ref.shape[0]
    tf = wug_ref.shape[1]
    rb = tm // FFN_ROW_BLOCKS
    tp = jnp.maximum(t - 1, 0)
    ip = tp // nf
    jp = tp % nf
    first = (ip % tiles_per_seq) == 0
    h_old[0:PAD, :] = jnp.where(first, 0.0, carry_ref[jp])
    carry_ref[jp] = h_old[tm:tm + PAD, :]
    cw = jnp.concatenate([cwg_ref[...], cwu_ref[...]], axis=1)
    cb = jnp.concatenate([cbg_ref[...], cbu_ref[...]], axis=1)
    wug = wug_ref[...]
    wuu = wuu_ref[...]
    wd = wd_ref[...]
    for r in range(FFN_ROW_BLOCKS):
        lo = r * rb
        xr = xn_ref[lo:lo + rb, :]
        h_new[PAD + lo:PAD + lo + rb, 0:tf] = jnp.dot(xr, wug, preferred_element_type=F32)
        h_new[PAD + lo:PAD + lo + rb, tf:2 * tf] = jnp.dot(xr, wuu, preferred_element_type=F32)
        hc = (cw[0:1] * h_old[PAD - 2 + lo:PAD - 2 + lo + rb, :]
              + cw[1:2] * h_old[PAD - 1 + lo:PAD - 1 + lo + rb, :]
              + cw[2:3] * h_old[PAD + lo:PAD + lo + rb, :] + cb)
        act = (jax.nn.silu(hc[:, 0:tf]) * hc[:, tf:2 * tf]).astype(BF16)
        acc_ref[lo:lo + rb, :] += jnp.dot(act, wd, preferred_element_type=F32)
    csg_ref[0] = h_old[PAD + tm - 2:PAD + tm, 0:tf]
    csu_ref[0] = h_old[PAD + tm - 2:PAD + tm, tf:2 * tf]


def _ffn_kernel(xn_ref, wug_ref, wuu_ref, wd_ref, cwg_ref, cwu_ref, cbg_ref, cbu_ref,
                o_ref, csg_ref, csu_ref, ha_ref, hb_ref, carry_ref, acc_ref, *, nf, tiles_per_seq):
    t = pl.program_id(0)
    step = functools.partial(_ffn_step, t, xn_ref, wug_ref, wuu_ref, wd_ref, cwg_ref, cwu_ref,
                             cbg_ref, cbu_ref, csg_ref, csu_ref, carry_ref=carry_ref, acc_ref=acc_ref,
                             nf=nf, tiles_per_seq=tiles_per_seq)

    @pl.when(t == 0)
    def _():
        hb_ref[...] = jnp.zeros_like(hb_ref)
        carry_ref[...] = jnp.zeros_like(carry_ref)

    @pl.when(jnp.maximum(t - 1, 0) % nf == 0)
    def _():
        acc_ref[...] = jnp.zeros_like(acc_ref)

    @pl.when(t % 2 == 0)
    def _():
        step(h_new=ha_ref, h_old=hb_ref)

    @pl.when(t % 2 == 1)
    def _():
        step(h_new=hb_ref, h_old=ha_ref)

    @pl.when(jnp.maximum(t - 1, 0) % nf == nf - 1)
    def _():
        o_ref[...] = acc_ref[...].astype(o_ref.dtype)


def _ffn_prompt(xn, w_up, w_down, cw, cb, *, batch, seq, tm, tf):
    m, k = xn.shape
    d_ff, d = w_down.shape
    nf = d_ff // tf
    nm = m // tm
    tiles_per_seq = seq // tm
    prev = lambda t: jnp.maximum(t - 1, 0)
    up_col = lambda off: (lambda t: (0, off + t % nf))
    dn_col = lambda off: (lambda t: (0, off + prev(t) % nf))
    state = lambda t: (prev(t) // nf, 0, prev(t) % nf)
    delta, csg, csu = pl.pallas_call(
        functools.partial(_ffn_kernel, nf=nf, tiles_per_seq=tiles_per_seq),
        grid=(nm * nf + 1,),
        in_specs=[
            pl.BlockSpec((tm, k), lambda t: (jnp.minimum(t // nf, nm - 1), 0)),
            pl.BlockSpec((k, tf), up_col(0)),
            pl.BlockSpec((k, tf), up_col(nf)),
            pl.BlockSpec((tf, d), lambda t: (prev(t) % nf, 0)),
            pl.BlockSpec((3, tf), dn_col(0)),
            pl.BlockSpec((3, tf), dn_col(nf)),
            pl.BlockSpec((1, tf), dn_col(0)),
            pl.BlockSpec((1, tf), dn_col(nf)),
        ],
        out_specs=[
            pl.BlockSpec((tm, d), lambda t: (prev(t) // nf, 0)),
            pl.BlockSpec((1, 2, tf), state),
            pl.BlockSpec((1, 2, tf), state),
        ],
        out_shape=[
            jax.ShapeDtypeStruct((m, d), BF16),
            jax.ShapeDtypeStruct((nm, 2, d_ff), F32),
            jax.ShapeDtypeStruct((nm, 2, d_ff), F32),
        ],
        scratch_shapes=[
            pltpu.VMEM((tm + PAD, 2 * tf), F32),
            pltpu.VMEM((tm + PAD, 2 * tf), F32),
            pltpu.VMEM((nf, PAD, 2 * tf), F32),
            pltpu.VMEM((tm, d), F32),
        ],
        compiler_params=_params(("arbitrary",), 58),
        name="ffn_prompt",
    )(xn, w_up, w_up, w_down, cw, cw, cb.reshape(1, -1), cb.reshape(1, -1))
    tails = jnp.concatenate([csg, csu], axis=-1).reshape(batch, tiles_per_seq, 2, 2 * d_ff)
    return delta, tails[:, -1]


def _ffn_down_sample_kernel(hg_ref, hu_ref, b0g_ref, b0u_ref, b1g_ref, b1u_ref, cwg_ref, cwu_ref,
                            cbg_ref, cbu_ref, wd_ref, x_ref, o_ref, wdb_ref):
    @pl.when(pl.program_id(0) == 0)
    def _():
        o_ref[...] = x_ref[...]

    def conv(h_ref, b0_ref, b1_ref, cw_ref, cb_ref):
        cw = cw_ref[...]
        return cw[0:1] * b0_ref[...] + cw[1:2] * b1_ref[...] + cw[2:3] * h_ref[...] + cb_ref[...]

    hcg = conv(hg_ref, b0g_ref, b1g_ref, cwg_ref, cbg_ref)
    hcu = conv(hu_ref, b0u_ref, b1u_ref, cwu_ref, cbu_ref)
    act = (jax.nn.silu(hcg) * hcu).astype(BF16)
    wdb = wd_ref[...].astype(BF16)
    wdb_ref[...] = wdb
    o_ref[...] += jnp.dot(act, wdb, preferred_element_type=F32)


def _ffn_down_sample(h, state2, cw, cb, wd, x, *, tf):
    nb, two_f = h.shape
    d_ff = two_f // 2
    nf = d_ff // tf
    d = wd.shape[1]
    colblk = lambda off: pl.BlockSpec((nb, tf), lambda j: (0, j + off))
    return pl.pallas_call(
        _ffn_down_sample_kernel,
        grid=(nf,),
        in_specs=[
            colblk(0), colblk(nf),
            colblk(0), colblk(nf), colblk(2 * nf), colblk(3 * nf),
            pl.BlockSpec((3, tf), lambda j: (0, j)),
            pl.BlockSpec((3, tf), lambda j: (0, j + nf)),
            pl.BlockSpec((1, tf), lambda j: (0, j)),
            pl.BlockSpec((1, tf), lambda j: (0, j + nf)),
            pl.BlockSpec((tf, d), lambda j: (j, 0)),
            pl.BlockSpec((nb, d), lambda j: (0, 0)),
        ],
        out_specs=[pl.BlockSpec((nb, d), lambda j: (0, 0)), pl.BlockSpec((tf, d), lambda j: (j, 0))],
        out_shape=[jax.ShapeDtypeStruct((nb, d), F32), jax.ShapeDtypeStruct((d_ff, d), BF16)],
        compiler_params=_params(("arbitrary",), 32),
        name="ffn_down_sample",
    )(h, h, state2, state2, state2, state2, cw, cw, cb.reshape(1, two_f), cb.reshape(1, two_f), wd, x)


def _ple_kernel(*refs, has_delta, emit_w):
    refs = list(refs)
    x_ref = refs.pop(0)
    d_ref = refs.pop(0) if has_delta else None
    nw_ref, w_ref, p_ref, wp_ref, o_ref = refs[:5]
    xn_ref = refs[-1]
    j = pl.program_id(1)
    tn = o_ref.shape[1]

    @pl.when(j == 0)
    def _():
        _rmsnorm_to(x_ref, nw_ref, xn_ref, add_ref=d_ref)

    wb = w_ref[...].astype(BF16)
    wpb = wp_ref[...].astype(BF16)
    if emit_w:
        wb_ref, wpb_ref = refs[5:7]
        wb_ref[...] = wb
        wpb_ref[...] = wpb
    gate = jax.nn.sigmoid(jnp.dot(xn_ref[...], wb, preferred_element_type=F32))
    emb = jnp.dot(p_ref[...].astype(BF16), wpb, preferred_element_type=F32)
    o_ref[...] = gate * emb
    for jj in range(x_ref.shape[1] // tn):
        @pl.when(j == jj)
        def _():
            res = x_ref[:, jj * tn:(jj + 1) * tn]
            if has_delta:
                res = res + d_ref[:, jj * tn:(jj + 1) * tn].astype(F32)
            o_ref[...] += res


def _ple_rows_kernel(x_ref, d_ref, nw_ref, w_ref, p_ref, wp_ref, o_ref):
    x2 = x_ref[...] + d_ref[...].astype(F32)
    xn = _rmsnorm_rows(x2, nw_ref[...]).astype(BF16)
    gate = jax.nn.sigmoid(jnp.dot(xn, w_ref[...], preferred_element_type=F32))
    emb = jnp.dot(p_ref[...].astype(BF16), wp_ref[...], preferred_element_type=F32)
    o_ref[...] = x2 + gate * emb


def _ple_rows(x, delta, nw, w, p, wp, *, tm):
    m, d = x.shape
    kp = p.shape[1]
    rows = lambda width: pl.BlockSpec((tm, width), lambda i: (i, 0))
    resident = lambda shape: pl.BlockSpec(shape, lambda i: (0, 0), pipeline_mode=pl.Buffered(1))
    return pl.pallas_call(
        _ple_rows_kernel,
        grid=(m // tm,),
        in_specs=[rows(d), rows(d), pl.BlockSpec((1, d), lambda i: (0, 0)), resident((d, d)),
                  rows(kp), resident((kp, d))],
        out_specs=rows(d),
        out_shape=jax.ShapeDtypeStruct((m, d), F32),
        compiler_params=_params(("parallel",), 56),
        name="ple_prompt",
    )(x, delta, nw.reshape(1, d), w, p, wp)


def _ple(x, delta, nw, w, p, wp, *, tm, tn, name, emit_w=False):
    m, k = x.shape
    n = w.shape[1]
    kp = p.shape[1]
    assert not emit_w or m == tm
    out_specs = [pl.BlockSpec((tm, tn), lambda i, j: (i, j))]
    out_shape = [jax.ShapeDtypeStruct((m, n), F32)]
    if emit_w:
        out_specs += [pl.BlockSpec((k, tn), lambda i, j: (0, j)), pl.BlockSpec((kp, tn), lambda i, j: (0, j))]
        out_shape += [jax.ShapeDtypeStruct((k, n), BF16), jax.ShapeDtypeStruct((kp, n), BF16)]
    row_full = pl.BlockSpec((tm, k), lambda i, j: (i, 0))
    in_specs = [row_full] + ([row_full] if delta is not None else []) + [
        pl.BlockSpec((1, k), lambda i, j: (0, 0)),
        pl.BlockSpec((k, tn), lambda i, j: (0, j)),
        pl.BlockSpec((tm, kp), lambda i, j: (i, 0)),
        pl.BlockSpec((kp, tn), lambda i, j: (0, j)),
    ]
    args = [x] + ([delta] if delta is not None else []) + [nw.reshape(1, k), w, p, wp]
    outs = pl.pallas_call(
        functools.partial(_ple_kernel, has_delta=delta is not None, emit_w=emit_w),
        grid=(m // tm, n // tn),
        in_specs=in_specs,
        out_specs=out_specs,
        out_shape=out_shape,
        scratch_shapes=[pltpu.VMEM((tm, k), BF16)],
        compiler_params=_params(("parallel", "arbitrary"), 56),
        name=name,
    )(*args)
    return outs if emit_w else outs[0]


def _layer_weights(i, attn_norm_w, w_in, q_norm_w, k_norm_w, attn_sinks, sgu_norm_w, sgu_w, sgu_b,
                   w_br_attn, w_br_gm, w_gate, b_gate, w_out, ffn_norm_w, w_up, conv_w, conv_b,
                   w_down, ple_norm_w, w_ple_gate, w_ple_proj):
    sinks = attn_sinks[i]
    gm_group_w = w_br_gm.shape[1] // GM_GROUPS
    return dict(
        attn_norm_w=attn_norm_w[i], w_in=w_in[i],
        qw2=jnp.tile(q_norm_w[i], HEADS_PER_VREG).reshape(1, LANES),
        kw2=jnp.tile(k_norm_w[i], HEADS_PER_VREG).reshape(1, LANES),
        sinks=sinks,
        sinks_row=jnp.pad(sinks, (0, LANES - N_HEADS)).reshape(1, LANES),
        sgu_norm_w=sgu_norm_w[i], sgu_w=sgu_w[i], sgu_b_t=jnp.transpose(sgu_b[i]),
        sgu_w0_row=jnp.repeat(sgu_w[i][:, 0, 0], gm_group_w).reshape(1, -1),
        sgu_b0_row=jnp.repeat(sgu_b[i][:, 0], gm_group_w).reshape(1, -1),
        w_br_attn=w_br_attn[i], w_br_gm=w_br_gm[i],
        w_gate=w_gate[i], b_gate=b_gate[i], w_out=w_out[i],
        ffn_norm_w=ffn_norm_w[i], w_up=w_up[i], conv_w=conv_w[i], conv_b=conv_b[i],
        w_down=w_down[i], ple_norm_w=ple_norm_w[i], w_ple_gate=w_ple_gate[i], w_ple_proj=w_ple_proj[i],
    )


def _prompt_layer(x, p, lw, wb):
    batch, seq, d = x.shape
    x2d = x.reshape(batch * seq, d)
    attn_w = N_HEADS * HEAD_DIM
    kv_w = N_KV_HEADS * HEAD_DIM
    gm_w = lw["w_br_gm"].shape[0]
    proj, gates = _in_proj(x2d, lw["attn_norm_w"], wb["w_in"], wb["w_gate"], lw["b_gate"], tm=1024, tn=512)
    attn, k_new, v_new = _attn_prompt(proj, lw["qw2"], lw["kw2"], lw["sinks"], batch=batch, seq=seq)
    sgu, sgu_state = _sgu_prompt(proj, lw["sgu_norm_w"], lw["sgu_w"], lw["sgu_b_t"], batch=batch, seq=seq,
                                 gu_off=attn_w + 2 * kv_w, gm_w=gm_w)
    x1, xn1 = _merge_out(attn, sgu, gates, x2d, wb["w_br_attn"], wb["w_br_gm"], wb["w_out"],
                         lw["ffn_norm_w"], tm=256)
    delta, conv_state = _ffn_prompt(xn1, wb["w_up"], wb["w_down"], lw["conv_w"], lw["conv_b"],
                                    batch=batch, seq=seq, tm=1024, tf=512)
    x3 = _ple_rows(x1, delta, lw["ple_norm_w"], wb["w_ple_gate"], p.reshape(batch * seq, -1),
                   wb["w_ple_proj"], tm=512)
    return (x3.reshape(batch, seq, d), k_new.reshape(batch, WINDOW, N_KV_HEADS, HEAD_DIM),
            v_new.reshape(batch, WINDOW, N_KV_HEADS, HEAD_DIM), sgu_state, conv_state)


def _sample_layer(x, p, k_buf, v_buf, conv_buf, lw):
    nb, t, d = x.shape
    x2d = x.reshape(nb * t, d)
    attn_w = N_HEADS * HEAD_DIM
    kv_w = N_KV_HEADS * HEAD_DIM
    gm_w = lw["w_br_gm"].shape[0]
    wb = {}
    proj, wb["w_in"] = _norm_mm(x2d, lw["attn_norm_w"], lw["w_in"], None, act=None,
                                tn=lw["w_in"].shape[1] // 2, name="proj_sample")
    gates, wb["w_gate"] = _norm_mm(x2d, lw["attn_norm_w"], lw["w_gate"], lw["b_gate"], act="sigmoid",
                                   tn=1024, name="gates_sample")
    q = proj[:, :attn_w]
    k = proj[:, attn_w:attn_w + kv_w]
    v = proj[:, attn_w + kv_w:attn_w + 2 * kv_w]
    gu = proj[:, attn_w + 2 * kv_w:attn_w + 2 * kv_w + gm_w]
    gv = proj[:, attn_w + 2 * kv_w + gm_w:]
    attn, k_new, v_new, wb["w_br_attn"], wb["w_br_gm"], wb["w_out"] = _attn_sample(
        q, k, v, k_buf.reshape(nb, WINDOW, kv_w), v_buf.reshape(nb, WINDOW, kv_w),
        lw["qw2"], lw["kw2"], lw["sinks_row"], lw["w_br_attn"], lw["w_br_gm"], lw["w_out"], group=8)
    sgu, sgu_state = _sgu_sample(gu, gv, lw["sgu_norm_w"], lw["sgu_w0_row"], lw["sgu_b0_row"])
    x1, _ = _merge_out(attn, sgu, gates, x2d, wb["w_br_attn"], wb["w_br_gm"], wb["w_out"],
                       lw["ffn_norm_w"], tm=nb)
    h, wb["w_up"] = _norm_mm(x1, lw["ffn_norm_w"], lw["w_up"], None, act=None, tn=1024, name="ffn_up_sample")
    x2, wb["w_down"] = _ffn_down_sample(h, conv_buf.reshape(nb, -1), lw["conv_w"], lw["conv_b"], lw["w_down"],
                                        x1, tf=512)
    x3, wb["w_ple_gate"], wb["w_ple_proj"] = _ple(
        x2, None, lw["ple_norm_w"], lw["w_ple_gate"], p.reshape(nb * t, -1), lw["w_ple_proj"],
        tm=nb, tn=1024, name="ple_sample", emit_w=True)
    conv_state = jnp.stack([conv_buf[:, 1, :], h], axis=1)
    return (x3.reshape(nb, t, d), k_new.reshape(nb, WINDOW, N_KV_HEADS, HEAD_DIM),
            v_new.reshape(nb, WINDOW, N_KV_HEADS, HEAD_DIM), sgu_state.reshape(nb, t, gm_w), conv_state), wb


def kernel(x_prompt, x_sample, p_prompt, p_sample, state_attn_k, state_attn_v, state_conv, attn_norm_w, w_in, q_norm_w, k_norm_w, attn_sinks, sgu_norm_w, sgu_w, sgu_b, w_br_attn, w_br_gm, w_gate, b_gate, w_out, ffn_norm_w, w_up, conv_w, conv_b, w_down, ple_norm_w, w_ple_gate, w_ple_proj):
    depth = w_in.shape[0]
    xp, xs = x_prompt, x_sample
    outs = [[] for _ in range(8)]
    for i in range(depth):
        lw = _layer_weights(i, attn_norm_w, w_in, q_norm_w, k_norm_w, attn_sinks, sgu_norm_w, sgu_w, sgu_b,
                            w_br_attn, w_br_gm, w_gate, b_gate, w_out, ffn_norm_w, w_up, conv_w, conv_b,
                            w_down, ple_norm_w, w_ple_gate, w_ple_proj)
        (xs, ks, vs, gs, cs), wb = _sample_layer(xs, p_sample[i], state_attn_k[i], state_attn_v[i],
                                                 state_conv[i], lw)
        xp, kp, vp, gp, cp = _prompt_layer(xp, p_prompt[i], lw, wb)
        for lst, val in zip(outs, (kp, vp, ks, vs, gp, gs, cp, cs)):
            lst.append(val)
    return (xp, xs) + tuple(jnp.stack(lst) for lst in outs)
```

```python
import functools

import jax
import jax.numpy as jnp
from jax import lax
from jax.experimental import pallas as pl
from jax.experimental.pallas import tpu as pltpu

F32 = jnp.float32
BF16 = jnp.bfloat16

HEAD_DIM = 64
N_HEADS = 16
N_KV_HEADS = 4
GQA_GROUP = N_HEADS // N_KV_HEADS
WINDOW = 128
CHUNK = 128
GM_GROUPS = 4
EPS = 1e-6
MASK_VALUE = -1e30
LANES = 128
SUBLANES = 8
HEADS_PER_VREG = LANES // HEAD_DIM
MIB = 1 << 20


def _alibi_slope(h):
    return float(2.0 ** (-8.0 * (h + 1) / N_HEADS))


def _params(semantics, vmem_mib, flags=None):
    return pltpu.CompilerParams(dimension_semantics=semantics, vmem_limit_bytes=vmem_mib * MIB, flags=flags)


def _rmsnorm_rows(x, w):
    ms = jnp.mean(x * x, axis=-1, keepdims=True)
    return x * lax.rsqrt(ms + EPS) * w


NORM_ROWS = 256


def _rmsnorm_to(x_ref, nw_ref, xn_ref, add_ref=None):
    tm = x_ref.shape[0]
    rows = min(NORM_ROWS, tm)

    def body(r, carry):
        sl = pl.ds(pl.multiple_of(r * rows, rows), rows)
        x = x_ref[sl, :]
        if add_ref is not None:
            x = x + add_ref[sl, :].astype(F32)
        xn_ref[sl, :] = _rmsnorm_rows(x, nw_ref[...]).astype(BF16)
        return carry

    lax.fori_loop(0, tm // rows, body, 0)


def _norm_mm_kernel(*refs, has_bias, act):
    if has_bias:
        x_ref, nw_ref, w_ref, b_ref, o_ref, wb_ref, xn_ref = refs
    else:
        x_ref, nw_ref, w_ref, o_ref, wb_ref, xn_ref = refs

    @pl.when(pl.program_id(0) == 0)
    def _():
        _rmsnorm_to(x_ref, nw_ref, xn_ref)

    wb = w_ref[...].astype(BF16)
    tile = wb_ref.shape[2]
    for c in range(wb_ref.shape[0]):
        wb_ref[c] = wb[:, c * tile:(c + 1) * tile]
    acc = jnp.dot(xn_ref[...], wb, preferred_element_type=F32)
    if has_bias:
        acc = acc + b_ref[...]
    if act == "sigmoid":
        acc = jax.nn.sigmoid(acc)
    o_ref[...] = acc.astype(o_ref.dtype)


def _norm_mm(x, nw, w, bias, *, act, tn, tile, name):
    m, k = x.shape
    n = w.shape[1]
    per_step = tn // tile
    in_specs = [
        pl.BlockSpec((m, k), lambda j: (0, 0)),
        pl.BlockSpec((1, k), lambda j: (0, 0)),
        pl.BlockSpec((k, tn), lambda j: (0, j)),
    ]
    args = [x, nw.reshape(1, k), w]
    if bias is not None:
        in_specs.append(pl.BlockSpec((1, tn), lambda j: (0, j)))
        args.append(bias.reshape(1, n))
    return pl.pallas_call(
        functools.partial(_norm_mm_kernel, has_bias=bias is not None, act=act),
        grid=(n // tn,),
        in_specs=in_specs,
        out_specs=[pl.BlockSpec((m, tn), lambda j: (0, j)),
                   pl.BlockSpec((per_step, k, tile), lambda j: (j, 0, 0))],
        out_shape=[jax.ShapeDtypeStruct((m, n), F32), jax.ShapeDtypeStruct((n // tile, k, tile), BF16)],
        scratch_shapes=[pltpu.VMEM((m, k), BF16)],
        compiler_params=_params(("arbitrary",), 56),
        name=name,
    )(*args)


def _in_proj_kernel(x_ref, nw_ref, win_ref, wg_ref, bg_ref, proj_ref, gates_ref, xn_ref, *, n_in):
    j = pl.program_id(1)

    @pl.when(j == 0)
    def _():
        _rmsnorm_to(x_ref, nw_ref, xn_ref)

    @pl.when(j < n_in)
    def _():
        acc = jnp.dot(xn_ref[...], win_ref[...].astype(BF16), preferred_element_type=F32)
        proj_ref[...] = acc.astype(proj_ref.dtype)

    @pl.when(j >= n_in)
    def _():
        acc = jnp.dot(xn_ref[...], wg_ref[...].astype(BF16), preferred_element_type=F32)
        gates_ref[...] = jax.nn.sigmoid(acc + bg_ref[...]).astype(gates_ref.dtype)


def _in_proj(x, nw, w_in, w_gate, b_gate, *, tm):
    m, k = x.shape
    n_in, _, tn = w_in.shape
    n_g = w_gate.shape[0]
    in_tile = lambda i, j: (jnp.minimum(j, n_in - 1), 0, 0)
    g_tile = lambda i, j: (jnp.maximum(j - n_in, 0), 0, 0)
    return pl.pallas_call(
        functools.partial(_in_proj_kernel, n_in=n_in),
        grid=(m // tm, n_in + n_g),
        in_specs=[
            pl.BlockSpec((tm, k), lambda i, j: (i, 0)),
            pl.BlockSpec((1, k), lambda i, j: (0, 0)),
            pl.BlockSpec((None, k, tn), in_tile),
            pl.BlockSpec((None, k, tn), g_tile),
            pl.BlockSpec((1, tn), lambda i, j: (0, jnp.maximum(j - n_in, 0))),
        ],
        out_specs=[
            pl.BlockSpec((tm, tn), lambda i, j: (i, jnp.minimum(j, n_in - 1))),
            pl.BlockSpec((tm, tn), lambda i, j: (i, jnp.maximum(j - n_in, 0))),
        ],
        out_shape=[
            jax.ShapeDtypeStruct((m, n_in * tn), BF16),
            jax.ShapeDtypeStruct((m, n_g * tn), BF16),
        ],
        scratch_shapes=[pltpu.VMEM((tm, k), BF16)],
        compiler_params=_params(("arbitrary", "arbitrary"), 56),
        name="in_proj",
    )(x, nw.reshape(1, k), w_in, w_gate, b_gate.reshape(1, -1))


def _head_group_matrix():
    r = lax.broadcasted_iota(jnp.int32, (LANES, LANES), 0)
    c = lax.broadcasted_iota(jnp.int32, (LANES, LANES), 1)
    return ((r >> 6) == (c >> 6)).astype(BF16)


def _head_rmsnorm(xcol, w, gmat):
    ss = jnp.dot((xcol * xcol).astype(BF16), gmat, preferred_element_type=F32)
    return xcol * lax.rsqrt(ss * (1.0 / HEAD_DIM) + EPS) * w


def _attn_prompt_kernel(sink_ref, q_ref, kc_ref, kp_ref, vc_ref, vp_ref, qw_ref, kw_ref,
                        o_ref, kn_ref, vn_ref):
    c = pl.program_id(1)
    blk = WINDOW
    gmat = _head_group_matrix()
    lane = lax.broadcasted_iota(jnp.int32, (1, LANES), 1)
    low = lane < HEAD_DIM

    kw = kw_ref[...]
    qw = qw_ref[...]
    kc = kc_ref[...].astype(F32)
    kp = kp_ref[...].astype(F32)
    vc = vc_ref[...].astype(F32)
    vp = vp_ref[...].astype(F32)
    ncol = kc.shape[1] // LANES
    kcn = [_head_rmsnorm(kc[:, p * LANES:(p + 1) * LANES], kw, gmat) for p in range(ncol)]
    kpn = [_head_rmsnorm(kp[:, p * LANES:(p + 1) * LANES], kw, gmat) for p in range(ncol)]
    kn_ref[0] = jnp.concatenate(kcn, axis=1)
    vn_ref[0] = vc

    kcat = [jnp.concatenate([kpn[p], kcn[p]], axis=0) for p in range(ncol)]
    vcat = [jnp.concatenate([vp[:, p * LANES:(p + 1) * LANES], vc[:, p * LANES:(p + 1) * LANES]], axis=0)
            for p in range(ncol)]
    krol = [pltpu.roll(kcat[p], HEAD_DIM, 1) for p in range(ncol)]
    vrol = [pltpu.roll(vcat[p], HEAD_DIM, 1) for p in range(ncol)]

    qi = lax.broadcasted_iota(jnp.int32, (blk, 2 * blk), 0)
    kj = lax.broadcasted_iota(jnp.int32, (blk, 2 * blk), 1)
    dist = blk + qi - kj
    valid = (dist >= 0) & (dist <= WINDOW) & ((kj >= blk) | (c > 0))
    distf = dist.astype(F32)

    q = q_ref[...].astype(F32)
    qn = [_head_rmsnorm(q[:, p * LANES:(p + 1) * LANES], qw, gmat) * (HEAD_DIM ** -0.5)
          for p in range(N_HEADS // HEADS_PER_VREG)]

    for g in range(N_KV_HEADS):
        p, half = divmod(g, HEADS_PER_VREG)
        if half == 0:
            kd = jnp.where(low, kcat[p], krol[p])
            vd = jnp.where(low, vcat[p], vrol[p])
        else:
            kd = jnp.where(low, krol[p], kcat[p])
            vd = jnp.where(low, vrol[p], vcat[p])
        kd = kd.astype(BF16)
        vd = vd.astype(BF16)
        qs = []
        for hl in range(GQA_GROUP):
            h = g * GQA_GROUP + hl
            pc, hh = divmod(h, HEADS_PER_VREG)
            keep = low if hh == 0 else jnp.logical_not(low)
            qs.append(jnp.where(keep, qn[pc], 0.0).astype(BF16))
        qstack = jnp.concatenate(qs, axis=0)
        s_all = lax.dot_general(qstack, kd, (((1,), (1,)), ((), ())), preferred_element_type=F32)
        ps = []
        for hl in range(GQA_GROUP):
            h = g * GQA_GROUP + hl
            s = s_all[hl * blk:(hl + 1) * blk] - _alibi_slope(h) * distf
            s = jnp.where(valid, s, MASK_VALUE)
            sink = sink_ref[h]
            mx = jnp.maximum(jnp.max(s, axis=-1, keepdims=True), sink)
            e = jnp.exp(s - mx)
            den = jnp.sum(e, axis=-1, keepdims=True) + jnp.exp(sink - mx)
            ps.append((e / den).astype(BF16))
        pstack = jnp.concatenate(ps, axis=0)
        o_all = jnp.dot(pstack, vd, preferred_element_type=F32)
        for pair in range(GQA_GROUP // HEADS_PER_VREG):
            pc = g * (GQA_GROUP // HEADS_PER_VREG) + pair
            even = o_all[(2 * pair) * blk:(2 * pair + 1) * blk]
            odd = o_all[(2 * pair + 1) * blk:(2 * pair + 2) * blk]
            o_ref[:, pc * LANES:(pc + 1) * LANES] = jnp.where(low, even, odd).astype(o_ref.dtype)


def _attn_prompt(proj, qw2, kw2, sinks, *, batch, seq):
    nb = seq // WINDOW
    attn_w = N_HEADS * HEAD_DIM
    kv_w = N_KV_HEADS * HEAD_DIM
    kcol = attn_w // kv_w
    vcol = kcol + 1

    def cur(col):
        return lambda b, c: (b * nb + c, col)

    def prev(col):
        return lambda b, c: (jnp.maximum(b * nb + c - 1, 0), col)

    return pl.pallas_call(
        _attn_prompt_kernel,
        grid=(batch, nb),
        in_specs=[
            pl.BlockSpec(memory_space=pltpu.SMEM),
            pl.BlockSpec((WINDOW, attn_w), cur(0)),
            pl.BlockSpec((WINDOW, kv_w), cur(kcol)),
            pl.BlockSpec((WINDOW, kv_w), prev(kcol)),
            pl.BlockSpec((WINDOW, kv_w), cur(vcol)),
            pl.BlockSpec((WINDOW, kv_w), prev(vcol)),
            pl.BlockSpec((1, LANES), lambda b, c: (0, 0)),
            pl.BlockSpec((1, LANES), lambda b, c: (0, 0)),
        ],
        out_specs=[
            pl.BlockSpec((WINDOW, attn_w), lambda b, c: (b * nb + c, 0)),
            pl.BlockSpec((1, WINDOW, kv_w), lambda b, c: (b, 0, 0)),
            pl.BlockSpec((1, WINDOW, kv_w), lambda b, c: (b, 0, 0)),
        ],
        out_shape=[
            jax.ShapeDtypeStruct((batch * seq, attn_w), BF16),
            jax.ShapeDtypeStruct((batch, WINDOW, kv_w), F32),
            jax.ShapeDtypeStruct((batch, WINDOW, kv_w), F32),
        ],
        compiler_params=_params(("parallel", "arbitrary"), 32),
        name="attn_prompt",
    )(sinks, proj, proj, proj, proj, proj, qw2, kw2)


def _layernorm_rows(x, w):
    mu = jnp.mean(x, axis=-1, keepdims=True)
    xc = x - mu
    return xc * lax.rsqrt(jnp.mean(xc * xc, axis=-1, keepdims=True) + EPS) * w


def _sgu_prompt_kernel(gu0_ref, gu1_ref, gv0_ref, gv1_ref, nw_ref, ws_ref, bs_ref, o_ref, st_ref):
    gv = jnp.concatenate([gv0_ref[...], gv1_ref[...]], axis=1).astype(F32)
    vn = _layernorm_rows(jax.nn.gelu(gv), nw_ref[...])
    st_ref[0] = vn
    vb = vn.astype(BF16)
    r = lax.broadcasted_iota(jnp.int32, (CHUNK, CHUNK), 0)
    c = lax.broadcasted_iota(jnp.int32, (CHUNK, CHUNK), 1)
    causal = r >= c
    gw = vb.shape[1] // GM_GROUPS
    half = gu0_ref.shape[1]
    for g in range(GM_GROUPS):
        w = jnp.where(causal, ws_ref[g], 0.0).astype(BF16)
        mix = jnp.dot(w, vb[:, g * gw:(g + 1) * gw], preferred_element_type=F32) + bs_ref[:, g:g + 1]
        src = gu0_ref if g * gw < half else gu1_ref
        off = g * gw - (0 if g * gw < half else half)
        u = jax.nn.gelu(src[:, off:off + gw].astype(F32))
        o_ref[:, g * gw:(g + 1) * gw] = (u * mix).astype(o_ref.dtype)


def _sgu_prompt(proj, nw, ws, bs_t, *, batch, seq, gu_off, gm_w):
    nc = seq // CHUNK
    half = gm_w // 2
    b0 = gu_off // half

    def col(k):
        return lambda b, c: (b * nc + c, b0 + k)

    return pl.pallas_call(
        _sgu_prompt_kernel,
        grid=(batch, nc),
        in_specs=[
            pl.BlockSpec((CHUNK, half), col(0)),
            pl.BlockSpec((CHUNK, half), col(1)),
            pl.BlockSpec((CHUNK, half), col(2)),
            pl.BlockSpec((CHUNK, half), col(3)),
            pl.BlockSpec((1, gm_w), lambda b, c: (0, 0)),
            pl.BlockSpec((GM_GROUPS, CHUNK, CHUNK), lambda b, c: (0, 0, 0)),
            pl.BlockSpec((CHUNK, GM_GROUPS), lambda b, c: (0, 0)),
        ],
        out_specs=[
            pl.BlockSpec((CHUNK, gm_w), lambda b, c: (b * nc + c, 0)),
            pl.BlockSpec((1, CHUNK, gm_w), lambda b, c: (b, 0, 0)),
        ],
        out_shape=[
            jax.ShapeDtypeStruct((batch * seq, gm_w), BF16),
            jax.ShapeDtypeStruct((batch, CHUNK, gm_w), F32),
        ],
        compiler_params=_params(("parallel", "arbitrary"), 32),
        name="sgu_prompt",
    )(proj, proj, proj, proj, nw.reshape(1, gm_w), ws, bs_t)


def _split_dot(x, m):
    hi = x.astype(BF16)
    lo = (x - hi.astype(F32)).astype(BF16)
    return jnp.dot(hi, m, preferred_element_type=F32) + jnp.dot(lo, m, preferred_element_type=F32)


def _attn_sample_kernel(q_ref, kn_ref, vn_ref, kb_ref, vb_ref, qw_ref, kw_ref, sink_ref,
                        wa_ref, wg_ref, wo_ref, o_ref, ko_ref, vo_ref, wab_ref, wgb_ref, wob_ref):
    wab_ref[...] = wa_ref[...].astype(BF16)
    wgb_ref[...] = wg_ref[...].astype(BF16)
    wob_ref[...] = wo_ref[...].astype(BF16)
    nsamp = q_ref.shape[0]
    attn_w = q_ref.shape[1]
    kv_w = kn_ref.shape[1]
    gmat = _head_group_matrix()
    q = q_ref[...].astype(F32)
    kn = kn_ref[...].astype(F32)
    qn = jnp.concatenate(
        [_head_rmsnorm(q[:, p * LANES:(p + 1) * LANES], qw_ref[...], gmat) * (HEAD_DIM ** -0.5)
         for p in range(attn_w // LANES)], axis=1)
    knn = jnp.concatenate(
        [_head_rmsnorm(kn[:, p * LANES:(p + 1) * LANES], kw_ref[...], gmat)
         for p in range(kv_w // LANES)], axis=1)
    vnn = vn_ref[...].astype(F32)

    ec = lax.broadcasted_iota(jnp.int32, (kv_w, attn_w), 0)
    el = lax.broadcasted_iota(jnp.int32, (kv_w, attn_w), 1)
    expand = (((ec >> 6) == (el >> 8)) & ((ec & 63) == (el & 63))).astype(BF16)
    gl = lax.broadcasted_iota(jnp.int32, (attn_w, LANES), 0)
    gh = lax.broadcasted_iota(jnp.int32, (attn_w, LANES), 1)
    hsum = ((gl >> 6) == gh).astype(BF16)
    tl = lax.broadcasted_iota(jnp.int32, (LANES, attn_w), 1)
    th = lax.broadcasted_iota(jnp.int32, (LANES, attn_w), 0)
    hexp = ((tl >> 6) == th).astype(BF16)

    nkeys = WINDOW + SUBLANES
    row = lax.broadcasted_iota(jnp.int32, (nkeys, LANES), 0)
    head = lax.broadcasted_iota(jnp.int32, (nkeys, LANES), 1)
    slope = jnp.exp2(-8.0 * (head + 1).astype(F32) / N_HEADS)
    dist = (WINDOW - row).astype(F32)
    key_ok = row <= WINDOW
    srow = lax.broadcasted_iota(jnp.int32, (WINDOW, kv_w), 0)
    sinks = sink_ref[...]

    for s in range(nsamp):
        kb = kb_ref[s]
        vb = vb_ref[s]
        knew = knn[s:s + 1]
        vnew = vnn[s:s + 1]
        ko_ref[s] = jnp.where(srow == WINDOW - 1, knew, pltpu.roll(kb, WINDOW - 1, 0))
        vo_ref[s] = jnp.where(srow == WINDOW - 1, vnew, pltpu.roll(vb, WINDOW - 1, 0))
        kk = jnp.concatenate([kb, jnp.broadcast_to(knew, (SUBLANES, kv_w))], axis=0)
        vv = jnp.concatenate([vb, jnp.broadcast_to(vnew, (SUBLANES, kv_w))], axis=0)
        kexp = _split_dot(kk, expand)
        vexp = _split_dot(vv, expand)
        sc = _split_dot(kexp * qn[s:s + 1], hsum)
        sc = sc - slope * dist
        sc = jnp.where(key_ok, sc, MASK_VALUE)
        mx = jnp.maximum(jnp.max(sc, axis=0, keepdims=True), sinks)
        e = jnp.exp(sc - mx)
        den = jnp.sum(e, axis=0, keepdims=True) + jnp.exp(sinks - mx)
        pexp = _split_dot(e / den, hexp)
        o_ref[s:s + 1, :] = jnp.sum(pexp * vexp, axis=0, keepdims=True)


def _attn_sample(q, knew, vnew, kbuf, vbuf, qw2, kw2, sinks_row, wa, wg, wo, *, group):
    nb, attn_w = q.shape
    kv_w = knew.shape[1]
    steps = nb // group
    row = lambda width: pl.BlockSpec((group, width), lambda i: (i, 0))
    buf = pl.BlockSpec((group, WINDOW, kv_w), lambda i: (i, 0, 0))
    vec = pl.BlockSpec((1, LANES), lambda i: (0, 0))
    slab = lambda w: pl.BlockSpec((w.shape[0] // steps, w.shape[1]), lambda i: (i, 0))
    return pl.pallas_call(
        _attn_sample_kernel,
        grid=(steps,),
        in_specs=[row(attn_w), row(kv_w), row(kv_w), buf, buf, vec, vec, vec, slab(wa), slab(wg), slab(wo)],
        out_specs=[row(attn_w), buf, buf, slab(wa), slab(wg), slab(wo)],
        out_shape=[
            jax.ShapeDtypeStruct((nb, attn_w), F32),
            jax.ShapeDtypeStruct((nb, WINDOW, kv_w), F32),
            jax.ShapeDtypeStruct((nb, WINDOW, kv_w), F32),
            jax.ShapeDtypeStruct(wa.shape, BF16),
            jax.ShapeDtypeStruct(wg.shape, BF16),
            jax.ShapeDtypeStruct(wo.shape, BF16),
        ],
        compiler_params=_params(("parallel",), 48),
        name="attn_sample",
    )(q, knew, vnew, kbuf, vbuf, qw2, kw2, sinks_row, wa, wg, wo)


def _sgu_sample_kernel(gu_ref, gv_ref, nw_ref, w0_ref, b0_ref, o_ref, st_ref):
    vn = _layernorm_rows(jax.nn.gelu(gv_ref[...].astype(F32)), nw_ref[...])
    st_ref[...] = vn
    mix = w0_ref[...] * vn + b0_ref[...]
    o_ref[...] = jax.nn.gelu(gu_ref[...].astype(F32)) * mix


def _sgu_sample(gu, gv, nw, w0_row, b0_row):
    nb, gm_w = gu.shape
    full = pl.BlockSpec((nb, gm_w), lambda i: (0, 0))
    vec = pl.BlockSpec((1, gm_w), lambda i: (0, 0))
    return pl.pallas_call(
        _sgu_sample_kernel,
        grid=(1,),
        in_specs=[full, full, vec, vec, vec],
        out_specs=[full, full],
        out_shape=[jax.ShapeDtypeStruct((nb, gm_w), F32), jax.ShapeDtypeStruct((nb, gm_w), F32)],
        name="sgu_sample",
    )(gu, gv, nw.reshape(1, gm_w), w0_row, b0_row)


def _merge_out_kernel(a_ref, s_ref, ga_ref, gb_ref, x_ref, wa_ref, wg_ref, wo_ref, nw_ref,
                      x1_ref, xn1_ref):
    a = jnp.dot(a_ref[...].astype(BF16), wa_ref[...], preferred_element_type=F32)
    m = jnp.dot(s_ref[...].astype(BF16), wg_ref[...], preferred_element_type=F32)
    merged = (gb_ref[...].astype(F32) * a + ga_ref[...].astype(F32) * m).astype(BF16)
    x1 = x_ref[...] + jnp.dot(merged, wo_ref[...], preferred_element_type=F32)
    x1_ref[...] = x1
    xn1_ref[...] = _rmsnorm_rows(x1, nw_ref[...]).astype(BF16)


def _merge_out(attn, sgu, gates, x, wa, wg, wo, nw, *, tm):
    m, ka = attn.shape
    kg = sgu.shape[1]
    d = wo.shape[1]
    rows = lambda width, col=0: pl.BlockSpec((tm, width), lambda i: (i, col))
    resident = lambda shape: pl.BlockSpec(shape, lambda i: (0, 0), pipeline_mode=pl.Buffered(1))
    return pl.pallas_call(
        _merge_out_kernel,
        grid=(m // tm,),
        in_specs=[
            rows(ka), rows(kg), rows(d, 0), rows(d, 1), rows(d),
            resident((ka, d)), resident((kg, d)), resident((d, d)),
            pl.BlockSpec((1, d), lambda i: (0, 0)),
        ],
        out_specs=[rows(d), rows(d)],
        out_shape=[jax.ShapeDtypeStruct((m, d), F32), jax.ShapeDtypeStruct((m, d), BF16)],
        compiler_params=_params(("parallel",), 48),
        name="merge_out",
    )(attn, sgu, gates, gates, x, wa, wg, wo, nw.reshape(1, d))


PAD = SUBLANES
FFN_ROW_BLOCKS = 4


def _ffn_step(t, xn_ref, wug_ref, wuu_ref, wd_ref, cwg_ref, cwu_ref, cbg_ref, cbu_ref,
              csg_ref, csu_ref, h_new, h_old, carry_ref, acc_ref, *, nf, tiles_per_seq):
    tm = xn_ref.shape[0]
    tf = wug_ref.shape[1]
    rb = tm // FFN_ROW_BLOCKS
    tp = jnp.maximum(t - 1, 0)
    ip = tp // nf
    jp = tp % nf
    first = (ip % tiles_per_seq) == 0
    h_old[0:PAD, :] = jnp.where(first, 0.0, carry_ref[jp])
    carry_ref[jp] = h_old[tm:tm + PAD, :]
    cw = jnp.concatenate([cwg_ref[...], cwu_ref[...]], axis=1)
    cb = jnp.concatenate([cbg_ref[...], cbu_ref[...]], axis=1)
    wug = wug_ref[...]
    wuu = wuu_ref[...]
    wd = wd_ref[...]
    for r in range(FFN_ROW_BLOCKS):
        lo = r * rb
        xr = xn_ref[lo:lo + rb, :]
        h_new[PAD + lo:PAD + lo + rb, 0:tf] = jnp.dot(xr, wug, preferred_element_type=F32)
        h_new[PAD + lo:PAD + lo + rb, tf:2 * tf] = jnp.dot(xr, wuu, preferred_element_type=F32)
        hc = (cw[0:1] * h_old[PAD - 2 + lo:PAD - 2 + lo + rb, :]
              + cw[1:2] * h_old[PAD - 1 + lo:PAD - 1 + lo + rb, :]
              + cw[2:3] * h_old[PAD + lo:PAD + lo + rb, :] + cb)
        act = (jax.nn.silu(hc[:, 0:tf]) * hc[:, tf:2 * tf]).astype(BF16)
        acc_ref[lo:lo + rb, :] += jnp.dot(act, wd, preferred_element_type=F32)
    csg_ref[0] = h_old[PAD + tm - 2:PAD + tm, 0:tf]
    csu_ref[0] = h_old[PAD + tm - 2:PAD + tm, tf:2 * tf]


def _ffn_kernel(xn_ref, wug_ref, wuu_ref, wd_ref, cwg_ref, cwu_ref, cbg_ref, cbu_ref,
                o_ref, csg_ref, csu_ref, ha_ref, hb_ref, carry_ref, acc_ref, *, nf, tiles_per_seq):
    t = pl.program_id(0)
    step = functools.partial(_ffn_step, t, xn_ref, wug_ref, wuu_ref, wd_ref, cwg_ref, cwu_ref,
                             cbg_ref, cbu_ref, csg_ref, csu_ref, carry_ref=carry_ref, acc_ref=acc_ref,
                             nf=nf, tiles_per_seq=tiles_per_seq)

    @pl.when(t == 0)
    def _():
        hb_ref[...] = jnp.zeros_like(hb_ref)
        carry_ref[...] = jnp.zeros_like(carry_ref)

    @pl.when(jnp.maximum(t - 1, 0) % nf == 0)
    def _():
        acc_ref[...] = jnp.zeros_like(acc_ref)

    @pl.when(t % 2 == 0)
    def _():
        step(h_new=ha_ref, h_old=hb_ref)

    @pl.when(t % 2 == 1)
    def _():
        step(h_new=hb_ref, h_old=ha_ref)

    @pl.when(jnp.maximum(t - 1, 0) % nf == nf - 1)
    def _():
        o_ref[...] = acc_ref[...].astype(o_ref.dtype)


def _ffn_prompt(xn, w_up, w_down, cw, cb, *, batch, seq, tm):
    m, k = xn.shape
    d_ff, d = w_down.shape
    tf = w_up.shape[2]
    nf = d_ff // tf
    nm = m // tm
    tiles_per_seq = seq // tm
    prev = lambda t: jnp.maximum(t - 1, 0)
    up_tile = lambda off: (lambda t: (off + t % nf, 0, 0))
    dn_col = lambda off: (lambda t: (0, off + prev(t) % nf))
    state = lambda t: (prev(t) // nf, 0, prev(t) % nf)
    delta, csg, csu = pl.pallas_call(
        functools.partial(_ffn_kernel, nf=nf, tiles_per_seq=tiles_per_seq),
        grid=(nm * nf + 1,),
        in_specs=[
            pl.BlockSpec((tm, k), lambda t: (jnp.minimum(t // nf, nm - 1), 0)),
            pl.BlockSpec((None, k, tf), up_tile(0)),
            pl.BlockSpec((None, k, tf), up_tile(nf)),
            pl.BlockSpec((tf, d), lambda t: (prev(t) % nf, 0)),
            pl.BlockSpec((3, tf), dn_col(0)),
            pl.BlockSpec((3, tf), dn_col(nf)),
            pl.BlockSpec((1, tf), dn_col(0)),
            pl.BlockSpec((1, tf), dn_col(nf)),
        ],
        out_specs=[
            pl.BlockSpec((tm, d), lambda t: (prev(t) // nf, 0)),
            pl.BlockSpec((1, 2, tf), state),
            pl.BlockSpec((1, 2, tf), state),
        ],
        out_shape=[
            jax.ShapeDtypeStruct((m, d), BF16),
            jax.ShapeDtypeStruct((nm, 2, d_ff), F32),
            jax.ShapeDtypeStruct((nm, 2, d_ff), F32),
        ],
        scratch_shapes=[
            pltpu.VMEM((tm + PAD, 2 * tf), F32),
            pltpu.VMEM((tm + PAD, 2 * tf), F32),
            pltpu.VMEM((nf, PAD, 2 * tf), F32),
            pltpu.VMEM((tm, d), F32),
        ],
        compiler_params=_params(("arbitrary",), 58),
        name="ffn_prompt",
    )(xn, w_up, w_up, w_down, cw, cw, cb.reshape(1, -1), cb.reshape(1, -1))
    tails = jnp.concatenate([csg, csu], axis=-1).reshape(batch, tiles_per_seq, 2, 2 * d_ff)
    return delta, tails[:, -1]


def _ffn_down_sample_kernel(hg_ref, hu_ref, b0g_ref, b0u_ref, b1g_ref, b1u_ref, cwg_ref, cwu_ref,
                            cbg_ref, cbu_ref, wd_ref, x_ref, o_ref, wdb_ref):
    @pl.when(pl.program_id(0) == 0)
    def _():
        o_ref[...] = x_ref[...]

    def conv(h_ref, b0_ref, b1_ref, cw_ref, cb_ref):
        cw = cw_ref[...]
        return cw[0:1] * b0_ref[...] + cw[1:2] * b1_ref[...] + cw[2:3] * h_ref[...] + cb_ref[...]

    hcg = conv(hg_ref, b0g_ref, b1g_ref, cwg_ref, cbg_ref)
    hcu = conv(hu_ref, b0u_ref, b1u_ref, cwu_ref, cbu_ref)
    act = (jax.nn.silu(hcg) * hcu).astype(BF16)
    wdb = wd_ref[...].astype(BF16)
    wdb_ref[...] = wdb
    o_ref[...] += jnp.dot(act, wdb, preferred_element_type=F32)


def _ffn_down_sample(h, state2, cw, cb, wd, x, *, tf):
    nb, two_f = h.shape
    d_ff = two_f // 2
    nf = d_ff // tf
    d = wd.shape[1]
    colblk = lambda off: pl.BlockSpec((nb, tf), lambda j: (0, j + off))
    return pl.pallas_call(
        _ffn_down_sample_kernel,
        grid=(nf,),
        in_specs=[
            colblk(0), colblk(nf),
            colblk(0), colblk(nf), colblk(2 * nf), colblk(3 * nf),
            pl.BlockSpec((3, tf), lambda j: (0, j)),
            pl.BlockSpec((3, tf), lambda j: (0, j + nf)),
            pl.BlockSpec((1, tf), lambda j: (0, j)),
            pl.BlockSpec((1, tf), lambda j: (0, j + nf)),
            pl.BlockSpec((tf, d), lambda j: (j, 0)),
            pl.BlockSpec((nb, d), lambda j: (0, 0)),
        ],
        out_specs=[pl.BlockSpec((nb, d), lambda j: (0, 0)), pl.BlockSpec((tf, d), lambda j: (j, 0))],
        out_shape=[jax.ShapeDtypeStruct((nb, d), F32), jax.ShapeDtypeStruct((d_ff, d), BF16)],
        compiler_params=_params(("arbitrary",), 32),
        name="ffn_down_sample",
    )(h, h, state2, state2, state2, state2, cw, cw, cb.reshape(1, two_f), cb.reshape(1, two_f), wd, x)


def _ple_kernel(*refs, has_delta, emit_w):
    refs = list(refs)
    x_ref = refs.pop(0)
    d_ref = refs.pop(0) if has_delta else None
    nw_ref, w_ref, p_ref, wp_ref, o_ref = refs[:5]
    xn_ref = refs[-1]
    j = pl.program_id(1)
    tn = o_ref.shape[1]

    @pl.when(j == 0)
    def _():
        _rmsnorm_to(x_ref, nw_ref, xn_ref, add_ref=d_ref)

    wb = w_ref[...].astype(BF16)
    wpb = wp_ref[...].astype(BF16)
    if emit_w:
        wb_ref, wpb_ref = refs[5:7]
        wb_ref[...] = wb
        wpb_ref[...] = wpb
    gate = jax.nn.sigmoid(jnp.dot(xn_ref[...], wb, preferred_element_type=F32))
    emb = jnp.dot(p_ref[...].astype(BF16), wpb, preferred_element_type=F32)
    o_ref[...] = gate * emb
    for jj in range(x_ref.shape[1] // tn):
        @pl.when(j == jj)
        def _():
            res = x_ref[:, jj * tn:(jj + 1) * tn]
            if has_delta:
                res = res + d_ref[:, jj * tn:(jj + 1) * tn].astype(F32)
            o_ref[...] += res


def _ple_rows_kernel(x_ref, d_ref, nw_ref, w_ref, p_ref, wp_ref, o_ref):
    x2 = x_ref[...] + d_ref[...].astype(F32)
    xn = _rmsnorm_rows(x2, nw_ref[...]).astype(BF16)
    gate = jax.nn.sigmoid(jnp.dot(xn, w_ref[...], preferred_element_type=F32))
    emb = jnp.dot(p_ref[...].astype(BF16), wp_ref[...], preferred_element_type=F32)
    o_ref[...] = x2 + gate * emb


def _ple_rows(x, delta, nw, w, p, wp, *, tm):
    m, d = x.shape
    kp = p.shape[1]
    rows = lambda width: pl.BlockSpec((tm, width), lambda i: (i, 0))
    resident = lambda shape: pl.BlockSpec(shape, lambda i: (0, 0), pipeline_mode=pl.Buffered(1))
    return pl.pallas_call(
        _ple_rows_kernel,
        grid=(m // tm,),
        in_specs=[rows(d), rows(d), pl.BlockSpec((1, d), lambda i: (0, 0)), resident((d, d)),
                  rows(kp), resident((kp, d))],
        out_specs=rows(d),
        out_shape=jax.ShapeDtypeStruct((m, d), F32),
        compiler_params=_params(("parallel",), 56),
        name="ple_prompt",
    )(x, delta, nw.reshape(1, d), w, p, wp)


def _ple(x, delta, nw, w, p, wp, *, tm, tn, name, emit_w=False):
    m, k = x.shape
    n = w.shape[1]
    kp = p.shape[1]
    assert not emit_w or m == tm
    out_specs = [pl.BlockSpec((tm, tn), lambda i, j: (i, j))]
    out_shape = [jax.ShapeDtypeStruct((m, n), F32)]
    if emit_w:
        out_specs += [pl.BlockSpec((k, tn), lambda i, j: (0, j)), pl.BlockSpec((kp, tn), lambda i, j: (0, j))]
        out_shape += [jax.ShapeDtypeStruct((k, n), BF16), jax.ShapeDtypeStruct((kp, n), BF16)]
    row_full = pl.BlockSpec((tm, k), lambda i, j: (i, 0))
    in_specs = [row_full] + ([row_full] if delta is not None else []) + [
        pl.BlockSpec((1, k), lambda i, j: (0, 0)),
        pl.BlockSpec((k, tn), lambda i, j: (0, j)),
        pl.BlockSpec((tm, kp), lambda i, j: (i, 0)),
        pl.BlockSpec((kp, tn), lambda i, j: (0, j)),
    ]
    args = [x] + ([delta] if delta is not None else []) + [nw.reshape(1, k), w, p, wp]
    outs = pl.pallas_call(
        functools.partial(_ple_kernel, has_delta=delta is not None, emit_w=emit_w),
        grid=(m // tm, n // tn),
        in_specs=in_specs,
        out_specs=out_specs,
        out_shape=out_shape,
        scratch_shapes=[pltpu.VMEM((tm, k), BF16)],
        compiler_params=_params(("parallel", "arbitrary"), 56),
        name=name,
    )(*args)
    return outs if emit_w else outs[0]


def _layer_weights(i, attn_norm_w, w_in, q_norm_w, k_norm_w, attn_sinks, sgu_norm_w, sgu_w, sgu_b,
                   w_br_attn, w_br_gm, w_gate, b_gate, w_out, ffn_norm_w, w_up, conv_w, conv_b,
                   w_down, ple_norm_w, w_ple_gate, w_ple_proj):
    sinks = attn_sinks[i]
    gm_group_w = w_br_gm.shape[1] // GM_GROUPS
    return dict(
        attn_norm_w=attn_norm_w[i], w_in=w_in[i],
        qw2=jnp.tile(q_norm_w[i], HEADS_PER_VREG).reshape(1, LANES),
        kw2=jnp.tile(k_norm_w[i], HEADS_PER_VREG).reshape(1, LANES),
        sinks=sinks,
        sinks_row=jnp.pad(sinks, (0, LANES - N_HEADS)).reshape(1, LANES),
        sgu_norm_w=sgu_norm_w[i], sgu_w=sgu_w[i], sgu_b_t=jnp.transpose(sgu_b[i]),
        sgu_w0_row=jnp.repeat(sgu_w[i][:, 0, 0], gm_group_w).reshape(1, -1),
        sgu_b0_row=jnp.repeat(sgu_b[i][:, 0], gm_group_w).reshape(1, -1),
        w_br_attn=w_br_attn[i], w_br_gm=w_br_gm[i],
        w_gate=w_gate[i], b_gate=b_gate[i], w_out=w_out[i],
        ffn_norm_w=ffn_norm_w[i], w_up=w_up[i], conv_w=conv_w[i], conv_b=conv_b[i],
        w_down=w_down[i], ple_norm_w=ple_norm_w[i], w_ple_gate=w_ple_gate[i], w_ple_proj=w_ple_proj[i],
    )


def _prompt_layer(x, p, lw, wb):
    batch, seq, d = x.shape
    x2d = x.reshape(batch * seq, d)
    attn_w = N_HEADS * HEAD_DIM
    kv_w = N_KV_HEADS * HEAD_DIM
    gm_w = lw["w_br_gm"].shape[0]
    proj, gates = _in_proj(x2d, lw["attn_norm_w"], wb["w_in"], wb["w_gate"], lw["b_gate"], tm=1024)
    attn, k_new, v_new = _attn_prompt(proj, lw["qw2"], lw["kw2"], lw["sinks"], batch=batch, seq=seq)
    sgu, sgu_state = _sgu_prompt(proj, lw["sgu_norm_w"], lw["sgu_w"], lw["sgu_b_t"], batch=batch, seq=seq,
                                 gu_off=attn_w + 2 * kv_w, gm_w=gm_w)
    x1, xn1 = _merge_out(attn, sgu, gates, x2d, wb["w_br_attn"], wb["w_br_gm"], wb["w_out"],
                         lw["ffn_norm_w"], tm=256)
    delta, conv_state = _ffn_prompt(xn1, wb["w_up"], wb["w_down"], lw["conv_w"], lw["conv_b"],
                                    batch=batch, seq=seq, tm=1024)
    x3 = _ple_rows(x1, delta, lw["ple_norm_w"], wb["w_ple_gate"], p.reshape(batch * seq, -1),
                   wb["w_ple_proj"], tm=512)
    return (x3.reshape(batch, seq, d), k_new.reshape(batch, WINDOW, N_KV_HEADS, HEAD_DIM),
            v_new.reshape(batch, WINDOW, N_KV_HEADS, HEAD_DIM), sgu_state, conv_state)


def _sample_layer(x, p, k_buf, v_buf, conv_buf, lw):
    nb, t, d = x.shape
    x2d = x.reshape(nb * t, d)
    attn_w = N_HEADS * HEAD_DIM
    kv_w = N_KV_HEADS * HEAD_DIM
    gm_w = lw["w_br_gm"].shape[0]
    wb = {}
    proj, wb["w_in"] = _norm_mm(x2d, lw["attn_norm_w"], lw["w_in"], None, act=None,
                                tn=512, tile=512, name="proj_sample")
    gates, wb["w_gate"] = _norm_mm(x2d, lw["attn_norm_w"], lw["w_gate"], lw["b_gate"], act="sigmoid",
                                   tn=1024, tile=512, name="gates_sample")
    q = proj[:, :attn_w]
    k = proj[:, attn_w:attn_w + kv_w]
    v = proj[:, attn_w + kv_w:attn_w + 2 * kv_w]
    gu = proj[:, attn_w + 2 * kv_w:attn_w + 2 * kv_w + gm_w]
    gv = proj[:, attn_w + 2 * kv_w + gm_w:]
    attn, k_new, v_new, wb["w_br_attn"], wb["w_br_gm"], wb["w_out"] = _attn_sample(
        q, k, v, k_buf.reshape(nb, WINDOW, kv_w), v_buf.reshape(nb, WINDOW, kv_w),
        lw["qw2"], lw["kw2"], lw["sinks_row"], lw["w_br_attn"], lw["w_br_gm"], lw["w_out"], group=8)
    sgu, sgu_state = _sgu_sample(gu, gv, lw["sgu_norm_w"], lw["sgu_w0_row"], lw["sgu_b0_row"])
    x1, _ = _merge_out(attn, sgu, gates, x2d, wb["w_br_attn"], wb["w_br_gm"], wb["w_out"],
                       lw["ffn_norm_w"], tm=nb)
    h, wb["w_up"] = _norm_mm(x1, lw["ffn_norm_w"], lw["w_up"], None, act=None, tn=1024, tile=512,
                             name="ffn_up_sample")
    x2, wb["w_down"] = _ffn_down_sample(h, conv_buf.reshape(nb, -1), lw["conv_w"], lw["conv_b"], lw["w_down"],
                                        x1, tf=512)
    x3, wb["w_ple_gate"], wb["w_ple_proj"] = _ple(
        x2, None, lw["ple_norm_w"], lw["w_ple_gate"], p.reshape(nb * t, -1), lw["w_ple_proj"],
        tm=nb, tn=1024, name="ple_sample", emit_w=True)
    conv_state = jnp.stack([conv_buf[:, 1, :], h], axis=1)
    return (x3.reshape(nb, t, d), k_new.reshape(nb, WINDOW, N_KV_HEADS, HEAD_DIM),
            v_new.reshape(nb, WINDOW, N_KV_HEADS, HEAD_DIM), sgu_state.reshape(nb, t, gm_w), conv_state), wb


def kernel(x_prompt, x_sample, p_prompt, p_sample, state_attn_k, state_attn_v, state_conv, attn_norm_w, w_in, q_norm_w, k_norm_w, attn_sinks, sgu_norm_w, sgu_w, sgu_b, w_br_attn, w_br_gm, w_gate, b_gate, w_out, ffn_norm_w, w_up, conv_w, conv_b, w_down, ple_norm_w, w_ple_gate, w_ple_proj):
    depth = w_in.shape[0]
    xp, xs = x_prompt, x_sample
    outs = [[] for _ in range(8)]
    for i in range(depth):
        lw = _layer_weights(i, attn_norm_w, w_in, q_norm_w, k_norm_w, attn_sinks, sgu_norm_w, sgu_w, sgu_b,
                            w_br_attn, w_br_gm, w_gate, b_gate, w_out, ffn_norm_w, w_up, conv_w, conv_b,
                            w_down, ple_norm_w, w_ple_gate, w_ple_proj)
        (xs, ks, vs, gs, cs), wb = _sample_layer(xs, p_sample[i], state_attn_k[i], state_attn_v[i],
                                                 state_conv[i], lw)
        xp, kp, vp, gp, cp = _prompt_layer(xp, p_prompt[i], lw, wb)
        for lst, val in zip(outs, (kp, vp, ks, vs, gp, gs, cp, cs)):
            lst.append(val)
    return (xp, xs) + tuple(jnp.stack(lst) for lst in outs)
```

```python
import functools

import jax
import jax.numpy as jnp
from jax import lax
from jax.experimental import pallas as pl
from jax.experimental.pallas import tpu as pltpu

F32 = jnp.float32
BF16 = jnp.bfloat16

HEAD_DIM = 64
N_HEADS = 16
N_KV_HEADS = 4
GQA_GROUP = N_HEADS // N_KV_HEADS
WINDOW = 128
CHUNK = 128
GM_GROUPS = 4
EPS = 1e-6
MASK_VALUE = -1e30
LANES = 128
SUBLANES = 8
HEADS_PER_VREG = LANES // HEAD_DIM
MIB = 1 << 20


def _alibi_slope(h):
    return float(2.0 ** (-8.0 * (h + 1) / N_HEADS))


def _params(semantics, vmem_mib, flags=None):
    return pltpu.CompilerParams(dimension_semantics=semantics, vmem_limit_bytes=vmem_mib * MIB, flags=flags)


def _rmsnorm_rows(x, w):
    ms = jnp.mean(x * x, axis=-1, keepdims=True)
    return x * lax.rsqrt(ms + EPS) * w


NORM_ROWS = 256


def _rmsnorm_to(x_ref, nw_ref, xn_ref, add_ref=None):
    tm = x_ref.shape[0]
    rows = min(NORM_ROWS, tm)

    def body(r, carry):
        sl = pl.ds(pl.multiple_of(r * rows, rows), rows)
        x = x_ref[sl, :]
        if add_ref is not None:
            x = x + add_ref[sl, :].astype(F32)
        xn_ref[sl, :] = _rmsnorm_rows(x, nw_ref[...]).astype(BF16)
        return carry

    lax.fori_loop(0, tm // rows, body, 0)


def _norm_mm_kernel(*refs, has_bias, act):
    if has_bias:
        x_ref, nw_ref, w_ref, b_ref, o_ref, wb_ref, xn_ref = refs
    else:
        x_ref, nw_ref, w_ref, o_ref, wb_ref, xn_ref = refs

    @pl.when(pl.program_id(0) == 0)
    def _():
        _rmsnorm_to(x_ref, nw_ref, xn_ref)

    wb = w_ref[...].astype(BF16)
    tile = wb_ref.shape[2]
    for c in range(wb_ref.shape[0]):
        wb_ref[c] = wb[:, c * tile:(c + 1) * tile]
    acc = jnp.dot(xn_ref[...], wb, preferred_element_type=F32)
    if has_bias:
        acc = acc + b_ref[...]
    if act == "sigmoid":
        acc = jax.nn.sigmoid(acc)
    o_ref[...] = acc.astype(o_ref.dtype)


def _norm_mm(x, nw, w, bias, *, act, tn, tile, name):
    m, k = x.shape
    n = w.shape[1]
    per_step = tn // tile
    in_specs = [
        pl.BlockSpec((m, k), lambda j: (0, 0)),
        pl.BlockSpec((1, k), lambda j: (0, 0)),
        pl.BlockSpec((k, tn), lambda j: (0, j)),
    ]
    args = [x, nw.reshape(1, k), w]
    if bias is not None:
        in_specs.append(pl.BlockSpec((1, tn), lambda j: (0, j)))
        args.append(bias.reshape(1, n))
    return pl.pallas_call(
        functools.partial(_norm_mm_kernel, has_bias=bias is not None, act=act),
        grid=(n // tn,),
        in_specs=in_specs,
        out_specs=[pl.BlockSpec((m, tn), lambda j: (0, j)),
                   pl.BlockSpec((per_step, k, tile), lambda j: (j, 0, 0))],
        out_shape=[jax.ShapeDtypeStruct((m, n), F32), jax.ShapeDtypeStruct((n // tile, k, tile), BF16)],
        scratch_shapes=[pltpu.VMEM((m, k), BF16)],
        compiler_params=_params(("arbitrary",), 56),
        name=name,
    )(*args)


GATE_TILES_PER_STEP = 2


def _in_proj_kernel(x_ref, nw_ref, win_ref, wg_ref, bg_ref, proj_ref, gates_ref, xn_ref, *, n_in):
    j = pl.program_id(1)

    @pl.when(j == 0)
    def _():
        _rmsnorm_to(x_ref, nw_ref, xn_ref)

    @pl.when(j < n_in)
    def _():
        acc = jnp.dot(xn_ref[...], win_ref[...].astype(BF16), preferred_element_type=F32)
        proj_ref[...] = acc.astype(proj_ref.dtype)

    @pl.when(j >= n_in)
    def _():
        tn = wg_ref.shape[2]
        for c in range(wg_ref.shape[0]):
            cols = slice(c * tn, (c + 1) * tn)
            acc = jnp.dot(xn_ref[...], wg_ref[c], preferred_element_type=F32)
            gates_ref[:, cols] = jax.nn.sigmoid(acc + bg_ref[:, cols]).astype(gates_ref.dtype)


def _in_proj(x, nw, w_in, w_gate, b_gate, *, tm):
    m, k = x.shape
    n_in, _, tn = w_in.shape
    gt = GATE_TILES_PER_STEP
    n_g = w_gate.shape[0] // gt
    in_tile = lambda i, j: (jnp.minimum(j, n_in - 1), 0, 0)
    g_tile = lambda i, j: (jnp.maximum(j - n_in, 0), 0, 0)
    return pl.pallas_call(
        functools.partial(_in_proj_kernel, n_in=n_in),
        grid=(m // tm, n_in + n_g),
        in_specs=[
            pl.BlockSpec((tm, k), lambda i, j: (i, 0)),
            pl.BlockSpec((1, k), lambda i, j: (0, 0)),
            pl.BlockSpec((None, k, tn), in_tile),
            pl.BlockSpec((gt, k, tn), g_tile),
            pl.BlockSpec((1, gt * tn), lambda i, j: (0, jnp.maximum(j - n_in, 0))),
        ],
        out_specs=[
            pl.BlockSpec((tm, tn), lambda i, j: (i, jnp.minimum(j, n_in - 1))),
            pl.BlockSpec((tm, gt * tn), lambda i, j: (i, jnp.maximum(j - n_in, 0))),
        ],
        out_shape=[
            jax.ShapeDtypeStruct((m, n_in * tn), BF16),
            jax.ShapeDtypeStruct((m, n_g * gt * tn), BF16),
        ],
        scratch_shapes=[pltpu.VMEM((tm, k), BF16)],
        compiler_params=_params(("arbitrary", "arbitrary"), 56),
        name="in_proj",
    )(x, nw.reshape(1, k), w_in, w_gate, b_gate.reshape(1, -1))


def _head_group_matrix():
    r = lax.broadcasted_iota(jnp.int32, (LANES, LANES), 0)
    c = lax.broadcasted_iota(jnp.int32, (LANES, LANES), 1)
    return ((r >> 6) == (c >> 6)).astype(BF16)


def _head_rmsnorm(xcol, w, gmat):
    ss = jnp.dot((xcol * xcol).astype(BF16), gmat, preferred_element_type=F32)
    return xcol * lax.rsqrt(ss * (1.0 / HEAD_DIM) + EPS) * w


def _attn_prompt_kernel(sink_ref, q_ref, kc_ref, kp_ref, vc_ref, vp_ref, qw_ref, kw_ref,
                        o_ref, kn_ref, vn_ref):
    c = pl.program_id(1)
    blk = WINDOW
    gmat = _head_group_matrix()
    lane = lax.broadcasted_iota(jnp.int32, (1, LANES), 1)
    low = lane < HEAD_DIM

    kw = kw_ref[...]
    qw = qw_ref[...]
    kc = kc_ref[...].astype(F32)
    kp = kp_ref[...].astype(F32)
    vc = vc_ref[...].astype(F32)
    vp = vp_ref[...].astype(F32)
    ncol = kc.shape[1] // LANES
    kcn = [_head_rmsnorm(kc[:, p * LANES:(p + 1) * LANES], kw, gmat) for p in range(ncol)]
    kpn = [_head_rmsnorm(kp[:, p * LANES:(p + 1) * LANES], kw, gmat) for p in range(ncol)]
    kn_ref[0] = jnp.concatenate(kcn, axis=1)
    vn_ref[0] = vc

    kcat = [jnp.concatenate([kpn[p], kcn[p]], axis=0) for p in range(ncol)]
    vcat = [jnp.concatenate([vp[:, p * LANES:(p + 1) * LANES], vc[:, p * LANES:(p + 1) * LANES]], axis=0)
            for p in range(ncol)]
    krol = [pltpu.roll(kcat[p], HEAD_DIM, 1) for p in range(ncol)]
    vrol = [pltpu.roll(vcat[p], HEAD_DIM, 1) for p in range(ncol)]

    qi = lax.broadcasted_iota(jnp.int32, (blk, 2 * blk), 0)
    kj = lax.broadcasted_iota(jnp.int32, (blk, 2 * blk), 1)
    dist = blk + qi - kj
    valid = (dist >= 0) & (dist <= WINDOW) & ((kj >= blk) | (c > 0))
    distf = dist.astype(F32)

    q = q_ref[...].astype(F32)
    qn = [_head_rmsnorm(q[:, p * LANES:(p + 1) * LANES], qw, gmat) * (HEAD_DIM ** -0.5)
          for p in range(N_HEADS // HEADS_PER_VREG)]

    for g in range(N_KV_HEADS):
        p, half = divmod(g, HEADS_PER_VREG)
        if half == 0:
            kd = jnp.where(low, kcat[p], krol[p])
            vd = jnp.where(low, vcat[p], vrol[p])
        else:
            kd = jnp.where(low, krol[p], kcat[p])
            vd = jnp.where(low, vrol[p], vcat[p])
        kd = kd.astype(BF16)
        vd = vd.astype(BF16)
        qs = []
        for hl in range(GQA_GROUP):
            h = g * GQA_GROUP + hl
            pc, hh = divmod(h, HEADS_PER_VREG)
            keep = low if hh == 0 else jnp.logical_not(low)
            qs.append(jnp.where(keep, qn[pc], 0.0).astype(BF16))
        qstack = jnp.concatenate(qs, axis=0)
        s_all = lax.dot_general(qstack, kd, (((1,), (1,)), ((), ())), preferred_element_type=F32)
        ps = []
        for hl in range(GQA_GROUP):
            h = g * GQA_GROUP + hl
            s = s_all[hl * blk:(hl + 1) * blk] - _alibi_slope(h) * distf
            s = jnp.where(valid, s, MASK_VALUE)
            sink = sink_ref[h]
            mx = jnp.maximum(jnp.max(s, axis=-1, keepdims=True), sink)
            e = jnp.exp(s - mx)
            den = jnp.sum(e, axis=-1, keepdims=True) + jnp.exp(sink - mx)
            ps.append((e / den).astype(BF16))
        pstack = jnp.concatenate(ps, axis=0)
        o_all = jnp.dot(pstack, vd, preferred_element_type=F32)
        for pair in range(GQA_GROUP // HEADS_PER_VREG):
            pc = g * (GQA_GROUP // HEADS_PER_VREG) + pair
            even = o_all[(2 * pair) * blk:(2 * pair + 1) * blk]
            odd = o_all[(2 * pair + 1) * blk:(2 * pair + 2) * blk]
            o_ref[:, pc * LANES:(pc + 1) * LANES] = jnp.where(low, even, odd).astype(o_ref.dtype)


def _attn_prompt(proj, qw2, kw2, sinks, *, batch, seq):
    nb = seq // WINDOW
    attn_w = N_HEADS * HEAD_DIM
    kv_w = N_KV_HEADS * HEAD_DIM
    kcol = attn_w // kv_w
    vcol = kcol + 1

    def cur(col):
        return lambda b, c: (b * nb + c, col)

    def prev(col):
        return lambda b, c: (jnp.maximum(b * nb + c - 1, 0), col)

    return pl.pallas_call(
        _attn_prompt_kernel,
        grid=(batch, nb),
        in_specs=[
            pl.BlockSpec(memory_space=pltpu.SMEM),
            pl.BlockSpec((WINDOW, attn_w), cur(0)),
            pl.BlockSpec((WINDOW, kv_w), cur(kcol)),
            pl.BlockSpec((WINDOW, kv_w), prev(kcol)),
            pl.BlockSpec((WINDOW, kv_w), cur(vcol)),
            pl.BlockSpec((WINDOW, kv_w), prev(vcol)),
            pl.BlockSpec((1, LANES), lambda b, c: (0, 0)),
            pl.BlockSpec((1, LANES), lambda b, c: (0, 0)),
        ],
        out_specs=[
            pl.BlockSpec((WINDOW, attn_w), lambda b, c: (b * nb + c, 0)),
            pl.BlockSpec((1, WINDOW, kv_w), lambda b, c: (b, 0, 0)),
            pl.BlockSpec((1, WINDOW, kv_w), lambda b, c: (b, 0, 0)),
        ],
        out_shape=[
            jax.ShapeDtypeStruct((batch * seq, attn_w), BF16),
            jax.ShapeDtypeStruct((batch, WINDOW, kv_w), F32),
            jax.ShapeDtypeStruct((batch, WINDOW, kv_w), F32),
        ],
        compiler_params=_params(("parallel", "arbitrary"), 32),
        name="attn_prompt",
    )(sinks, proj, proj, proj, proj, proj, qw2, kw2)


def _layernorm_rows(x, w):
    mu = jnp.mean(x, axis=-1, keepdims=True)
    xc = x - mu
    return xc * lax.rsqrt(jnp.mean(xc * xc, axis=-1, keepdims=True) + EPS) * w


GELU_C = 0.7978845608028654
GELU_A = 0.044715


def _gelu_tanh(x):
    return x * jax.nn.sigmoid((2.0 * GELU_C) * (x + GELU_A * (x * x * x)))


def _sgu_prompt_kernel(gu0_ref, gu1_ref, gv0_ref, gv1_ref, nw_ref, ws_ref, bs_ref, o_ref, st_ref):
    rows = gu0_ref.shape[0]
    gv = jnp.concatenate([gv0_ref[...], gv1_ref[...]], axis=1).astype(F32)
    vn = _layernorm_rows(_gelu_tanh(gv), nw_ref[...])
    st_ref[0] = vn[rows - CHUNK:]
    vb = vn.astype(BF16)
    r = lax.broadcasted_iota(jnp.int32, (CHUNK, CHUNK), 0)
    c = lax.broadcasted_iota(jnp.int32, (CHUNK, CHUNK), 1)
    causal = r >= c
    gw = vb.shape[1] // GM_GROUPS
    half = gu0_ref.shape[1]
    for g in range(GM_GROUPS):
        w = jnp.where(causal, ws_ref[g], 0.0).astype(BF16)
        src = gu0_ref if g * gw < half else gu1_ref
        off = g * gw - (0 if g * gw < half else half)
        for ch in range(rows // CHUNK):
            rs = slice(ch * CHUNK, (ch + 1) * CHUNK)
            mix = jnp.dot(w, vb[rs, g * gw:(g + 1) * gw], preferred_element_type=F32) + bs_ref[:, g:g + 1]
            u = _gelu_tanh(src[rs, off:off + gw].astype(F32))
            o_ref[rs, g * gw:(g + 1) * gw] = (u * mix).astype(o_ref.dtype)


def _sgu_prompt(proj, nw, ws, bs_t, *, batch, seq, gu_off, gm_w, chunks_per_step):
    rows = chunks_per_step * CHUNK
    nc = seq // rows
    half = gm_w // 2
    b0 = gu_off // half

    def col(k):
        return lambda b, c: (b * nc + c, b0 + k)

    return pl.pallas_call(
        _sgu_prompt_kernel,
        grid=(batch, nc),
        in_specs=[
            pl.BlockSpec((rows, half), col(0)),
            pl.BlockSpec((rows, half), col(1)),
            pl.BlockSpec((rows, half), col(2)),
            pl.BlockSpec((rows, half), col(3)),
            pl.BlockSpec((1, gm_w), lambda b, c: (0, 0)),
            pl.BlockSpec((GM_GROUPS, CHUNK, CHUNK), lambda b, c: (0, 0, 0)),
            pl.BlockSpec((CHUNK, GM_GROUPS), lambda b, c: (0, 0)),
        ],
        out_specs=[
            pl.BlockSpec((rows, gm_w), lambda b, c: (b * nc + c, 0)),
            pl.BlockSpec((1, CHUNK, gm_w), lambda b, c: (b, 0, 0)),
        ],
        out_shape=[
            jax.ShapeDtypeStruct((batch * seq, gm_w), BF16),
            jax.ShapeDtypeStruct((batch, CHUNK, gm_w), F32),
        ],
        compiler_params=_params(("parallel", "arbitrary"), 32),
        name="sgu_prompt",
    )(proj, proj, proj, proj, nw.reshape(1, gm_w), ws, bs_t)


def _split_dot(x, m):
    hi = x.astype(BF16)
    lo = (x - hi.astype(F32)).astype(BF16)
    return jnp.dot(hi, m, preferred_element_type=F32) + jnp.dot(lo, m, preferred_element_type=F32)


def _attn_sample_kernel(q_ref, kn_ref, vn_ref, kb_ref, vb_ref, qw_ref, kw_ref, sink_ref,
                        wa_ref, wg_ref, wo_ref, o_ref, ko_ref, vo_ref, wab_ref, wgb_ref, wob_ref):
    wab_ref[...] = wa_ref[...].astype(BF16)
    wgb_ref[...] = wg_ref[...].astype(BF16)
    wob_ref[...] = wo_ref[...].astype(BF16)
    nsamp = q_ref.shape[0]
    attn_w = q_ref.shape[1]
    kv_w = kn_ref.shape[1]
    gmat = _head_group_matrix()
    q = q_ref[...].astype(F32)
    kn = kn_ref[...].astype(F32)
    qn = jnp.concatenate(
        [_head_rmsnorm(q[:, p * LANES:(p + 1) * LANES], qw_ref[...], gmat) * (HEAD_DIM ** -0.5)
         for p in range(attn_w // LANES)], axis=1)
    knn = jnp.concatenate(
        [_head_rmsnorm(kn[:, p * LANES:(p + 1) * LANES], kw_ref[...], gmat)
         for p in range(kv_w // LANES)], axis=1)
    vnn = vn_ref[...].astype(F32)

    ec = lax.broadcasted_iota(jnp.int32, (kv_w, attn_w), 0)
    el = lax.broadcasted_iota(jnp.int32, (kv_w, attn_w), 1)
    expand = (((ec >> 6) == (el >> 8)) & ((ec & 63) == (el & 63))).astype(BF16)
    gl = lax.broadcasted_iota(jnp.int32, (attn_w, LANES), 0)
    gh = lax.broadcasted_iota(jnp.int32, (attn_w, LANES), 1)
    hsum = ((gl >> 6) == gh).astype(BF16)
    tl = lax.broadcasted_iota(jnp.int32, (LANES, attn_w), 1)
    th = lax.broadcasted_iota(jnp.int32, (LANES, attn_w), 0)
    hexp = ((tl >> 6) == th).astype(BF16)

    nkeys = WINDOW + SUBLANES
    row = lax.broadcasted_iota(jnp.int32, (nkeys, LANES), 0)
    head = lax.broadcasted_iota(jnp.int32, (nkeys, LANES), 1)
    slope = jnp.exp2(-8.0 * (head + 1).astype(F32) / N_HEADS)
    dist = (WINDOW - row).astype(F32)
    key_ok = row <= WINDOW
    srow = lax.broadcasted_iota(jnp.int32, (WINDOW, kv_w), 0)
    sinks = sink_ref[...]

    for s in range(nsamp):
        kb = kb_ref[s]
        vb = vb_ref[s]
        knew = knn[s:s + 1]
        vnew = vnn[s:s + 1]
        ko_ref[s] = jnp.where(srow == WINDOW - 1, knew, pltpu.roll(kb, WINDOW - 1, 0))
        vo_ref[s] = jnp.where(srow == WINDOW - 1, vnew, pltpu.roll(vb, WINDOW - 1, 0))
        kk = jnp.concatenate([kb, jnp.broadcast_to(knew, (SUBLANES, kv_w))], axis=0)
        vv = jnp.concatenate([vb, jnp.broadcast_to(vnew, (SUBLANES, kv_w))], axis=0)
        kexp = _split_dot(kk, expand)
        vexp = _split_dot(vv, expand)
        sc = _split_dot(kexp * qn[s:s + 1], hsum)
        sc = sc - slope * dist
        sc = jnp.where(key_ok, sc, MASK_VALUE)
        mx = jnp.maximum(jnp.max(sc, axis=0, keepdims=True), sinks)
        e = jnp.exp(sc - mx)
        den = jnp.sum(e, axis=0, keepdims=True) + jnp.exp(sinks - mx)
        pexp = _split_dot(e / den, hexp)
        o_ref[s:s + 1, :] = jnp.sum(pexp * vexp, axis=0, keepdims=True)


def _attn_sample(q, knew, vnew, kbuf, vbuf, qw2, kw2, sinks_row, wa, wg, wo, *, group):
    nb, attn_w = q.shape
    kv_w = knew.shape[1]
    steps = nb // group
    row = lambda width: pl.BlockSpec((group, width), lambda i: (i, 0))
    buf = pl.BlockSpec((group, WINDOW, kv_w), lambda i: (i, 0, 0))
    vec = pl.BlockSpec((1, LANES), lambda i: (0, 0))
    slab = lambda w: pl.BlockSpec((w.shape[0] // steps, w.shape[1]), lambda i: (i, 0))
    return pl.pallas_call(
        _attn_sample_kernel,
        grid=(steps,),
        in_specs=[row(attn_w), row(kv_w), row(kv_w), buf, buf, vec, vec, vec, slab(wa), slab(wg), slab(wo)],
        out_specs=[row(attn_w), buf, buf, slab(wa), slab(wg), slab(wo)],
        out_shape=[
            jax.ShapeDtypeStruct((nb, attn_w), F32),
            jax.ShapeDtypeStruct((nb, WINDOW, kv_w), F32),
            jax.ShapeDtypeStruct((nb, WINDOW, kv_w), F32),
            jax.ShapeDtypeStruct(wa.shape, BF16),
            jax.ShapeDtypeStruct(wg.shape, BF16),
            jax.ShapeDtypeStruct(wo.shape, BF16),
        ],
        compiler_params=_params(("parallel",), 48),
        name="attn_sample",
    )(q, knew, vnew, kbuf, vbuf, qw2, kw2, sinks_row, wa, wg, wo)


def _sgu_sample_kernel(gu_ref, gv_ref, nw_ref, w0_ref, b0_ref, o_ref, st_ref):
    vn = _layernorm_rows(jax.nn.gelu(gv_ref[...].astype(F32)), nw_ref[...])
    st_ref[...] = vn
    mix = w0_ref[...] * vn + b0_ref[...]
    o_ref[...] = jax.nn.gelu(gu_ref[...].astype(F32)) * mix


def _sgu_sample(gu, gv, nw, w0_row, b0_row):
    nb, gm_w = gu.shape
    full = pl.BlockSpec((nb, gm_w), lambda i: (0, 0))
    vec = pl.BlockSpec((1, gm_w), lambda i: (0, 0))
    return pl.pallas_call(
        _sgu_sample_kernel,
        grid=(1,),
        in_specs=[full, full, vec, vec, vec],
        out_specs=[full, full],
        out_shape=[jax.ShapeDtypeStruct((nb, gm_w), F32), jax.ShapeDtypeStruct((nb, gm_w), F32)],
        name="sgu_sample",
    )(gu, gv, nw.reshape(1, gm_w), w0_row, b0_row)


def _merge_out_kernel(a_ref, s_ref, ga_ref, gb_ref, x_ref, wa_ref, wg_ref, wo_ref, nw_ref,
                      x1_ref, xn1_ref):
    a = jnp.dot(a_ref[...].astype(BF16), wa_ref[...], preferred_element_type=F32)
    m = jnp.dot(s_ref[...].astype(BF16), wg_ref[...], preferred_element_type=F32)
    merged = (gb_ref[...].astype(F32) * a + ga_ref[...].astype(F32) * m).astype(BF16)
    x1 = x_ref[...] + jnp.dot(merged, wo_ref[...], preferred_element_type=F32)
    x1_ref[...] = x1
    xn1_ref[...] = _rmsnorm_rows(x1, nw_ref[...]).astype(BF16)


def _merge_out(attn, sgu, gates, x, wa, wg, wo, nw, *, tm):
    m, ka = attn.shape
    kg = sgu.shape[1]
    d = wo.shape[1]
    rows = lambda width, col=0: pl.BlockSpec((tm, width), lambda i: (i, col))
    resident = lambda shape: pl.BlockSpec(shape, lambda i: (0, 0), pipeline_mode=pl.Buffered(1))
    return pl.pallas_call(
        _merge_out_kernel,
        grid=(m // tm,),
        in_specs=[
            rows(ka), rows(kg), rows(d, 0), rows(d, 1), rows(d),
            resident((ka, d)), resident((kg, d)), resident((d, d)),
            pl.BlockSpec((1, d), lambda i: (0, 0)),
        ],
        out_specs=[rows(d), rows(d)],
        out_shape=[jax.ShapeDtypeStruct((m, d), F32), jax.ShapeDtypeStruct((m, d), BF16)],
        compiler_params=_params(("parallel",), 48),
        name="merge_out",
    )(attn, sgu, gates, gates, x, wa, wg, wo, nw.reshape(1, d))


PAD = SUBLANES
FFN_ROW_BLOCKS = 4


def _ffn_step(t, xn_ref, wug_ref, wuu_ref, wd_ref, cwg_ref, cwu_ref, cbg_ref, cbu_ref,
              csg_ref, csu_ref, h_new, h_old, carry_ref, acc_ref, *, nf, tiles_per_seq):
    tm = xn_ref.shape[0]
    tf = wug_ref.shape[1]
    rb = tm // FFN_ROW_BLOCKS
    tp = jnp.maximum(t - 1, 0)
    ip = tp // nf
    jp = tp % nf
    first = (ip % tiles_per_seq) == 0
    h_old[0:PAD, :] = jnp.where(first, 0.0, carry_ref[jp])
    carry_ref[jp] = h_old[tm:tm + PAD, :]
    cw = jnp.concatenate([cwg_ref[...], cwu_ref[...]], axis=1)
    cb = jnp.concatenate([cbg_ref[...], cbu_ref[...]], axis=1)
    wug = wug_ref[...]
    wuu = wuu_ref[...]
    wd = wd_ref[...]
    for r in range(FFN_ROW_BLOCKS):
        lo = r * rb
        xr = xn_ref[lo:lo + rb, :]
        h_new[PAD + lo:PAD + lo + rb, 0:tf] = jnp.dot(xr, wug, preferred_element_type=F32)
        h_new[PAD + lo:PAD + lo + rb, tf:2 * tf] = jnp.dot(xr, wuu, preferred_element_type=F32)
        hp = h_old[lo:lo + rb + PAD, :]
        hc = (cw[0:1] * pltpu.roll(hp, 2, 0)[PAD:] + cw[1:2] * pltpu.roll(hp, 1, 0)[PAD:]
              + cw[2:3] * hp[PAD:] + cb)
        act = (jax.nn.silu(hc[:, 0:tf]) * hc[:, tf:2 * tf]).astype(BF16)
        acc_ref[lo:lo + rb, :] += jnp.dot(act, wd, preferred_element_type=F32)
    csg_ref[0] = h_old[PAD + tm - 2:PAD + tm, 0:tf]
    csu_ref[0] = h_old[PAD + tm - 2:PAD + tm, tf:2 * tf]


def _ffn_kernel(xn_ref, wug_ref, wuu_ref, wd_ref, cwg_ref, cwu_ref, cbg_ref, cbu_ref,
                o_ref, csg_ref, csu_ref, ha_ref, hb_ref, carry_ref, acc_ref, *, nf, tiles_per_seq):
    t = pl.program_id(0)
    step = functools.partial(_ffn_step, t, xn_ref, wug_ref, wuu_ref, wd_ref, cwg_ref, cwu_ref,
                             cbg_ref, cbu_ref, csg_ref, csu_ref, carry_ref=carry_ref, acc_ref=acc_ref,
                             nf=nf, tiles_per_seq=tiles_per_seq)

    @pl.when(t == 0)
    def _():
        hb_ref[...] = jnp.zeros_like(hb_ref)
        carry_ref[...] = jnp.zeros_like(carry_ref)

    @pl.when(jnp.maximum(t - 1, 0) % nf == 0)
    def _():
        acc_ref[...] = jnp.zeros_like(acc_ref)

    @pl.when(t % 2 == 0)
    def _():
        step(h_new=ha_ref, h_old=hb_ref)

    @pl.when(t % 2 == 1)
    def _():
        step(h_new=hb_ref, h_old=ha_ref)

    @pl.when(jnp.maximum(t - 1, 0) % nf == nf - 1)
    def _():
        o_ref[...] = acc_ref[...].astype(o_ref.dtype)


def _ffn_prompt(xn, w_up, w_down, cw, cb, *, batch, seq, tm):
    m, k = xn.shape
    d_ff, d = w_down.shape
    tf = w_up.shape[2]
    nf = d_ff // tf
    nm = m // tm
    tiles_per_seq = seq // tm
    prev = lambda t: jnp.maximum(t - 1, 0)
    up_tile = lambda off: (lambda t: (off + t % nf, 0, 0))
    dn_col = lambda off: (lambda t: (0, off + prev(t) % nf))
    state = lambda t: (prev(t) // nf, 0, prev(t) % nf)
    delta, csg, csu = pl.pallas_call(
        functools.partial(_ffn_kernel, nf=nf, tiles_per_seq=tiles_per_seq),
        grid=(nm * nf + 1,),
        in_specs=[
            pl.BlockSpec((tm, k), lambda t: (jnp.minimum(t // nf, nm - 1), 0)),
            pl.BlockSpec((None, k, tf), up_tile(0)),
            pl.BlockSpec((None, k, tf), up_tile(nf)),
            pl.BlockSpec((tf, d), lambda t: (prev(t) % nf, 0)),
            pl.BlockSpec((3, tf), dn_col(0)),
            pl.BlockSpec((3, tf), dn_col(nf)),
            pl.BlockSpec((1, tf), dn_col(0)),
            pl.BlockSpec((1, tf), dn_col(nf)),
        ],
        out_specs=[
            pl.BlockSpec((tm, d), lambda t: (prev(t) // nf, 0)),
            pl.BlockSpec((1, 2, tf), state),
            pl.BlockSpec((1, 2, tf), state),
        ],
        out_shape=[
            jax.ShapeDtypeStruct((m, d), BF16),
            jax.ShapeDtypeStruct((nm, 2, d_ff), F32),
            jax.ShapeDtypeStruct((nm, 2, d_ff), F32),
        ],
        scratch_shapes=[
            pltpu.VMEM((tm + PAD, 2 * tf), F32),
            pltpu.VMEM((tm + PAD, 2 * tf), F32),
            pltpu.VMEM((nf, PAD, 2 * tf), F32),
            pltpu.VMEM((tm, d), F32),
        ],
        compiler_params=_params(("arbitrary",), 58),
        name="ffn_prompt",
    )(xn, w_up, w_up, w_down, cw, cw, cb.reshape(1, -1), cb.reshape(1, -1))
    tails = jnp.concatenate([csg, csu], axis=-1).reshape(batch, tiles_per_seq, 2, 2 * d_ff)
    return delta, tails[:, -1]


def _ffn_down_sample_kernel(hg_ref, hu_ref, b0g_ref, b0u_ref, b1g_ref, b1u_ref, cwg_ref, cwu_ref,
                            cbg_ref, cbu_ref, wd_ref, x_ref, o_ref, wdb_ref):
    @pl.when(pl.program_id(0) == 0)
    def _():
        o_ref[...] = x_ref[...]

    def conv(h_ref, b0_ref, b1_ref, cw_ref, cb_ref):
        cw = cw_ref[...]
        return cw[0:1] * b0_ref[...] + cw[1:2] * b1_ref[...] + cw[2:3] * h_ref[...] + cb_ref[...]

    hcg = conv(hg_ref, b0g_ref, b1g_ref, cwg_ref, cbg_ref)
    hcu = conv(hu_ref, b0u_ref, b1u_ref, cwu_ref, cbu_ref)
    act = (jax.nn.silu(hcg) * hcu).astype(BF16)
    wdb = wd_ref[...].astype(BF16)
    wdb_ref[...] = wdb
    o_ref[...] += jnp.dot(act, wdb, preferred_element_type=F32)


def _ffn_down_sample(h, state2, cw, cb, wd, x, *, tf):
    nb, two_f = h.shape
    d_ff = two_f // 2
    nf = d_ff // tf
    d = wd.shape[1]
    colblk = lambda off: pl.BlockSpec((nb, tf), lambda j: (0, j + off))
    return pl.pallas_call(
        _ffn_down_sample_kernel,
        grid=(nf,),
        in_specs=[
            colblk(0), colblk(nf),
            colblk(0), colblk(nf), colblk(2 * nf), colblk(3 * nf),
            pl.BlockSpec((3, tf), lambda j: (0, j)),
            pl.BlockSpec((3, tf), lambda j: (0, j + nf)),
            pl.BlockSpec((1, tf), lambda j: (0, j)),
            pl.BlockSpec((1, tf), lambda j: (0, j + nf)),
            pl.BlockSpec((tf, d), lambda j: (j, 0)),
            pl.BlockSpec((nb, d), lambda j: (0, 0)),
        ],
        out_specs=[pl.BlockSpec((nb, d), lambda j: (0, 0)), pl.BlockSpec((tf, d), lambda j: (j, 0))],
        out_shape=[jax.ShapeDtypeStruct((nb, d), F32), jax.ShapeDtypeStruct((d_ff, d), BF16)],
        compiler_params=_params(("arbitrary",), 32),
        name="ffn_down_sample",
    )(h, h, state2, state2, state2, state2, cw, cw, cb.reshape(1, two_f), cb.reshape(1, two_f), wd, x)


def _ple_kernel(*refs, has_delta, emit_w):
    refs = list(refs)
    x_ref = refs.pop(0)
    d_ref = refs.pop(0) if has_delta else None
    nw_ref, w_ref, p_ref, wp_ref, o_ref = refs[:5]
    xn_ref = refs[-1]
    j = pl.program_id(1)
    tn = o_ref.shape[1]

    @pl.when(j == 0)
    def _():
        _rmsnorm_to(x_ref, nw_ref, xn_ref, add_ref=d_ref)

    wb = w_ref[...].astype(BF16)
    wpb = wp_ref[...].astype(BF16)
    if emit_w:
        wb_ref, wpb_ref = refs[5:7]
        wb_ref[...] = wb
        wpb_ref[...] = wpb
    gate = jax.nn.sigmoid(jnp.dot(xn_ref[...], wb, preferred_element_type=F32))
    emb = jnp.dot(p_ref[...].astype(BF16), wpb, preferred_element_type=F32)
    o_ref[...] = gate * emb
    for jj in range(x_ref.shape[1] // tn):
        @pl.when(j == jj)
        def _():
            res = x_ref[:, jj * tn:(jj + 1) * tn]
            if has_delta:
                res = res + d_ref[:, jj * tn:(jj + 1) * tn].astype(F32)
            o_ref[...] += res


def _ple_rows_kernel(x_ref, d_ref, nw_ref, w_ref, p_ref, wp_ref, o_ref):
    x2 = x_ref[...] + d_ref[...].astype(F32)
    xn = _rmsnorm_rows(x2, nw_ref[...]).astype(BF16)
    gate = jax.nn.sigmoid(jnp.dot(xn, w_ref[...], preferred_element_type=F32))
    emb = jnp.dot(p_ref[...].astype(BF16), wp_ref[...], preferred_element_type=F32)
    o_ref[...] = x2 + gate * emb


def _ple_rows(x, delta, nw, w, p, wp, *, tm):
    m, d = x.shape
    kp = p.shape[1]
    rows = lambda width: pl.BlockSpec((tm, width), lambda i: (i, 0))
    resident = lambda shape: pl.BlockSpec(shape, lambda i: (0, 0), pipeline_mode=pl.Buffered(1))
    return pl.pallas_call(
        _ple_rows_kernel,
        grid=(m // tm,),
        in_specs=[rows(d), rows(d), pl.BlockSpec((1, d), lambda i: (0, 0)), resident((d, d)),
                  rows(kp), resident((kp, d))],
        out_specs=rows(d),
        out_shape=jax.ShapeDtypeStruct((m, d), F32),
        compiler_params=_params(("parallel",), 56),
        name="ple_prompt",
    )(x, delta, nw.reshape(1, d), w, p, wp)


def _ple(x, delta, nw, w, p, wp, *, tm, tn, name, emit_w=False):
    m, k = x.shape
    n = w.shape[1]
    kp = p.shape[1]
    assert not emit_w or m == tm
    out_specs = [pl.BlockSpec((tm, tn), lambda i, j: (i, j))]
    out_shape = [jax.ShapeDtypeStruct((m, n), F32)]
    if emit_w:
        out_specs += [pl.BlockSpec((k, tn), lambda i, j: (0, j)), pl.BlockSpec((kp, tn), lambda i, j: (0, j))]
        out_shape += [jax.ShapeDtypeStruct((k, n), BF16), jax.ShapeDtypeStruct((kp, n), BF16)]
    row_full = pl.BlockSpec((tm, k), lambda i, j: (i, 0))
    in_specs = [row_full] + ([row_full] if delta is not None else []) + [
        pl.BlockSpec((1, k), lambda i, j: (0, 0)),
        pl.BlockSpec((k, tn), lambda i, j: (0, j)),
        pl.BlockSpec((tm, kp), lambda i, j: (i, 0)),
        pl.BlockSpec((kp, tn), lambda i, j: (0, j)),
    ]
    args = [x] + ([delta] if delta is not None else []) + [nw.reshape(1, k), w, p, wp]
    outs = pl.pallas_call(
        functools.partial(_ple_kernel, has_delta=delta is not None, emit_w=emit_w),
        grid=(m // tm, n // tn),
        in_specs=in_specs,
        out_specs=out_specs,
        out_shape=out_shape,
        scratch_shapes=[pltpu.VMEM((tm, k), BF16)],
        compiler_params=_params(("parallel", "arbitrary"), 56),
        name=name,
    )(*args)
    return outs if emit_w else outs[0]


def _layer_weights(i, attn_norm_w, w_in, q_norm_w, k_norm_w, attn_sinks, sgu_norm_w, sgu_w, sgu_b,
                   w_br_attn, w_br_gm, w_gate, b_gate, w_out, ffn_norm_w, w_up, conv_w, conv_b,
                   w_down, ple_norm_w, w_ple_gate, w_ple_proj):
    sinks = attn_sinks[i]
    gm_group_w = w_br_gm.shape[1] // GM_GROUPS
    return dict(
        attn_norm_w=attn_norm_w[i], w_in=w_in[i],
        qw2=jnp.tile(q_norm_w[i], HEADS_PER_VREG).reshape(1, LANES),
        kw2=jnp.tile(k_norm_w[i], HEADS_PER_VREG).reshape(1, LANES),
        sinks=sinks,
        sinks_row=jnp.pad(sinks, (0, LANES - N_HEADS)).reshape(1, LANES),
        sgu_norm_w=sgu_norm_w[i], sgu_w=sgu_w[i], sgu_b_t=jnp.transpose(sgu_b[i]),
        sgu_w0_row=jnp.repeat(sgu_w[i][:, 0, 0], gm_group_w).reshape(1, -1),
        sgu_b0_row=jnp.repeat(sgu_b[i][:, 0], gm_group_w).reshape(1, -1),
        w_br_attn=w_br_attn[i], w_br_gm=w_br_gm[i],
        w_gate=w_gate[i], b_gate=b_gate[i], w_out=w_out[i],
        ffn_norm_w=ffn_norm_w[i], w_up=w_up[i], conv_w=conv_w[i], conv_b=conv_b[i],
        w_down=w_down[i], ple_norm_w=ple_norm_w[i], w_ple_gate=w_ple_gate[i], w_ple_proj=w_ple_proj[i],
    )


def _prompt_layer(x, p, lw, wb):
    batch, seq, d = x.shape
    x2d = x.reshape(batch * seq, d)
    attn_w = N_HEADS * HEAD_DIM
    kv_w = N_KV_HEADS * HEAD_DIM
    gm_w = lw["w_br_gm"].shape[0]
    proj, gates = _in_proj(x2d, lw["attn_norm_w"], wb["w_in"], wb["w_gate"], lw["b_gate"], tm=1024)
    attn, k_new, v_new = _attn_prompt(proj, lw["qw2"], lw["kw2"], lw["sinks"], batch=batch, seq=seq)
    sgu, sgu_state = _sgu_prompt(proj, lw["sgu_norm_w"], lw["sgu_w"], lw["sgu_b_t"], batch=batch, seq=seq,
                                 gu_off=attn_w + 2 * kv_w, gm_w=gm_w, chunks_per_step=2)
    x1, xn1 = _merge_out(attn, sgu, gates,x2d, wb["w_br_attn"], wb["w_br_gm"], wb["w_out"],
                         lw["ffn_norm_w"], tm=256)
    delta, conv_state = _ffn_prompt(xn1, wb["w_up"], wb["w_down"], lw["conv_w"], lw["conv_b"],
                                    batch=batch, seq=seq, tm=1024)
    x3 = _ple_rows(x1, delta, lw["ple_norm_w"], wb["w_ple_gate"], p.reshape(batch * seq, -1),
                   wb["w_ple_proj"], tm=512)
    return (x3.reshape(batch, seq, d), k_new.reshape(batch, WINDOW, N_KV_HEADS, HEAD_DIM),
            v_new.reshape(batch, WINDOW, N_KV_HEADS, HEAD_DIM), sgu_state, conv_state)


def _sample_layer(x, p, k_buf, v_buf, conv_buf, lw):
    nb, t, d = x.shape
    x2d = x.reshape(nb * t, d)
    attn_w = N_HEADS * HEAD_DIM
    kv_w = N_KV_HEADS * HEAD_DIM
    gm_w = lw["w_br_gm"].shape[0]
    wb = {}
    proj, wb["w_in"] = _norm_mm(x2d, lw["attn_norm_w"], lw["w_in"], None, act=None,
                                tn=512, tile=512, name="proj_sample")
    gates, wb["w_gate"] = _norm_mm(x2d, lw["attn_norm_w"], lw["w_gate"], lw["b_gate"], act="sigmoid",
                                   tn=1024, tile=512, name="gates_sample")
    q = proj[:, :attn_w]
    k = proj[:, attn_w:attn_w + kv_w]
    v = proj[:, attn_w + kv_w:attn_w + 2 * kv_w]
    gu = proj[:, attn_w + 2 * kv_w:attn_w + 2 * kv_w + gm_w]
    gv = proj[:, attn_w + 2 * kv_w + gm_w:]
    attn, k_new, v_new, wb["w_br_attn"], wb["w_br_gm"], wb["w_out"] = _attn_sample(
        q, k, v, k_buf.reshape(nb, WINDOW, kv_w), v_buf.reshape(nb, WINDOW, kv_w),
        lw["qw2"], lw["kw2"], lw["sinks_row"], lw["w_br_attn"], lw["w_br_gm"], lw["w_out"], group=8)
    sgu, sgu_state = _sgu_sample(gu, gv, lw["sgu_norm_w"], lw["sgu_w0_row"], lw["sgu_b0_row"])
    x1, _ = _merge_out(attn, sgu, gates, x2d, wb["w_br_attn"], wb["w_br_gm"], wb["w_out"],
                       lw["ffn_norm_w"], tm=nb)
    h, wb["w_up"] = _norm_mm(x1, lw["ffn_norm_w"], lw["w_up"], None, act=None, tn=1024, tile=512,
                             name="ffn_up_sample")
    x2, wb["w_down"] = _ffn_down_sample(h, conv_buf.reshape(nb, -1), lw["conv_w"], lw["conv_b"], lw["w_down"],
                                        x1, tf=512)
    x3, wb["w_ple_gate"], wb["w_ple_proj"] = _ple(
        x2, None, lw["ple_norm_w"], lw["w_ple_gate"], p.reshape(nb * t, -1), lw["w_ple_proj"],
        tm=nb, tn=1024, name="ple_sample", emit_w=True)
    conv_state = jnp.stack([conv_buf[:, 1, :], h], axis=1)
    return (x3.reshape(nb, t, d), k_new.reshape(nb, WINDOW, N_KV_HEADS, HEAD_DIM),
            v_new.reshape(nb, WINDOW, N_KV_HEADS, HEAD_DIM), sgu_state.reshape(nb, t, gm_w), conv_state), wb


def kernel(x_prompt, x_sample, p_prompt, p_sample, state_attn_k, state_attn_v, state_conv, attn_norm_w, w_in, q_norm_w, k_norm_w, attn_sinks, sgu_norm_w, sgu_w, sgu_b, w_br_attn, w_br_gm, w_gate, b_gate, w_out, ffn_norm_w, w_up, conv_w, conv_b, w_down, ple_norm_w, w_ple_gate, w_ple_proj):
    depth = w_in.shape[0]
    xp, xs = x_prompt, x_sample
    outs = [[] for _ in range(8)]
    for i in range(depth):
        lw = _layer_weights(i, attn_norm_w, w_in, q_norm_w, k_norm_w, attn_sinks, sgu_norm_w, sgu_w, sgu_b,
                            w_br_attn, w_br_gm, w_gate, b_gate, w_out, ffn_norm_w, w_up, conv_w, conv_b,
                            w_down, ple_norm_w, w_ple_gate, w_ple_proj)
        (xs, ks, vs, gs, cs), wb = _sample_layer(xs, p_sample[i], state_attn_k[i], state_attn_v[i],
                                                 state_conv[i], lw)
        xp, kp, vp, gp, cp = _prompt_layer(xp, p_prompt[i], lw, wb)
        for lst, val in zip(outs, (kp, vp, ks, vs, gp, gs, cp, cs)):
            lst.append(val)
    return (xp, xs) + tuple(jnp.stack(lst) for lst in outs)
```

```python
import functools

import jax
import jax.numpy as jnp
from jax import lax
from jax.experimental import pallas as pl
from jax.experimental.pallas import tpu as pltpu

F32 = jnp.float32
BF16 = jnp.bfloat16

HEAD_DIM = 64
N_HEADS = 16
N_KV_HEADS = 4
GQA_GROUP = N_HEADS // N_KV_HEADS
WINDOW = 128
CHUNK = 128
GM_GROUPS = 4
EPS = 1e-6
MASK_VALUE = -1e30
LANES = 128
SUBLANES = 8
HEADS_PER_VREG = LANES // HEAD_DIM
MIB = 1 << 20


def _alibi_slope(h):
    return float(2.0 ** (-8.0 * (h + 1) / N_HEADS))


def _params(semantics, vmem_mib, flags=None):
    return pltpu.CompilerParams(dimension_semantics=semantics, vmem_limit_bytes=vmem_mib * MIB, flags=flags)


def _rmsnorm_rows(x, w):
    ms = jnp.mean(x * x, axis=-1, keepdims=True)
    return x * lax.rsqrt(ms + EPS) * w


NORM_ROWS = 256


def _rmsnorm_to(x_ref, nw_ref, xn_ref, add_ref=None):
    tm = x_ref.shape[0]
    rows = min(NORM_ROWS, tm)

    def body(r, carry):
        sl = pl.ds(pl.multiple_of(r * rows, rows), rows)
        x = x_ref[sl, :]
        if add_ref is not None:
            x = x + add_ref[sl, :].astype(F32)
        xn_ref[sl, :] = _rmsnorm_rows(x, nw_ref[...]).astype(BF16)
        return carry

    lax.fori_loop(0, tm // rows, body, 0)


def _norm_mm_kernel(*refs, has_bias, act):
    if has_bias:
        x_ref, nw_ref, w_ref, b_ref, o_ref, wb_ref, xn_ref = refs
    else:
        x_ref, nw_ref, w_ref, o_ref, wb_ref, xn_ref = refs

    @pl.when(pl.program_id(0) == 0)
    def _():
        _rmsnorm_to(x_ref, nw_ref, xn_ref)

    wb = w_ref[...].astype(BF16)
    tile = wb_ref.shape[2]
    for c in range(wb_ref.shape[0]):
        wb_ref[c] = wb[:, c * tile:(c + 1) * tile]
    acc = jnp.dot(xn_ref[...], wb, preferred_element_type=F32)
    if has_bias:
        acc = acc + b_ref[...]
    if act == "sigmoid":
        acc = jax.nn.sigmoid(acc)
    o_ref[...] = acc.astype(o_ref.dtype)


def _norm_mm(x, nw, w, bias, *, act, tn, tile, name):
    m, k = x.shape
    n = w.shape[1]
    per_step = tn // tile
    in_specs = [
        pl.BlockSpec((m, k), lambda j: (0, 0)),
        pl.BlockSpec((1, k), lambda j: (0, 0)),
        pl.BlockSpec((k, tn), lambda j: (0, j)),
    ]
    args = [x, nw.reshape(1, k), w]
    if bias is not None:
        in_specs.append(pl.BlockSpec((1, tn), lambda j: (0, j)))
        args.append(bias.reshape(1, n))
    return pl.pallas_call(
        functools.partial(_norm_mm_kernel, has_bias=bias is not None, act=act),
        grid=(n // tn,),
        in_specs=in_specs,
        out_specs=[pl.BlockSpec((m, tn), lambda j: (0, j)),
                   pl.BlockSpec((per_step, k, tile), lambda j: (j, 0, 0))],
        out_shape=[jax.ShapeDtypeStruct((m, n), F32), jax.ShapeDtypeStruct((n // tile, k, tile), BF16)],
        scratch_shapes=[pltpu.VMEM((m, k), BF16)],
        compiler_params=_params(("arbitrary",), 56),
        name=name,
    )(*args)


def _in_proj_kernel(x_ref, nw_ref, win_ref, wg_ref, bg_ref, proj_ref, gates_ref, xn_ref):
    xn_ref[...] = _rmsnorm_rows(x_ref[...], nw_ref[...]).astype(BF16)
    tn = win_ref.shape[2]
    for c in range(win_ref.shape[0]):
        acc = jnp.dot(xn_ref[...], win_ref[c], preferred_element_type=F32)
        proj_ref[:, c * tn:(c + 1) * tn] = acc.astype(proj_ref.dtype)
    for c in range(wg_ref.shape[0]):
        cols = slice(c * tn, (c + 1) * tn)
        acc = jnp.dot(xn_ref[...], wg_ref[c], preferred_element_type=F32)
        gates_ref[:, cols] = jax.nn.sigmoid(acc + bg_ref[:, cols]).astype(gates_ref.dtype)


def _in_proj(x, nw, w_in, w_gate, b_gate, *, tm):
    m, k = x.shape
    n_in, _, tn = w_in.shape
    n_g = w_gate.shape[0]
    rows = lambda width: pl.BlockSpec((tm, width), lambda i: (i, 0))
    resident = lambda a: pl.BlockSpec(a.shape, lambda i: (0,) * a.ndim, pipeline_mode=pl.Buffered(1))
    bg = b_gate.reshape(1, -1)
    return pl.pallas_call(
        _in_proj_kernel,
        grid=(m // tm,),
        in_specs=[rows(k), pl.BlockSpec((1, k), lambda i: (0, 0)), resident(w_in), resident(w_gate),
                  pl.BlockSpec(bg.shape, lambda i: (0, 0))],
        out_specs=[rows(n_in * tn), rows(n_g * tn)],
        out_shape=[
            jax.ShapeDtypeStruct((m, n_in * tn), BF16),
            jax.ShapeDtypeStruct((m, n_g * tn), BF16),
        ],
        scratch_shapes=[pltpu.VMEM((tm, k), BF16)],
        compiler_params=_params(("parallel",), 56),
        name="in_proj",
    )(x, nw.reshape(1, k), w_in, w_gate, bg)


def _head_group_matrix():
    r = lax.broadcasted_iota(jnp.int32, (LANES, LANES), 0)
    c = lax.broadcasted_iota(jnp.int32, (LANES, LANES), 1)
    return ((r >> 6) == (c >> 6)).astype(BF16)


def _head_rmsnorm(xcol, w, gmat):
    ss = jnp.dot((xcol * xcol).astype(BF16), gmat, preferred_element_type=F32)
    return xcol * lax.rsqrt(ss * (1.0 / HEAD_DIM) + EPS) * w


def _attn_prompt_kernel(sink_ref, q_ref, kc_ref, kp_ref, vc_ref, vp_ref, qw_ref, kw_ref,
                        o_ref, kn_ref, vn_ref):
    c = pl.program_id(1)
    blk = WINDOW
    gmat = _head_group_matrix()
    lane = lax.broadcasted_iota(jnp.int32, (1, LANES), 1)
    low = lane < HEAD_DIM

    kw = kw_ref[...]
    qw = qw_ref[...]
    kc = kc_ref[...].astype(F32)
    kp = kp_ref[...].astype(F32)
    vc = vc_ref[...].astype(F32)
    vp = vp_ref[...].astype(F32)
    ncol = kc.shape[1] // LANES
    kcn = [_head_rmsnorm(kc[:, p * LANES:(p + 1) * LANES], kw, gmat) for p in range(ncol)]
    kpn = [_head_rmsnorm(kp[:, p * LANES:(p + 1) * LANES], kw, gmat) for p in range(ncol)]
    kn_ref[0] = jnp.concatenate(kcn, axis=1)
    vn_ref[0] = vc

    kcat = [jnp.concatenate([kpn[p], kcn[p]], axis=0) for p in range(ncol)]
    vcat = [jnp.concatenate([vp[:, p * LANES:(p + 1) * LANES], vc[:, p * LANES:(p + 1) * LANES]], axis=0)
            for p in range(ncol)]
    krol = [pltpu.roll(kcat[p], HEAD_DIM, 1) for p in range(ncol)]
    vrol = [pltpu.roll(vcat[p], HEAD_DIM, 1) for p in range(ncol)]

    qi = lax.broadcasted_iota(jnp.int32, (blk, 2 * blk), 0)
    kj = lax.broadcasted_iota(jnp.int32, (blk, 2 * blk), 1)
    dist = blk + qi - kj
    valid = (dist >= 0) & (dist <= WINDOW) & ((kj >= blk) | (c > 0))
    distf = dist.astype(F32)

    q = q_ref[...].astype(F32)
    qn = [_head_rmsnorm(q[:, p * LANES:(p + 1) * LANES], qw, gmat) * (HEAD_DIM ** -0.5)
          for p in range(N_HEADS // HEADS_PER_VREG)]

    for g in range(N_KV_HEADS):
        p, half = divmod(g, HEADS_PER_VREG)
        if half == 0:
            kd = jnp.where(low, kcat[p], krol[p])
            vd = jnp.where(low, vcat[p], vrol[p])
        else:
            kd = jnp.where(low, krol[p], kcat[p])
            vd = jnp.where(low, vrol[p], vcat[p])
        kd = kd.astype(BF16)
        vd = vd.astype(BF16)
        qs = []
        for hl in range(GQA_GROUP):
            h = g * GQA_GROUP + hl
            pc, hh = divmod(h, HEADS_PER_VREG)
            keep = low if hh == 0 else jnp.logical_not(low)
            qs.append(jnp.where(keep, qn[pc], 0.0).astype(BF16))
        qstack = jnp.concatenate(qs, axis=0)
        s_all = lax.dot_general(qstack, kd, (((1,), (1,)), ((), ())), preferred_element_type=F32)
        ps = []
        for hl in range(GQA_GROUP):
            h = g * GQA_GROUP + hl
            s = s_all[hl * blk:(hl + 1) * blk] - _alibi_slope(h) * distf
            s = jnp.where(valid, s, MASK_VALUE)
            sink = sink_ref[h]
            mx = jnp.maximum(jnp.max(s, axis=-1, keepdims=True), sink)
            e = jnp.exp(s - mx)
            den = jnp.sum(e, axis=-1, keepdims=True) + jnp.exp(sink - mx)
            ps.append((e / den).astype(BF16))
        pstack = jnp.concatenate(ps, axis=0)
        o_all = jnp.dot(pstack, vd, preferred_element_type=F32)
        for pair in range(GQA_GROUP // HEADS_PER_VREG):
            pc = g * (GQA_GROUP // HEADS_PER_VREG) + pair
            even = o_all[(2 * pair) * blk:(2 * pair + 1) * blk]
            odd = o_all[(2 * pair + 1) * blk:(2 * pair + 2) * blk]
            o_ref[:, pc * LANES:(pc + 1) * LANES] = jnp.where(low, even, odd).astype(o_ref.dtype)


def _attn_prompt(proj, qw2, kw2, sinks, *, batch, seq):
    nb = seq // WINDOW
    attn_w = N_HEADS * HEAD_DIM
    kv_w = N_KV_HEADS * HEAD_DIM
    kcol = attn_w // kv_w
    vcol = kcol + 1

    def cur(col):
        return lambda b, c: (b * nb + c, col)

    def prev(col):
        return lambda b, c: (jnp.maximum(b * nb + c - 1, 0), col)

    return pl.pallas_call(
        _attn_prompt_kernel,
        grid=(batch, nb),
        in_specs=[
            pl.BlockSpec(memory_space=pltpu.SMEM),
            pl.BlockSpec((WINDOW, attn_w), cur(0)),
            pl.BlockSpec((WINDOW, kv_w), cur(kcol)),
            pl.BlockSpec((WINDOW, kv_w), prev(kcol)),
            pl.BlockSpec((WINDOW, kv_w), cur(vcol)),
            pl.BlockSpec((WINDOW, kv_w), prev(vcol)),
            pl.BlockSpec((1, LANES), lambda b, c: (0, 0)),
            pl.BlockSpec((1, LANES), lambda b, c: (0, 0)),
        ],
        out_specs=[
            pl.BlockSpec((WINDOW, attn_w), lambda b, c: (b * nb + c, 0)),
            pl.BlockSpec((1, WINDOW, kv_w), lambda b, c: (b, 0, 0)),
            pl.BlockSpec((1, WINDOW, kv_w), lambda b, c: (b, 0, 0)),
        ],
        out_shape=[
            jax.ShapeDtypeStruct((batch * seq, attn_w), BF16),
            jax.ShapeDtypeStruct((batch, WINDOW, kv_w), F32),
            jax.ShapeDtypeStruct((batch, WINDOW, kv_w), F32),
        ],
        compiler_params=_params(("parallel", "arbitrary"), 32),
        name="attn_prompt",
    )(sinks, proj, proj, proj, proj, proj, qw2, kw2)


def _layernorm_rows(x, w):
    mu = jnp.mean(x, axis=-1, keepdims=True)
    xc = x - mu
    return xc * lax.rsqrt(jnp.mean(xc * xc, axis=-1, keepdims=True) + EPS) * w


GELU_C = 0.7978845608028654
GELU_A = 0.044715


def _gelu_tanh(x):
    return x * jax.nn.sigmoid((2.0 * GELU_C) * (x + GELU_A * (x * x * x)))


def _sgu_prompt_kernel(gu0_ref, gu1_ref, gv0_ref, gv1_ref, nw_ref, ws_ref, bs_ref, o_ref, st_ref):
    rows = gu0_ref.shape[0]
    gv = jnp.concatenate([gv0_ref[...], gv1_ref[...]], axis=1).astype(F32)
    vn = _layernorm_rows(_gelu_tanh(gv), nw_ref[...])
    st_ref[0] = vn[rows - CHUNK:]
    vb = vn.astype(BF16)
    r = lax.broadcasted_iota(jnp.int32, (CHUNK, CHUNK), 0)
    c = lax.broadcasted_iota(jnp.int32, (CHUNK, CHUNK), 1)
    causal = r >= c
    gw = vb.shape[1] // GM_GROUPS
    half = gu0_ref.shape[1]
    for g in range(GM_GROUPS):
        w = jnp.where(causal, ws_ref[g], 0.0).astype(BF16)
        src = gu0_ref if g * gw < half else gu1_ref
        off = g * gw - (0 if g * gw < half else half)
        for ch in range(rows // CHUNK):
            rs = slice(ch * CHUNK, (ch + 1) * CHUNK)
            mix = jnp.dot(w, vb[rs, g * gw:(g + 1) * gw], preferred_element_type=F32) + bs_ref[:, g:g + 1]
            u = _gelu_tanh(src[rs, off:off + gw].astype(F32))
            o_ref[rs, g * gw:(g + 1) * gw] = (u * mix).astype(o_ref.dtype)


def _sgu_prompt(proj, nw, ws, bs_t, *, batch, seq, gu_off, gm_w, chunks_per_step):
    rows = chunks_per_step * CHUNK
    nc = seq // rows
    half = gm_w // 2
    b0 = gu_off // half

    def col(k):
        return lambda b, c: (b * nc + c, b0 + k)

    return pl.pallas_call(
        _sgu_prompt_kernel,
        grid=(batch, nc),
        in_specs=[
            pl.BlockSpec((rows, half), col(0)),
            pl.BlockSpec((rows, half), col(1)),
            pl.BlockSpec((rows, half), col(2)),
            pl.BlockSpec((rows, half), col(3)),
            pl.BlockSpec((1, gm_w), lambda b, c: (0, 0)),
            pl.BlockSpec((GM_GROUPS, CHUNK, CHUNK), lambda b, c: (0, 0, 0)),
            pl.BlockSpec((CHUNK, GM_GROUPS), lambda b, c: (0, 0)),
        ],
        out_specs=[
            pl.BlockSpec((rows, gm_w), lambda b, c: (b * nc + c, 0)),
            pl.BlockSpec((1, CHUNK, gm_w), lambda b, c: (b, 0, 0)),
        ],
        out_shape=[
            jax.ShapeDtypeStruct((batch * seq, gm_w), BF16),
            jax.ShapeDtypeStruct((batch, CHUNK, gm_w), F32),
        ],
        compiler_params=_params(("parallel", "arbitrary"), 32),
        name="sgu_prompt",
    )(proj, proj, proj, proj, nw.reshape(1, gm_w), ws, bs_t)


def _split_dot(x, m):
    hi = x.astype(BF16)
    lo = (x - hi.astype(F32)).astype(BF16)
    return jnp.dot(hi, m, preferred_element_type=F32) + jnp.dot(lo, m, preferred_element_type=F32)


def _attn_sample_kernel(q_ref, kn_ref, vn_ref, kb_ref, vb_ref, qw_ref, kw_ref, sink_ref,
                        wa_ref, wg_ref, wo_ref, o_ref, ko_ref, vo_ref, wab_ref, wgb_ref, wob_ref):
    wab_ref[...] = wa_ref[...].astype(BF16)
    wgb_ref[...] = wg_ref[...].astype(BF16)
    wob_ref[...] = wo_ref[...].astype(BF16)
    nsamp = q_ref.shape[0]
    attn_w = q_ref.shape[1]
    kv_w = kn_ref.shape[1]
    gmat = _head_group_matrix()
    q = q_ref[...].astype(F32)
    kn = kn_ref[...].astype(F32)
    qn = jnp.concatenate(
        [_head_rmsnorm(q[:, p * LANES:(p + 1) * LANES], qw_ref[...], gmat) * (HEAD_DIM ** -0.5)
         for p in range(attn_w // LANES)], axis=1)
    knn = jnp.concatenate(
        [_head_rmsnorm(kn[:, p * LANES:(p + 1) * LANES], kw_ref[...], gmat)
         for p in range(kv_w // LANES)], axis=1)
    vnn = vn_ref[...].astype(F32)

    ec = lax.broadcasted_iota(jnp.int32, (kv_w, attn_w), 0)
    el = lax.broadcasted_iota(jnp.int32, (kv_w, attn_w), 1)
    expand = (((ec >> 6) == (el >> 8)) & ((ec & 63) == (el & 63))).astype(BF16)
    gl = lax.broadcasted_iota(jnp.int32, (attn_w, LANES), 0)
    gh = lax.broadcasted_iota(jnp.int32, (attn_w, LANES), 1)
    hsum = ((gl >> 6) == gh).astype(BF16)
    tl = lax.broadcasted_iota(jnp.int32, (LANES, attn_w), 1)
    th = lax.broadcasted_iota(jnp.int32, (LANES, attn_w), 0)
    hexp = ((tl >> 6) == th).astype(BF16)

    nkeys = WINDOW + SUBLANES
    row = lax.broadcasted_iota(jnp.int32, (nkeys, LANES), 0)
    head = lax.broadcasted_iota(jnp.int32, (nkeys, LANES), 1)
    slope = jnp.exp2(-8.0 * (head + 1).astype(F32) / N_HEADS)
    dist = (WINDOW - row).astype(F32)
    key_ok = row <= WINDOW
    srow = lax.broadcasted_iota(jnp.int32, (WINDOW, kv_w), 0)
    sinks = sink_ref[...]

    for s in range(nsamp):
        kb = kb_ref[s]
        vb = vb_ref[s]
        knew = knn[s:s + 1]
        vnew = vnn[s:s + 1]
        ko_ref[s] = jnp.where(srow == WINDOW - 1, knew, pltpu.roll(kb, WINDOW - 1, 0))
        vo_ref[s] = jnp.where(srow == WINDOW - 1, vnew, pltpu.roll(vb, WINDOW - 1, 0))
        kk = jnp.concatenate([kb, jnp.broadcast_to(knew, (SUBLANES, kv_w))], axis=0)
        vv = jnp.concatenate([vb, jnp.broadcast_to(vnew, (SUBLANES, kv_w))], axis=0)
        kexp = _split_dot(kk, expand)
        vexp = _split_dot(vv, expand)
        sc = _split_dot(kexp * qn[s:s + 1], hsum)
        sc = sc - slope * dist
        sc = jnp.where(key_ok, sc, MASK_VALUE)
        mx = jnp.maximum(jnp.max(sc, axis=0, keepdims=True), sinks)
        e = jnp.exp(sc - mx)
        den = jnp.sum(e, axis=0, keepdims=True) + jnp.exp(sinks - mx)
        pexp = _split_dot(e / den, hexp)
        o_ref[s:s + 1, :] = jnp.sum(pexp * vexp, axis=0, keepdims=True)


def _attn_sample(q, knew, vnew, kbuf, vbuf, qw2, kw2, sinks_row, wa, wg, wo, *, group):
    nb, attn_w = q.shape
    kv_w = knew.shape[1]
    steps = nb // group
    row = lambda width: pl.BlockSpec((group, width), lambda i: (i, 0))
    buf = pl.BlockSpec((group, WINDOW, kv_w), lambda i: (i, 0, 0))
    vec = pl.BlockSpec((1, LANES), lambda i: (0, 0))
    slab = lambda w: pl.BlockSpec((w.shape[0] // steps, w.shape[1]), lambda i: (i, 0))
    return pl.pallas_call(
        _attn_sample_kernel,
        grid=(steps,),
        in_specs=[row(attn_w), row(kv_w), row(kv_w), buf, buf, vec, vec, vec, slab(wa), slab(wg), slab(wo)],
        out_specs=[row(attn_w), buf, buf, slab(wa), slab(wg), slab(wo)],
        out_shape=[
            jax.ShapeDtypeStruct((nb, attn_w), F32),
            jax.ShapeDtypeStruct((nb, WINDOW, kv_w), F32),
            jax.ShapeDtypeStruct((nb, WINDOW, kv_w), F32),
            jax.ShapeDtypeStruct(wa.shape, BF16),
            jax.ShapeDtypeStruct(wg.shape, BF16),
            jax.ShapeDtypeStruct(wo.shape, BF16),
        ],
        compiler_params=_params(("parallel",), 48),
        name="attn_sample",
    )(q, knew, vnew, kbuf, vbuf, qw2, kw2, sinks_row, wa, wg, wo)


def _sgu_sample_kernel(gu_ref, gv_ref, nw_ref, w0_ref, b0_ref, o_ref, st_ref):
    vn = _layernorm_rows(jax.nn.gelu(gv_ref[...].astype(F32)), nw_ref[...])
    st_ref[...] = vn
    mix = w0_ref[...] * vn + b0_ref[...]
    o_ref[...] = jax.nn.gelu(gu_ref[...].astype(F32)) * mix


def _sgu_sample(gu, gv, nw, w0_row, b0_row):
    nb, gm_w = gu.shape
    full = pl.BlockSpec((nb, gm_w), lambda i: (0, 0))
    vec = pl.BlockSpec((1, gm_w), lambda i: (0, 0))
    return pl.pallas_call(
        _sgu_sample_kernel,
        grid=(1,),
        in_specs=[full, full, vec, vec, vec],
        out_specs=[full, full],
        out_shape=[jax.ShapeDtypeStruct((nb, gm_w), F32), jax.ShapeDtypeStruct((nb, gm_w), F32)],
        name="sgu_sample",
    )(gu, gv, nw.reshape(1, gm_w), w0_row, b0_row)


def _merge_out_kernel(a_ref, s_ref, ga_ref, gb_ref, x_ref, wa_ref, wg_ref, wo_ref, nw_ref,
                      x1_ref, xn1_ref):
    a = jnp.dot(a_ref[...].astype(BF16), wa_ref[...], preferred_element_type=F32)
    m = jnp.dot(s_ref[...].astype(BF16), wg_ref[...], preferred_element_type=F32)
    merged = (gb_ref[...].astype(F32) * a + ga_ref[...].astype(F32) * m).astype(BF16)
    x1 = x_ref[...] + jnp.dot(merged, wo_ref[...], preferred_element_type=F32)
    x1_ref[...] = x1
    xn1_ref[...] = _rmsnorm_rows(x1, nw_ref[...]).astype(BF16)


def _merge_out(attn, sgu, gates, x, wa, wg, wo, nw, *, tm):
    m, ka = attn.shape
    kg = sgu.shape[1]
    d = wo.shape[1]
    rows = lambda width, col=0: pl.BlockSpec((tm, width), lambda i: (i, col))
    resident = lambda shape: pl.BlockSpec(shape, lambda i: (0, 0), pipeline_mode=pl.Buffered(1))
    return pl.pallas_call(
        _merge_out_kernel,
        grid=(m // tm,),
        in_specs=[
            rows(ka), rows(kg), rows(d, 0), rows(d, 1), rows(d),
            resident((ka, d)), resident((kg, d)), resident((d, d)),
            pl.BlockSpec((1, d), lambda i: (0, 0)),
        ],
        out_specs=[rows(d), rows(d)],
        out_shape=[jax.ShapeDtypeStruct((m, d), F32), jax.ShapeDtypeStruct((m, d), BF16)],
        compiler_params=_params(("parallel",), 48),
        name="merge_out",
    )(attn, sgu, gates, gates, x, wa, wg, wo, nw.reshape(1, d))


PAD = SUBLANES
FFN_ROW_BLOCKS = 4


def _ffn_step(t, xn_ref, wug_ref, wuu_ref, wd_ref, cwg_ref, cwu_ref, cbg_ref, cbu_ref,
              csg_ref, csu_ref, h_new, h_old, carry_ref, acc_ref, *, nf, tiles_per_seq):
    tm = xn_ref.shape[0]
    tf = wug_ref.shape[1]
    rb = tm // FFN_ROW_BLOCKS
    tp = jnp.maximum(t - 1, 0)
    ip = tp // nf
    jp = tp % nf
    first = (ip % tiles_per_seq) == 0
    h_old[0:PAD, :] = jnp.where(first, 0.0, carry_ref[jp])
    carry_ref[jp] = h_old[tm:tm + PAD, :]
    cw = jnp.concatenate([cwg_ref[...], cwu_ref[...]], axis=1)
    cb = jnp.concatenate([cbg_ref[...], cbu_ref[...]], axis=1)
    wug = wug_ref[...]
    wuu = wuu_ref[...]
    wd = wd_ref[...]
    for r in range(FFN_ROW_BLOCKS):
        lo = r * rb
        xr = xn_ref[lo:lo + rb, :]
        h_new[PAD + lo:PAD + lo + rb, 0:tf] = jnp.dot(xr, wug, preferred_element_type=F32)
        h_new[PAD + lo:PAD + lo + rb, tf:2 * tf] = jnp.dot(xr, wuu, preferred_element_type=F32)
        hp = h_old[lo:lo + rb + PAD, :]
        hc = (cw[0:1] * pltpu.roll(hp, 2, 0)[PAD:] + cw[1:2] * pltpu.roll(hp, 1, 0)[PAD:]
              + cw[2:3] * hp[PAD:] + cb)
        act = (jax.nn.silu(hc[:, 0:tf]) * hc[:, tf:2 * tf]).astype(BF16)
        acc_ref[lo:lo + rb, :] += jnp.dot(act, wd, preferred_element_type=F32)
    csg_ref[0] = h_old[PAD + tm - 2:PAD + tm, 0:tf]
    csu_ref[0] = h_old[PAD + tm - 2:PAD + tm, tf:2 * tf]


def _ffn_kernel(xn_ref, wug_ref, wuu_ref, wd_ref, cwg_ref, cwu_ref, cbg_ref, cbu_ref,
                o_ref, csg_ref, csu_ref, ha_ref, hb_ref, carry_ref, acc_ref, *, nf, tiles_per_seq):
    t = pl.program_id(0)
    step = functools.partial(_ffn_step, t, xn_ref, wug_ref, wuu_ref, wd_ref, cwg_ref, cwu_ref,
                             cbg_ref, cbu_ref, csg_ref, csu_ref, carry_ref=carry_ref, acc_ref=acc_ref,
                             nf=nf, tiles_per_seq=tiles_per_seq)

    @pl.when(t == 0)
    def _():
        hb_ref[...] = jnp.zeros_like(hb_ref)
        carry_ref[...] = jnp.zeros_like(carry_ref)

    @pl.when(jnp.maximum(t - 1, 0) % nf == 0)
    def _():
        acc_ref[...] = jnp.zeros_like(acc_ref)

    @pl.when(t % 2 == 0)
    def _():
        step(h_new=ha_ref, h_old=hb_ref)

    @pl.when(t % 2 == 1)
    def _():
        step(h_new=hb_ref, h_old=ha_ref)

    @pl.when(jnp.maximum(t - 1, 0) % nf == nf - 1)
    def _():
        o_ref[...] = acc_ref[...].astype(o_ref.dtype)


def _ffn_prompt(xn, w_up, w_down, cw, cb, *, batch, seq, tm):
    m, k = xn.shape
    d_ff, d = w_down.shape
    tf = w_up.shape[2]
    nf = d_ff // tf
    nm = m // tm
    tiles_per_seq = seq // tm
    prev = lambda t: jnp.maximum(t - 1, 0)
    up_tile = lambda off: (lambda t: (off + t % nf, 0, 0))
    dn_col = lambda off: (lambda t: (0, off + prev(t) % nf))
    state = lambda t: (prev(t) // nf, 0, prev(t) % nf)
    delta, csg, csu = pl.pallas_call(
        functools.partial(_ffn_kernel, nf=nf, tiles_per_seq=tiles_per_seq),
        grid=(nm * nf + 1,),
        in_specs=[
            pl.BlockSpec((tm, k), lambda t: (jnp.minimum(t // nf, nm - 1), 0)),
            pl.BlockSpec((None, k, tf), up_tile(0)),
            pl.BlockSpec((None, k, tf), up_tile(nf)),
            pl.BlockSpec((tf, d), lambda t: (prev(t) % nf, 0)),
            pl.BlockSpec((3, tf), dn_col(0)),
            pl.BlockSpec((3, tf), dn_col(nf)),
            pl.BlockSpec((1, tf), dn_col(0)),
            pl.BlockSpec((1, tf), dn_col(nf)),
        ],
        out_specs=[
            pl.BlockSpec((tm, d), lambda t: (prev(t) // nf, 0)),
            pl.BlockSpec((1, 2, tf), state),
            pl.BlockSpec((1, 2, tf), state),
        ],
        out_shape=[
            jax.ShapeDtypeStruct((m, d), BF16),
            jax.ShapeDtypeStruct((nm, 2, d_ff), F32),
            jax.ShapeDtypeStruct((nm, 2, d_ff), F32),
        ],
        scratch_shapes=[
            pltpu.VMEM((tm + PAD, 2 * tf), F32),
            pltpu.VMEM((tm + PAD, 2 * tf), F32),
            pltpu.VMEM((nf, PAD, 2 * tf), F32),
            pltpu.VMEM((tm, d), F32),
        ],
        compiler_params=_params(("arbitrary",), 58),
        name="ffn_prompt",
    )(xn, w_up, w_up, w_down, cw, cw, cb.reshape(1, -1), cb.reshape(1, -1))
    tails = jnp.concatenate([csg, csu], axis=-1).reshape(batch, tiles_per_seq, 2, 2 * d_ff)
    return delta, tails[:, -1]


def _ffn_down_sample_kernel(hg_ref, hu_ref, b0g_ref, b0u_ref, b1g_ref, b1u_ref, cwg_ref, cwu_ref,
                            cbg_ref, cbu_ref, wd_ref, x_ref, o_ref, wdb_ref):
    @pl.when(pl.program_id(0) == 0)
    def _():
        o_ref[...] = x_ref[...]

    def conv(h_ref, b0_ref, b1_ref, cw_ref, cb_ref):
        cw = cw_ref[...]
        return cw[0:1] * b0_ref[...] + cw[1:2] * b1_ref[...] + cw[2:3] * h_ref[...] + cb_ref[...]

    hcg = conv(hg_ref, b0g_ref, b1g_ref, cwg_ref, cbg_ref)
    hcu = conv(hu_ref, b0u_ref, b1u_ref, cwu_ref, cbu_ref)
    act = (jax.nn.silu(hcg) * hcu).astype(BF16)
    wdb = wd_ref[...].astype(BF16)
    wdb_ref[...] = wdb
    o_ref[...] += jnp.dot(act, wdb, preferred_element_type=F32)


def _ffn_down_sample(h, state2, cw, cb, wd, x, *, tf):
    nb, two_f = h.shape
    d_ff = two_f // 2
    nf = d_ff // tf
    d = wd.shape[1]
    colblk = lambda off: pl.BlockSpec((nb, tf), lambda j: (0, j + off))
    return pl.pallas_call(
        _ffn_down_sample_kernel,
        grid=(nf,),
        in_specs=[
            colblk(0), colblk(nf),
            colblk(0), colblk(nf), colblk(2 * nf), colblk(3 * nf),
            pl.BlockSpec((3, tf), lambda j: (0, j)),
            pl.BlockSpec((3, tf), lambda j: (0, j + nf)),
            pl.BlockSpec((1, tf), lambda j: (0, j)),
            pl.BlockSpec((1, tf), lambda j: (0, j + nf)),
            pl.BlockSpec((tf, d), lambda j: (j, 0)),
            pl.BlockSpec((nb, d), lambda j: (0, 0)),
        ],
        out_specs=[pl.BlockSpec((nb, d), lambda j: (0, 0)), pl.BlockSpec((tf, d), lambda j: (j, 0))],
        out_shape=[jax.ShapeDtypeStruct((nb, d), F32), jax.ShapeDtypeStruct((d_ff, d), BF16)],
        compiler_params=_params(("arbitrary",), 32),
        name="ffn_down_sample",
    )(h, h, state2, state2, state2, state2, cw, cw, cb.reshape(1, two_f), cb.reshape(1, two_f), wd, x)


def _ple_kernel(*refs, has_delta, emit_w):
    refs = list(refs)
    x_ref = refs.pop(0)
    d_ref = refs.pop(0) if has_delta else None
    nw_ref, w_ref, p_ref, wp_ref, o_ref = refs[:5]
    xn_ref = refs[-1]
    j = pl.program_id(1)
    tn = o_ref.shape[1]

    @pl.when(j == 0)
    def _():
        _rmsnorm_to(x_ref, nw_ref, xn_ref, add_ref=d_ref)

    wb = w_ref[...].astype(BF16)
    wpb = wp_ref[...].astype(BF16)
    if emit_w:
        wb_ref, wpb_ref = refs[5:7]
        wb_ref[...] = wb
        wpb_ref[...] = wpb
    gate = jax.nn.sigmoid(jnp.dot(xn_ref[...], wb, preferred_element_type=F32))
    emb = jnp.dot(p_ref[...].astype(BF16), wpb, preferred_element_type=F32)
    o_ref[...] = gate * emb
    for jj in range(x_ref.shape[1] // tn):
        @pl.when(j == jj)
        def _():
            res = x_ref[:, jj * tn:(jj + 1) * tn]
            if has_delta:
                res = res + d_ref[:, jj * tn:(jj + 1) * tn].astype(F32)
            o_ref[...] += res


def _ple_rows_kernel(x_ref, d_ref, nw_ref, w_ref, p_ref, wp_ref, o_ref):
    x2 = x_ref[...] + d_ref[...].astype(F32)
    xn = _rmsnorm_rows(x2, nw_ref[...]).astype(BF16)
    gate = jax.nn.sigmoid(jnp.dot(xn, w_ref[...], preferred_element_type=F32))
    emb = jnp.dot(p_ref[...].astype(BF16), wp_ref[...], preferred_element_type=F32)
    o_ref[...] = x2 + gate * emb


def _ple_rows(x, delta, nw, w, p, wp, *, tm):
    m, d = x.shape
    kp = p.shape[1]
    rows = lambda width: pl.BlockSpec((tm, width), lambda i: (i, 0))
    resident = lambda shape: pl.BlockSpec(shape, lambda i: (0, 0), pipeline_mode=pl.Buffered(1))
    return pl.pallas_call(
        _ple_rows_kernel,
        grid=(m // tm,),
        in_specs=[rows(d), rows(d), pl.BlockSpec((1, d), lambda i: (0, 0)), resident((d, d)),
                  rows(kp), resident((kp, d))],
        out_specs=rows(d),
        out_shape=jax.ShapeDtypeStruct((m, d), F32),
        compiler_params=_params(("parallel",), 56),
        name="ple_prompt",
    )(x, delta, nw.reshape(1, d), w, p, wp)


def _ple(x, delta, nw, w, p, wp, *, tm, tn, name, emit_w=False):
    m, k = x.shape
    n = w.shape[1]
    kp = p.shape[1]
    assert not emit_w or m == tm
    out_specs = [pl.BlockSpec((tm, tn), lambda i, j: (i, j))]
    out_shape = [jax.ShapeDtypeStruct((m, n), F32)]
    if emit_w:
        out_specs += [pl.BlockSpec((k, tn), lambda i, j: (0, j)), pl.BlockSpec((kp, tn), lambda i, j: (0, j))]
        out_shape += [jax.ShapeDtypeStruct((k, n), BF16), jax.ShapeDtypeStruct((kp, n), BF16)]
    row_full = pl.BlockSpec((tm, k), lambda i, j: (i, 0))
    in_specs = [row_full] + ([row_full] if delta is not None else []) + [
        pl.BlockSpec((1, k), lambda i, j: (0, 0)),
        pl.BlockSpec((k, tn), lambda i, j: (0, j)),
        pl.BlockSpec((tm, kp), lambda i, j: (i, 0)),
        pl.BlockSpec((kp, tn), lambda i, j: (0, j)),
    ]
    args = [x] + ([delta] if delta is not None else []) + [nw.reshape(1, k), w, p, wp]
    outs = pl.pallas_call(
        functools.partial(_ple_kernel, has_delta=delta is not None, emit_w=emit_w),
        grid=(m // tm, n // tn),
        in_specs=in_specs,
        out_specs=out_specs,
        out_shape=out_shape,
        scratch_shapes=[pltpu.VMEM((tm, k), BF16)],
        compiler_params=_params(("parallel", "arbitrary"), 56),
        name=name,
    )(*args)
    return outs if emit_w else outs[0]


def _layer_weights(i, attn_norm_w, w_in, q_norm_w, k_norm_w, attn_sinks, sgu_norm_w, sgu_w, sgu_b,
                   w_br_attn, w_br_gm, w_gate, b_gate, w_out, ffn_norm_w, w_up, conv_w, conv_b,
                   w_down, ple_norm_w, w_ple_gate, w_ple_proj):
    sinks = attn_sinks[i]
    gm_group_w = w_br_gm.shape[1] // GM_GROUPS
    return dict(
        attn_norm_w=attn_norm_w[i], w_in=w_in[i],
        qw2=jnp.tile(q_norm_w[i], HEADS_PER_VREG).reshape(1, LANES),
        kw2=jnp.tile(k_norm_w[i], HEADS_PER_VREG).reshape(1, LANES),
        sinks=sinks,
        sinks_row=jnp.pad(sinks, (0, LANES - N_HEADS)).reshape(1, LANES),
        sgu_norm_w=sgu_norm_w[i], sgu_w=sgu_w[i], sgu_b_t=jnp.transpose(sgu_b[i]),
        sgu_w0_row=jnp.repeat(sgu_w[i][:, 0, 0], gm_group_w).reshape(1, -1),
        sgu_b0_row=jnp.repeat(sgu_b[i][:, 0], gm_group_w).reshape(1, -1),
        w_br_attn=w_br_attn[i], w_br_gm=w_br_gm[i],
        w_gate=w_gate[i], b_gate=b_gate[i], w_out=w_out[i],
        ffn_norm_w=ffn_norm_w[i], w_up=w_up[i], conv_w=conv_w[i], conv_b=conv_b[i],
        w_down=w_down[i], ple_norm_w=ple_norm_w[i], w_ple_gate=w_ple_gate[i], w_ple_proj=w_ple_proj[i],
    )


def _prompt_layer(x, p, lw, wb):
    batch, seq, d = x.shape
    x2d = x.reshape(batch * seq, d)
    attn_w = N_HEADS * HEAD_DIM
    kv_w = N_KV_HEADS * HEAD_DIM
    gm_w = lw["w_br_gm"].shape[0]
    proj, gates = _in_proj(x2d, lw["attn_norm_w"], wb["w_in"], wb["w_gate"], lw["b_gate"], tm=256)
    attn, k_new, v_new = _attn_prompt(proj, lw["qw2"], lw["kw2"], lw["sinks"], batch=batch, seq=seq)
    sgu, sgu_state = _sgu_prompt(proj, lw["sgu_norm_w"], lw["sgu_w"], lw["sgu_b_t"], batch=batch, seq=seq,
                                 gu_off=attn_w + 2 * kv_w, gm_w=gm_w, chunks_per_step=2)
    x1, xn1 = _merge_out(attn, sgu, gates,x2d, wb["w_br_attn"], wb["w_br_gm"], wb["w_out"],
                         lw["ffn_norm_w"], tm=256)
    delta, conv_state = _ffn_prompt(xn1, wb["w_up"], wb["w_down"], lw["conv_w"], lw["conv_b"],
                                    batch=batch, seq=seq, tm=1024)
    x3 = _ple_rows(x1, delta, lw["ple_norm_w"], wb["w_ple_gate"], p.reshape(batch * seq, -1),
                   wb["w_ple_proj"], tm=512)
    return (x3.reshape(batch, seq, d), k_new.reshape(batch, WINDOW, N_KV_HEADS, HEAD_DIM),
            v_new.reshape(batch, WINDOW, N_KV_HEADS, HEAD_DIM), sgu_state, conv_state)


def _sample_layer(x, p, k_buf, v_buf, conv_buf, lw):
    nb, t, d = x.shape
    x2d = x.reshape(nb * t, d)
    attn_w = N_HEADS * HEAD_DIM
    kv_w = N_KV_HEADS * HEAD_DIM
    gm_w = lw["w_br_gm"].shape[0]
    wb = {}
    proj, wb["w_in"] = _norm_mm(x2d, lw["attn_norm_w"], lw["w_in"], None, act=None,
                                tn=512, tile=512, name="proj_sample")
    gates, wb["w_gate"] = _norm_mm(x2d, lw["attn_norm_w"], lw["w_gate"], lw["b_gate"], act="sigmoid",
                                   tn=1024, tile=512, name="gates_sample")
    q = proj[:, :attn_w]
    k = proj[:, attn_w:attn_w + kv_w]
    v = proj[:, attn_w + kv_w:attn_w + 2 * kv_w]
    gu = proj[:, attn_w + 2 * kv_w:attn_w + 2 * kv_w + gm_w]
    gv = proj[:, attn_w + 2 * kv_w + gm_w:]
    attn, k_new, v_new, wb["w_br_attn"], wb["w_br_gm"], wb["w_out"] = _attn_sample(
        q, k, v, k_buf.reshape(nb, WINDOW, kv_w), v_buf.reshape(nb, WINDOW, kv_w),
        lw["qw2"], lw["kw2"], lw["sinks_row"], lw["w_br_attn"], lw["w_br_gm"], lw["w_out"], group=8)
    sgu, sgu_state = _sgu_sample(gu, gv, lw["sgu_norm_w"], lw["sgu_w0_row"], lw["sgu_b0_row"])
    x1, _ = _merge_out(attn, sgu, gates, x2d, wb["w_br_attn"], wb["w_br_gm"], wb["w_out"],
                       lw["ffn_norm_w"], tm=nb)
    h, wb["w_up"] = _norm_mm(x1, lw["ffn_norm_w"], lw["w_up"], None, act=None, tn=1024, tile=512,
                             name="ffn_up_sample")
    x2, wb["w_down"] = _ffn_down_sample(h, conv_buf.reshape(nb, -1), lw["conv_w"], lw["conv_b"], lw["w_down"],
                                        x1, tf=512)
    x3, wb["w_ple_gate"], wb["w_ple_proj"] = _ple(
        x2, None, lw["ple_norm_w"], lw["w_ple_gate"], p.reshape(nb * t, -1), lw["w_ple_proj"],
        tm=nb, tn=1024, name="ple_sample", emit_w=True)
    conv_state = jnp.stack([conv_buf[:, 1, :], h], axis=1)
    return (x3.reshape(nb, t, d), k_new.reshape(nb, WINDOW, N_KV_HEADS, HEAD_DIM),
            v_new.reshape(nb, WINDOW, N_KV_HEADS, HEAD_DIM), sgu_state.reshape(nb, t, gm_w), conv_state), wb


def kernel(x_prompt, x_sample, p_prompt, p_sample, state_attn_k, state_attn_v, state_conv, attn_norm_w, w_in, q_norm_w, k_norm_w, attn_sinks, sgu_norm_w, sgu_w, sgu_b, w_br_attn, w_br_gm, w_gate, b_gate, w_out, ffn_norm_w, w_up, conv_w, conv_b, w_down, ple_norm_w, w_ple_gate, w_ple_proj):
    depth = w_in.shape[0]
    xp, xs = x_prompt, x_sample
    outs = [[] for _ in range(8)]
    for i in range(depth):
        lw = _layer_weights(i, attn_norm_w, w_in, q_norm_w, k_norm_w, attn_sinks, sgu_norm_w, sgu_w, sgu_b,
                            w_br_attn, w_br_gm, w_gate, b_gate, w_out, ffn_norm_w, w_up, conv_w, conv_b,
                            w_down, ple_norm_w, w_ple_gate, w_ple_proj)
        (xs, ks, vs, gs, cs), wb = _sample_layer(xs, p_sample[i], state_attn_k[i], state_attn_v[i],
                                                 state_conv[i], lw)
        xp, kp, vp, gp, cp = _prompt_layer(xp, p_prompt[i], lw, wb)
        for lst, val in zip(outs, (kp, vp, ks, vs, gp, gs, cp, cs)):
            lst.append(val)
    return (xp, xs) + tuple(jnp.stack(lst) for lst in outs)
```

```python
import functools

import jax
import jax.numpy as jnp
from jax import lax
from jax.experimental import pallas as pl
from jax.experimental.pallas import tpu as pltpu

F32 = jnp.float32
BF16 = jnp.bfloat16

HEAD_DIM = 64
N_HEADS = 16
N_KV_HEADS = 4
GQA_GROUP = N_HEADS // N_KV_HEADS
WINDOW = 128
CHUNK = 128
GM_GROUPS = 4
EPS = 1e-6
MASK_VALUE = -1e30
LANES = 128
SUBLANES = 8
HEADS_PER_VREG = LANES // HEAD_DIM
MIB = 1 << 20


def _alibi_slope(h):
    return float(2.0 ** (-8.0 * (h + 1) / N_HEADS))


def _params(semantics, vmem_mib, flags=None):
    return pltpu.CompilerParams(dimension_semantics=semantics, vmem_limit_bytes=vmem_mib * MIB, flags=flags)


def _rmsnorm_rows(x, w):
    ms = jnp.mean(x * x, axis=-1, keepdims=True)
    return x * lax.rsqrt(ms + EPS) * w


NORM_ROWS = 256


def _rmsnorm_to(x_ref, nw_ref, xn_ref, add_ref=None):
    tm = x_ref.shape[0]
    rows = min(NORM_ROWS, tm)

    def body(r, carry):
        sl = pl.ds(pl.multiple_of(r * rows, rows), rows)
        x = x_ref[sl, :]
        if add_ref is not None:
            x = x + add_ref[sl, :].astype(F32)
        xn_ref[sl, :] = _rmsnorm_rows(x, nw_ref[...]).astype(BF16)
        return carry

    lax.fori_loop(0, tm // rows, body, 0)


def _norm_mm_kernel(*refs, has_bias, act):
    if has_bias:
        x_ref, nw_ref, w_ref, b_ref, o_ref, wb_ref, xn_ref = refs
    else:
        x_ref, nw_ref, w_ref, o_ref, wb_ref, xn_ref = refs

    @pl.when(pl.program_id(0) == 0)
    def _():
        _rmsnorm_to(x_ref, nw_ref, xn_ref)

    wb = w_ref[...].astype(BF16)
    tile = wb_ref.shape[2]
    for c in range(wb_ref.shape[0]):
        wb_ref[c] = wb[:, c * tile:(c + 1) * tile]
    acc = jnp.dot(xn_ref[...], wb, preferred_element_type=F32)
    if has_bias:
        acc = acc + b_ref[...]
    if act == "sigmoid":
        acc = jax.nn.sigmoid(acc)
    o_ref[...] = acc.astype(o_ref.dtype)


def _norm_mm(x, nw, w, bias, *, act, tn, tile, name):
    m, k = x.shape
    n = w.shape[1]
    per_step = tn // tile
    in_specs = [
        pl.BlockSpec((m, k), lambda j: (0, 0)),
        pl.BlockSpec((1, k), lambda j: (0, 0)),
        pl.BlockSpec((k, tn), lambda j: (0, j)),
    ]
    args = [x, nw.reshape(1, k), w]
    if bias is not None:
        in_specs.append(pl.BlockSpec((1, tn), lambda j: (0, j)))
        args.append(bias.reshape(1, n))
    return pl.pallas_call(
        functools.partial(_norm_mm_kernel, has_bias=bias is not None, act=act),
        grid=(n // tn,),
        in_specs=in_specs,
        out_specs=[pl.BlockSpec((m, tn), lambda j: (0, j)),
                   pl.BlockSpec((per_step, k, tile), lambda j: (j, 0, 0))],
        out_shape=[jax.ShapeDtypeStruct((m, n), F32), jax.ShapeDtypeStruct((n // tile, k, tile), BF16)],
        scratch_shapes=[pltpu.VMEM((m, k), BF16)],
        compiler_params=_params(("arbitrary",), 56),
        name=name,
    )(*args)


def _in_proj_kernel(x_ref, nw_ref, win_ref, wg_ref, bg_ref, proj_ref, gates_ref, xn_ref):
    xn_ref[...] = _rmsnorm_rows(x_ref[...], nw_ref[...]).astype(BF16)
    tn = win_ref.shape[2]
    for c in range(win_ref.shape[0]):
        acc = jnp.dot(xn_ref[...], win_ref[c], preferred_element_type=F32)
        proj_ref[:, c * tn:(c + 1) * tn] = acc.astype(proj_ref.dtype)
    for c in range(wg_ref.shape[0]):
        cols = slice(c * tn, (c + 1) * tn)
        acc = jnp.dot(xn_ref[...], wg_ref[c], preferred_element_type=F32)
        gates_ref[:, cols] = jax.nn.sigmoid(acc + bg_ref[:, cols]).astype(gates_ref.dtype)


def _in_proj(x, nw, w_in, w_gate, b_gate, *, tm):
    m, k = x.shape
    n_in, _, tn = w_in.shape
    n_g = w_gate.shape[0]
    rows = lambda width: pl.BlockSpec((tm, width), lambda i: (i, 0))
    resident = lambda a: pl.BlockSpec(a.shape, lambda i: (0,) * a.ndim, pipeline_mode=pl.Buffered(1))
    bg = b_gate.reshape(1, -1)
    return pl.pallas_call(
        _in_proj_kernel,
        grid=(m // tm,),
        in_specs=[rows(k), pl.BlockSpec((1, k), lambda i: (0, 0)), resident(w_in), resident(w_gate),
                  pl.BlockSpec(bg.shape, lambda i: (0, 0))],
        out_specs=[rows(n_in * tn), rows(n_g * tn)],
        out_shape=[
            jax.ShapeDtypeStruct((m, n_in * tn), BF16),
            jax.ShapeDtypeStruct((m, n_g * tn), BF16),
        ],
        scratch_shapes=[pltpu.VMEM((tm, k), BF16)],
        compiler_params=_params(("parallel",), 56),
        name="in_proj",
    )(x, nw.reshape(1, k), w_in, w_gate, bg)


def _head_group_matrix():
    r = lax.broadcasted_iota(jnp.int32, (LANES, LANES), 0)
    c = lax.broadcasted_iota(jnp.int32, (LANES, LANES), 1)
    return jnp.where((r >> 6) == (c >> 6), 1.0 / HEAD_DIM, 0.0).astype(BF16)


def _head_rmsnorm(xcol, w, gmat):
    ms = jnp.dot((xcol * xcol).astype(BF16), gmat, preferred_element_type=F32)
    return xcol * lax.rsqrt(ms + EPS) * w


def _attn_prompt_kernel(sink_ref, q_ref, kc_ref, kp_ref, vc_ref, vp_ref, qw_ref, kw_ref,
                        o_ref, kn_ref, vn_ref):
    c = pl.program_id(1)
    blk = WINDOW
    gmat = _head_group_matrix()
    lane = lax.broadcasted_iota(jnp.int32, (1, LANES), 1)
    low = lane < HEAD_DIM

    kw = kw_ref[...]
    qw = qw_ref[...]
    kc = kc_ref[...].astype(F32)
    kp = kp_ref[...].astype(F32)
    vc = vc_ref[...].astype(F32)
    vp = vp_ref[...].astype(F32)
    ncol = kc.shape[1] // LANES
    kcn = [_head_rmsnorm(kc[:, p * LANES:(p + 1) * LANES], kw, gmat) for p in range(ncol)]
    kpn = [_head_rmsnorm(kp[:, p * LANES:(p + 1) * LANES], kw, gmat) for p in range(ncol)]
    kn_ref[0] = jnp.concatenate(kcn, axis=1)
    vn_ref[0] = vc

    kcat = [jnp.concatenate([kpn[p], kcn[p]], axis=0) for p in range(ncol)]
    vcat = [jnp.concatenate([vp[:, p * LANES:(p + 1) * LANES], vc[:, p * LANES:(p + 1) * LANES]], axis=0)
            for p in range(ncol)]
    krol = [pltpu.roll(kcat[p], HEAD_DIM, 1) for p in range(ncol)]
    vrol = [pltpu.roll(vcat[p], HEAD_DIM, 1) for p in range(ncol)]

    qi = lax.broadcasted_iota(jnp.int32, (blk, 2 * blk), 0)
    kj = lax.broadcasted_iota(jnp.int32, (blk, 2 * blk), 1)
    dist = blk + qi - kj
    valid = (dist >= 0) & (dist <= WINDOW) & ((kj >= blk) | (c > 0))
    distf = dist.astype(F32)

    q = q_ref[...].astype(F32)
    qn = [_head_rmsnorm(q[:, p * LANES:(p + 1) * LANES], qw, gmat)
          for p in range(N_HEADS // HEADS_PER_VREG)]

    for g in range(N_KV_HEADS):
        p, half = divmod(g, HEADS_PER_VREG)
        if half == 0:
            kd = jnp.where(low, kcat[p], krol[p])
            vd = jnp.where(low, vcat[p], vrol[p])
        else:
            kd = jnp.where(low, krol[p], kcat[p])
            vd = jnp.where(low, vrol[p], vcat[p])
        kd = kd.astype(BF16)
        vd = vd.astype(BF16)
        qs = []
        for hl in range(GQA_GROUP):
            h = g * GQA_GROUP + hl
            pc, hh = divmod(h, HEADS_PER_VREG)
            keep = low if hh == 0 else jnp.logical_not(low)
            qs.append(jnp.where(keep, qn[pc], 0.0).astype(BF16))
        qstack = jnp.concatenate(qs, axis=0)
        s_all = lax.dot_general(qstack, kd, (((1,), (1,)), ((), ())), preferred_element_type=F32)
        ps = []
        for hl in range(GQA_GROUP):
            h = g * GQA_GROUP + hl
            s = s_all[hl * blk:(hl + 1) * blk] - _alibi_slope(h) * distf
            s = jnp.where(valid, s, MASK_VALUE)
            sink = sink_ref[h]
            mx = jnp.maximum(jnp.max(s, axis=-1, keepdims=True), sink)
            e = jnp.exp(s - mx)
            den = jnp.sum(e, axis=-1, keepdims=True) + jnp.exp(sink - mx)
            ps.append((e / den).astype(BF16))
        pstack = jnp.concatenate(ps, axis=0)
        o_all = jnp.dot(pstack, vd, preferred_element_type=F32)
        for pair in range(GQA_GROUP // HEADS_PER_VREG):
            pc = g * (GQA_GROUP // HEADS_PER_VREG) + pair
            even = o_all[(2 * pair) * blk:(2 * pair + 1) * blk]
            odd = o_all[(2 * pair + 1) * blk:(2 * pair + 2) * blk]
            o_ref[:, pc * LANES:(pc + 1) * LANES] = jnp.where(low, even, odd).astype(o_ref.dtype)


def _attn_prompt(proj, qw2, kw2, sinks, *, batch, seq):
    nb = seq // WINDOW
    attn_w = N_HEADS * HEAD_DIM
    kv_w = N_KV_HEADS * HEAD_DIM
    kcol = attn_w // kv_w
    vcol = kcol + 1

    def cur(col):
        return lambda b, c: (b * nb + c, col)

    def prev(col):
        return lambda b, c: (jnp.maximum(b * nb + c - 1, 0), col)

    return pl.pallas_call(
        _attn_prompt_kernel,
        grid=(batch, nb),
        in_specs=[
            pl.BlockSpec(memory_space=pltpu.SMEM),
            pl.BlockSpec((WINDOW, attn_w), cur(0)),
            pl.BlockSpec((WINDOW, kv_w), cur(kcol)),
            pl.BlockSpec((WINDOW, kv_w), prev(kcol)),
            pl.BlockSpec((WINDOW, kv_w), cur(vcol)),
            pl.BlockSpec((WINDOW, kv_w), prev(vcol)),
            pl.BlockSpec((1, LANES), lambda b, c: (0, 0)),
            pl.BlockSpec((1, LANES), lambda b, c: (0, 0)),
        ],
        out_specs=[
            pl.BlockSpec((WINDOW, attn_w), lambda b, c: (b * nb + c, 0)),
            pl.BlockSpec((1, WINDOW, kv_w), lambda b, c: (b, 0, 0)),
            pl.BlockSpec((1, WINDOW, kv_w), lambda b, c: (b, 0, 0)),
        ],
        out_shape=[
            jax.ShapeDtypeStruct((batch * seq, attn_w), BF16),
            jax.ShapeDtypeStruct((batch, WINDOW, kv_w), F32),
            jax.ShapeDtypeStruct((batch, WINDOW, kv_w), F32),
        ],
        compiler_params=_params(("parallel", "arbitrary"), 32),
        name="attn_prompt",
    )(sinks, proj, proj, proj, proj, proj, qw2, kw2)


def _layernorm_rows(x, w):
    mu = jnp.mean(x, axis=-1, keepdims=True)
    xc = x - mu
    return xc * lax.rsqrt(jnp.mean(xc * xc, axis=-1, keepdims=True) + EPS) * w


GELU_C = 0.7978845608028654
GELU_A = 0.044715


def _gelu_tanh(x):
    return x * jax.nn.sigmoid((2.0 * GELU_C) * (x + GELU_A * (x * x * x)))


def _sgu_prompt_kernel(gu0_ref, gu1_ref, gv0_ref, gv1_ref, nw_ref, ws_ref, bs_ref, o_ref, st_ref):
    rows = gu0_ref.shape[0]
    gv = jnp.concatenate([gv0_ref[...], gv1_ref[...]], axis=1).astype(F32)
    vn = _layernorm_rows(_gelu_tanh(gv), nw_ref[...])
    st_ref[0] = vn[rows - CHUNK:]
    vb = vn.astype(BF16)
    r = lax.broadcasted_iota(jnp.int32, (CHUNK, CHUNK), 0)
    c = lax.broadcasted_iota(jnp.int32, (CHUNK, CHUNK), 1)
    causal = r >= c
    gw = vb.shape[1] // GM_GROUPS
    half = gu0_ref.shape[1]
    for g in range(GM_GROUPS):
        w = jnp.where(causal, ws_ref[g], 0.0).astype(BF16)
        src = gu0_ref if g * gw < half else gu1_ref
        off = g * gw - (0 if g * gw < half else half)
        for ch in range(rows // CHUNK):
            rs = slice(ch * CHUNK, (ch + 1) * CHUNK)
            mix = jnp.dot(w, vb[rs, g * gw:(g + 1) * gw], preferred_element_type=F32) + bs_ref[:, g:g + 1]
            u = _gelu_tanh(src[rs, off:off + gw].astype(F32))
            o_ref[rs, g * gw:(g + 1) * gw] = (u * mix).astype(o_ref.dtype)


def _sgu_prompt(proj, nw, ws, bs_t, *, batch, seq, gu_off, gm_w, chunks_per_step):
    rows = chunks_per_step * CHUNK
    nc = seq // rows
    half = gm_w // 2
    b0 = gu_off // half

    def col(k):
        return lambda b, c: (b * nc + c, b0 + k)

    return pl.pallas_call(
        _sgu_prompt_kernel,
        grid=(batch, nc),
        in_specs=[
            pl.BlockSpec((rows, half), col(0)),
            pl.BlockSpec((rows, half), col(1)),
            pl.BlockSpec((rows, half), col(2)),
            pl.BlockSpec((rows, half), col(3)),
            pl.BlockSpec((1, gm_w), lambda b, c: (0, 0)),
            pl.BlockSpec((GM_GROUPS, CHUNK, CHUNK), lambda b, c: (0, 0, 0)),
            pl.BlockSpec((CHUNK, GM_GROUPS), lambda b, c: (0, 0)),
        ],
        out_specs=[
            pl.BlockSpec((rows, gm_w), lambda b, c: (b * nc + c, 0)),
            pl.BlockSpec((1, CHUNK, gm_w), lambda b, c: (b, 0, 0)),
        ],
        out_shape=[
            jax.ShapeDtypeStruct((batch * seq, gm_w), BF16),
            jax.ShapeDtypeStruct((batch, CHUNK, gm_w), F32),
        ],
        compiler_params=_params(("parallel", "arbitrary"), 32),
        name="sgu_prompt",
    )(proj, proj, proj, proj, nw.reshape(1, gm_w), ws, bs_t)


def _split_dot(x, m):
    hi = x.astype(BF16)
    lo = (x - hi.astype(F32)).astype(BF16)
    return jnp.dot(hi, m, preferred_element_type=F32) + jnp.dot(lo, m, preferred_element_type=F32)


def _attn_sample_kernel(q_ref, kn_ref, vn_ref, kb_ref, vb_ref, qw_ref, kw_ref, sink_ref,
                        wa_ref, wg_ref, wo_ref, o_ref, ko_ref, vo_ref, wab_ref, wgb_ref, wob_ref):
    wab_ref[...] = wa_ref[...].astype(BF16)
    wgb_ref[...] = wg_ref[...].astype(BF16)
    wob_ref[...] = wo_ref[...].astype(BF16)
    nsamp = q_ref.shape[0]
    attn_w = q_ref.shape[1]
    kv_w = kn_ref.shape[1]
    gmat = _head_group_matrix()
    q = q_ref[...].astype(F32)
    kn = kn_ref[...].astype(F32)
    qn = jnp.concatenate(
        [_head_rmsnorm(q[:, p * LANES:(p + 1) * LANES], qw_ref[...], gmat)
         for p in range(attn_w // LANES)], axis=1)
    knn = jnp.concatenate(
        [_head_rmsnorm(kn[:, p * LANES:(p + 1) * LANES], kw_ref[...], gmat)
         for p in range(kv_w // LANES)], axis=1)
    vnn = vn_ref[...].astype(F32)

    ec = lax.broadcasted_iota(jnp.int32, (kv_w, attn_w), 0)
    el = lax.broadcasted_iota(jnp.int32, (kv_w, attn_w), 1)
    expand = (((ec >> 6) == (el >> 8)) & ((ec & 63) == (el & 63))).astype(BF16)
    gl = lax.broadcasted_iota(jnp.int32, (attn_w, LANES), 0)
    gh = lax.broadcasted_iota(jnp.int32, (attn_w, LANES), 1)
    hsum = ((gl >> 6) == gh).astype(BF16)
    tl = lax.broadcasted_iota(jnp.int32, (LANES, attn_w), 1)
    th = lax.broadcasted_iota(jnp.int32, (LANES, attn_w), 0)
    hexp = ((tl >> 6) == th).astype(BF16)

    nkeys = WINDOW + SUBLANES
    row = lax.broadcasted_iota(jnp.int32, (nkeys, LANES), 0)
    head = lax.broadcasted_iota(jnp.int32, (nkeys, LANES), 1)
    slope = jnp.exp2(-8.0 * (head + 1).astype(F32) / N_HEADS)
    dist = (WINDOW - row).astype(F32)
    key_ok = row <= WINDOW
    srow = lax.broadcasted_iota(jnp.int32, (WINDOW, kv_w), 0)
    sinks = sink_ref[...]

    for s in range(nsamp):
        kb = kb_ref[s]
        vb = vb_ref[s]
        knew = knn[s:s + 1]
        vnew = vnn[s:s + 1]
        ko_ref[s] = jnp.where(srow == WINDOW - 1, knew, pltpu.roll(kb, WINDOW - 1, 0))
        vo_ref[s] = jnp.where(srow == WINDOW - 1, vnew, pltpu.roll(vb, WINDOW - 1, 0))
        kk = jnp.concatenate([kb, jnp.broadcast_to(knew, (SUBLANES, kv_w))], axis=0)
        vv = jnp.concatenate([vb, jnp.broadcast_to(vnew, (SUBLANES, kv_w))], axis=0)
        kexp = jnp.dot(kk.astype(BF16), expand, preferred_element_type=F32)
        vexp = jnp.dot(vv.astype(BF16), expand, preferred_element_type=F32)
        sc = _split_dot(kexp * qn[s:s + 1], hsum)
        sc = sc - slope * dist
        sc = jnp.where(key_ok, sc, MASK_VALUE)
        mx = jnp.maximum(jnp.max(sc, axis=0, keepdims=True), sinks)
        e = jnp.exp(sc - mx)
        den = jnp.sum(e, axis=0, keepdims=True) + jnp.exp(sinks - mx)
        pexp = jnp.dot((e / den).astype(BF16), hexp, preferred_element_type=F32)
        o_ref[s:s + 1, :] = jnp.sum(pexp * vexp, axis=0, keepdims=True)


def _attn_sample(q, knew, vnew, kbuf, vbuf, qw2, kw2, sinks_row, wa, wg, wo, *, group):
    nb, attn_w = q.shape
    kv_w = knew.shape[1]
    steps = nb // group
    row = lambda width: pl.BlockSpec((group, width), lambda i: (i, 0))
    buf = pl.BlockSpec((group, WINDOW, kv_w), lambda i: (i, 0, 0))
    vec = pl.BlockSpec((1, LANES), lambda i: (0, 0))
    slab = lambda w: pl.BlockSpec((w.shape[0] // steps, w.shape[1]), lambda i: (i, 0))
    return pl.pallas_call(
        _attn_sample_kernel,
        grid=(steps,),
        in_specs=[row(attn_w), row(kv_w), row(kv_w), buf, buf, vec, vec, vec, slab(wa), slab(wg), slab(wo)],
        out_specs=[row(attn_w), buf, buf, slab(wa), slab(wg), slab(wo)],
        out_shape=[
            jax.ShapeDtypeStruct((nb, attn_w), F32),
            jax.ShapeDtypeStruct((nb, WINDOW, kv_w), F32),
            jax.ShapeDtypeStruct((nb, WINDOW, kv_w), F32),
            jax.ShapeDtypeStruct(wa.shape, BF16),
            jax.ShapeDtypeStruct(wg.shape, BF16),
            jax.ShapeDtypeStruct(wo.shape, BF16),
        ],
        compiler_params=_params(("parallel",), 48),
        name="attn_sample",
    )(q, knew, vnew, kbuf, vbuf, qw2, kw2, sinks_row, wa, wg, wo)


def _sgu_sample_kernel(gu_ref, gv_ref, nw_ref, w0_ref, b0_ref, o_ref, st_ref):
    vn = _layernorm_rows(jax.nn.gelu(gv_ref[...].astype(F32)), nw_ref[...])
    st_ref[...] = vn
    mix = w0_ref[...] * vn + b0_ref[...]
    o_ref[...] = jax.nn.gelu(gu_ref[...].astype(F32)) * mix


def _sgu_sample(gu, gv, nw, w0_row, b0_row):
    nb, gm_w = gu.shape
    full = pl.BlockSpec((nb, gm_w), lambda i: (0, 0))
    vec = pl.BlockSpec((1, gm_w), lambda i: (0, 0))
    return pl.pallas_call(
        _sgu_sample_kernel,
        grid=(1,),
        in_specs=[full, full, vec, vec, vec],
        out_specs=[full, full],
        out_shape=[jax.ShapeDtypeStruct((nb, gm_w), F32), jax.ShapeDtypeStruct((nb, gm_w), F32)],
        name="sgu_sample",
    )(gu, gv, nw.reshape(1, gm_w), w0_row, b0_row)


def _merge_out_kernel(a_ref, s_ref, ga_ref, gb_ref, x_ref, wa_ref, wg_ref, wo_ref, nw_ref,
                      x1_ref, xn1_ref):
    a = jnp.dot(a_ref[...].astype(BF16), wa_ref[...], preferred_element_type=F32)
    m = jnp.dot(s_ref[...].astype(BF16), wg_ref[...], preferred_element_type=F32)
    merged = (gb_ref[...].astype(F32) * a + ga_ref[...].astype(F32) * m).astype(BF16)
    x1 = x_ref[...] + jnp.dot(merged, wo_ref[...], preferred_element_type=F32)
    x1_ref[...] = x1
    xn1_ref[...] = _rmsnorm_rows(x1, nw_ref[...]).astype(BF16)


def _merge_out(attn, sgu, gates, x, wa, wg, wo, nw, *, tm):
    m, ka = attn.shape
    kg = sgu.shape[1]
    d = wo.shape[1]
    rows = lambda width, col=0: pl.BlockSpec((tm, width), lambda i: (i, col))
    resident = lambda shape: pl.BlockSpec(shape, lambda i: (0, 0), pipeline_mode=pl.Buffered(1))
    return pl.pallas_call(
        _merge_out_kernel,
        grid=(m // tm,),
        in_specs=[
            rows(ka), rows(kg), rows(d, 0), rows(d, 1), rows(d),
            resident((ka, d)), resident((kg, d)), resident((d, d)),
            pl.BlockSpec((1, d), lambda i: (0, 0)),
        ],
        out_specs=[rows(d), rows(d)],
        out_shape=[jax.ShapeDtypeStruct((m, d), F32), jax.ShapeDtypeStruct((m, d), BF16)],
        compiler_params=_params(("parallel",), 48),
        name="merge_out",
    )(attn, sgu, gates, gates, x, wa, wg, wo, nw.reshape(1, d))


PAD = SUBLANES
FFN_ROW_BLOCKS = 8


def _ffn_step(t, xn_ref, wug_ref, wuu_ref, wd_ref, cwg_ref, cwu_ref, cbg_ref, cbu_ref,
              csg_ref, csu_ref, h_new, h_old, carry_ref, acc_ref, *, nf, tiles_per_seq):
    tm = xn_ref.shape[0]
    tf = wug_ref.shape[1]
    rb = tm // FFN_ROW_BLOCKS
    tp = jnp.maximum(t - 1, 0)
    ip = tp // nf
    jp = tp % nf
    first = (ip % tiles_per_seq) == 0
    h_old[0:PAD, :] = jnp.where(first, 0.0, carry_ref[jp])
    carry_ref[jp] = h_old[tm:tm + PAD, :]
    cw = jnp.concatenate([cwg_ref[...], cwu_ref[...]], axis=1)
    cb = jnp.concatenate([cbg_ref[...], cbu_ref[...]], axis=1)
    wug = wug_ref[...]
    wuu = wuu_ref[...]
    wd = wd_ref[...]
    for r in range(FFN_ROW_BLOCKS):
        lo = r * rb
        xr = xn_ref[lo:lo + rb, :]
        h_new[PAD + lo:PAD + lo + rb, 0:tf] = jnp.dot(xr, wug, preferred_element_type=F32)
        h_new[PAD + lo:PAD + lo + rb, tf:2 * tf] = jnp.dot(xr, wuu, preferred_element_type=F32)
        hp = h_old[lo:lo + rb + PAD, :]
        hc = (cw[0:1] * pltpu.roll(hp, 2, 0)[PAD:] + cw[1:2] * pltpu.roll(hp, 1, 0)[PAD:]
              + cw[2:3] * hp[PAD:] + cb)
        act = (jax.nn.silu(hc[:, 0:tf]) * hc[:, tf:2 * tf]).astype(BF16)
        acc_ref[lo:lo + rb, :] += jnp.dot(act, wd, preferred_element_type=F32)
    csg_ref[0] = h_old[PAD + tm - 2:PAD + tm, 0:tf]
    csu_ref[0] = h_old[PAD + tm - 2:PAD + tm, tf:2 * tf]


def _ffn_kernel(xn_ref, wug_ref, wuu_ref, wd_ref, cwg_ref, cwu_ref, cbg_ref, cbu_ref,
                o_ref, csg_ref, csu_ref, ha_ref, hb_ref, carry_ref, acc_ref, *, nf, tiles_per_seq):
    t = pl.program_id(0)
    step = functools.partial(_ffn_step, t, xn_ref, wug_ref, wuu_ref, wd_ref, cwg_ref, cwu_ref,
                             cbg_ref, cbu_ref, csg_ref, csu_ref, carry_ref=carry_ref, acc_ref=acc_ref,
                             nf=nf, tiles_per_seq=tiles_per_seq)

    @pl.when(t == 0)
    def _():
        hb_ref[...] = jnp.zeros_like(hb_ref)
        carry_ref[...] = jnp.zeros_like(carry_ref)

    @pl.when(jnp.maximum(t - 1, 0) % nf == 0)
    def _():
        acc_ref[...] = jnp.zeros_like(acc_ref)

    @pl.when(t % 2 == 0)
    def _():
        step(h_new=ha_ref, h_old=hb_ref)

    @pl.when(t % 2 == 1)
    def _():
        step(h_new=hb_ref, h_old=ha_ref)

    @pl.when(jnp.maximum(t - 1, 0) % nf == nf - 1)
    def _():
        o_ref[...] = acc_ref[...].astype(o_ref.dtype)


def _ffn_prompt(xn, w_up, w_down, cw, cb, *, batch, seq, tm):
    m, k = xn.shape
    d_ff, d = w_down.shape
    tf = w_up.shape[2]
    nf = d_ff // tf
    nm = m // tm
    tiles_per_seq = seq // tm
    prev = lambda t: jnp.maximum(t - 1, 0)
    up_tile = lambda off: (lambda t: (off + t % nf, 0, 0))
    dn_col = lambda off: (lambda t: (0, off + prev(t) % nf))
    state = lambda t: (prev(t) // nf, 0, prev(t) % nf)
    delta, csg, csu = pl.pallas_call(
        functools.partial(_ffn_kernel, nf=nf, tiles_per_seq=tiles_per_seq),
        grid=(nm * nf + 1,),
        in_specs=[
            pl.BlockSpec((tm, k), lambda t: (jnp.minimum(t // nf, nm - 1), 0)),
            pl.BlockSpec((None, k, tf), up_tile(0)),
            pl.BlockSpec((None, k, tf), up_tile(nf)),
            pl.BlockSpec((tf, d), lambda t: (prev(t) % nf, 0)),
            pl.BlockSpec((3, tf), dn_col(0)),
            pl.BlockSpec((3, tf), dn_col(nf)),
            pl.BlockSpec((1, tf), dn_col(0)),
            pl.BlockSpec((1, tf), dn_col(nf)),
        ],
        out_specs=[
            pl.BlockSpec((tm, d), lambda t: (prev(t) // nf, 0)),
            pl.BlockSpec((1, 2, tf), state),
            pl.BlockSpec((1, 2, tf), state),
        ],
        out_shape=[
            jax.ShapeDtypeStruct((m, d), BF16),
            jax.ShapeDtypeStruct((nm, 2, d_ff), F32),
            jax.ShapeDtypeStruct((nm, 2, d_ff), F32),
        ],
        scratch_shapes=[
            pltpu.VMEM((tm + PAD, 2 * tf), F32),
            pltpu.VMEM((tm + PAD, 2 * tf), F32),
            pltpu.VMEM((nf, PAD, 2 * tf), F32),
            pltpu.VMEM((tm, d), F32),
        ],
        compiler_params=_params(("arbitrary",), 58),
        name="ffn_prompt",
    )(xn, w_up, w_up, w_down, cw, cw, cb.reshape(1, -1), cb.reshape(1, -1))
    tails = jnp.concatenate([csg, csu], axis=-1).reshape(batch, tiles_per_seq, 2, 2 * d_ff)
    return delta, tails[:, -1]


def _ffn_down_sample_kernel(hg_ref, hu_ref, b0g_ref, b0u_ref, b1g_ref, b1u_ref, cwg_ref, cwu_ref,
                            cbg_ref, cbu_ref, wd_ref, x_ref, o_ref, wdb_ref):
    @pl.when(pl.program_id(0) == 0)
    def _():
        o_ref[...] = x_ref[...]

    def conv(h_ref, b0_ref, b1_ref, cw_ref, cb_ref):
        cw = cw_ref[...]
        return cw[0:1] * b0_ref[...] + cw[1:2] * b1_ref[...] + cw[2:3] * h_ref[...] + cb_ref[...]

    hcg = conv(hg_ref, b0g_ref, b1g_ref, cwg_ref, cbg_ref)
    hcu = conv(hu_ref, b0u_ref, b1u_ref, cwu_ref, cbu_ref)
    act = (jax.nn.silu(hcg) * hcu).astype(BF16)
    wdb = wd_ref[...].astype(BF16)
    wdb_ref[...] = wdb
    o_ref[...] += jnp.dot(act, wdb, preferred_element_type=F32)


def _ffn_down_sample(h, state2, cw, cb, wd, x, *, tf):
    nb, two_f = h.shape
    d_ff = two_f // 2
    nf = d_ff // tf
    d = wd.shape[1]
    colblk = lambda off: pl.BlockSpec((nb, tf), lambda j: (0, j + off))
    return pl.pallas_call(
        _ffn_down_sample_kernel,
        grid=(nf,),
        in_specs=[
            colblk(0), colblk(nf),
            colblk(0), colblk(nf), colblk(2 * nf), colblk(3 * nf),
            pl.BlockSpec((3, tf), lambda j: (0, j)),
            pl.BlockSpec((3, tf), lambda j: (0, j + nf)),
            pl.BlockSpec((1, tf), lambda j: (0, j)),
            pl.BlockSpec((1, tf), lambda j: (0, j + nf)),
            pl.BlockSpec((tf, d), lambda j: (j, 0)),
            pl.BlockSpec((nb, d), lambda j: (0, 0)),
        ],
        out_specs=[pl.BlockSpec((nb, d), lambda j: (0, 0)), pl.BlockSpec((tf, d), lambda j: (j, 0))],
        out_shape=[jax.ShapeDtypeStruct((nb, d), F32), jax.ShapeDtypeStruct((d_ff, d), BF16)],
        compiler_params=_params(("arbitrary",), 32),
        name="ffn_down_sample",
    )(h, h, state2, state2, state2, state2, cw, cw, cb.reshape(1, two_f), cb.reshape(1, two_f), wd, x)


def _ple_kernel(*refs, has_delta, emit_w):
    refs = list(refs)
    x_ref = refs.pop(0)
    d_ref = refs.pop(0) if has_delta else None
    nw_ref, w_ref, p_ref, wp_ref, o_ref = refs[:5]
    xn_ref = refs[-1]
    j = pl.program_id(1)
    tn = o_ref.shape[1]

    @pl.when(j == 0)
    def _():
        _rmsnorm_to(x_ref, nw_ref, xn_ref, add_ref=d_ref)

    wb = w_ref[...].astype(BF16)
    wpb = wp_ref[...].astype(BF16)
    if emit_w:
        wb_ref, wpb_ref = refs[5:7]
        wb_ref[...] = wb
        wpb_ref[...] = wpb
    gate = jax.nn.sigmoid(jnp.dot(xn_ref[...], wb, preferred_element_type=F32))
    emb = jnp.dot(p_ref[...].astype(BF16), wpb, preferred_element_type=F32)
    o_ref[...] = gate * emb
    for jj in range(x_ref.shape[1] // tn):
        @pl.when(j == jj)
        def _():
            res = x_ref[:, jj * tn:(jj + 1) * tn]
            if has_delta:
                res = res + d_ref[:, jj * tn:(jj + 1) * tn].astype(F32)
            o_ref[...] += res


def _ple_rows_kernel(x_ref, d_ref, nw_ref, w_ref, p_ref, wp_ref, o_ref):
    x2 = x_ref[...] + d_ref[...].astype(F32)
    xn = _rmsnorm_rows(x2, nw_ref[...]).astype(BF16)
    gate = jax.nn.sigmoid(jnp.dot(xn, w_ref[...], preferred_element_type=F32))
    emb = jnp.dot(p_ref[...].astype(BF16), wp_ref[...], preferred_element_type=F32)
    o_ref[...] = x2 + gate * emb


def _ple_rows(x, delta, nw, w, p, wp, *, tm):
    m, d = x.shape
    kp = p.shape[1]
    rows = lambda width: pl.BlockSpec((tm, width), lambda i: (i, 0))
    resident = lambda shape: pl.BlockSpec(shape, lambda i: (0, 0), pipeline_mode=pl.Buffered(1))
    return pl.pallas_call(
        _ple_rows_kernel,
        grid=(m // tm,),
        in_specs=[rows(d), rows(d), pl.BlockSpec((1, d), lambda i: (0, 0)), resident((d, d)),
                  rows(kp), resident((kp, d))],
        out_specs=rows(d),
        out_shape=jax.ShapeDtypeStruct((m, d), F32),
        compiler_params=_params(("parallel",), 56),
        name="ple_prompt",
    )(x, delta, nw.reshape(1, d), w, p, wp)


def _ple(x, delta, nw, w, p, wp, *, tm, tn, name, emit_w=False):
    m, k = x.shape
    n = w.shape[1]
    kp = p.shape[1]
    assert not emit_w or m == tm
    out_specs = [pl.BlockSpec((tm, tn), lambda i, j: (i, j))]
    out_shape = [jax.ShapeDtypeStruct((m, n), F32)]
    if emit_w:
        out_specs += [pl.BlockSpec((k, tn), lambda i, j: (0, j)), pl.BlockSpec((kp, tn), lambda i, j: (0, j))]
        out_shape += [jax.ShapeDtypeStruct((k, n), BF16), jax.ShapeDtypeStruct((kp, n), BF16)]
    row_full = pl.BlockSpec((tm, k), lambda i, j: (i, 0))
    in_specs = [row_full] + ([row_full] if delta is not None else []) + [
        pl.BlockSpec((1, k), lambda i, j: (0, 0)),
        pl.BlockSpec((k, tn), lambda i, j: (0, j)),
        pl.BlockSpec((tm, kp), lambda i, j: (i, 0)),
        pl.BlockSpec((kp, tn), lambda i, j: (0, j)),
    ]
    args = [x] + ([delta] if delta is not None else []) + [nw.reshape(1, k), w, p, wp]
    outs = pl.pallas_call(
        functools.partial(_ple_kernel, has_delta=delta is not None, emit_w=emit_w),
        grid=(m // tm, n // tn),
        in_specs=in_specs,
        out_specs=out_specs,
        out_shape=out_shape,
        scratch_shapes=[pltpu.VMEM((tm, k), BF16)],
        compiler_params=_params(("parallel", "arbitrary"), 56),
        name=name,
    )(*args)
    return outs if emit_w else outs[0]


def _layer_weights(i, attn_norm_w, w_in, q_norm_w, k_norm_w, attn_sinks, sgu_norm_w, sgu_w, sgu_b,
                   w_br_attn, w_br_gm, w_gate, b_gate, w_out, ffn_norm_w, w_up, conv_w, conv_b,
                   w_down, ple_norm_w, w_ple_gate, w_ple_proj):
    sinks = attn_sinks[i]
    gm_group_w = w_br_gm.shape[1] // GM_GROUPS
    return dict(
        attn_norm_w=attn_norm_w[i], w_in=w_in[i],
        qw2=jnp.tile(q_norm_w[i] * (HEAD_DIM ** -0.5), HEADS_PER_VREG).reshape(1, LANES),
        kw2=jnp.tile(k_norm_w[i], HEADS_PER_VREG).reshape(1, LANES),
        sinks=sinks,
        sinks_row=jnp.pad(sinks, (0, LANES - N_HEADS)).reshape(1, LANES),
        sgu_norm_w=sgu_norm_w[i], sgu_w=sgu_w[i], sgu_b_t=jnp.transpose(sgu_b[i]),
        sgu_w0_row=jnp.repeat(sgu_w[i][:, 0, 0], gm_group_w).reshape(1, -1),
        sgu_b0_row=jnp.repeat(sgu_b[i][:, 0], gm_group_w).reshape(1, -1),
        w_br_attn=w_br_attn[i], w_br_gm=w_br_gm[i],
        w_gate=w_gate[i], b_gate=b_gate[i], w_out=w_out[i],
        ffn_norm_w=ffn_norm_w[i], w_up=w_up[i], conv_w=conv_w[i], conv_b=conv_b[i],
        w_down=w_down[i], ple_norm_w=ple_norm_w[i], w_ple_gate=w_ple_gate[i], w_ple_proj=w_ple_proj[i],
    )


def _prompt_layer(x, p, lw, wb):
    batch, seq, d = x.shape
    x2d = x.reshape(batch * seq, d)
    attn_w = N_HEADS * HEAD_DIM
    kv_w = N_KV_HEADS * HEAD_DIM
    gm_w = lw["w_br_gm"].shape[0]
    proj, gates = _in_proj(x2d, lw["attn_norm_w"], wb["w_in"], wb["w_gate"], lw["b_gate"], tm=256)
    attn, k_new, v_new = _attn_prompt(proj, lw["qw2"], lw["kw2"], lw["sinks"], batch=batch, seq=seq)
    sgu, sgu_state = _sgu_prompt(proj, lw["sgu_norm_w"], lw["sgu_w"], lw["sgu_b_t"], batch=batch, seq=seq,
                                 gu_off=attn_w + 2 * kv_w, gm_w=gm_w, chunks_per_step=2)
    x1, xn1 = _merge_out(attn, sgu, gates,x2d, wb["w_br_attn"], wb["w_br_gm"], wb["w_out"],
                         lw["ffn_norm_w"], tm=256)
    delta, conv_state = _ffn_prompt(xn1, wb["w_up"], wb["w_down"], lw["conv_w"], lw["conv_b"],
                                    batch=batch, seq=seq, tm=1024)
    x3 = _ple_rows(x1, delta, lw["ple_norm_w"], wb["w_ple_gate"], p.reshape(batch * seq, -1),
                   wb["w_ple_proj"], tm=512)
    return (x3.reshape(batch, seq, d), k_new.reshape(batch, WINDOW, N_KV_HEADS, HEAD_DIM),
            v_new.reshape(batch, WINDOW, N_KV_HEADS, HEAD_DIM), sgu_state, conv_state)


def _sample_layer(x, p, k_buf, v_buf, conv_buf, lw):
    nb, t, d = x.shape
    x2d = x.reshape(nb * t, d)
    attn_w = N_HEADS * HEAD_DIM
    kv_w = N_KV_HEADS * HEAD_DIM
    gm_w = lw["w_br_gm"].shape[0]
    wb = {}
    proj, wb["w_in"] = _norm_mm(x2d, lw["attn_norm_w"], lw["w_in"], None, act=None,
                                tn=512, tile=512, name="proj_sample")
    gates, wb["w_gate"] = _norm_mm(x2d, lw["attn_norm_w"], lw["w_gate"], lw["b_gate"], act="sigmoid",
                                   tn=1024, tile=512, name="gates_sample")
    q = proj[:, :attn_w]
    k = proj[:, attn_w:attn_w + kv_w]
    v = proj[:, attn_w + kv_w:attn_w + 2 * kv_w]
    gu = proj[:, attn_w + 2 * kv_w:attn_w + 2 * kv_w + gm_w]
    gv = proj[:, attn_w + 2 * kv_w + gm_w:]
    attn, k_new, v_new, wb["w_br_attn"], wb["w_br_gm"], wb["w_out"] = _attn_sample(
        q, k, v, k_buf.reshape(nb, WINDOW, kv_w), v_buf.reshape(nb, WINDOW, kv_w),
        lw["qw2"], lw["kw2"], lw["sinks_row"], lw["w_br_attn"], lw["w_br_gm"], lw["w_out"], group=8)
    sgu, sgu_state = _sgu_sample(gu, gv, lw["sgu_norm_w"], lw["sgu_w0_row"], lw["sgu_b0_row"])
    x1, _ = _merge_out(attn, sgu, gates, x2d, wb["w_br_attn"], wb["w_br_gm"], wb["w_out"],
                       lw["ffn_norm_w"], tm=nb)
    h, wb["w_up"] = _norm_mm(x1, lw["ffn_norm_w"], lw["w_up"], None, act=None, tn=1024, tile=512,
                             name="ffn_up_sample")
    x2, wb["w_down"] = _ffn_down_sample(h, conv_buf.reshape(nb, -1), lw["conv_w"], lw["conv_b"], lw["w_down"],
                                        x1, tf=512)
    x3, wb["w_ple_gate"], wb["w_ple_proj"] = _ple(
        x2, None, lw["ple_norm_w"], lw["w_ple_gate"], p.reshape(nb * t, -1), lw["w_ple_proj"],
        tm=nb, tn=1024, name="ple_sample", emit_w=True)
    conv_state = jnp.stack([conv_buf[:, 1, :], h], axis=1)
    return (x3.reshape(nb, t, d), k_new.reshape(nb, WINDOW, N_KV_HEADS, HEAD_DIM),
            v_new.reshape(nb, WINDOW, N_KV_HEADS, HEAD_DIM), sgu_state.reshape(nb, t, gm_w), conv_state), wb


def kernel(x_prompt, x_sample, p_prompt, p_sample, state_attn_k, state_attn_v, state_conv, attn_norm_w, w_in, q_norm_w, k_norm_w, attn_sinks, sgu_norm_w, sgu_w, sgu_b, w_br_attn, w_br_gm, w_gate, b_gate, w_out, ffn_norm_w, w_up, conv_w, conv_b, w_down, ple_norm_w, w_ple_gate, w_ple_proj):
    depth = w_in.shape[0]
    xp, xs = x_prompt, x_sample
    outs = [[] for _ in range(8)]
    for i in range(depth):
        lw = _layer_weights(i, attn_norm_w, w_in, q_norm_w, k_norm_w, attn_sinks, sgu_norm_w, sgu_w, sgu_b,
                            w_br_attn, w_br_gm, w_gate, b_gate, w_out, ffn_norm_w, w_up, conv_w, conv_b,
                            w_down, ple_norm_w, w_ple_gate, w_ple_proj)
        (xs, ks, vs, gs, cs), wb = _sample_layer(xs, p_sample[i], state_attn_k[i], state_attn_v[i],
                                                 state_conv[i], lw)
        xp, kp, vp, gp, cp = _prompt_layer(xp, p_prompt[i], lw, wb)
        for lst, val in zip(outs, (kp, vp, ks, vs, gp, gs, cp, cs)):
            lst.append(val)
    return (xp, xs) + tuple(jnp.stack(lst) for lst in outs)
```

```python
import functools

import jax
import jax.numpy as jnp
from jax import lax
from jax.experimental import pallas as pl
from jax.experimental.pallas import tpu as pltpu

F32 = jnp.float32
BF16 = jnp.bfloat16

HEAD_DIM = 64
N_HEADS = 16
N_KV_HEADS = 4
GQA_GROUP = N_HEADS // N_KV_HEADS
WINDOW = 128
CHUNK = 128
GM_GROUPS = 4
EPS = 1e-6
MASK_VALUE = -1e30
LANES = 128
SUBLANES = 8
HEADS_PER_VREG = LANES // HEAD_DIM
MIB = 1 << 20


def _alibi_slope(h):
    return float(2.0 ** (-8.0 * (h + 1) / N_HEADS))


def _params(semantics, vmem_mib, flags=None):
    return pltpu.CompilerParams(dimension_semantics=semantics, vmem_limit_bytes=vmem_mib * MIB, flags=flags)


def _rmsnorm_rows(x, w):
    ms = jnp.mean(x * x, axis=-1, keepdims=True)
    return x * lax.rsqrt(ms + EPS) * w


NORM_ROWS = 256


def _rmsnorm_to(x_ref, nw_ref, xn_ref, add_ref=None):
    tm = x_ref.shape[0]
    rows = min(NORM_ROWS, tm)

    def body(r, carry):
        sl = pl.ds(pl.multiple_of(r * rows, rows), rows)
        x = x_ref[sl, :]
        if add_ref is not None:
            x = x + add_ref[sl, :].astype(F32)
        xn_ref[sl, :] = _rmsnorm_rows(x, nw_ref[...]).astype(BF16)
        return carry

    lax.fori_loop(0, tm // rows, body, 0)


def _norm_mm_kernel(*refs, has_bias, act):
    if has_bias:
        x_ref, nw_ref, w_ref, b_ref, o_ref, wb_ref, xn_ref = refs
    else:
        x_ref, nw_ref, w_ref, o_ref, wb_ref, xn_ref = refs

    @pl.when(pl.program_id(0) == 0)
    def _():
        _rmsnorm_to(x_ref, nw_ref, xn_ref)

    wb = w_ref[...].astype(BF16)
    tile = wb_ref.shape[2]
    for c in range(wb_ref.shape[0]):
        wb_ref[c] = wb[:, c * tile:(c + 1) * tile]
    acc = jnp.dot(xn_ref[...], wb, preferred_element_type=F32)
    if has_bias:
        acc = acc + b_ref[...]
    if act == "sigmoid":
        acc = jax.nn.sigmoid(acc)
    o_ref[...] = acc.astype(o_ref.dtype)


def _norm_mm(x, nw, w, bias, *, act, tn, tile, name):
    m, k = x.shape
    n = w.shape[1]
    per_step = tn // tile
    in_specs = [
        pl.BlockSpec((m, k), lambda j: (0, 0)),
        pl.BlockSpec((1, k), lambda j: (0, 0)),
        pl.BlockSpec((k, tn), lambda j: (0, j)),
    ]
    args = [x, nw.reshape(1, k), w]
    if bias is not None:
        in_specs.append(pl.BlockSpec((1, tn), lambda j: (0, j)))
        args.append(bias.reshape(1, n))
    return pl.pallas_call(
        functools.partial(_norm_mm_kernel, has_bias=bias is not None, act=act),
        grid=(n // tn,),
        in_specs=in_specs,
        out_specs=[pl.BlockSpec((m, tn), lambda j: (0, j)),
                   pl.BlockSpec((per_step, k, tile), lambda j: (j, 0, 0))],
        out_shape=[jax.ShapeDtypeStruct((m, n), F32), jax.ShapeDtypeStruct((n // tile, k, tile), BF16)],
        scratch_shapes=[pltpu.VMEM((m, k), BF16)],
        compiler_params=_params(("arbitrary",), 56),
        name=name,
    )(*args)


def _in_proj_kernel(x_ref, nw_ref, win_ref, wg_ref, bg_ref, proj_ref, gates_ref, xn_ref):
    xn_ref[...] = _rmsnorm_rows(x_ref[...], nw_ref[...]).astype(BF16)
    tn = win_ref.shape[2]
    for c in range(win_ref.shape[0]):
        acc = jnp.dot(xn_ref[...], win_ref[c], preferred_element_type=F32)
        proj_ref[:, c * tn:(c + 1) * tn] = acc.astype(proj_ref.dtype)
    for c in range(wg_ref.shape[0]):
        cols = slice(c * tn, (c + 1) * tn)
        acc = jnp.dot(xn_ref[...], wg_ref[c], preferred_element_type=F32)
        gates_ref[:, cols] = jax.nn.sigmoid(acc + bg_ref[:, cols]).astype(gates_ref.dtype)


def _in_proj(x, nw, w_in, w_gate, b_gate, *, tm):
    m, k = x.shape
    n_in, _, tn = w_in.shape
    n_g = w_gate.shape[0]
    rows = lambda width: pl.BlockSpec((tm, width), lambda i: (i, 0))
    resident = lambda a: pl.BlockSpec(a.shape, lambda i: (0,) * a.ndim, pipeline_mode=pl.Buffered(1))
    bg = b_gate.reshape(1, -1)
    return pl.pallas_call(
        _in_proj_kernel,
        grid=(m // tm,),
        in_specs=[rows(k), pl.BlockSpec((1, k), lambda i: (0, 0)), resident(w_in), resident(w_gate),
                  pl.BlockSpec(bg.shape, lambda i: (0, 0))],
        out_specs=[rows(n_in * tn), rows(n_g * tn)],
        out_shape=[
            jax.ShapeDtypeStruct((m, n_in * tn), BF16),
            jax.ShapeDtypeStruct((m, n_g * tn), BF16),
        ],
        scratch_shapes=[pltpu.VMEM((tm, k), BF16)],
        compiler_params=_params(("parallel",), 56),
        name="in_proj",
    )(x, nw.reshape(1, k), w_in, w_gate, bg)


def _head_group_matrix():
    r = lax.broadcasted_iota(jnp.int32, (LANES, LANES), 0)
    c = lax.broadcasted_iota(jnp.int32, (LANES, LANES), 1)
    return jnp.where((r >> 6) == (c >> 6), 1.0 / HEAD_DIM, 0.0).astype(BF16)


def _head_rmsnorm(xcol, w, gmat):
    ms = jnp.dot((xcol * xcol).astype(BF16), gmat, preferred_element_type=F32)
    return xcol * lax.rsqrt(ms + EPS) * w


def _attn_prompt_kernel(sink_ref, q_ref, kc_ref, kp_ref, vc_ref, vp_ref, qw_ref, kw_ref,
                        o_ref, kn_ref, vn_ref):
    c = pl.program_id(1)
    blk = WINDOW
    gmat = _head_group_matrix()
    lane = lax.broadcasted_iota(jnp.int32, (1, LANES), 1)
    low = lane < HEAD_DIM

    kw = kw_ref[...]
    qw = qw_ref[...]
    kc = kc_ref[...].astype(F32)
    kp = kp_ref[...].astype(F32)
    vc = vc_ref[...].astype(F32)
    vp = vp_ref[...].astype(F32)
    ncol = kc.shape[1] // LANES
    kcn = [_head_rmsnorm(kc[:, p * LANES:(p + 1) * LANES], kw, gmat) for p in range(ncol)]
    kpn = [_head_rmsnorm(kp[:, p * LANES:(p + 1) * LANES], kw, gmat) for p in range(ncol)]
    kn_ref[0] = jnp.concatenate(kcn, axis=1)
    vn_ref[0] = vc

    kcat = [jnp.concatenate([kpn[p], kcn[p]], axis=0) for p in range(ncol)]
    vcat = [jnp.concatenate([vp[:, p * LANES:(p + 1) * LANES], vc[:, p * LANES:(p + 1) * LANES]], axis=0)
            for p in range(ncol)]
    krol = [pltpu.roll(kcat[p], HEAD_DIM, 1) for p in range(ncol)]
    vrol = [pltpu.roll(vcat[p], HEAD_DIM, 1) for p in range(ncol)]

    qi = lax.broadcasted_iota(jnp.int32, (blk, 2 * blk), 0)
    kj = lax.broadcasted_iota(jnp.int32, (blk, 2 * blk), 1)
    dist = blk + qi - kj
    valid = (dist >= 0) & (dist <= WINDOW) & ((kj >= blk) | (c > 0))
    distf = dist.astype(F32)

    q = q_ref[...].astype(F32)
    qn = [_head_rmsnorm(q[:, p * LANES:(p + 1) * LANES], qw, gmat)
          for p in range(N_HEADS // HEADS_PER_VREG)]

    for g in range(N_KV_HEADS):
        p, half = divmod(g, HEADS_PER_VREG)
        if half == 0:
            kd = jnp.where(low, kcat[p], krol[p])
            vd = jnp.where(low, vcat[p], vrol[p])
        else:
            kd = jnp.where(low, krol[p], kcat[p])
            vd = jnp.where(low, vrol[p], vcat[p])
        kd = kd.astype(BF16)
        vd = vd.astype(BF16)
        qs = []
        for hl in range(GQA_GROUP):
            h = g * GQA_GROUP + hl
            pc, hh = divmod(h, HEADS_PER_VREG)
            keep = low if hh == 0 else jnp.logical_not(low)
            qs.append(jnp.where(keep, qn[pc], 0.0).astype(BF16))
        qstack = jnp.concatenate(qs, axis=0)
        s_all = lax.dot_general(qstack, kd, (((1,), (1,)), ((), ())), preferred_element_type=F32)
        ps = []
        for hl in range(GQA_GROUP):
            h = g * GQA_GROUP + hl
            s = s_all[hl * blk:(hl + 1) * blk] - _alibi_slope(h) * distf
            s = jnp.where(valid, s, MASK_VALUE)
            sink = sink_ref[h]
            mx = jnp.maximum(jnp.max(s, axis=-1, keepdims=True), sink)
            e = jnp.exp(s - mx)
            den = jnp.sum(e, axis=-1, keepdims=True) + jnp.exp(sink - mx)
            ps.append((e / den).astype(BF16))
        pstack = jnp.concatenate(ps, axis=0)
        o_all = jnp.dot(pstack, vd, preferred_element_type=F32)
        for pair in range(GQA_GROUP // HEADS_PER_VREG):
            pc = g * (GQA_GROUP // HEADS_PER_VREG) + pair
            even = o_all[(2 * pair) * blk:(2 * pair + 1) * blk]
            odd = o_all[(2 * pair + 1) * blk:(2 * pair + 2) * blk]
            o_ref[:, pc * LANES:(pc + 1) * LANES] = jnp.where(low, even, odd).astype(o_ref.dtype)


def _attn_prompt(proj, qw2, kw2, sinks, *, batch, seq):
    nb = seq // WINDOW
    attn_w = N_HEADS * HEAD_DIM
    kv_w = N_KV_HEADS * HEAD_DIM
    kcol = attn_w // kv_w
    vcol = kcol + 1

    def cur(col):
        return lambda b, c: (b * nb + c, col)

    def prev(col):
        return lambda b, c: (jnp.maximum(b * nb + c - 1, 0), col)

    return pl.pallas_call(
        _attn_prompt_kernel,
        grid=(batch, nb),
        in_specs=[
            pl.BlockSpec(memory_space=pltpu.SMEM),
            pl.BlockSpec((WINDOW, attn_w), cur(0)),
            pl.BlockSpec((WINDOW, kv_w), cur(kcol)),
            pl.BlockSpec((WINDOW, kv_w), prev(kcol)),
            pl.BlockSpec((WINDOW, kv_w), cur(vcol)),
            pl.BlockSpec((WINDOW, kv_w), prev(vcol)),
            pl.BlockSpec((1, LANES), lambda b, c: (0, 0)),
            pl.BlockSpec((1, LANES), lambda b, c: (0, 0)),
        ],
        out_specs=[
            pl.BlockSpec((WINDOW, attn_w), lambda b, c: (b * nb + c, 0)),
            pl.BlockSpec((1, WINDOW, kv_w), lambda b, c: (b, 0, 0)),
            pl.BlockSpec((1, WINDOW, kv_w), lambda b, c: (b, 0, 0)),
        ],
        out_shape=[
            jax.ShapeDtypeStruct((batch * seq, attn_w), BF16),
            jax.ShapeDtypeStruct((batch, WINDOW, kv_w), F32),
            jax.ShapeDtypeStruct((batch, WINDOW, kv_w), F32),
        ],
        compiler_params=_params(("parallel", "arbitrary"), 32),
        name="attn_prompt",
    )(sinks, proj, proj, proj, proj, proj, qw2, kw2)


def _layernorm_rows(x, w):
    mu = jnp.mean(x, axis=-1, keepdims=True)
    xc = x - mu
    return xc * lax.rsqrt(jnp.mean(xc * xc, axis=-1, keepdims=True) + EPS) * w


GELU_C = 0.7978845608028654
GELU_A = 0.044715


def _gelu_tanh(x):
    return x * jax.nn.sigmoid((2.0 * GELU_C) * (x + GELU_A * (x * x * x)))


def _sgu_prompt_kernel(gu0_ref, gu1_ref, gv0_ref, gv1_ref, nw_ref, ws_ref, bs_ref, o_ref, st_ref):
    rows = gu0_ref.shape[0]
    gv = jnp.concatenate([gv0_ref[...], gv1_ref[...]], axis=1).astype(F32)
    vn = _layernorm_rows(_gelu_tanh(gv), nw_ref[...])
    st_ref[0] = vn[rows - CHUNK:]
    vb = vn.astype(BF16)
    r = lax.broadcasted_iota(jnp.int32, (CHUNK, CHUNK), 0)
    c = lax.broadcasted_iota(jnp.int32, (CHUNK, CHUNK), 1)
    causal = r >= c
    gw = vb.shape[1] // GM_GROUPS
    half = gu0_ref.shape[1]
    for g in range(GM_GROUPS):
        w = jnp.where(causal, ws_ref[g], 0.0).astype(BF16)
        src = gu0_ref if g * gw < half else gu1_ref
        off = g * gw - (0 if g * gw < half else half)
        for ch in range(rows // CHUNK):
            rs = slice(ch * CHUNK, (ch + 1) * CHUNK)
            mix = jnp.dot(w, vb[rs, g * gw:(g + 1) * gw], preferred_element_type=F32) + bs_ref[:, g:g + 1]
            u = _gelu_tanh(src[rs, off:off + gw].astype(F32))
            o_ref[rs, g * gw:(g + 1) * gw] = (u * mix).astype(o_ref.dtype)


def _sgu_prompt(proj, nw, ws, bs_t, *, batch, seq, gu_off, gm_w, chunks_per_step):
    rows = chunks_per_step * CHUNK
    nc = seq // rows
    half = gm_w // 2
    b0 = gu_off // half

    def col(k):
        return lambda b, c: (b * nc + c, b0 + k)

    return pl.pallas_call(
        _sgu_prompt_kernel,
        grid=(batch, nc),
        in_specs=[
            pl.BlockSpec((rows, half), col(0)),
            pl.BlockSpec((rows, half), col(1)),
            pl.BlockSpec((rows, half), col(2)),
            pl.BlockSpec((rows, half), col(3)),
            pl.BlockSpec((1, gm_w), lambda b, c: (0, 0)),
            pl.BlockSpec((GM_GROUPS, CHUNK, CHUNK), lambda b, c: (0, 0, 0)),
            pl.BlockSpec((CHUNK, GM_GROUPS), lambda b, c: (0, 0)),
        ],
        out_specs=[
            pl.BlockSpec((rows, gm_w), lambda b, c: (b * nc + c, 0)),
            pl.BlockSpec((1, CHUNK, gm_w), lambda b, c: (b, 0, 0)),
        ],
        out_shape=[
            jax.ShapeDtypeStruct((batch * seq, gm_w), BF16),
            jax.ShapeDtypeStruct((batch, CHUNK, gm_w), F32),
        ],
        compiler_params=_params(("parallel", "arbitrary"), 32),
        name="sgu_prompt",
    )(proj, proj, proj, proj, nw.reshape(1, gm_w), ws, bs_t)


def _split_dot(x, m):
    hi = x.astype(BF16)
    lo = (x - hi.astype(F32)).astype(BF16)
    return jnp.dot(hi, m, preferred_element_type=F32) + jnp.dot(lo, m, preferred_element_type=F32)


def _attn_sample_kernel(q_ref, kn_ref, vn_ref, kb_ref, vb_ref, qw_ref, kw_ref, sink_ref,
                        wa_ref, wg_ref, wo_ref, o_ref, ko_ref, vo_ref, wab_ref, wgb_ref, wob_ref):
    wab_ref[...] = wa_ref[...].astype(BF16)
    wgb_ref[...] = wg_ref[...].astype(BF16)
    wob_ref[...] = wo_ref[...].astype(BF16)
    nsamp = q_ref.shape[0]
    attn_w = q_ref.shape[1]
    kv_w = kn_ref.shape[1]
    gmat = _head_group_matrix()
    q = q_ref[...].astype(F32)
    kn = kn_ref[...].astype(F32)
    qn = jnp.concatenate(
        [_head_rmsnorm(q[:, p * LANES:(p + 1) * LANES], qw_ref[...], gmat)
         for p in range(attn_w // LANES)], axis=1)
    knn = jnp.concatenate(
        [_head_rmsnorm(kn[:, p * LANES:(p + 1) * LANES], kw_ref[...], gmat)
         for p in range(kv_w // LANES)], axis=1)
    vnn = vn_ref[...].astype(F32)

    ec = lax.broadcasted_iota(jnp.int32, (kv_w, attn_w), 0)
    el = lax.broadcasted_iota(jnp.int32, (kv_w, attn_w), 1)
    expand = (((ec >> 6) == (el >> 8)) & ((ec & 63) == (el & 63))).astype(BF16)
    gl = lax.broadcasted_iota(jnp.int32, (attn_w, LANES), 0)
    gh = lax.broadcasted_iota(jnp.int32, (attn_w, LANES), 1)
    hsum = ((gl >> 6) == gh).astype(BF16)
    tl = lax.broadcasted_iota(jnp.int32, (LANES, attn_w), 1)
    th = lax.broadcasted_iota(jnp.int32, (LANES, attn_w), 0)
    hexp = ((tl >> 6) == th).astype(BF16)

    nkeys = WINDOW + SUBLANES
    row = lax.broadcasted_iota(jnp.int32, (nkeys, LANES), 0)
    head = lax.broadcasted_iota(jnp.int32, (nkeys, LANES), 1)
    slope = jnp.exp2(-8.0 * (head + 1).astype(F32) / N_HEADS)
    dist = (WINDOW - row).astype(F32)
    key_ok = row <= WINDOW
    srow = lax.broadcasted_iota(jnp.int32, (WINDOW, kv_w), 0)
    sinks = sink_ref[...]

    for s in range(nsamp):
        kb = kb_ref[s]
        vb = vb_ref[s]
        knew = knn[s:s + 1]
        vnew = vnn[s:s + 1]
        ko_ref[s] = jnp.where(srow == WINDOW - 1, knew, pltpu.roll(kb, WINDOW - 1, 0))
        vo_ref[s] = jnp.where(srow == WINDOW - 1, vnew, pltpu.roll(vb, WINDOW - 1, 0))
        kk = jnp.concatenate([kb, jnp.broadcast_to(knew, (SUBLANES, kv_w))], axis=0)
        vv = jnp.concatenate([vb, jnp.broadcast_to(vnew, (SUBLANES, kv_w))], axis=0)
        kexp = jnp.dot(kk.astype(BF16), expand, preferred_element_type=F32)
        vexp = jnp.dot(vv.astype(BF16), expand, preferred_element_type=F32)
        sc = _split_dot(kexp * qn[s:s + 1], hsum)
        sc = sc - slope * dist
        sc = jnp.where(key_ok, sc, MASK_VALUE)
        mx = jnp.maximum(jnp.max(sc, axis=0, keepdims=True), sinks)
        e = jnp.exp(sc - mx)
        den = jnp.sum(e, axis=0, keepdims=True) + jnp.exp(sinks - mx)
        pexp = jnp.dot((e / den).astype(BF16), hexp, preferred_element_type=F32)
        o_ref[s:s + 1, :] = jnp.sum(pexp * vexp, axis=0, keepdims=True)


def _attn_sample(q, knew, vnew, kbuf, vbuf, qw2, kw2, sinks_row, wa, wg, wo, *, group):
    nb, attn_w = q.shape
    kv_w = knew.shape[1]
    steps = nb // group
    row = lambda width: pl.BlockSpec((group, width), lambda i: (i, 0))
    buf = pl.BlockSpec((group, WINDOW, kv_w), lambda i: (i, 0, 0))
    vec = pl.BlockSpec((1, LANES), lambda i: (0, 0))
    slab = lambda w: pl.BlockSpec((w.shape[0] // steps, w.shape[1]), lambda i: (i, 0))
    return pl.pallas_call(
        _attn_sample_kernel,
        grid=(steps,),
        in_specs=[row(attn_w), row(kv_w), row(kv_w), buf, buf, vec, vec, vec, slab(wa), slab(wg), slab(wo)],
        out_specs=[row(attn_w), buf, buf, slab(wa), slab(wg), slab(wo)],
        out_shape=[
            jax.ShapeDtypeStruct((nb, attn_w), F32),
            jax.ShapeDtypeStruct((nb, WINDOW, kv_w), F32),
            jax.ShapeDtypeStruct((nb, WINDOW, kv_w), F32),
            jax.ShapeDtypeStruct(wa.shape, BF16),
            jax.ShapeDtypeStruct(wg.shape, BF16),
            jax.ShapeDtypeStruct(wo.shape, BF16),
        ],
        compiler_params=_params(("parallel",), 48),
        name="attn_sample",
    )(q, knew, vnew, kbuf, vbuf, qw2, kw2, sinks_row, wa, wg, wo)


def _sgu_sample_kernel(gu_ref, gv_ref, nw_ref, w0_ref, b0_ref, o_ref, st_ref):
    vn = _layernorm_rows(jax.nn.gelu(gv_ref[...].astype(F32)), nw_ref[...])
    st_ref[...] = vn
    mix = w0_ref[...] * vn + b0_ref[...]
    o_ref[...] = jax.nn.gelu(gu_ref[...].astype(F32)) * mix


def _sgu_sample(gu, gv, nw, w0_row, b0_row):
    nb, gm_w = gu.shape
    full = pl.BlockSpec((nb, gm_w), lambda i: (0, 0))
    vec = pl.BlockSpec((1, gm_w), lambda i: (0, 0))
    return pl.pallas_call(
        _sgu_sample_kernel,
        grid=(1,),
        in_specs=[full, full, vec, vec, vec],
        out_specs=[full, full],
        out_shape=[jax.ShapeDtypeStruct((nb, gm_w), F32), jax.ShapeDtypeStruct((nb, gm_w), F32)],
        name="sgu_sample",
    )(gu, gv, nw.reshape(1, gm_w), w0_row, b0_row)


def _merge_out_kernel(a_ref, s_ref, ga_ref, gb_ref, x_ref, wa_ref, wg_ref, wo_ref, nw_ref,
                      x1_ref, xn1_ref):
    a = jnp.dot(a_ref[...].astype(BF16), wa_ref[...], preferred_element_type=F32)
    m = jnp.dot(s_ref[...].astype(BF16), wg_ref[...], preferred_element_type=F32)
    merged = (gb_ref[...].astype(F32) * a + ga_ref[...].astype(F32) * m).astype(BF16)
    x1 = x_ref[...] + jnp.dot(merged, wo_ref[...], preferred_element_type=F32)
    x1_ref[...] = x1
    xn1_ref[...] = _rmsnorm_rows(x1, nw_ref[...]).astype(BF16)


def _merge_out(attn, sgu, gates, x, wa, wg, wo, nw, *, tm):
    m, ka = attn.shape
    kg = sgu.shape[1]
    d = wo.shape[1]
    rows = lambda width, col=0: pl.BlockSpec((tm, width), lambda i: (i, col))
    resident = lambda shape: pl.BlockSpec(shape, lambda i: (0, 0), pipeline_mode=pl.Buffered(1))
    return pl.pallas_call(
        _merge_out_kernel,
        grid=(m // tm,),
        in_specs=[
            rows(ka), rows(kg), rows(d, 0), rows(d, 1), rows(d),
            resident((ka, d)), resident((kg, d)), resident((d, d)),
            pl.BlockSpec((1, d), lambda i: (0, 0)),
        ],
        out_specs=[rows(d), rows(d)],
        out_shape=[jax.ShapeDtypeStruct((m, d), F32), jax.ShapeDtypeStruct((m, d), BF16)],
        compiler_params=_params(("parallel",), 48),
        name="merge_out",
    )(attn, sgu, gates, gates, x, wa, wg, wo, nw.reshape(1, d))


PAD = SUBLANES
FFN_ROW_BLOCKS = 2


def _ffn_step(t, xn_ref, wug_ref, wuu_ref, wd_ref, cwg_ref, cwu_ref, cbg_ref, cbu_ref,
              csg_ref, csu_ref, h_new, h_old, carry_ref, acc_ref, *, nf, tiles_per_seq):
    tm = xn_ref.shape[0]
    tf = wug_ref.shape[1]
    rb = tm // FFN_ROW_BLOCKS
    tp = jnp.maximum(t - 1, 0)
    ip = tp // nf
    jp = tp % nf
    first = (ip % tiles_per_seq) == 0
    h_old[0:PAD, :] = jnp.where(first, 0.0, carry_ref[jp])
    carry_ref[jp] = h_old[tm:tm + PAD, :]
    cw = jnp.concatenate([cwg_ref[...], cwu_ref[...]], axis=1)
    cb = jnp.concatenate([cbg_ref[...], cbu_ref[...]], axis=1)
    wug = wug_ref[...]
    wuu = wuu_ref[...]
    wd = wd_ref[...]
    for r in range(FFN_ROW_BLOCKS):
        lo = r * rb
        xr = xn_ref[lo:lo + rb, :]
        h_new[PAD + lo:PAD + lo + rb, 0:tf] = jnp.dot(xr, wug, preferred_element_type=F32)
        h_new[PAD + lo:PAD + lo + rb, tf:2 * tf] = jnp.dot(xr, wuu, preferred_element_type=F32)
        hp = h_old[lo:lo + rb + PAD, :]
        hc = (cw[0:1] * pltpu.roll(hp, 2, 0)[PAD:] + cw[1:2] * pltpu.roll(hp, 1, 0)[PAD:]
              + cw[2:3] * hp[PAD:] + cb)
        act = (jax.nn.silu(hc[:, 0:tf]) * hc[:, tf:2 * tf]).astype(BF16)
        acc_ref[lo:lo + rb, :] += jnp.dot(act, wd, preferred_element_type=F32)
    csg_ref[0] = h_old[PAD + tm - 2:PAD + tm, 0:tf]
    csu_ref[0] = h_old[PAD + tm - 2:PAD + tm, tf:2 * tf]


def _ffn_kernel(xn_ref, wug_ref, wuu_ref, wd_ref, cwg_ref, cwu_ref, cbg_ref, cbu_ref,
                o_ref, csg_ref, csu_ref, ha_ref, hb_ref, carry_ref, acc_ref, *, nf, tiles_per_seq):
    t = pl.program_id(0)
    step = functools.partial(_ffn_step, t, xn_ref, wug_ref, wuu_ref, wd_ref, cwg_ref, cwu_ref,
                             cbg_ref, cbu_ref, csg_ref, csu_ref, carry_ref=carry_ref, acc_ref=acc_ref,
                             nf=nf, tiles_per_seq=tiles_per_seq)

    @pl.when(t == 0)
    def _():
        hb_ref[...] = jnp.zeros_like(hb_ref)
        carry_ref[...] = jnp.zeros_like(carry_ref)

    @pl.when(jnp.maximum(t - 1, 0) % nf == 0)
    def _():
        acc_ref[...] = jnp.zeros_like(acc_ref)

    @pl.when(t % 2 == 0)
    def _():
        step(h_new=ha_ref, h_old=hb_ref)

    @pl.when(t % 2 == 1)
    def _():
        step(h_new=hb_ref, h_old=ha_ref)

    @pl.when(jnp.maximum(t - 1, 0) % nf == nf - 1)
    def _():
        o_ref[...] = acc_ref[...].astype(o_ref.dtype)


def _ffn_prompt(xn, w_up, w_down, cw, cb, *, batch, seq, tm):
    m, k = xn.shape
    d_ff, d = w_down.shape
    tf = w_up.shape[2]
    nf = d_ff // tf
    nm = m // tm
    tiles_per_seq = seq // tm
    prev = lambda t: jnp.maximum(t - 1, 0)
    up_tile = lambda off: (lambda t: (off + t % nf, 0, 0))
    dn_col = lambda off: (lambda t: (0, off + prev(t) % nf))
    state = lambda t: (prev(t) // nf, 0, prev(t) % nf)
    delta, csg, csu = pl.pallas_call(
        functools.partial(_ffn_kernel, nf=nf, tiles_per_seq=tiles_per_seq),
        grid=(nm * nf + 1,),
        in_specs=[
            pl.BlockSpec((tm, k), lambda t: (jnp.minimum(t // nf, nm - 1), 0)),
            pl.BlockSpec((None, k, tf), up_tile(0)),
            pl.BlockSpec((None, k, tf), up_tile(nf)),
            pl.BlockSpec((tf, d), lambda t: (prev(t) % nf, 0)),
            pl.BlockSpec((3, tf), dn_col(0)),
            pl.BlockSpec((3, tf), dn_col(nf)),
            pl.BlockSpec((1, tf), dn_col(0)),
            pl.BlockSpec((1, tf), dn_col(nf)),
        ],
        out_specs=[
            pl.BlockSpec((tm, d), lambda t: (prev(t) // nf, 0)),
            pl.BlockSpec((1, 2, tf), state),
            pl.BlockSpec((1, 2, tf), state),
        ],
        out_shape=[
            jax.ShapeDtypeStruct((m, d), BF16),
            jax.ShapeDtypeStruct((nm, 2, d_ff), F32),
            jax.ShapeDtypeStruct((nm, 2, d_ff), F32),
        ],
        scratch_shapes=[
            pltpu.VMEM((tm + PAD, 2 * tf), F32),
            pltpu.VMEM((tm + PAD, 2 * tf), F32),
            pltpu.VMEM((nf, PAD, 2 * tf), F32),
            pltpu.VMEM((tm, d), F32),
        ],
        compiler_params=_params(("arbitrary",), 58),
        name="ffn_prompt",
    )(xn, w_up, w_up, w_down, cw, cw, cb.reshape(1, -1), cb.reshape(1, -1))
    tails = jnp.concatenate([csg, csu], axis=-1).reshape(batch, tiles_per_seq, 2, 2 * d_ff)
    return delta, tails[:, -1]


def _ffn_down_sample_kernel(hg_ref, hu_ref, b0g_ref, b0u_ref, b1g_ref, b1u_ref, cwg_ref, cwu_ref,
                            cbg_ref, cbu_ref, wd_ref, x_ref, o_ref, wdb_ref):
    @pl.when(pl.program_id(0) == 0)
    def _():
        o_ref[...] = x_ref[...]

    def conv(h_ref, b0_ref, b1_ref, cw_ref, cb_ref):
        cw = cw_ref[...]
        return cw[0:1] * b0_ref[...] + cw[1:2] * b1_ref[...] + cw[2:3] * h_ref[...] + cb_ref[...]

    hcg = conv(hg_ref, b0g_ref, b1g_ref, cwg_ref, cbg_ref)
    hcu = conv(hu_ref, b0u_ref, b1u_ref, cwu_ref, cbu_ref)
    act = (jax.nn.silu(hcg) * hcu).astype(BF16)
    wdb = wd_ref[...].astype(BF16)
    wdb_ref[...] = wdb
    o_ref[...] += jnp.dot(act, wdb, preferred_element_type=F32)


def _ffn_down_sample(h, state2, cw, cb, wd, x, *, tf):
    nb, two_f = h.shape
    d_ff = two_f // 2
    nf = d_ff // tf
    d = wd.shape[1]
    colblk = lambda off: pl.BlockSpec((nb, tf), lambda j: (0, j + off))
    return pl.pallas_call(
        _ffn_down_sample_kernel,
        grid=(nf,),
        in_specs=[
            colblk(0), colblk(nf),
            colblk(0), colblk(nf), colblk(2 * nf), colblk(3 * nf),
            pl.BlockSpec((3, tf), lambda j: (0, j)),
            pl.BlockSpec((3, tf), lambda j: (0, j + nf)),
            pl.BlockSpec((1, tf), lambda j: (0, j)),
            pl.BlockSpec((1, tf), lambda j: (0, j + nf)),
            pl.BlockSpec((tf, d), lambda j: (j, 0)),
            pl.BlockSpec((nb, d), lambda j: (0, 0)),
        ],
        out_specs=[pl.BlockSpec((nb, d), lambda j: (0, 0)), pl.BlockSpec((tf, d), lambda j: (j, 0))],
        out_shape=[jax.ShapeDtypeStruct((nb, d), F32), jax.ShapeDtypeStruct((d_ff, d), BF16)],
        compiler_params=_params(("arbitrary",), 32),
        name="ffn_down_sample",
    )(h, h, state2, state2, state2, state2, cw, cw, cb.reshape(1, two_f), cb.reshape(1, two_f), wd, x)


def _ple_kernel(*refs, has_delta, emit_w):
    refs = list(refs)
    x_ref = refs.pop(0)
    d_ref = refs.pop(0) if has_delta else None
    nw_ref, w_ref, p_ref, wp_ref, o_ref = refs[:5]
    xn_ref = refs[-1]
    j = pl.program_id(1)
    tn = o_ref.shape[1]

    @pl.when(j == 0)
    def _():
        _rmsnorm_to(x_ref, nw_ref, xn_ref, add_ref=d_ref)

    wb = w_ref[...].astype(BF16)
    wpb = wp_ref[...].astype(BF16)
    if emit_w:
        wb_ref, wpb_ref = refs[5:7]
        wb_ref[...] = wb
        wpb_ref[...] = wpb
    gate = jax.nn.sigmoid(jnp.dot(xn_ref[...], wb, preferred_element_type=F32))
    emb = jnp.dot(p_ref[...].astype(BF16), wpb, preferred_element_type=F32)
    o_ref[...] = gate * emb
    for jj in range(x_ref.shape[1] // tn):
        @pl.when(j == jj)
        def _():
            res = x_ref[:, jj * tn:(jj + 1) * tn]
            if has_delta:
                res = res + d_ref[:, jj * tn:(jj + 1) * tn].astype(F32)
            o_ref[...] += res


def _ple_rows_kernel(x_ref, d_ref, nw_ref, w_ref, p_ref, wp_ref, o_ref):
    x2 = x_ref[...] + d_ref[...].astype(F32)
    xn = _rmsnorm_rows(x2, nw_ref[...]).astype(BF16)
    gate = jax.nn.sigmoid(jnp.dot(xn, w_ref[...], preferred_element_type=F32))
    emb = jnp.dot(p_ref[...].astype(BF16), wp_ref[...], preferred_element_type=F32)
    o_ref[...] = x2 + gate * emb


def _ple_rows(x, delta, nw, w, p, wp, *, tm):
    m, d = x.shape
    kp = p.shape[1]
    rows = lambda width: pl.BlockSpec((tm, width), lambda i: (i, 0))
    resident = lambda shape: pl.BlockSpec(shape, lambda i: (0, 0), pipeline_mode=pl.Buffered(1))
    return pl.pallas_call(
        _ple_rows_kernel,
        grid=(m // tm,),
        in_specs=[rows(d), rows(d), pl.BlockSpec((1, d), lambda i: (0, 0)), resident((d, d)),
                  rows(kp), resident((kp, d))],
        out_specs=rows(d),
        out_shape=jax.ShapeDtypeStruct((m, d), F32),
        compiler_params=_params(("parallel",), 56),
        name="ple_prompt",
    )(x, delta, nw.reshape(1, d), w, p, wp)


def _ple(x, delta, nw, w, p, wp, *, tm, tn, name, emit_w=False):
    m, k = x.shape
    n = w.shape[1]
    kp = p.shape[1]
    assert not emit_w or m == tm
    out_specs = [pl.BlockSpec((tm, tn), lambda i, j: (i, j))]
    out_shape = [jax.ShapeDtypeStruct((m, n), F32)]
    if emit_w:
        out_specs += [pl.BlockSpec((k, tn), lambda i, j: (0, j)), pl.BlockSpec((kp, tn), lambda i, j: (0, j))]
        out_shape += [jax.ShapeDtypeStruct((k, n), BF16), jax.ShapeDtypeStruct((kp, n), BF16)]
    row_full = pl.BlockSpec((tm, k), lambda i, j: (i, 0))
    in_specs = [row_full] + ([row_full] if delta is not None else []) + [
        pl.BlockSpec((1, k), lambda i, j: (0, 0)),
        pl.BlockSpec((k, tn), lambda i, j: (0, j)),
        pl.BlockSpec((tm, kp), lambda i, j: (i, 0)),
        pl.BlockSpec((kp, tn), lambda i, j: (0, j)),
    ]
    args = [x] + ([delta] if delta is not None else []) + [nw.reshape(1, k), w, p, wp]
    outs = pl.pallas_call(
        functools.partial(_ple_kernel, has_delta=delta is not None, emit_w=emit_w),
        grid=(m // tm, n // tn),
        in_specs=in_specs,
        out_specs=out_specs,
        out_shape=out_shape,
        scratch_shapes=[pltpu.VMEM((tm, k), BF16)],
        compiler_params=_params(("parallel", "arbitrary"), 56),
        name=name,
    )(*args)
    return outs if emit_w else outs[0]


def _layer_weights(i, attn_norm_w, w_in, q_norm_w, k_norm_w, attn_sinks, sgu_norm_w, sgu_w, sgu_b,
                   w_br_attn, w_br_gm, w_gate, b_gate, w_out, ffn_norm_w, w_up, conv_w, conv_b,
                   w_down, ple_norm_w, w_ple_gate, w_ple_proj):
    sinks = attn_sinks[i]
    gm_group_w = w_br_gm.shape[1] // GM_GROUPS
    return dict(
        attn_norm_w=attn_norm_w[i], w_in=w_in[i],
        qw2=jnp.tile(q_norm_w[i] * (HEAD_DIM ** -0.5), HEADS_PER_VREG).reshape(1, LANES),
        kw2=jnp.tile(k_norm_w[i], HEADS_PER_VREG).reshape(1, LANES),
        sinks=sinks,
        sinks_row=jnp.pad(sinks, (0, LANES - N_HEADS)).reshape(1, LANES),
        sgu_norm_w=sgu_norm_w[i], sgu_w=sgu_w[i], sgu_b_t=jnp.transpose(sgu_b[i]),
        sgu_w0_row=jnp.repeat(sgu_w[i][:, 0, 0], gm_group_w).reshape(1, -1),
        sgu_b0_row=jnp.repeat(sgu_b[i][:, 0], gm_group_w).reshape(1, -1),
        w_br_attn=w_br_attn[i], w_br_gm=w_br_gm[i],
        w_gate=w_gate[i], b_gate=b_gate[i], w_out=w_out[i],
        ffn_norm_w=ffn_norm_w[i], w_up=w_up[i], conv_w=conv_w[i], conv_b=conv_b[i],
        w_down=w_down[i], ple_norm_w=ple_norm_w[i], w_ple_gate=w_ple_gate[i], w_ple_proj=w_ple_proj[i],
    )


def _prompt_layer(x, p, lw, wb):
    batch, seq, d = x.shape
    x2d = x.reshape(batch * seq, d)
    attn_w = N_HEADS * HEAD_DIM
    kv_w = N_KV_HEADS * HEAD_DIM
    gm_w = lw["w_br_gm"].shape[0]
    proj, gates = _in_proj(x2d, lw["attn_norm_w"], wb["w_in"], wb["w_gate"], lw["b_gate"], tm=256)
    attn, k_new, v_new = _attn_prompt(proj, lw["qw2"], lw["kw2"], lw["sinks"], batch=batch, seq=seq)
    sgu, sgu_state = _sgu_prompt(proj, lw["sgu_norm_w"], lw["sgu_w"], lw["sgu_b_t"], batch=batch, seq=seq,
                                 gu_off=attn_w + 2 * kv_w, gm_w=gm_w, chunks_per_step=2)
    x1, xn1 = _merge_out(attn, sgu, gates,x2d, wb["w_br_attn"], wb["w_br_gm"], wb["w_out"],
                         lw["ffn_norm_w"], tm=256)
    delta, conv_state = _ffn_prompt(xn1, wb["w_up"], wb["w_down"], lw["conv_w"], lw["conv_b"],
                                    batch=batch, seq=seq, tm=1024)
    x3 = _ple_rows(x1, delta, lw["ple_norm_w"], wb["w_ple_gate"], p.reshape(batch * seq, -1),
                   wb["w_ple_proj"], tm=512)
    return (x3.reshape(batch, seq, d), k_new.reshape(batch, WINDOW, N_KV_HEADS, HEAD_DIM),
            v_new.reshape(batch, WINDOW, N_KV_HEADS, HEAD_DIM), sgu_state, conv_state)


def _sample_layer(x, p, k_buf, v_buf, conv_buf, lw):
    nb, t, d = x.shape
    x2d = x.reshape(nb * t, d)
    attn_w = N_HEADS * HEAD_DIM
    kv_w = N_KV_HEADS * HEAD_DIM
    gm_w = lw["w_br_gm"].shape[0]
    wb = {}
    proj, wb["w_in"] = _norm_mm(x2d, lw["attn_norm_w"], lw["w_in"], None, act=None,
                                tn=512, tile=512, name="proj_sample")
    gates, wb["w_gate"] = _norm_mm(x2d, lw["attn_norm_w"], lw["w_gate"], lw["b_gate"], act="sigmoid",
                                   tn=1024, tile=512, name="gates_sample")
    q = proj[:, :attn_w]
    k = proj[:, attn_w:attn_w + kv_w]
    v = proj[:, attn_w + kv_w:attn_w + 2 * kv_w]
    gu = proj[:, attn_w + 2 * kv_w:attn_w + 2 * kv_w + gm_w]
    gv = proj[:, attn_w + 2 * kv_w + gm_w:]
    attn, k_new, v_new, wb["w_br_attn"], wb["w_br_gm"], wb["w_out"] = _attn_sample(
        q, k, v, k_buf.reshape(nb, WINDOW, kv_w), v_buf.reshape(nb, WINDOW, kv_w),
        lw["qw2"], lw["kw2"], lw["sinks_row"], lw["w_br_attn"], lw["w_br_gm"], lw["w_out"], group=8)
    sgu, sgu_state = _sgu_sample(gu, gv, lw["sgu_norm_w"], lw["sgu_w0_row"], lw["sgu_b0_row"])
    x1, _ = _merge_out(attn, sgu, gates, x2d, wb["w_br_attn"], wb["w_br_gm"], wb["w_out"],
                       lw["ffn_norm_w"], tm=nb)
    h, wb["w_up"] = _norm_mm(x1, lw["ffn_norm_w"], lw["w_up"], None, act=None, tn=1024, tile=512,
                             name="ffn_up_sample")
    x2, wb["w_down"] = _ffn_down_sample(h, conv_buf.reshape(nb, -1), lw["conv_w"], lw["conv_b"], lw["w_down"],
                                        x1, tf=512)
    x3, wb["w_ple_gate"], wb["w_ple_proj"] = _ple(
        x2, None, lw["ple_norm_w"], lw["w_ple_gate"], p.reshape(nb * t, -1), lw["w_ple_proj"],
        tm=nb, tn=1024, name="ple_sample", emit_w=True)
    conv_state = jnp.stack([conv_buf[:, 1, :], h], axis=1)
    return (x3.reshape(nb, t, d), k_new.reshape(nb, WINDOW, N_KV_HEADS, HEAD_DIM),
            v_new.reshape(nb, WINDOW, N_KV_HEADS, HEAD_DIM), sgu_state.reshape(nb, t, gm_w), conv_state), wb


def kernel(x_prompt, x_sample, p_prompt, p_sample, state_attn_k, state_attn_v, state_conv, attn_norm_w, w_in, q_norm_w, k_norm_w, attn_sinks, sgu_norm_w, sgu_w, sgu_b, w_br_attn, w_br_gm, w_gate, b_gate, w_out, ffn_norm_w, w_up, conv_w, conv_b, w_down, ple_norm_w, w_ple_gate, w_ple_proj):
    depth = w_in.shape[0]
    xp, xs = x_prompt, x_sample
    outs = [[] for _ in range(8)]
    for i in range(depth):
        lw = _layer_weights(i, attn_norm_w, w_in, q_norm_w, k_norm_w, attn_sinks, sgu_norm_w, sgu_w, sgu_b,
                            w_br_attn, w_br_gm, w_gate, b_gate, w_out, ffn_norm_w, w_up, conv_w, conv_b,
                            w_down, ple_norm_w, w_ple_gate, w_ple_proj)
        (xs, ks, vs, gs, cs), wb = _sample_layer(xs, p_sample[i], state_attn_k[i], state_attn_v[i],
                                                 state_conv[i], lw)
        xp, kp, vp, gp, cp = _prompt_layer(xp, p_prompt[i], lw, wb)
        for lst, val in zip(outs, (kp, vp, ks, vs, gp, gs, cp, cs)):
            lst.append(val)
    return (xp, xs) + tuple(jnp.stack(lst) for lst in outs)
```

```python
import functools

import jax
import jax.numpy as jnp
from jax import lax
from jax.experimental import pallas as pl
from jax.experimental.pallas import tpu as pltpu

F32 = jnp.float32
BF16 = jnp.bfloat16

HEAD_DIM = 64
N_HEADS = 16
N_KV_HEADS = 4
GQA_GROUP = N_HEADS // N_KV_HEADS
WINDOW = 128
CHUNK = 128
GM_GROUPS = 4
EPS = 1e-6
MASK_VALUE = -1e30
LANES = 128
SUBLANES = 8
HEADS_PER_VREG = LANES // HEAD_DIM
MIB = 1 << 20


def _alibi_slope(h):
    return float(2.0 ** (-8.0 * (h + 1) / N_HEADS))


def _params(semantics, vmem_mib, flags=None):
    return pltpu.CompilerParams(dimension_semantics=semantics, vmem_limit_bytes=vmem_mib * MIB, flags=flags)


def _rmsnorm_rows(x, w):
    ms = jnp.mean(x * x, axis=-1, keepdims=True)
    return x * lax.rsqrt(ms + EPS) * w


NORM_ROWS = 256


def _rmsnorm_to(x_ref, nw_ref, xn_ref, add_ref=None):
    tm = x_ref.shape[0]
    rows = min(NORM_ROWS, tm)

    def body(r, carry):
        sl = pl.ds(pl.multiple_of(r * rows, rows), rows)
        x = x_ref[sl, :]
        if add_ref is not None:
            x = x + add_ref[sl, :].astype(F32)
        xn_ref[sl, :] = _rmsnorm_rows(x, nw_ref[...]).astype(BF16)
        return carry

    lax.fori_loop(0, tm // rows, body, 0)


def _norm_mm_kernel(*refs, has_bias, act):
    if has_bias:
        x_ref, nw_ref, w_ref, b_ref, o_ref, wb_ref, xn_ref = refs
    else:
        x_ref, nw_ref, w_ref, o_ref, wb_ref, xn_ref = refs

    @pl.when(pl.program_id(0) == 0)
    def _():
        _rmsnorm_to(x_ref, nw_ref, xn_ref)

    wb = w_ref[...].astype(BF16)
    tile = wb_ref.shape[2]
    for c in range(wb_ref.shape[0]):
        wb_ref[c] = wb[:, c * tile:(c + 1) * tile]
    acc = jnp.dot(xn_ref[...], wb, preferred_element_type=F32)
    if has_bias:
        acc = acc + b_ref[...]
    if act == "sigmoid":
        acc = jax.nn.sigmoid(acc)
    o_ref[...] = acc.astype(o_ref.dtype)


def _norm_mm(x, nw, w, bias, *, act, tn, tile, name):
    m, k = x.shape
    n = w.shape[1]
    per_step = tn // tile
    in_specs = [
        pl.BlockSpec((m, k), lambda j: (0, 0)),
        pl.BlockSpec((1, k), lambda j: (0, 0)),
        pl.BlockSpec((k, tn), lambda j: (0, j)),
    ]
    args = [x, nw.reshape(1, k), w]
    if bias is not None:
        in_specs.append(pl.BlockSpec((1, tn), lambda j: (0, j)))
        args.append(bias.reshape(1, n))
    return pl.pallas_call(
        functools.partial(_norm_mm_kernel, has_bias=bias is not None, act=act),
        grid=(n // tn,),
        in_specs=in_specs,
        out_specs=[pl.BlockSpec((m, tn), lambda j: (0, j)),
                   pl.BlockSpec((per_step, k, tile), lambda j: (j, 0, 0))],
        out_shape=[jax.ShapeDtypeStruct((m, n), F32), jax.ShapeDtypeStruct((n // tile, k, tile), BF16)],
        scratch_shapes=[pltpu.VMEM((m, k), BF16)],
        compiler_params=_params(("arbitrary",), 56),
        name=name,
    )(*args)


def _in_proj_kernel(x_ref, nw_ref, win_ref, wg_ref, bg_ref, proj_ref, gates_ref, xn_ref):
    xn_ref[...] = _rmsnorm_rows(x_ref[...], nw_ref[...]).astype(BF16)
    tn = win_ref.shape[2]
    for c in range(win_ref.shape[0]):
        acc = jnp.dot(xn_ref[...], win_ref[c], preferred_element_type=F32)
        proj_ref[:, c * tn:(c + 1) * tn] = acc.astype(proj_ref.dtype)
    for c in range(wg_ref.shape[0]):
        cols = slice(c * tn, (c + 1) * tn)
        acc = jnp.dot(xn_ref[...], wg_ref[c], preferred_element_type=F32)
        gates_ref[:, cols] = jax.nn.sigmoid(acc + bg_ref[:, cols]).astype(gates_ref.dtype)


def _in_proj(x, nw, w_in, w_gate, b_gate, *, tm):
    m, k = x.shape
    n_in, _, tn = w_in.shape
    n_g = w_gate.shape[0]
    rows = lambda width: pl.BlockSpec((tm, width), lambda i: (i, 0))
    resident = lambda a: pl.BlockSpec(a.shape, lambda i: (0,) * a.ndim, pipeline_mode=pl.Buffered(1))
    bg = b_gate.reshape(1, -1)
    return pl.pallas_call(
        _in_proj_kernel,
        grid=(m // tm,),
        in_specs=[rows(k), pl.BlockSpec((1, k), lambda i: (0, 0)), resident(w_in), resident(w_gate),
                  pl.BlockSpec(bg.shape, lambda i: (0, 0))],
        out_specs=[rows(n_in * tn), rows(n_g * tn)],
        out_shape=[
            jax.ShapeDtypeStruct((m, n_in * tn), BF16),
            jax.ShapeDtypeStruct((m, n_g * tn), BF16),
        ],
        scratch_shapes=[pltpu.VMEM((tm, k), BF16)],
        compiler_params=_params(("parallel",), 56),
        name="in_proj",
    )(x, nw.reshape(1, k), w_in, w_gate, bg)


def _head_group_matrix():
    r = lax.broadcasted_iota(jnp.int32, (LANES, LANES), 0)
    c = lax.broadcasted_iota(jnp.int32, (LANES, LANES), 1)
    return jnp.where((r >> 6) == (c >> 6), 1.0 / HEAD_DIM, 0.0).astype(BF16)


def _head_rmsnorm(xcol, w, gmat):
    ms = jnp.dot((xcol * xcol).astype(BF16), gmat, preferred_element_type=F32)
    return xcol * lax.rsqrt(ms + EPS) * w


def _attn_prompt_kernel(sink_ref, q_ref, kc_ref, kp_ref, vc_ref, vp_ref, qw_ref, kw_ref,
                        o_ref, kn_ref, vn_ref):
    c = pl.program_id(1)
    blk = WINDOW
    gmat = _head_group_matrix()
    lane = lax.broadcasted_iota(jnp.int32, (1, LANES), 1)
    low = lane < HEAD_DIM

    kw = kw_ref[...]
    qw = qw_ref[...]
    kc = kc_ref[...].astype(F32)
    kp = kp_ref[...].astype(F32)
    vc = vc_ref[...].astype(F32)
    vp = vp_ref[...].astype(F32)
    ncol = kc.shape[1] // LANES
    kcn = [_head_rmsnorm(kc[:, p * LANES:(p + 1) * LANES], kw, gmat) for p in range(ncol)]
    kpn = [_head_rmsnorm(kp[:, p * LANES:(p + 1) * LANES], kw, gmat) for p in range(ncol)]
    kn_ref[0] = jnp.concatenate(kcn, axis=1)
    vn_ref[0] = vc

    kcat = [jnp.concatenate([kpn[p], kcn[p]], axis=0) for p in range(ncol)]
    vcat = [jnp.concatenate([vp[:, p * LANES:(p + 1) * LANES], vc[:, p * LANES:(p + 1) * LANES]], axis=0)
            for p in range(ncol)]
    krol = [pltpu.roll(kcat[p], HEAD_DIM, 1) for p in range(ncol)]
    vrol = [pltpu.roll(vcat[p], HEAD_DIM, 1) for p in range(ncol)]

    qi = lax.broadcasted_iota(jnp.int32, (blk, 2 * blk), 0)
    kj = lax.broadcasted_iota(jnp.int32, (blk, 2 * blk), 1)
    dist = blk + qi - kj
    valid = (dist >= 0) & (dist <= WINDOW) & ((kj >= blk) | (c > 0))
    distf = dist.astype(F32)

    q = q_ref[...].astype(F32)
    qn = [_head_rmsnorm(q[:, p * LANES:(p + 1) * LANES], qw, gmat)
          for p in range(N_HEADS // HEADS_PER_VREG)]

    for g in range(N_KV_HEADS):
        p, half = divmod(g, HEADS_PER_VREG)
        if half == 0:
            kd = jnp.where(low, kcat[p], krol[p])
            vd = jnp.where(low, vcat[p], vrol[p])
        else:
            kd = jnp.where(low, krol[p], kcat[p])
            vd = jnp.where(low, vrol[p], vcat[p])
        kd = kd.astype(BF16)
        vd = vd.astype(BF16)
        qs = []
        for hl in range(GQA_GROUP):
            h = g * GQA_GROUP + hl
            pc, hh = divmod(h, HEADS_PER_VREG)
            keep = low if hh == 0 else jnp.logical_not(low)
            qs.append(jnp.where(keep, qn[pc], 0.0).astype(BF16))
        qstack = jnp.concatenate(qs, axis=0)
        s_all = lax.dot_general(qstack, kd, (((1,), (1,)), ((), ())), preferred_element_type=F32)
        ps = []
        for hl in range(GQA_GROUP):
            h = g * GQA_GROUP + hl
            s = s_all[hl * blk:(hl + 1) * blk] - _alibi_slope(h) * distf
            s = jnp.where(valid, s, MASK_VALUE)
            sink = sink_ref[h]
            mx = jnp.maximum(jnp.max(s, axis=-1, keepdims=True), sink)
            e = jnp.exp(s - mx)
            den = jnp.sum(e, axis=-1, keepdims=True) + jnp.exp(sink - mx)
            ps.append((e / den).astype(BF16))
        pstack = jnp.concatenate(ps, axis=0)
        o_all = jnp.dot(pstack, vd, preferred_element_type=F32)
        for pair in range(GQA_GROUP // HEADS_PER_VREG):
            pc = g * (GQA_GROUP // HEADS_PER_VREG) + pair
            even = o_all[(2 * pair) * blk:(2 * pair + 1) * blk]
            odd = o_all[(2 * pair + 1) * blk:(2 * pair + 2) * blk]
            o_ref[:, pc * LANES:(pc + 1) * LANES] = jnp.where(low, even, odd).astype(o_ref.dtype)


def _attn_prompt(proj, qw2, kw2, sinks, *, batch, seq):
    nb = seq // WINDOW
    attn_w = N_HEADS * HEAD_DIM
    kv_w = N_KV_HEADS * HEAD_DIM
    kcol = attn_w // kv_w
    vcol = kcol + 1

    def cur(col):
        return lambda b, c: (b * nb + c, col)

    def prev(col):
        return lambda b, c: (jnp.maximum(b * nb + c - 1, 0), col)

    return pl.pallas_call(
        _attn_prompt_kernel,
        grid=(batch, nb),
        in_specs=[
            pl.BlockSpec(memory_space=pltpu.SMEM),
            pl.BlockSpec((WINDOW, attn_w), cur(0)),
            pl.BlockSpec((WINDOW, kv_w), cur(kcol)),
            pl.BlockSpec((WINDOW, kv_w), prev(kcol)),
            pl.BlockSpec((WINDOW, kv_w), cur(vcol)),
            pl.BlockSpec((WINDOW, kv_w), prev(vcol)),
            pl.BlockSpec((1, LANES), lambda b, c: (0, 0)),
            pl.BlockSpec((1, LANES), lambda b, c: (0, 0)),
        ],
        out_specs=[
            pl.BlockSpec((WINDOW, attn_w), lambda b, c: (b * nb + c, 0)),
            pl.BlockSpec((1, WINDOW, kv_w), lambda b, c: (b, 0, 0)),
            pl.BlockSpec((1, WINDOW, kv_w), lambda b, c: (b, 0, 0)),
        ],
        out_shape=[
            jax.ShapeDtypeStruct((batch * seq, attn_w), BF16),
            jax.ShapeDtypeStruct((batch, WINDOW, kv_w), F32),
            jax.ShapeDtypeStruct((batch, WINDOW, kv_w), F32),
        ],
        compiler_params=_params(("parallel", "arbitrary"), 32),
        name="attn_prompt",
    )(sinks, proj, proj, proj, proj, proj, qw2, kw2)


def _layernorm_rows(x, w):
    mu = jnp.mean(x, axis=-1, keepdims=True)
    xc = x - mu
    return xc * lax.rsqrt(jnp.mean(xc * xc, axis=-1, keepdims=True) + EPS) * w


GELU_C = 0.7978845608028654
GELU_A = 0.044715


def _gelu_tanh(x):
    return x * jax.nn.sigmoid((2.0 * GELU_C) * (x + GELU_A * (x * x * x)))


def _sgu_prompt_kernel(gu0_ref, gu1_ref, gv0_ref, gv1_ref, nw_ref, ws_ref, bs_ref, o_ref, st_ref):
    rows = gu0_ref.shape[0]
    gv = jnp.concatenate([gv0_ref[...], gv1_ref[...]], axis=1).astype(F32)
    vn = _layernorm_rows(_gelu_tanh(gv), nw_ref[...])
    st_ref[0] = vn[rows - CHUNK:]
    vb = vn.astype(BF16)
    r = lax.broadcasted_iota(jnp.int32, (CHUNK, CHUNK), 0)
    c = lax.broadcasted_iota(jnp.int32, (CHUNK, CHUNK), 1)
    causal = r >= c
    gw = vb.shape[1] // GM_GROUPS
    half = gu0_ref.shape[1]
    for g in range(GM_GROUPS):
        w = jnp.where(causal, ws_ref[g], 0.0).astype(BF16)
        src = gu0_ref if g * gw < half else gu1_ref
        off = g * gw - (0 if g * gw < half else half)
        for ch in range(rows // CHUNK):
            rs = slice(ch * CHUNK, (ch + 1) * CHUNK)
            mix = jnp.dot(w, vb[rs, g * gw:(g + 1) * gw], preferred_element_type=F32) + bs_ref[:, g:g + 1]
            u = _gelu_tanh(src[rs, off:off + gw].astype(F32))
            o_ref[rs, g * gw:(g + 1) * gw] = (u * mix).astype(o_ref.dtype)


def _sgu_prompt(proj, nw, ws, bs_t, *, batch, seq, gu_off, gm_w, chunks_per_step):
    rows = chunks_per_step * CHUNK
    nc = seq // rows
    half = gm_w // 2
    b0 = gu_off // half

    def col(k):
        return lambda b, c: (b * nc + c, b0 + k)

    return pl.pallas_call(
        _sgu_prompt_kernel,
        grid=(batch, nc),
        in_specs=[
            pl.BlockSpec((rows, half), col(0)),
            pl.BlockSpec((rows, half), col(1)),
            pl.BlockSpec((rows, half), col(2)),
            pl.BlockSpec((rows, half), col(3)),
            pl.BlockSpec((1, gm_w), lambda b, c: (0, 0)),
            pl.BlockSpec((GM_GROUPS, CHUNK, CHUNK), lambda b, c: (0, 0, 0)),
            pl.BlockSpec((CHUNK, GM_GROUPS), lambda b, c: (0, 0)),
        ],
        out_specs=[
            pl.BlockSpec((rows, gm_w), lambda b, c: (b * nc + c, 0)),
            pl.BlockSpec((1, CHUNK, gm_w), lambda b, c: (b, 0, 0)),
        ],
        out_shape=[
            jax.ShapeDtypeStruct((batch * seq, gm_w), BF16),
            jax.ShapeDtypeStruct((batch, CHUNK, gm_w), F32),
        ],
        compiler_params=_params(("parallel", "arbitrary"), 32),
        name="sgu_prompt",
    )(proj, proj, proj, proj, nw.reshape(1, gm_w), ws, bs_t)


N_ATTN_IN, N_ATTN_OUT, N_SGU_IN = 8, 3, 7


def _mixers_prompt_kernel(*refs):
    attn_in = refs[:N_ATTN_IN]
    sgu_in = refs[N_ATTN_IN:N_ATTN_IN + N_SGU_IN]
    outs = refs[N_ATTN_IN + N_SGU_IN:]
    _sgu_prompt_kernel(*sgu_in, *outs[N_ATTN_OUT:])
    _attn_prompt_kernel(*attn_in, *outs[:N_ATTN_OUT])


def _mixers_prompt(proj, qw2, kw2, sinks, nw, ws, bs_t, *, batch, seq, gu_off, gm_w):
    nb = seq // WINDOW
    attn_w = N_HEADS * HEAD_DIM
    kv_w = N_KV_HEADS * HEAD_DIM
    kcol = attn_w // kv_w
    vcol = kcol + 1
    half = gm_w // 2
    b0 = gu_off // half
    cur = lambda col: (lambda b, c: (b * nb + c, col))
    prev = lambda col: (lambda b, c: (jnp.maximum(b * nb + c - 1, 0), col))
    const = lambda shape: pl.BlockSpec(shape, lambda b, c: (0,) * len(shape))
    state = lambda width: pl.BlockSpec((1, WINDOW, width), lambda b, c: (b, 0, 0))
    return pl.pallas_call(
        _mixers_prompt_kernel,
        grid=(batch, nb),
        in_specs=[
            pl.BlockSpec(memory_space=pltpu.SMEM),
            pl.BlockSpec((WINDOW, attn_w), cur(0)),
            pl.BlockSpec((WINDOW, kv_w), cur(kcol)),
            pl.BlockSpec((WINDOW, kv_w), prev(kcol)),
            pl.BlockSpec((WINDOW, kv_w), cur(vcol)),
            pl.BlockSpec((WINDOW, kv_w), prev(vcol)),
            const((1, LANES)), const((1, LANES)),
            pl.BlockSpec((CHUNK, half), cur(b0)),
            pl.BlockSpec((CHUNK, half), cur(b0 + 1)),
            pl.BlockSpec((CHUNK, half), cur(b0 + 2)),
            pl.BlockSpec((CHUNK, half), cur(b0 + 3)),
            const((1, gm_w)), const((GM_GROUPS, CHUNK, CHUNK)), const((CHUNK, GM_GROUPS)),
        ],
        out_specs=[
            pl.BlockSpec((WINDOW, attn_w), cur(0)), state(kv_w), state(kv_w),
            pl.BlockSpec((CHUNK, gm_w), cur(0)), state(gm_w),
        ],
        out_shape=[
            jax.ShapeDtypeStruct((batch * seq, attn_w), BF16),
            jax.ShapeDtypeStruct((batch, WINDOW, kv_w), F32),
            jax.ShapeDtypeStruct((batch, WINDOW, kv_w), F32),
            jax.ShapeDtypeStruct((batch * seq, gm_w), BF16),
            jax.ShapeDtypeStruct((batch, CHUNK, gm_w), F32),
        ],
        compiler_params=_params(("parallel", "arbitrary"), 32),
        name="mixers_prompt",
    )(sinks, proj, proj, proj, proj, proj, qw2, kw2, proj, proj, proj, proj, nw.reshape(1, gm_w), ws, bs_t)


def _split_dot(x, m):
    hi = x.astype(BF16)
    lo = (x - hi.astype(F32)).astype(BF16)
    return jnp.dot(hi, m, preferred_element_type=F32) + jnp.dot(lo, m, preferred_element_type=F32)


def _attn_sample_kernel(q_ref, kn_ref, vn_ref, kb_ref, vb_ref, qw_ref, kw_ref, sink_ref,
                        wa_ref, wg_ref, wo_ref, o_ref, ko_ref, vo_ref, wab_ref, wgb_ref, wob_ref):
    wab_ref[...] = wa_ref[...].astype(BF16)
    wgb_ref[...] = wg_ref[...].astype(BF16)
    wob_ref[...] = wo_ref[...].astype(BF16)
    nsamp = q_ref.shape[0]
    attn_w = q_ref.shape[1]
    kv_w = kn_ref.shape[1]
    gmat = _head_group_matrix()
    q = q_ref[...].astype(F32)
    kn = kn_ref[...].astype(F32)
    qn = jnp.concatenate(
        [_head_rmsnorm(q[:, p * LANES:(p + 1) * LANES], qw_ref[...], gmat)
         for p in range(attn_w // LANES)], axis=1)
    knn = jnp.concatenate(
        [_head_rmsnorm(kn[:, p * LANES:(p + 1) * LANES], kw_ref[...], gmat)
         for p in range(kv_w // LANES)], axis=1)
    vnn = vn_ref[...].astype(F32)

    ec = lax.broadcasted_iota(jnp.int32, (kv_w, attn_w), 0)
    el = lax.broadcasted_iota(jnp.int32, (kv_w, attn_w), 1)
    expand = (((ec >> 6) == (el >> 8)) & ((ec & 63) == (el & 63))).astype(BF16)
    gl = lax.broadcasted_iota(jnp.int32, (attn_w, LANES), 0)
    gh = lax.broadcasted_iota(jnp.int32, (attn_w, LANES), 1)
    hsum = ((gl >> 6) == gh).astype(BF16)
    tl = lax.broadcasted_iota(jnp.int32, (LANES, attn_w), 1)
    th = lax.broadcasted_iota(jnp.int32, (LANES, attn_w), 0)
    hexp = ((tl >> 6) == th).astype(BF16)

    nkeys = WINDOW + SUBLANES
    row = lax.broadcasted_iota(jnp.int32, (nkeys, LANES), 0)
    head = lax.broadcasted_iota(jnp.int32, (nkeys, LANES), 1)
    slope = jnp.exp2(-8.0 * (head + 1).astype(F32) / N_HEADS)
    dist = (WINDOW - row).astype(F32)
    key_ok = row <= WINDOW
    srow = lax.broadcasted_iota(jnp.int32, (WINDOW, kv_w), 0)
    sinks = sink_ref[...]

    for s in range(nsamp):
        kb = kb_ref[s]
        vb = vb_ref[s]
        knew = knn[s:s + 1]
        vnew = vnn[s:s + 1]
        ko_ref[s] = jnp.where(srow == WINDOW - 1, knew, pltpu.roll(kb, WINDOW - 1, 0))
        vo_ref[s] = jnp.where(srow == WINDOW - 1, vnew, pltpu.roll(vb, WINDOW - 1, 0))
        kk = jnp.concatenate([kb, jnp.broadcast_to(knew, (SUBLANES, kv_w))], axis=0)
        vv = jnp.concatenate([vb, jnp.broadcast_to(vnew, (SUBLANES, kv_w))], axis=0)
        kexp = jnp.dot(kk.astype(BF16), expand, preferred_element_type=F32)
        vexp = jnp.dot(vv.astype(BF16), expand, preferred_element_type=F32)
        sc = _split_dot(kexp * qn[s:s + 1], hsum)
        sc = sc - slope * dist
        sc = jnp.where(key_ok, sc, MASK_VALUE)
        mx = jnp.maximum(jnp.max(sc, axis=0, keepdims=True), sinks)
        e = jnp.exp(sc - mx)
        den = jnp.sum(e, axis=0, keepdims=True) + jnp.exp(sinks - mx)
        pexp = jnp.dot((e / den).astype(BF16), hexp, preferred_element_type=F32)
        o_ref[s:s + 1, :] = jnp.sum(pexp * vexp, axis=0, keepdims=True)


def _attn_sample(q, knew, vnew, kbuf, vbuf, qw2, kw2, sinks_row, wa, wg, wo, *, group):
    nb, attn_w = q.shape
    kv_w = knew.shape[1]
    steps = nb // group
    row = lambda width: pl.BlockSpec((group, width), lambda i: (i, 0))
    buf = pl.BlockSpec((group, WINDOW, kv_w), lambda i: (i, 0, 0))
    vec = pl.BlockSpec((1, LANES), lambda i: (0, 0))
    slab = lambda w: pl.BlockSpec((w.shape[0] // steps, w.shape[1]), lambda i: (i, 0))
    return pl.pallas_call(
        _attn_sample_kernel,
        grid=(steps,),
        in_specs=[row(attn_w), row(kv_w), row(kv_w), buf, buf, vec, vec, vec, slab(wa), slab(wg), slab(wo)],
        out_specs=[row(attn_w), buf, buf, slab(wa), slab(wg), slab(wo)],
        out_shape=[
            jax.ShapeDtypeStruct((nb, attn_w), F32),
            jax.ShapeDtypeStruct((nb, WINDOW, kv_w), F32),
            jax.ShapeDtypeStruct((nb, WINDOW, kv_w), F32),
            jax.ShapeDtypeStruct(wa.shape, BF16),
            jax.ShapeDtypeStruct(wg.shape, BF16),
            jax.ShapeDtypeStruct(wo.shape, BF16),
        ],
        compiler_params=_params(("parallel",), 48),
        name="attn_sample",
    )(q, knew, vnew, kbuf, vbuf, qw2, kw2, sinks_row, wa, wg, wo)


def _sgu_sample_kernel(gu_ref, gv_ref, nw_ref, w0_ref, b0_ref, o_ref, st_ref):
    vn = _layernorm_rows(jax.nn.gelu(gv_ref[...].astype(F32)), nw_ref[...])
    st_ref[...] = vn
    mix = w0_ref[...] * vn + b0_ref[...]
    o_ref[...] = jax.nn.gelu(gu_ref[...].astype(F32)) * mix


def _sgu_sample(gu, gv, nw, w0_row, b0_row):
    nb, gm_w = gu.shape
    full = pl.BlockSpec((nb, gm_w), lambda i: (0, 0))
    vec = pl.BlockSpec((1, gm_w), lambda i: (0, 0))
    return pl.pallas_call(
        _sgu_sample_kernel,
        grid=(1,),
        in_specs=[full, full, vec, vec, vec],
        out_specs=[full, full],
        out_shape=[jax.ShapeDtypeStruct((nb, gm_w), F32), jax.ShapeDtypeStruct((nb, gm_w), F32)],
        name="sgu_sample",
    )(gu, gv, nw.reshape(1, gm_w), w0_row, b0_row)


def _merge_out_kernel(a_ref, s_ref, ga_ref, gb_ref, x_ref, wa_ref, wg_ref, wo_ref, nw_ref,
                      x1_ref, xn1_ref):
    a = jnp.dot(a_ref[...].astype(BF16), wa_ref[...], preferred_element_type=F32)
    m = jnp.dot(s_ref[...].astype(BF16), wg_ref[...], preferred_element_type=F32)
    merged = (gb_ref[...].astype(F32) * a + ga_ref[...].astype(F32) * m).astype(BF16)
    x1 = x_ref[...] + jnp.dot(merged, wo_ref[...], preferred_element_type=F32)
    x1_ref[...] = x1
    xn1_ref[...] = _rmsnorm_rows(x1, nw_ref[...]).astype(BF16)


def _merge_out(attn, sgu, gates, x, wa, wg, wo, nw, *, tm):
    m, ka = attn.shape
    kg = sgu.shape[1]
    d = wo.shape[1]
    rows = lambda width, col=0: pl.BlockSpec((tm, width), lambda i: (i, col))
    resident = lambda shape: pl.BlockSpec(shape, lambda i: (0, 0), pipeline_mode=pl.Buffered(1))
    return pl.pallas_call(
        _merge_out_kernel,
        grid=(m // tm,),
        in_specs=[
            rows(ka), rows(kg), rows(d, 0), rows(d, 1), rows(d),
            resident((ka, d)), resident((kg, d)), resident((d, d)),
            pl.BlockSpec((1, d), lambda i: (0, 0)),
        ],
        out_specs=[rows(d), rows(d)],
        out_shape=[jax.ShapeDtypeStruct((m, d), F32), jax.ShapeDtypeStruct((m, d), BF16)],
        compiler_params=_params(("parallel",), 48),
        name="merge_out",
    )(attn, sgu, gates, gates, x, wa, wg, wo, nw.reshape(1, d))


PAD = SUBLANES
FFN_ROW_BLOCKS = 4


def _ffn_step(t, xn_ref, wug_ref, wuu_ref, wd_ref, cwg_ref, cwu_ref, cbg_ref, cbu_ref,
              csg_ref, csu_ref, h_new, h_old, carry_ref, acc_ref, *, nf, tiles_per_seq):
    tm = xn_ref.shape[0]
    tf = wug_ref.shape[1]
    rb = tm // FFN_ROW_BLOCKS
    tp = jnp.maximum(t - 1, 0)
    ip = tp // nf
    jp = tp % nf
    first = (ip % tiles_per_seq) == 0
    h_old[0:PAD, :] = jnp.where(first, 0.0, carry_ref[jp])
    carry_ref[jp] = h_old[tm:tm + PAD, :]
    cw = jnp.concatenate([cwg_ref[...], cwu_ref[...]], axis=1)
    cb = jnp.concatenate([cbg_ref[...], cbu_ref[...]], axis=1)
    wug = wug_ref[...]
    wuu = wuu_ref[...]
    wd = wd_ref[...]
    for r in range(FFN_ROW_BLOCKS):
        lo = r * rb
        xr = xn_ref[lo:lo + rb, :]
        h_new[PAD + lo:PAD + lo + rb, 0:tf] = jnp.dot(xr, wug, preferred_element_type=F32)
        h_new[PAD + lo:PAD + lo + rb, tf:2 * tf] = jnp.dot(xr, wuu, preferred_element_type=F32)
        hp = h_old[lo:lo + rb + PAD, :]
        hc = (cw[0:1] * pltpu.roll(hp, 2, 0)[PAD:] + cw[1:2] * pltpu.roll(hp, 1, 0)[PAD:]
              + cw[2:3] * hp[PAD:] + cb)
        act = (jax.nn.silu(hc[:, 0:tf]) * hc[:, tf:2 * tf]).astype(BF16)
        acc_ref[lo:lo + rb, :] += jnp.dot(act, wd, preferred_element_type=F32)
    csg_ref[0] = h_old[PAD + tm - 2:PAD + tm, 0:tf]
    csu_ref[0] = h_old[PAD + tm - 2:PAD + tm, tf:2 * tf]


def _ffn_kernel(xn_ref, wug_ref, wuu_ref, wd_ref, cwg_ref, cwu_ref, cbg_ref, cbu_ref,
                o_ref, csg_ref, csu_ref, ha_ref, hb_ref, carry_ref, acc_ref, *, nf, tiles_per_seq):
    t = pl.program_id(0)
    step = functools.partial(_ffn_step, t, xn_ref, wug_ref, wuu_ref, wd_ref, cwg_ref, cwu_ref,
                             cbg_ref, cbu_ref, csg_ref, csu_ref, carry_ref=carry_ref, acc_ref=acc_ref,
                             nf=nf, tiles_per_seq=tiles_per_seq)

    @pl.when(t == 0)
    def _():
        hb_ref[...] = jnp.zeros_like(hb_ref)
        carry_ref[...] = jnp.zeros_like(carry_ref)

    @pl.when(jnp.maximum(t - 1, 0) % nf == 0)
    def _():
        acc_ref[...] = jnp.zeros_like(acc_ref)

    @pl.when(t % 2 == 0)
    def _():
        step(h_new=ha_ref, h_old=hb_ref)

    @pl.when(t % 2 == 1)
    def _():
        step(h_new=hb_ref, h_old=ha_ref)

    @pl.when(jnp.maximum(t - 1, 0) % nf == nf - 1)
    def _():
        o_ref[...] = acc_ref[...].astype(o_ref.dtype)


def _ffn_prompt(xn, w_up, w_down, cw, cb, *, batch, seq, tm):
    m, k = xn.shape
    d_ff, d = w_down.shape
    tf = w_up.shape[2]
    nf = d_ff // tf
    nm = m // tm
    tiles_per_seq = seq // tm
    prev = lambda t: jnp.maximum(t - 1, 0)
    up_tile = lambda off: (lambda t: (off + t % nf, 0, 0))
    dn_col = lambda off: (lambda t: (0, off + prev(t) % nf))
    state = lambda t: (prev(t) // nf, 0, prev(t) % nf)
    delta, csg, csu = pl.pallas_call(
        functools.partial(_ffn_kernel, nf=nf, tiles_per_seq=tiles_per_seq),
        grid=(nm * nf + 1,),
        in_specs=[
            pl.BlockSpec((tm, k), lambda t: (jnp.minimum(t // nf, nm - 1), 0)),
            pl.BlockSpec((None, k, tf), up_tile(0)),
            pl.BlockSpec((None, k, tf), up_tile(nf)),
            pl.BlockSpec((tf, d), lambda t: (prev(t) % nf, 0)),
            pl.BlockSpec((3, tf), dn_col(0)),
            pl.BlockSpec((3, tf), dn_col(nf)),
            pl.BlockSpec((1, tf), dn_col(0)),
            pl.BlockSpec((1, tf), dn_col(nf)),
        ],
        out_specs=[
            pl.BlockSpec((tm, d), lambda t: (prev(t) // nf, 0)),
            pl.BlockSpec((1, 2, tf), state),
            pl.BlockSpec((1, 2, tf), state),
        ],
        out_shape=[
            jax.ShapeDtypeStruct((m, d), BF16),
            jax.ShapeDtypeStruct((nm, 2, d_ff), F32),
            jax.ShapeDtypeStruct((nm, 2, d_ff), F32),
        ],
        scratch_shapes=[
            pltpu.VMEM((tm + PAD, 2 * tf), F32),
            pltpu.VMEM((tm + PAD, 2 * tf), F32),
            pltpu.VMEM((nf, PAD, 2 * tf), F32),
            pltpu.VMEM((tm, d), F32),
        ],
        compiler_params=_params(("arbitrary",), 58),
        name="ffn_prompt",
    )(xn, w_up, w_up, w_down, cw, cw, cb.reshape(1, -1), cb.reshape(1, -1))
    tails = jnp.concatenate([csg, csu], axis=-1).reshape(batch, tiles_per_seq, 2, 2 * d_ff)
    return delta, tails[:, -1]


def _ffn_down_sample_kernel(hg_ref, hu_ref, b0g_ref, b0u_ref, b1g_ref, b1u_ref, cwg_ref, cwu_ref,
                            cbg_ref, cbu_ref, wd_ref, x_ref, o_ref, wdb_ref):
    @pl.when(pl.program_id(0) == 0)
    def _():
        o_ref[...] = x_ref[...]

    def conv(h_ref, b0_ref, b1_ref, cw_ref, cb_ref):
        cw = cw_ref[...]
        return cw[0:1] * b0_ref[...] + cw[1:2] * b1_ref[...] + cw[2:3] * h_ref[...] + cb_ref[...]

    hcg = conv(hg_ref, b0g_ref, b1g_ref, cwg_ref, cbg_ref)
    hcu = conv(hu_ref, b0u_ref, b1u_ref, cwu_ref, cbu_ref)
    act = (jax.nn.silu(hcg) * hcu).astype(BF16)
    wdb = wd_ref[...].astype(BF16)
    wdb_ref[...] = wdb
    o_ref[...] += jnp.dot(act, wdb, preferred_element_type=F32)


def _ffn_down_sample(h, state2, cw, cb, wd, x, *, tf):
    nb, two_f = h.shape
    d_ff = two_f // 2
    nf = d_ff // tf
    d = wd.shape[1]
    colblk = lambda off: pl.BlockSpec((nb, tf), lambda j: (0, j + off))
    return pl.pallas_call(
        _ffn_down_sample_kernel,
        grid=(nf,),
        in_specs=[
            colblk(0), colblk(nf),
            colblk(0), colblk(nf), colblk(2 * nf), colblk(3 * nf),
            pl.BlockSpec((3, tf), lambda j: (0, j)),
            pl.BlockSpec((3, tf), lambda j: (0, j + nf)),
            pl.BlockSpec((1, tf), lambda j: (0, j)),
            pl.BlockSpec((1, tf), lambda j: (0, j + nf)),
            pl.BlockSpec((tf, d), lambda j: (j, 0)),
            pl.BlockSpec((nb, d), lambda j: (0, 0)),
        ],
        out_specs=[pl.BlockSpec((nb, d), lambda j: (0, 0)), pl.BlockSpec((tf, d), lambda j: (j, 0))],
        out_shape=[jax.ShapeDtypeStruct((nb, d), F32), jax.ShapeDtypeStruct((d_ff, d), BF16)],
        compiler_params=_params(("arbitrary",), 32),
        name="ffn_down_sample",
    )(h, h, state2, state2, state2, state2, cw, cw, cb.reshape(1, two_f), cb.reshape(1, two_f), wd, x)


def _ple_kernel(*refs, has_delta, emit_w):
    refs = list(refs)
    x_ref = refs.pop(0)
    d_ref = refs.pop(0) if has_delta else None
    nw_ref, w_ref, p_ref, wp_ref, o_ref = refs[:5]
    xn_ref = refs[-1]
    j = pl.program_id(1)
    tn = o_ref.shape[1]

    @pl.when(j == 0)
    def _():
        _rmsnorm_to(x_ref, nw_ref, xn_ref, add_ref=d_ref)

    wb = w_ref[...].astype(BF16)
    wpb = wp_ref[...].astype(BF16)
    if emit_w:
        wb_ref, wpb_ref = refs[5:7]
        wb_ref[...] = wb
        wpb_ref[...] = wpb
    gate = jax.nn.sigmoid(jnp.dot(xn_ref[...], wb, preferred_element_type=F32))
    emb = jnp.dot(p_ref[...].astype(BF16), wpb, preferred_element_type=F32)
    o_ref[...] = gate * emb
    for jj in range(x_ref.shape[1] // tn):
        @pl.when(j == jj)
        def _():
            res = x_ref[:, jj * tn:(jj + 1) * tn]
            if has_delta:
                res = res + d_ref[:, jj * tn:(jj + 1) * tn].astype(F32)
            o_ref[...] += res


def _ple_rows_kernel(x_ref, d_ref, nw_ref, w_ref, p_ref, wp_ref, o_ref):
    x2 = x_ref[...] + d_ref[...].astype(F32)
    xn = _rmsnorm_rows(x2, nw_ref[...]).astype(BF16)
    gate = jax.nn.sigmoid(jnp.dot(xn, w_ref[...], preferred_element_type=F32))
    emb = jnp.dot(p_ref[...].astype(BF16), wp_ref[...], preferred_element_type=F32)
    o_ref[...] = x2 + gate * emb


def _ple_rows(x, delta, nw, w, p, wp, *, tm):
    m, d = x.shape
    kp = p.shape[1]
    rows = lambda width: pl.BlockSpec((tm, width), lambda i: (i, 0))
    resident = lambda shape: pl.BlockSpec(shape, lambda i: (0, 0), pipeline_mode=pl.Buffered(1))
    return pl.pallas_call(
        _ple_rows_kernel,
        grid=(m // tm,),
        in_specs=[rows(d), rows(d), pl.BlockSpec((1, d), lambda i: (0, 0)), resident((d, d)),
                  rows(kp), resident((kp, d))],
        out_specs=rows(d),
        out_shape=jax.ShapeDtypeStruct((m, d), F32),
        compiler_params=_params(("parallel",), 56),
        name="ple_prompt",
    )(x, delta, nw.reshape(1, d), w, p, wp)


def _ple(x, delta, nw, w, p, wp, *, tm, tn, name, emit_w=False):
    m, k = x.shape
    n = w.shape[1]
    kp = p.shape[1]
    assert not emit_w or m == tm
    out_specs = [pl.BlockSpec((tm, tn), lambda i, j: (i, j))]
    out_shape = [jax.ShapeDtypeStruct((m, n), F32)]
    if emit_w:
        out_specs += [pl.BlockSpec((k, tn), lambda i, j: (0, j)), pl.BlockSpec((kp, tn), lambda i, j: (0, j))]
        out_shape += [jax.ShapeDtypeStruct((k, n), BF16), jax.ShapeDtypeStruct((kp, n), BF16)]
    row_full = pl.BlockSpec((tm, k), lambda i, j: (i, 0))
    in_specs = [row_full] + ([row_full] if delta is not None else []) + [
        pl.BlockSpec((1, k), lambda i, j: (0, 0)),
        pl.BlockSpec((k, tn), lambda i, j: (0, j)),
        pl.BlockSpec((tm, kp), lambda i, j: (i, 0)),
        pl.BlockSpec((kp, tn), lambda i, j: (0, j)),
    ]
    args = [x] + ([delta] if delta is not None else []) + [nw.reshape(1, k), w, p, wp]
    outs = pl.pallas_call(
        functools.partial(_ple_kernel, has_delta=delta is not None, emit_w=emit_w),
        grid=(m // tm, n // tn),
        in_specs=in_specs,
        out_specs=out_specs,
        out_shape=out_shape,
        scratch_shapes=[pltpu.VMEM((tm, k), BF16)],
        compiler_params=_params(("parallel", "arbitrary"), 56),
        name=name,
    )(*args)
    return outs if emit_w else outs[0]


def _layer_weights(i, attn_norm_w, w_in, q_norm_w, k_norm_w, attn_sinks, sgu_norm_w, sgu_w, sgu_b,
                   w_br_attn, w_br_gm, w_gate, b_gate, w_out, ffn_norm_w, w_up, conv_w, conv_b,
                   w_down, ple_norm_w, w_ple_gate, w_ple_proj):
    sinks = attn_sinks[i]
    gm_group_w = w_br_gm.shape[1] // GM_GROUPS
    return dict(
        attn_norm_w=attn_norm_w[i], w_in=w_in[i],
        qw2=jnp.tile(q_norm_w[i] * (HEAD_DIM ** -0.5), HEADS_PER_VREG).reshape(1, LANES),
        kw2=jnp.tile(k_norm_w[i], HEADS_PER_VREG).reshape(1, LANES),
        sinks=sinks,
        sinks_row=jnp.pad(sinks, (0, LANES - N_HEADS)).reshape(1, LANES),
        sgu_norm_w=sgu_norm_w[i], sgu_w=sgu_w[i], sgu_b_t=jnp.transpose(sgu_b[i]),
        sgu_w0_row=jnp.repeat(sgu_w[i][:, 0, 0], gm_group_w).reshape(1, -1),
        sgu_b0_row=jnp.repeat(sgu_b[i][:, 0], gm_group_w).reshape(1, -1),
        w_br_attn=w_br_attn[i], w_br_gm=w_br_gm[i],
        w_gate=w_gate[i], b_gate=b_gate[i], w_out=w_out[i],
        ffn_norm_w=ffn_norm_w[i], w_up=w_up[i], conv_w=conv_w[i], conv_b=conv_b[i],
        w_down=w_down[i], ple_norm_w=ple_norm_w[i], w_ple_gate=w_ple_gate[i], w_ple_proj=w_ple_proj[i],
    )


def _prompt_layer(x, p, lw, wb):
    batch, seq, d = x.shape
    x2d = x.reshape(batch * seq, d)
    attn_w = N_HEADS * HEAD_DIM
    kv_w = N_KV_HEADS * HEAD_DIM
    gm_w = lw["w_br_gm"].shape[0]
    proj, gates = _in_proj(x2d, lw["attn_norm_w"], wb["w_in"], wb["w_gate"], lw["b_gate"], tm=256)
    attn, k_new, v_new, sgu, sgu_state = _mixers_prompt(
        proj, lw["qw2"], lw["kw2"], lw["sinks"], lw["sgu_norm_w"], lw["sgu_w"], lw["sgu_b_t"],
        batch=batch, seq=seq, gu_off=attn_w + 2 * kv_w, gm_w=gm_w)
    x1, xn1 = _merge_out(attn, sgu, gates,x2d, wb["w_br_attn"], wb["w_br_gm"], wb["w_out"],
                         lw["ffn_norm_w"], tm=256)
    delta, conv_state = _ffn_prompt(xn1, wb["w_up"], wb["w_down"], lw["conv_w"], lw["conv_b"],
                                    batch=batch, seq=seq, tm=1024)
    x3 = _ple_rows(x1, delta, lw["ple_norm_w"], wb["w_ple_gate"], p.reshape(batch * seq, -1),
                   wb["w_ple_proj"], tm=512)
    return (x3.reshape(batch, seq, d), k_new.reshape(batch, WINDOW, N_KV_HEADS, HEAD_DIM),
            v_new.reshape(batch, WINDOW, N_KV_HEADS, HEAD_DIM), sgu_state, conv_state)


def _sample_layer(x, p, k_buf, v_buf, conv_buf, lw):
    nb, t, d = x.shape
    x2d = x.reshape(nb * t, d)
    attn_w = N_HEADS * HEAD_DIM
    kv_w = N_KV_HEADS * HEAD_DIM
    gm_w = lw["w_br_gm"].shape[0]
    wb = {}
    proj, wb["w_in"] = _norm_mm(x2d, lw["attn_norm_w"], lw["w_in"], None, act=None,
                                tn=512, tile=512, name="proj_sample")
    gates, wb["w_gate"] = _norm_mm(x2d, lw["attn_norm_w"], lw["w_gate"], lw["b_gate"], act="sigmoid",
                                   tn=1024, tile=512, name="gates_sample")
    q = proj[:, :attn_w]
    k = proj[:, attn_w:attn_w + kv_w]
    v = proj[:, attn_w + kv_w:attn_w + 2 * kv_w]
    gu = proj[:, attn_w + 2 * kv_w:attn_w + 2 * kv_w + gm_w]
    gv = proj[:, attn_w + 2 * kv_w + gm_w:]
    attn, k_new, v_new, wb["w_br_attn"], wb["w_br_gm"], wb["w_out"] = _attn_sample(
        q, k, v, k_buf.reshape(nb, WINDOW, kv_w), v_buf.reshape(nb, WINDOW, kv_w),
        lw["qw2"], lw["kw2"], lw["sinks_row"], lw["w_br_attn"], lw["w_br_gm"], lw["w_out"], group=8)
    sgu, sgu_state = _sgu_sample(gu, gv, lw["sgu_norm_w"], lw["sgu_w0_row"], lw["sgu_b0_row"])
    x1, _ = _merge_out(attn, sgu, gates, x2d, wb["w_br_attn"], wb["w_br_gm"], wb["w_out"],
                       lw["ffn_norm_w"], tm=nb)
    h, wb["w_up"] = _norm_mm(x1, lw["ffn_norm_w"], lw["w_up"], None, act=None, tn=1024, tile=512,
                             name="ffn_up_sample")
    x2, wb["w_down"] = _ffn_down_sample(h, conv_buf.reshape(nb, -1), lw["conv_w"], lw["conv_b"], lw["w_down"],
                                        x1, tf=512)
    x3, wb["w_ple_gate"], wb["w_ple_proj"] = _ple(
        x2, None, lw["ple_norm_w"], lw["w_ple_gate"], p.reshape(nb * t, -1), lw["w_ple_proj"],
        tm=nb, tn=1024, name="ple_sample", emit_w=True)
    conv_state = jnp.stack([conv_buf[:, 1, :], h], axis=1)
    return (x3.reshape(nb, t, d), k_new.reshape(nb, WINDOW, N_KV_HEADS, HEAD_DIM),
            v_new.reshape(nb, WINDOW, N_KV_HEADS, HEAD_DIM), sgu_state.reshape(nb, t, gm_w), conv_state), wb


def kernel(x_prompt, x_sample, p_prompt, p_sample, state_attn_k, state_attn_v, state_conv, attn_norm_w, w_in, q_norm_w, k_norm_w, attn_sinks, sgu_norm_w, sgu_w, sgu_b, w_br_attn, w_br_gm, w_gate, b_gate, w_out, ffn_norm_w, w_up, conv_w, conv_b, w_down, ple_norm_w, w_ple_gate, w_ple_proj):
    depth = w_in.shape[0]
    xp, xs = x_prompt, x_sample
    outs = [[] for _ in range(8)]
    for i in range(depth):
        lw = _layer_weights(i, attn_norm_w, w_in, q_norm_w, k_norm_w, attn_sinks, sgu_norm_w, sgu_w, sgu_b,
                            w_br_attn, w_br_gm, w_gate, b_gate, w_out, ffn_norm_w, w_up, conv_w, conv_b,
                            w_down, ple_norm_w, w_ple_gate, w_ple_proj)
        (xs, ks, vs, gs, cs), wb = _sample_layer(xs, p_sample[i], state_attn_k[i], state_attn_v[i],
                                                 state_conv[i], lw)
        xp, kp, vp, gp, cp = _prompt_layer(xp, p_prompt[i], lw, wb)
        for lst, val in zip(outs, (kp, vp, ks, vs, gp, gs, cp, cs)):
            lst.append(val)
    return (xp, xs) + tuple(jnp.stack(lst) for lst in outs)
```

```python
import functools

import jax
import jax.numpy as jnp
from jax import lax
from jax.experimental import pallas as pl
from jax.experimental.pallas import tpu as pltpu

F32 = jnp.float32
BF16 = jnp.bfloat16

HEAD_DIM = 64
N_HEADS = 16
N_KV_HEADS = 4
GQA_GROUP = N_HEADS // N_KV_HEADS
WINDOW = 128
CHUNK = 128
GM_GROUPS = 4
EPS = 1e-6
MASK_VALUE = -1e30
LANES = 128
SUBLANES = 8
HEADS_PER_VREG = LANES // HEAD_DIM
MIB = 1 << 20


def _alibi_slope(h):
    return float(2.0 ** (-8.0 * (h + 1) / N_HEADS))


def _params(semantics, vmem_mib, flags=None):
    return pltpu.CompilerParams(dimension_semantics=semantics, vmem_limit_bytes=vmem_mib * MIB, flags=flags)


def _rmsnorm_rows(x, w):
    ms = jnp.mean(x * x, axis=-1, keepdims=True)
    return x * lax.rsqrt(ms + EPS) * w


NORM_ROWS = 256


def _rmsnorm_to(x_ref, nw_ref, xn_ref, add_ref=None):
    tm = x_ref.shape[0]
    rows = min(NORM_ROWS, tm)

    def body(r, carry):
        sl = pl.ds(pl.multiple_of(r * rows, rows), rows)
        x = x_ref[sl, :]
        if add_ref is not None:
            x = x + add_ref[sl, :].astype(F32)
        xn_ref[sl, :] = _rmsnorm_rows(x, nw_ref[...]).astype(BF16)
        return carry

    lax.fori_loop(0, tm // rows, body, 0)


def _norm_mm_kernel(*refs, has_bias, act):
    if has_bias:
        x_ref, nw_ref, w_ref, b_ref, o_ref, wb_ref, xn_ref = refs
    else:
        x_ref, nw_ref, w_ref, o_ref, wb_ref, xn_ref = refs

    @pl.when(pl.program_id(0) == 0)
    def _():
        _rmsnorm_to(x_ref, nw_ref, xn_ref)

    wb = w_ref[...].astype(BF16)
    tile = wb_ref.shape[2]
    for c in range(wb_ref.shape[0]):
        wb_ref[c] = wb[:, c * tile:(c + 1) * tile]
    acc = jnp.dot(xn_ref[...], wb, preferred_element_type=F32)
    if has_bias:
        acc = acc + b_ref[...]
    if act == "sigmoid":
        acc = jax.nn.sigmoid(acc)
    o_ref[...] = acc.astype(o_ref.dtype)


def _norm_mm(x, nw, w, bias, *, act, tn, tile, name):
    m, k = x.shape
    n = w.shape[1]
    per_step = tn // tile
    in_specs = [
        pl.BlockSpec((m, k), lambda j: (0, 0)),
        pl.BlockSpec((1, k), lambda j: (0, 0)),
        pl.BlockSpec((k, tn), lambda j: (0, j)),
    ]
    args = [x, nw.reshape(1, k), w]
    if bias is not None:
        in_specs.append(pl.BlockSpec((1, tn), lambda j: (0, j)))
        args.append(bias.reshape(1, n))
    return pl.pallas_call(
        functools.partial(_norm_mm_kernel, has_bias=bias is not None, act=act),
        grid=(n // tn,),
        in_specs=in_specs,
        out_specs=[pl.BlockSpec((m, tn), lambda j: (0, j)),
                   pl.BlockSpec((per_step, k, tile), lambda j: (j, 0, 0))],
        out_shape=[jax.ShapeDtypeStruct((m, n), F32), jax.ShapeDtypeStruct((n // tile, k, tile), BF16)],
        scratch_shapes=[pltpu.VMEM((m, k), BF16)],
        compiler_params=_params(("arbitrary",), 56),
        name=name,
    )(*args)


def _norm_mm_tiles_kernel(x_ref, nw_ref, w_ref, o_ref, xn_ref):
    @pl.when(pl.program_id(0) == 0)
    def _():
        _rmsnorm_to(x_ref, nw_ref, xn_ref)

    tile = w_ref.shape[2]
    for c in range(w_ref.shape[0]):
        o_ref[:, c * tile:(c + 1) * tile] = jnp.dot(xn_ref[...], w_ref[c], preferred_element_type=F32)


def _norm_mm_tiles(x, nw, wt, *, tiles_per_step, name):
    m, k = x.shape
    n_tiles, _, tile = wt.shape
    tn = tiles_per_step * tile
    return pl.pallas_call(
        _norm_mm_tiles_kernel,
        grid=(n_tiles // tiles_per_step,),
        in_specs=[
            pl.BlockSpec((m, k), lambda j: (0, 0)),
            pl.BlockSpec((1, k), lambda j: (0, 0)),
            pl.BlockSpec((tiles_per_step, k, tile), lambda j: (j, 0, 0)),
        ],
        out_specs=pl.BlockSpec((m, tn), lambda j: (0, j)),
        out_shape=jax.ShapeDtypeStruct((m, n_tiles * tile), F32),
        scratch_shapes=[pltpu.VMEM((m, k), BF16)],
        compiler_params=_params(("arbitrary",), 48),
        name=name,
    )(x, nw.reshape(1, k), wt)


def _in_proj_kernel(x_ref, nw_ref, win_ref, wg_ref, bg_ref, proj_ref, gates_ref, xn_ref):
    xn_ref[...] = _rmsnorm_rows(x_ref[...], nw_ref[...]).astype(BF16)
    tn = win_ref.shape[2]
    for c in range(win_ref.shape[0]):
        acc = jnp.dot(xn_ref[...], win_ref[c], preferred_element_type=F32)
        proj_ref[:, c * tn:(c + 1) * tn] = acc.astype(proj_ref.dtype)
    for c in range(wg_ref.shape[0]):
        cols = slice(c * tn, (c + 1) * tn)
        acc = jnp.dot(xn_ref[...], wg_ref[c], preferred_element_type=F32)
        gates_ref[:, cols] = jax.nn.sigmoid(acc + bg_ref[:, cols]).astype(gates_ref.dtype)


def _in_proj(x, nw, w_in, w_gate, b_gate, *, tm):
    m, k = x.shape
    n_in, _, tn = w_in.shape
    n_g = w_gate.shape[0]
    rows = lambda width: pl.BlockSpec((tm, width), lambda i: (i, 0))
    resident = lambda a: pl.BlockSpec(a.shape, lambda i: (0,) * a.ndim, pipeline_mode=pl.Buffered(1))
    bg = b_gate.reshape(1, -1)
    return pl.pallas_call(
        _in_proj_kernel,
        grid=(m // tm,),
        in_specs=[rows(k), pl.BlockSpec((1, k), lambda i: (0, 0)), resident(w_in), resident(w_gate),
                  pl.BlockSpec(bg.shape, lambda i: (0, 0))],
        out_specs=[rows(n_in * tn), rows(n_g * tn)],
        out_shape=[
            jax.ShapeDtypeStruct((m, n_in * tn), BF16),
            jax.ShapeDtypeStruct((m, n_g * tn), BF16),
        ],
        scratch_shapes=[pltpu.VMEM((tm, k), BF16)],
        compiler_params=_params(("parallel",), 56),
        name="in_proj",
    )(x, nw.reshape(1, k), w_in, w_gate, bg)


def _head_group_matrix():
    r = lax.broadcasted_iota(jnp.int32, (LANES, LANES), 0)
    c = lax.broadcasted_iota(jnp.int32, (LANES, LANES), 1)
    return jnp.where((r >> 6) == (c >> 6), 1.0 / HEAD_DIM, 0.0).astype(BF16)


def _head_rmsnorm(xcol, w, gmat):
    ms = jnp.dot((xcol * xcol).astype(BF16), gmat, preferred_element_type=F32)
    return xcol * lax.rsqrt(ms + EPS) * w


def _attn_prompt_kernel(sink_ref, q_ref, kc_ref, kp_ref, vc_ref, vp_ref, qw_ref, kw_ref,
                        o_ref, kn_ref, vn_ref):
    c = pl.program_id(1)
    blk = WINDOW
    gmat = _head_group_matrix()
    lane = lax.broadcasted_iota(jnp.int32, (1, LANES), 1)
    low = lane < HEAD_DIM

    kw = kw_ref[...]
    qw = qw_ref[...]
    kc = kc_ref[...].astype(F32)
    kp = kp_ref[...].astype(F32)
    vc = vc_ref[...].astype(F32)
    vp = vp_ref[...].astype(F32)
    ncol = kc.shape[1] // LANES
    kcn = [_head_rmsnorm(kc[:, p * LANES:(p + 1) * LANES], kw, gmat) for p in range(ncol)]
    kpn = [_head_rmsnorm(kp[:, p * LANES:(p + 1) * LANES], kw, gmat) for p in range(ncol)]
    kn_ref[0] = jnp.concatenate(kcn, axis=1)
    vn_ref[0] = vc

    kcat = [jnp.concatenate([kpn[p], kcn[p]], axis=0) for p in range(ncol)]
    vcat = [jnp.concatenate([vp[:, p * LANES:(p + 1) * LANES], vc[:, p * LANES:(p + 1) * LANES]], axis=0)
            for p in range(ncol)]
    krol = [pltpu.roll(kcat[p], HEAD_DIM, 1) for p in range(ncol)]
    vrol = [pltpu.roll(vcat[p], HEAD_DIM, 1) for p in range(ncol)]

    qi = lax.broadcasted_iota(jnp.int32, (blk, 2 * blk), 0)
    kj = lax.broadcasted_iota(jnp.int32, (blk, 2 * blk), 1)
    dist = blk + qi - kj
    valid = (dist >= 0) & (dist <= WINDOW) & ((kj >= blk) | (c > 0))
    distf = dist.astype(F32)

    q = q_ref[...].astype(F32)
    qn = [_head_rmsnorm(q[:, p * LANES:(p + 1) * LANES], qw, gmat)
          for p in range(N_HEADS // HEADS_PER_VREG)]

    for g in range(N_KV_HEADS):
        p, half = divmod(g, HEADS_PER_VREG)
        if half == 0:
            kd = jnp.where(low, kcat[p], krol[p])
            vd = jnp.where(low, vcat[p], vrol[p])
        else:
            kd = jnp.where(low, krol[p], kcat[p])
            vd = jnp.where(low, vrol[p], vcat[p])
        kd = kd.astype(BF16)
        vd = vd.astype(BF16)
        qs = []
        for hl in range(GQA_GROUP):
            h = g * GQA_GROUP + hl
            pc, hh = divmod(h, HEADS_PER_VREG)
            keep = low if hh == 0 else jnp.logical_not(low)
            qs.append(jnp.where(keep, qn[pc], 0.0).astype(BF16))
        qstack = jnp.concatenate(qs, axis=0)
        s_all = lax.dot_general(qstack, kd, (((1,), (1,)), ((), ())), preferred_element_type=F32)
        ps = []
        for hl in range(GQA_GROUP):
            h = g * GQA_GROUP + hl
            s = s_all[hl * blk:(hl + 1) * blk] - _alibi_slope(h) * distf
            s = jnp.where(valid, s, MASK_VALUE)
            sink = sink_ref[h]
            mx = jnp.maximum(jnp.max(s, axis=-1, keepdims=True), sink)
            e = jnp.exp(s - mx)
            den = jnp.sum(e, axis=-1, keepdims=True) + jnp.exp(sink - mx)
            ps.append((e / den).astype(BF16))
        pstack = jnp.concatenate(ps, axis=0)
        o_all = jnp.dot(pstack, vd, preferred_element_type=F32)
        for pair in range(GQA_GROUP // HEADS_PER_VREG):
            pc = g * (GQA_GROUP // HEADS_PER_VREG) + pair
            even = o_all[(2 * pair) * blk:(2 * pair + 1) * blk]
            odd = o_all[(2 * pair + 1) * blk:(2 * pair + 2) * blk]
            o_ref[:, pc * LANES:(pc + 1) * LANES] = jnp.where(low, even, odd).astype(o_ref.dtype)


def _attn_prompt(proj, qw2, kw2, sinks, *, batch, seq):
    nb = seq // WINDOW
    attn_w = N_HEADS * HEAD_DIM
    kv_w = N_KV_HEADS * HEAD_DIM
    kcol = attn_w // kv_w
    vcol = kcol + 1

    def cur(col):
        return lambda b, c: (b * nb + c, col)

    def prev(col):
        return lambda b, c: (jnp.maximum(b * nb + c - 1, 0), col)

    return pl.pallas_call(
        _attn_prompt_kernel,
        grid=(batch, nb),
        in_specs=[
            pl.BlockSpec(memory_space=pltpu.SMEM),
            pl.BlockSpec((WINDOW, attn_w), cur(0)),
            pl.BlockSpec((WINDOW, kv_w), cur(kcol)),
            pl.BlockSpec((WINDOW, kv_w), prev(kcol)),
            pl.BlockSpec((WINDOW, kv_w), cur(vcol)),
            pl.BlockSpec((WINDOW, kv_w), prev(vcol)),
            pl.BlockSpec((1, LANES), lambda b, c: (0, 0)),
            pl.BlockSpec((1, LANES), lambda b, c: (0, 0)),
        ],
        out_specs=[
            pl.BlockSpec((WINDOW, attn_w), lambda b, c: (b * nb + c, 0)),
            pl.BlockSpec((1, WINDOW, kv_w), lambda b, c: (b, 0, 0)),
            pl.BlockSpec((1, WINDOW, kv_w), lambda b, c: (b, 0, 0)),
        ],
        out_shape=[
            jax.ShapeDtypeStruct((batch * seq, attn_w), BF16),
            jax.ShapeDtypeStruct((batch, WINDOW, kv_w), F32),
            jax.ShapeDtypeStruct((batch, WINDOW, kv_w), F32),
        ],
        compiler_params=_params(("parallel", "arbitrary"), 32),
        name="attn_prompt",
    )(sinks, proj, proj, proj, proj, proj, qw2, kw2)


def _layernorm_rows(x, w):
    mu = jnp.mean(x, axis=-1, keepdims=True)
    xc = x - mu
    return xc * lax.rsqrt(jnp.mean(xc * xc, axis=-1, keepdims=True) + EPS) * w


GELU_C = 0.7978845608028654
GELU_A = 0.044715


def _gelu_tanh(x):
    return x * jax.nn.sigmoid((2.0 * GELU_C) * (x + GELU_A * (x * x * x)))


def _sgu_prompt_kernel(gu0_ref, gu1_ref, gv0_ref, gv1_ref, nw_ref, ws_ref, bs_ref, o_ref, st_ref):
    rows = gu0_ref.shape[0]
    gv = jnp.concatenate([gv0_ref[...], gv1_ref[...]], axis=1).astype(F32)
    vn = _layernorm_rows(_gelu_tanh(gv), nw_ref[...])
    st_ref[0] = vn[rows - CHUNK:]
    vb = vn.astype(BF16)
    r = lax.broadcasted_iota(jnp.int32, (CHUNK, CHUNK), 0)
    c = lax.broadcasted_iota(jnp.int32, (CHUNK, CHUNK), 1)
    causal = r >= c
    gw = vb.shape[1] // GM_GROUPS
    half = gu0_ref.shape[1]
    for g in range(GM_GROUPS):
        w = jnp.where(causal, ws_ref[g], 0.0).astype(BF16)
        src = gu0_ref if g * gw < half else gu1_ref
        off = g * gw - (0 if g * gw < half else half)
        for ch in range(rows // CHUNK):
            rs = slice(ch * CHUNK, (ch + 1) * CHUNK)
            mix = jnp.dot(w, vb[rs, g * gw:(g + 1) * gw], preferred_element_type=F32) + bs_ref[:, g:g + 1]
            u = _gelu_tanh(src[rs, off:off + gw].astype(F32))
            o_ref[rs, g * gw:(g + 1) * gw] = (u * mix).astype(o_ref.dtype)


def _sgu_prompt(proj, nw, ws, bs_t, *, batch, seq, gu_off, gm_w, chunks_per_step):
    rows = chunks_per_step * CHUNK
    nc = seq // rows
    half = gm_w // 2
    b0 = gu_off // half

    def col(k):
        return lambda b, c: (b * nc + c, b0 + k)

    return pl.pallas_call(
        _sgu_prompt_kernel,
        grid=(batch, nc),
        in_specs=[
            pl.BlockSpec((rows, half), col(0)),
            pl.BlockSpec((rows, half), col(1)),
            pl.BlockSpec((rows, half), col(2)),
            pl.BlockSpec((rows, half), col(3)),
            pl.BlockSpec((1, gm_w), lambda b, c: (0, 0)),
            pl.BlockSpec((GM_GROUPS, CHUNK, CHUNK), lambda b, c: (0, 0, 0)),
            pl.BlockSpec((CHUNK, GM_GROUPS), lambda b, c: (0, 0)),
        ],
        out_specs=[
            pl.BlockSpec((rows, gm_w), lambda b, c: (b * nc + c, 0)),
            pl.BlockSpec((1, CHUNK, gm_w), lambda b, c: (b, 0, 0)),
        ],
        out_shape=[
            jax.ShapeDtypeStruct((batch * seq, gm_w), BF16),
            jax.ShapeDtypeStruct((batch, CHUNK, gm_w), F32),
        ],
        compiler_params=_params(("parallel", "arbitrary"), 32),
        name="sgu_prompt",
    )(proj, proj, proj, proj, nw.reshape(1, gm_w), ws, bs_t)


N_ATTN_IN, N_ATTN_OUT, N_SGU_IN, N_SGU_OUT, N_CAST = 8, 3, 7, 2, 3


def _mixers_prompt_kernel(*refs):
    attn_in = refs[:N_ATTN_IN]
    sgu_in = refs[N_ATTN_IN:N_ATTN_IN + N_SGU_IN]
    cast_in = refs[N_ATTN_IN + N_SGU_IN:N_ATTN_IN + N_SGU_IN + N_CAST]
    outs = refs[N_ATTN_IN + N_SGU_IN + N_CAST:]
    attn_out = outs[:N_ATTN_OUT]
    sgu_out = outs[N_ATTN_OUT:N_ATTN_OUT + N_SGU_OUT]
    wup_ref, wdn_ref, wpg_ref = cast_in
    wupb_ref, wdnb_ref, wpgb_ref = outs[N_ATTN_OUT + N_SGU_OUT:]
    tile = wupb_ref.shape[2]
    for c in range(wupb_ref.shape[0]):
        wupb_ref[c] = wup_ref[:, c * tile:(c + 1) * tile].astype(BF16)
    wdnb_ref[...] = wdn_ref[...].astype(BF16)
    wpgb_ref[...] = wpg_ref[...].astype(BF16)
    _sgu_prompt_kernel(*sgu_in, *sgu_out)
    _attn_prompt_kernel(*attn_in, *attn_out)


def _mixers_prompt(proj, qw2, kw2, sinks, nw, ws, bs_t, w_up, w_down, w_pg, *, batch, seq, gu_off, gm_w, tile):
    nb = seq // WINDOW
    steps = batch * nb
    flat = lambda b, c: b * nb + c
    k_up, n_up = w_up.shape
    wdn2 = w_down.reshape(2 * w_down.shape[0], w_down.shape[1] // 2)
    slab = lambda a: pl.BlockSpec((a.shape[0] // steps, a.shape[1]), lambda b, c: (flat(b, c), 0))
    attn_w = N_HEADS * HEAD_DIM
    kv_w = N_KV_HEADS * HEAD_DIM
    kcol = attn_w // kv_w
    vcol = kcol + 1
    half = gm_w // 2
    b0 = gu_off // half
    cur = lambda col: (lambda b, c: (b * nb + c, col))
    prev = lambda col: (lambda b, c: (jnp.maximum(b * nb + c - 1, 0), col))
    const = lambda shape: pl.BlockSpec(shape, lambda b, c: (0,) * len(shape))
    state = lambda width: pl.BlockSpec((1, WINDOW, width), lambda b, c: (b, 0, 0))
    return pl.pallas_call(
        _mixers_prompt_kernel,
        grid=(batch, nb),
        in_specs=[
            pl.BlockSpec(memory_space=pltpu.SMEM),
            pl.BlockSpec((WINDOW, attn_w), cur(0)),
            pl.BlockSpec((WINDOW, kv_w), cur(kcol)),
            pl.BlockSpec((WINDOW, kv_w), prev(kcol)),
            pl.BlockSpec((WINDOW, kv_w), cur(vcol)),
            pl.BlockSpec((WINDOW, kv_w), prev(vcol)),
            const((1, LANES)), const((1, LANES)),
            pl.BlockSpec((CHUNK, half), cur(b0)),
            pl.BlockSpec((CHUNK, half), cur(b0 + 1)),
            pl.BlockSpec((CHUNK, half), cur(b0 + 2)),
            pl.BlockSpec((CHUNK, half), cur(b0 + 3)),
            const((1, gm_w)), const((GM_GROUPS, CHUNK, CHUNK)), const((CHUNK, GM_GROUPS)),
            slab(w_up), slab(wdn2), slab(w_pg),
        ],
        out_specs=[
            pl.BlockSpec((WINDOW, attn_w), cur(0)), state(kv_w), state(kv_w),
            pl.BlockSpec((CHUNK, gm_w), cur(0)), state(gm_w),
            pl.BlockSpec((n_up // tile, k_up // steps, tile), lambda b, c: (0, flat(b, c), 0)),
            slab(wdn2), slab(w_pg),
        ],
        out_shape=[
            jax.ShapeDtypeStruct((batch * seq, attn_w), BF16),
            jax.ShapeDtypeStruct((batch, WINDOW, kv_w), F32),
            jax.ShapeDtypeStruct((batch, WINDOW, kv_w), F32),
            jax.ShapeDtypeStruct((batch * seq, gm_w), BF16),
            jax.ShapeDtypeStruct((batch, CHUNK, gm_w), F32),
            jax.ShapeDtypeStruct((n_up // tile, k_up, tile), BF16),
            jax.ShapeDtypeStruct(wdn2.shape, BF16),
            jax.ShapeDtypeStruct(w_pg.shape, BF16),
        ],
        compiler_params=_params(("arbitrary", "arbitrary"), 32),
        name="mixers_prompt",
    )(sinks, proj, proj, proj, proj, proj, qw2, kw2, proj, proj, proj, proj, nw.reshape(1, gm_w), ws, bs_t,
      w_up, wdn2, w_pg)


def _split_dot(x, m):
    hi = x.astype(BF16)
    lo = (x - hi.astype(F32)).astype(BF16)
    return jnp.dot(hi, m, preferred_element_type=F32) + jnp.dot(lo, m, preferred_element_type=F32)


def _attn_sample_kernel(q_ref, kn_ref, vn_ref, kb_ref, vb_ref, qw_ref, kw_ref, sink_ref,
                        wa_ref, wg_ref, wo_ref, o_ref, ko_ref, vo_ref, wab_ref, wgb_ref, wob_ref):
    wab_ref[...] = wa_ref[...].astype(BF16)
    wgb_ref[...] = wg_ref[...].astype(BF16)
    wob_ref[...] = wo_ref[...].astype(BF16)
    nsamp = q_ref.shape[0]
    attn_w = q_ref.shape[1]
    kv_w = kn_ref.shape[1]
    gmat = _head_group_matrix()
    q = q_ref[...].astype(F32)
    kn = kn_ref[...].astype(F32)
    qn = jnp.concatenate(
        [_head_rmsnorm(q[:, p * LANES:(p + 1) * LANES], qw_ref[...], gmat)
         for p in range(attn_w // LANES)], axis=1)
    knn = jnp.concatenate(
        [_head_rmsnorm(kn[:, p * LANES:(p + 1) * LANES], kw_ref[...], gmat)
         for p in range(kv_w // LANES)], axis=1)
    vnn = vn_ref[...].astype(F32)

    ec = lax.broadcasted_iota(jnp.int32, (kv_w, attn_w), 0)
    el = lax.broadcasted_iota(jnp.int32, (kv_w, attn_w), 1)
    expand = (((ec >> 6) == (el >> 8)) & ((ec & 63) == (el & 63))).astype(BF16)
    gl = lax.broadcasted_iota(jnp.int32, (attn_w, LANES), 0)
    gh = lax.broadcasted_iota(jnp.int32, (attn_w, LANES), 1)
    hsum = ((gl >> 6) == gh).astype(BF16)
    tl = lax.broadcasted_iota(jnp.int32, (LANES, attn_w), 1)
    th = lax.broadcasted_iota(jnp.int32, (LANES, attn_w), 0)
    hexp = ((tl >> 6) == th).astype(BF16)

    nkeys = WINDOW + SUBLANES
    row = lax.broadcasted_iota(jnp.int32, (nkeys, LANES), 0)
    head = lax.broadcasted_iota(jnp.int32, (nkeys, LANES), 1)
    slope = jnp.exp2(-8.0 * (head + 1).astype(F32) / N_HEADS)
    dist = (WINDOW - row).astype(F32)
    key_ok = row <= WINDOW
    srow = lax.broadcasted_iota(jnp.int32, (WINDOW, kv_w), 0)
    sinks = sink_ref[...]

    for s in range(nsamp):
        kb = kb_ref[s]
        vb = vb_ref[s]
        knew = knn[s:s + 1]
        vnew = vnn[s:s + 1]
        ko_ref[s] = jnp.where(srow == WINDOW - 1, knew, pltpu.roll(kb, WINDOW - 1, 0))
        vo_ref[s] = jnp.where(srow == WINDOW - 1, vnew, pltpu.roll(vb, WINDOW - 1, 0))
        kk = jnp.concatenate([kb, jnp.broadcast_to(knew, (SUBLANES, kv_w))], axis=0)
        vv = jnp.concatenate([vb, jnp.broadcast_to(vnew, (SUBLANES, kv_w))], axis=0)
        kexp = jnp.dot(kk.astype(BF16), expand, preferred_element_type=F32)
        vexp = jnp.dot(vv.astype(BF16), expand, preferred_element_type=F32)
        sc = _split_dot(kexp * qn[s:s + 1], hsum)
        sc = sc - slope * dist
        sc = jnp.where(key_ok, sc, MASK_VALUE)
        mx = jnp.maximum(jnp.max(sc, axis=0, keepdims=True), sinks)
        e = jnp.exp(sc - mx)
        den = jnp.sum(e, axis=0, keepdims=True) + jnp.exp(sinks - mx)
        pexp = jnp.dot((e / den).astype(BF16), hexp, preferred_element_type=F32)
        o_ref[s:s + 1, :] = jnp.sum(pexp * vexp, axis=0, keepdims=True)


def _attn_sample(q, knew, vnew, kbuf, vbuf, qw2, kw2, sinks_row, wa, wg, wo, *, group):
    nb, attn_w = q.shape
    kv_w = knew.shape[1]
    steps = nb // group
    row = lambda width: pl.BlockSpec((group, width), lambda i: (i, 0))
    buf = pl.BlockSpec((group, WINDOW, kv_w), lambda i: (i, 0, 0))
    vec = pl.BlockSpec((1, LANES), lambda i: (0, 0))
    slab = lambda w: pl.BlockSpec((w.shape[0] // steps, w.shape[1]), lambda i: (i, 0))
    return pl.pallas_call(
        _attn_sample_kernel,
        grid=(steps,),
        in_specs=[row(attn_w), row(kv_w), row(kv_w), buf, buf, vec, vec, vec, slab(wa), slab(wg), slab(wo)],
        out_specs=[row(attn_w), buf, buf, slab(wa), slab(wg), slab(wo)],
        out_shape=[
            jax.ShapeDtypeStruct((nb, attn_w), F32),
            jax.ShapeDtypeStruct((nb, WINDOW, kv_w), F32),
            jax.ShapeDtypeStruct((nb, WINDOW, kv_w), F32),
            jax.ShapeDtypeStruct(wa.shape, BF16),
            jax.ShapeDtypeStruct(wg.shape, BF16),
            jax.ShapeDtypeStruct(wo.shape, BF16),
        ],
        compiler_params=_params(("parallel",), 48),
        name="attn_sample",
    )(q, knew, vnew, kbuf, vbuf, qw2, kw2, sinks_row, wa, wg, wo)


def _sgu_sample_kernel(gu_ref, gv_ref, nw_ref, w0_ref, b0_ref, o_ref, st_ref):
    vn = _layernorm_rows(jax.nn.gelu(gv_ref[...].astype(F32)), nw_ref[...])
    st_ref[...] = vn
    mix = w0_ref[...] * vn + b0_ref[...]
    o_ref[...] = jax.nn.gelu(gu_ref[...].astype(F32)) * mix


def _sgu_sample(gu, gv, nw, w0_row, b0_row):
    nb, gm_w = gu.shape
    full = pl.BlockSpec((nb, gm_w), lambda i: (0, 0))
    vec = pl.BlockSpec((1, gm_w), lambda i: (0, 0))
    return pl.pallas_call(
        _sgu_sample_kernel,
        grid=(1,),
        in_specs=[full, full, vec, vec, vec],
        out_specs=[full, full],
        out_shape=[jax.ShapeDtypeStruct((nb, gm_w), F32), jax.ShapeDtypeStruct((nb, gm_w), F32)],
        name="sgu_sample",
    )(gu, gv, nw.reshape(1, gm_w), w0_row, b0_row)


def _merge_out_kernel(a_ref, s_ref, ga_ref, gb_ref, x_ref, wa_ref, wg_ref, wo_ref, nw_ref,
                      x1_ref, xn1_ref):
    a = jnp.dot(a_ref[...].astype(BF16), wa_ref[...], preferred_element_type=F32)
    m = jnp.dot(s_ref[...].astype(BF16), wg_ref[...], preferred_element_type=F32)
    merged = (gb_ref[...].astype(F32) * a + ga_ref[...].astype(F32) * m).astype(BF16)
    x1 = x_ref[...] + jnp.dot(merged, wo_ref[...], preferred_element_type=F32)
    x1_ref[...] = x1
    xn1_ref[...] = _rmsnorm_rows(x1, nw_ref[...]).astype(BF16)


def _merge_out(attn, sgu, gates, x, wa, wg, wo, nw, *, tm):
    m, ka = attn.shape
    kg = sgu.shape[1]
    d = wo.shape[1]
    rows = lambda width, col=0: pl.BlockSpec((tm, width), lambda i: (i, col))
    resident = lambda shape: pl.BlockSpec(shape, lambda i: (0, 0), pipeline_mode=pl.Buffered(1))
    return pl.pallas_call(
        _merge_out_kernel,
        grid=(m // tm,),
        in_specs=[
            rows(ka), rows(kg), rows(d, 0), rows(d, 1), rows(d),
            resident((ka, d)), resident((kg, d)), resident((d, d)),
            pl.BlockSpec((1, d), lambda i: (0, 0)),
        ],
        out_specs=[rows(d), rows(d)],
        out_shape=[jax.ShapeDtypeStruct((m, d), F32), jax.ShapeDtypeStruct((m, d), BF16)],
        compiler_params=_params(("parallel",), 48),
        name="merge_out",
    )(attn, sgu, gates, gates, x, wa, wg, wo, nw.reshape(1, d))


PAD = SUBLANES
FFN_ROW_BLOCKS = 4


def _ffn_step(t, xn_ref, wug_ref, wuu_ref, wd_ref, cwg_ref, cwu_ref, cbg_ref, cbu_ref,
              csg_ref, csu_ref, h_new, h_old, carry_ref, acc_ref, *, nf, tiles_per_seq):
    tm = xn_ref.shape[0]
    tf = wug_ref.shape[1]
    rb = tm // FFN_ROW_BLOCKS
    tp = jnp.maximum(t - 1, 0)
    ip = tp // nf
    jp = tp % nf
    first = (ip % tiles_per_seq) == 0
    h_old[0:PAD, :] = jnp.where(first, 0.0, carry_ref[jp])
    carry_ref[jp] = h_old[tm:tm + PAD, :]
    cw = jnp.concatenate([cwg_ref[...], cwu_ref[...]], axis=1)
    cb = jnp.concatenate([cbg_ref[...], cbu_ref[...]], axis=1)
    wug = wug_ref[...]
    wuu = wuu_ref[...]
    wd = wd_ref[...]
    for r in range(FFN_ROW_BLOCKS):
        lo = r * rb
        xr = xn_ref[lo:lo + rb, :]
        h_new[PAD + lo:PAD + lo + rb, 0:tf] = jnp.dot(xr, wug, preferred_element_type=F32)
        h_new[PAD + lo:PAD + lo + rb, tf:2 * tf] = jnp.dot(xr, wuu, preferred_element_type=F32)
        hp = h_old[lo:lo + rb + PAD, :]
        hc = (cw[0:1] * pltpu.roll(hp, 2, 0)[PAD:] + cw[1:2] * pltpu.roll(hp, 1, 0)[PAD:]
              + cw[2:3] * hp[PAD:] + cb)
        act = (jax.nn.silu(hc[:, 0:tf]) * hc[:, tf:2 * tf]).astype(BF16)
        acc_ref[lo:lo + rb, :] += jnp.dot(act, wd, preferred_element_type=F32)
    csg_ref[0] = h_old[PAD + tm - 2:PAD + tm, 0:tf]
    csu_ref[0] = h_old[PAD + tm - 2:PAD + tm, tf:2 * tf]


def _ffn_kernel(xn_ref, wug_ref, wuu_ref, wd_ref, cwg_ref, cwu_ref, cbg_ref, cbu_ref,
                o_ref, csg_ref, csu_ref, ha_ref, hb_ref, carry_ref, acc_ref, *, nf, tiles_per_seq):
    t = pl.program_id(0)
    step = functools.partial(_ffn_step, t, xn_ref, wug_ref, wuu_ref, wd_ref, cwg_ref, cwu_ref,
                             cbg_ref, cbu_ref, csg_ref, csu_ref, carry_ref=carry_ref, acc_ref=acc_ref,
                             nf=nf, tiles_per_seq=tiles_per_seq)

    @pl.when(t == 0)
    def _():
        hb_ref[...] = jnp.zeros_like(hb_ref)
        carry_ref[...] = jnp.zeros_like(carry_ref)

    @pl.when(jnp.maximum(t - 1, 0) % nf == 0)
    def _():
        acc_ref[...] = jnp.zeros_like(acc_ref)

    @pl.when(t % 2 == 0)
    def _():
        step(h_new=ha_ref, h_old=hb_ref)

    @pl.when(t % 2 == 1)
    def _():
        step(h_new=hb_ref, h_old=ha_ref)

    @pl.when(jnp.maximum(t - 1, 0) % nf == nf - 1)
    def _():
        o_ref[...] = acc_ref[...].astype(o_ref.dtype)


def _ffn_prompt(xn, w_up, w_down, cw, cb, *, batch, seq, tm):
    m, k = xn.shape
    d_ff, d = w_down.shape
    tf = w_up.shape[2]
    nf = d_ff // tf
    nm = m // tm
    tiles_per_seq = seq // tm
    prev = lambda t: jnp.maximum(t - 1, 0)
    up_tile = lambda off: (lambda t: (off + t % nf, 0, 0))
    dn_col = lambda off: (lambda t: (0, off + prev(t) % nf))
    state = lambda t: (prev(t) // nf, 0, prev(t) % nf)
    delta, csg, csu = pl.pallas_call(
        functools.partial(_ffn_kernel, nf=nf, tiles_per_seq=tiles_per_seq),
        grid=(nm * nf + 1,),
        in_specs=[
            pl.BlockSpec((tm, k), lambda t: (jnp.minimum(t // nf, nm - 1), 0)),
            pl.BlockSpec((None, k, tf), up_tile(0)),
            pl.BlockSpec((None, k, tf), up_tile(nf)),
            pl.BlockSpec((tf, d), lambda t: (prev(t) % nf, 0)),
            pl.BlockSpec((3, tf), dn_col(0)),
            pl.BlockSpec((3, tf), dn_col(nf)),
            pl.BlockSpec((1, tf), dn_col(0)),
            pl.BlockSpec((1, tf), dn_col(nf)),
        ],
        out_specs=[
            pl.BlockSpec((tm, d), lambda t: (prev(t) // nf, 0)),
            pl.BlockSpec((1, 2, tf), state),
            pl.BlockSpec((1, 2, tf), state),
        ],
        out_shape=[
            jax.ShapeDtypeStruct((m, d), BF16),
            jax.ShapeDtypeStruct((nm, 2, d_ff), F32),
            jax.ShapeDtypeStruct((nm, 2, d_ff), F32),
        ],
        scratch_shapes=[
            pltpu.VMEM((tm + PAD, 2 * tf), F32),
            pltpu.VMEM((tm + PAD, 2 * tf), F32),
            pltpu.VMEM((nf, PAD, 2 * tf), F32),
            pltpu.VMEM((tm, d), F32),
        ],
        compiler_params=_params(("arbitrary",), 58),
        name="ffn_prompt",
    )(xn, w_up, w_up, w_down, cw, cw, cb.reshape(1, -1), cb.reshape(1, -1))
    tails = jnp.concatenate([csg, csu], axis=-1).reshape(batch, tiles_per_seq, 2, 2 * d_ff)
    return delta, tails[:, -1]


def _ffn_down_sample_kernel(hg_ref, hu_ref, b0g_ref, b0u_ref, b1g_ref, b1u_ref, cwg_ref, cwu_ref,
                            cbg_ref, cbu_ref, wd_ref, x_ref, o_ref):
    @pl.when(pl.program_id(0) == 0)
    def _():
        o_ref[...] = x_ref[...]

    def conv(h_ref, b0_ref, b1_ref, cw_ref, cb_ref):
        cw = cw_ref[...]
        return cw[0:1] * b0_ref[...] + cw[1:2] * b1_ref[...] + cw[2:3] * h_ref[...] + cb_ref[...]

    hcg = conv(hg_ref, b0g_ref, b1g_ref, cwg_ref, cbg_ref)
    hcu = conv(hu_ref, b0u_ref, b1u_ref, cwu_ref, cbu_ref)
    act = (jax.nn.silu(hcg) * hcu).astype(BF16)
    o_ref[...] += jnp.dot(act, wd_ref[...], preferred_element_type=F32)


def _ffn_down_sample(h, state2, cw, cb, wd, x, *, tf):
    nb, two_f = h.shape
    d_ff = two_f // 2
    nf = d_ff // tf
    d = wd.shape[1]
    colblk = lambda off: pl.BlockSpec((nb, tf), lambda j: (0, j + off))
    return pl.pallas_call(
        _ffn_down_sample_kernel,
        grid=(nf,),
        in_specs=[
            colblk(0), colblk(nf),
            colblk(0), colblk(nf), colblk(2 * nf), colblk(3 * nf),
            pl.BlockSpec((3, tf), lambda j: (0, j)),
            pl.BlockSpec((3, tf), lambda j: (0, j + nf)),
            pl.BlockSpec((1, tf), lambda j: (0, j)),
            pl.BlockSpec((1, tf), lambda j: (0, j + nf)),
            pl.BlockSpec((tf, d), lambda j: (j, 0)),
            pl.BlockSpec((nb, d), lambda j: (0, 0)),
        ],
        out_specs=pl.BlockSpec((nb, d), lambda j: (0, 0)),
        out_shape=jax.ShapeDtypeStruct((nb, d), F32),
        compiler_params=_params(("arbitrary",), 32),
        name="ffn_down_sample",
    )(h, h, state2, state2, state2, state2, cw, cw, cb.reshape(1, two_f), cb.reshape(1, two_f), wd, x)


def _ple_sample_kernel(x_ref, nw_ref, w_ref, p_ref, wp_ref, o_ref, wpb_ref):
    x2 = x_ref[...]
    xn = _rmsnorm_rows(x2, nw_ref[...]).astype(BF16)
    wpb = wp_ref[...].astype(BF16)
    wpb_ref[...] = wpb
    gate = jax.nn.sigmoid(jnp.dot(xn, w_ref[...], preferred_element_type=F32))
    emb = jnp.dot(p_ref[...].astype(BF16), wpb, preferred_element_type=F32)
    o_ref[...] = x2 + gate * emb


def _ple_sample(x, nw, w, p, wp):
    m, d = x.shape
    kp = p.shape[1]
    full = lambda shape: pl.BlockSpec(shape, lambda i: (0, 0))
    return pl.pallas_call(
        _ple_sample_kernel,
        grid=(1,),
        in_specs=[full((m, d)), full((1, d)), full((d, d)), full((m, kp)), full((kp, d))],
        out_specs=[full((m, d)), full((kp, d))],
        out_shape=[jax.ShapeDtypeStruct((m, d), F32), jax.ShapeDtypeStruct((kp, d), BF16)],
        compiler_params=_params(("arbitrary",), 32),
        name="ple_sample",
    )(x, nw.reshape(1, d), w, p, wp)


def _ple_rows_kernel(x_ref, d_ref, nw_ref, w_ref, p_ref, wp_ref, o_ref):
    x2 = x_ref[...] + d_ref[...].astype(F32)
    xn = _rmsnorm_rows(x2, nw_ref[...]).astype(BF16)
    gate = jax.nn.sigmoid(jnp.dot(xn, w_ref[...], preferred_element_type=F32))
    emb = jnp.dot(p_ref[...].astype(BF16), wp_ref[...], preferred_element_type=F32)
    o_ref[...] = x2 + gate * emb


def _ple_rows(x, delta, nw, w, p, wp, *, tm):
    m, d = x.shape
    kp = p.shape[1]
    rows = lambda width: pl.BlockSpec((tm, width), lambda i: (i, 0))
    resident = lambda shape: pl.BlockSpec(shape, lambda i: (0, 0), pipeline_mode=pl.Buffered(1))
    return pl.pallas_call(
        _ple_rows_kernel,
        grid=(m // tm,),
        in_specs=[rows(d), rows(d), pl.BlockSpec((1, d), lambda i: (0, 0)), resident((d, d)),
                  rows(kp), resident((kp, d))],
        out_specs=rows(d),
        out_shape=jax.ShapeDtypeStruct((m, d), F32),
        compiler_params=_params(("parallel",), 56),
        name="ple_prompt",
    )(x, delta, nw.reshape(1, d), w, p, wp)


def _layer_weights(i, attn_norm_w, w_in, q_norm_w, k_norm_w, attn_sinks, sgu_norm_w, sgu_w, sgu_b,
                   w_br_attn, w_br_gm, w_gate, b_gate, w_out, ffn_norm_w, w_up, conv_w, conv_b,
                   w_down, ple_norm_w, w_ple_gate, w_ple_proj):
    sinks = attn_sinks[i]
    gm_group_w = w_br_gm.shape[1] // GM_GROUPS
    return dict(
        attn_norm_w=attn_norm_w[i], w_in=w_in[i],
        qw2=jnp.tile(q_norm_w[i] * (HEAD_DIM ** -0.5), HEADS_PER_VREG).reshape(1, LANES),
        kw2=jnp.tile(k_norm_w[i], HEADS_PER_VREG).reshape(1, LANES),
        sinks=sinks,
        sinks_row=jnp.pad(sinks, (0, LANES - N_HEADS)).reshape(1, LANES),
        sgu_norm_w=sgu_norm_w[i], sgu_w=sgu_w[i], sgu_b_t=jnp.transpose(sgu_b[i]),
        sgu_w0_row=jnp.repeat(sgu_w[i][:, 0, 0], gm_group_w).reshape(1, -1),
        sgu_b0_row=jnp.repeat(sgu_b[i][:, 0], gm_group_w).reshape(1, -1),
        w_br_attn=w_br_attn[i], w_br_gm=w_br_gm[i],
        w_gate=w_gate[i], b_gate=b_gate[i], w_out=w_out[i],
        ffn_norm_w=ffn_norm_w[i], w_up=w_up[i], conv_w=conv_w[i], conv_b=conv_b[i],
        w_down=w_down[i], ple_norm_w=ple_norm_w[i], w_ple_gate=w_ple_gate[i], w_ple_proj=w_ple_proj[i],
    )


W_TILE = 512


def _prompt_mixers(x, lw, wb):
    batch, seq, d = x.shape
    x2d = x.reshape(batch * seq, d)
    attn_w = N_HEADS * HEAD_DIM
    kv_w = N_KV_HEADS * HEAD_DIM
    gm_w = lw["w_br_gm"].shape[0]
    proj, gates = _in_proj(x2d, lw["attn_norm_w"], wb["w_in"], wb["w_gate"], lw["b_gate"], tm=256)
    attn, k_new, v_new, sgu, sgu_state, wb["w_up"], w_down2, wb["w_ple_gate"] = _mixers_prompt(
        proj, lw["qw2"], lw["kw2"], lw["sinks"], lw["sgu_norm_w"], lw["sgu_w"], lw["sgu_b_t"],
        lw["w_up"], lw["w_down"], lw["w_ple_gate"],
        batch=batch, seq=seq, gu_off=attn_w + 2 * kv_w, gm_w=gm_w, tile=W_TILE)
    wb["w_down"] = w_down2.reshape(lw["w_down"].shape)
    states = (k_new.reshape(batch, WINDOW, N_KV_HEADS, HEAD_DIM),
              v_new.reshape(batch, WINDOW, N_KV_HEADS, HEAD_DIM), sgu_state)
    return (attn, sgu, gates), states


def _prompt_tail(x, p, mixed, lw, wb):
    batch, seq, d = x.shape
    x2d = x.reshape(batch * seq, d)
    attn, sgu, gates = mixed
    x1, xn1 = _merge_out(attn, sgu, gates, x2d, wb["w_br_attn"], wb["w_br_gm"], wb["w_out"],
                         lw["ffn_norm_w"], tm=256)
    delta, conv_state = _ffn_prompt(xn1, wb["w_up"], wb["w_down"], lw["conv_w"], lw["conv_b"],
                                    batch=batch, seq=seq, tm=1024)
    x3 = _ple_rows(x1, delta, lw["ple_norm_w"], wb["w_ple_gate"], p.reshape(batch * seq, -1),
                   wb["w_ple_proj"], tm=512)
    return x3.reshape(batch, seq, d), conv_state


def _sample_mixers(x, k_buf, v_buf, lw, wb):
    nb, t, d = x.shape
    x2d = x.reshape(nb * t, d)
    attn_w = N_HEADS * HEAD_DIM
    kv_w = N_KV_HEADS * HEAD_DIM
    gm_w = lw["w_br_gm"].shape[0]
    proj, wb["w_in"] = _norm_mm(x2d, lw["attn_norm_w"], lw["w_in"], None, act=None,
                                tn=W_TILE, tile=W_TILE, name="proj_sample")
    gates, wb["w_gate"] = _norm_mm(x2d, lw["attn_norm_w"], lw["w_gate"], lw["b_gate"], act="sigmoid",
                                   tn=2 * W_TILE, tile=W_TILE, name="gates_sample")
    q = proj[:, :attn_w]
    k = proj[:, attn_w:attn_w + kv_w]
    v = proj[:, attn_w + kv_w:attn_w + 2 * kv_w]
    gu = proj[:, attn_w + 2 * kv_w:attn_w + 2 * kv_w + gm_w]
    gv = proj[:, attn_w + 2 * kv_w + gm_w:]
    attn, k_new, v_new, wb["w_br_attn"], wb["w_br_gm"], wb["w_out"] = _attn_sample(
        q, k, v, k_buf.reshape(nb, WINDOW, kv_w), v_buf.reshape(nb, WINDOW, kv_w),
        lw["qw2"], lw["kw2"], lw["sinks_row"], lw["w_br_attn"], lw["w_br_gm"], lw["w_out"], group=8)
    sgu, sgu_state = _sgu_sample(gu, gv, lw["sgu_norm_w"], lw["sgu_w0_row"], lw["sgu_b0_row"])
    x1, _ = _merge_out(attn, sgu, gates, x2d, wb["w_br_attn"], wb["w_br_gm"], wb["w_out"],
                       lw["ffn_norm_w"], tm=nb)
    states = (k_new.reshape(nb, WINDOW, N_KV_HEADS, HEAD_DIM), v_new.reshape(nb, WINDOW, N_KV_HEADS, HEAD_DIM),
              sgu_state.reshape(nb, t, gm_w))
    return x1, states


def _sample_tail(x1, p, conv_buf, lw, wb):
    nb = x1.shape[0]
    h = _norm_mm_tiles(x1, lw["ffn_norm_w"], wb["w_up"], tiles_per_step=2, name="ffn_up_sample")
    x2 = _ffn_down_sample(h, conv_buf.reshape(nb, -1), lw["conv_w"], lw["conv_b"], wb["w_down"], x1, tf=W_TILE)
    x3, wb["w_ple_proj"] = _ple_sample(x2, lw["ple_norm_w"], wb["w_ple_gate"], p.reshape(nb, -1),
                                       lw["w_ple_proj"])
    conv_state = jnp.stack([conv_buf[:, 1, :], h], axis=1)
    return x3, conv_state


def kernel(x_prompt, x_sample, p_prompt, p_sample, state_attn_k, state_attn_v, state_conv, attn_norm_w, w_in, q_norm_w, k_norm_w, attn_sinks, sgu_norm_w, sgu_w, sgu_b, w_br_attn, w_br_gm, w_gate, b_gate, w_out, ffn_norm_w, w_up, conv_w, conv_b, w_down, ple_norm_w, w_ple_gate, w_ple_proj):
    depth = w_in.shape[0]
    xp, xs = x_prompt, x_sample
    outs = [[] for _ in range(8)]
    for i in range(depth):
        lw = _layer_weights(i, attn_norm_w, w_in, q_norm_w, k_norm_w, attn_sinks, sgu_norm_w, sgu_w, sgu_b,
                            w_br_attn, w_br_gm, w_gate, b_gate, w_out, ffn_norm_w, w_up, conv_w, conv_b,
                            w_down, ple_norm_w, w_ple_gate, w_ple_proj)
        wb = {}
        xs1, (ks, vs, gs) = _sample_mixers(xs, state_attn_k[i], state_attn_v[i], lw, wb)
        mixed, (kp, vp, gp) = _prompt_mixers(xp, lw, wb)
        xs3, cs = _sample_tail(xs1, p_sample[i], state_conv[i], lw, wb)
        xs = xs3.reshape(xs.shape)
        xp, cp = _prompt_tail(xp, p_prompt[i], mixed, lw, wb)
        for lst, val in zip(outs, (kp, vp, ks, vs, gp, gs, cp, cs)):
            lst.append(val)
    return (xp, xs) + tuple(jnp.stack(lst) for lst in outs)
```

```python
import functools

import jax
import jax.numpy as jnp
from jax import lax
from jax.experimental import pallas as pl
from jax.experimental.pallas import tpu as pltpu

F32 = jnp.float32
BF16 = jnp.bfloat16

HEAD_DIM = 64
N_HEADS = 16
N_KV_HEADS = 4
GQA_GROUP = N_HEADS // N_KV_HEADS
WINDOW = 128
CHUNK = 128
GM_GROUPS = 4
EPS = 1e-6
MASK_VALUE = -1e30
LANES = 128
SUBLANES = 8
HEADS_PER_VREG = LANES // HEAD_DIM
MIB = 1 << 20


def _alibi_slope(h):
    return float(2.0 ** (-8.0 * (h + 1) / N_HEADS))


def _params(semantics, vmem_mib, flags=None):
    return pltpu.CompilerParams(dimension_semantics=semantics, vmem_limit_bytes=vmem_mib * MIB, flags=flags)


def _rmsnorm_rows(x, w):
    ms = jnp.mean(x * x, axis=-1, keepdims=True)
    return x * lax.rsqrt(ms + EPS) * w


NORM_ROWS = 256


def _rmsnorm_to(x_ref, nw_ref, xn_ref, add_ref=None):
    tm = x_ref.shape[0]
    rows = min(NORM_ROWS, tm)

    def body(r, carry):
        sl = pl.ds(pl.multiple_of(r * rows, rows), rows)
        x = x_ref[sl, :]
        if add_ref is not None:
            x = x + add_ref[sl, :].astype(F32)
        xn_ref[sl, :] = _rmsnorm_rows(x, nw_ref[...]).astype(BF16)
        return carry

    lax.fori_loop(0, tm // rows, body, 0)


def _norm_mm_kernel(*refs, has_bias, act):
    if has_bias:
        x_ref, nw_ref, w_ref, b_ref, o_ref, wb_ref, xn_ref = refs
    else:
        x_ref, nw_ref, w_ref, o_ref, wb_ref, xn_ref = refs

    @pl.when(pl.program_id(0) == 0)
    def _():
        _rmsnorm_to(x_ref, nw_ref, xn_ref)

    wb = w_ref[...].astype(BF16)
    tile = wb_ref.shape[2]
    for c in range(wb_ref.shape[0]):
        wb_ref[c] = wb[:, c * tile:(c + 1) * tile]
    acc = jnp.dot(xn_ref[...], wb, preferred_element_type=F32)
    if has_bias:
        acc = acc + b_ref[...]
    if act == "sigmoid":
        acc = jax.nn.sigmoid(acc)
    o_ref[...] = acc.astype(o_ref.dtype)


def _norm_mm(x, nw, w, bias, *, act, tn, tile, name):
    m, k = x.shape
    n = w.shape[1]
    per_step = tn // tile
    in_specs = [
        pl.BlockSpec((m, k), lambda j: (0, 0)),
        pl.BlockSpec((1, k), lambda j: (0, 0)),
        pl.BlockSpec((k, tn), lambda j: (0, j)),
    ]
    args = [x, nw.reshape(1, k), w]
    if bias is not None:
        in_specs.append(pl.BlockSpec((1, tn), lambda j: (0, j)))
        args.append(bias.reshape(1, n))
    return pl.pallas_call(
        functools.partial(_norm_mm_kernel, has_bias=bias is not None, act=act),
        grid=(n // tn,),
        in_specs=in_specs,
        out_specs=[pl.BlockSpec((m, tn), lambda j: (0, j)),
                   pl.BlockSpec((per_step, k, tile), lambda j: (j, 0, 0))],
        out_shape=[jax.ShapeDtypeStruct((m, n), F32), jax.ShapeDtypeStruct((n // tile, k, tile), BF16)],
        scratch_shapes=[pltpu.VMEM((m, k), BF16)],
        compiler_params=_params(("arbitrary",), 56),
        name=name,
    )(*args)


def _norm_mm_tiles_kernel(x_ref, nw_ref, w_ref, o_ref, xn_ref):
    @pl.when(pl.program_id(0) == 0)
    def _():
        _rmsnorm_to(x_ref, nw_ref, xn_ref)

    tile = w_ref.shape[2]
    for c in range(w_ref.shape[0]):
        o_ref[:, c * tile:(c + 1) * tile] = jnp.dot(xn_ref[...], w_ref[c], preferred_element_type=F32)


def _norm_mm_tiles(x, nw, wt, *, tiles_per_step, name):
    m, k = x.shape
    n_tiles, _, tile = wt.shape
    tn = tiles_per_step * tile
    return pl.pallas_call(
        _norm_mm_tiles_kernel,
        grid=(n_tiles // tiles_per_step,),
        in_specs=[
            pl.BlockSpec((m, k), lambda j: (0, 0)),
            pl.BlockSpec((1, k), lambda j: (0, 0)),
            pl.BlockSpec((tiles_per_step, k, tile), lambda j: (j, 0, 0)),
        ],
        out_specs=pl.BlockSpec((m, tn), lambda j: (0, j)),
        out_shape=jax.ShapeDtypeStruct((m, n_tiles * tile), F32),
        scratch_shapes=[pltpu.VMEM((m, k), BF16)],
        compiler_params=_params(("arbitrary",), 48),
        name=name,
    )(x, nw.reshape(1, k), wt)


def _in_proj_kernel(x_ref, nw_ref, win_ref, wg_ref, bg_ref, proj_ref, gates_ref, xn_ref):
    xn_ref[...] = _rmsnorm_rows(x_ref[...], nw_ref[...]).astype(BF16)
    tn = win_ref.shape[2]
    for c in range(win_ref.shape[0]):
        acc = jnp.dot(xn_ref[...], win_ref[c], preferred_element_type=F32)
        proj_ref[:, c * tn:(c + 1) * tn] = acc.astype(proj_ref.dtype)
    for c in range(wg_ref.shape[0]):
        cols = slice(c * tn, (c + 1) * tn)
        acc = jnp.dot(xn_ref[...], wg_ref[c], preferred_element_type=F32)
        gates_ref[:, cols] = jax.nn.sigmoid(acc + bg_ref[:, cols]).astype(gates_ref.dtype)


def _in_proj(x, nw, w_in, w_gate, b_gate, *, tm):
    m, k = x.shape
    n_in, _, tn = w_in.shape
    n_g = w_gate.shape[0]
    rows = lambda width: pl.BlockSpec((tm, width), lambda i: (i, 0))
    resident = lambda a: pl.BlockSpec(a.shape, lambda i: (0,) * a.ndim, pipeline_mode=pl.Buffered(1))
    bg = b_gate.reshape(1, -1)
    return pl.pallas_call(
        _in_proj_kernel,
        grid=(m // tm,),
        in_specs=[rows(k), pl.BlockSpec((1, k), lambda i: (0, 0)), resident(w_in), resident(w_gate),
                  pl.BlockSpec(bg.shape, lambda i: (0, 0))],
        out_specs=[rows(n_in * tn), rows(n_g * tn)],
        out_shape=[
            jax.ShapeDtypeStruct((m, n_in * tn), BF16),
            jax.ShapeDtypeStruct((m, n_g * tn), BF16),
        ],
        scratch_shapes=[pltpu.VMEM((tm, k), BF16)],
        compiler_params=_params(("parallel",), 56),
        name="in_proj",
    )(x, nw.reshape(1, k), w_in, w_gate, bg)


def _head_group_matrix():
    r = lax.broadcasted_iota(jnp.int32, (LANES, LANES), 0)
    c = lax.broadcasted_iota(jnp.int32, (LANES, LANES), 1)
    return jnp.where((r >> 6) == (c >> 6), 1.0 / HEAD_DIM, 0.0).astype(BF16)


def _head_rmsnorm(xcol, w, gmat):
    ms = jnp.dot((xcol * xcol).astype(BF16), gmat, preferred_element_type=F32)
    return xcol * lax.rsqrt(ms + EPS) * w


def _attn_prompt_kernel(sink_ref, q_ref, kc_ref, kp_ref, vc_ref, vp_ref, qw_ref, kw_ref,
                        o_ref, kn_ref, vn_ref):
    c = pl.program_id(1)
    blk = WINDOW
    gmat = _head_group_matrix()
    lane = lax.broadcasted_iota(jnp.int32, (1, LANES), 1)
    low = lane < HEAD_DIM

    kw = kw_ref[...]
    qw = qw_ref[...]
    kc = kc_ref[...].astype(F32)
    kp = kp_ref[...].astype(F32)
    vc = vc_ref[...].astype(F32)
    vp = vp_ref[...].astype(F32)
    ncol = kc.shape[1] // LANES
    kcn = [_head_rmsnorm(kc[:, p * LANES:(p + 1) * LANES], kw, gmat) for p in range(ncol)]
    kpn = [_head_rmsnorm(kp[:, p * LANES:(p + 1) * LANES], kw, gmat) for p in range(ncol)]
    kn_ref[0] = jnp.concatenate(kcn, axis=1)
    vn_ref[0] = vc

    kcat = [jnp.concatenate([kpn[p], kcn[p]], axis=0) for p in range(ncol)]
    vcat = [jnp.concatenate([vp[:, p * LANES:(p + 1) * LANES], vc[:, p * LANES:(p + 1) * LANES]], axis=0)
            for p in range(ncol)]
    krol = [pltpu.roll(kcat[p], HEAD_DIM, 1) for p in range(ncol)]
    vrol = [pltpu.roll(vcat[p], HEAD_DIM, 1) for p in range(ncol)]

    qi = lax.broadcasted_iota(jnp.int32, (blk, 2 * blk), 0)
    kj = lax.broadcasted_iota(jnp.int32, (blk, 2 * blk), 1)
    dist = blk + qi - kj
    valid = (dist >= 0) & (dist <= WINDOW) & ((kj >= blk) | (c > 0))
    distf = dist.astype(F32)

    q = q_ref[...].astype(F32)
    qn = [_head_rmsnorm(q[:, p * LANES:(p + 1) * LANES], qw, gmat)
          for p in range(N_HEADS // HEADS_PER_VREG)]

    for g in range(N_KV_HEADS):
        p, half = divmod(g, HEADS_PER_VREG)
        if half == 0:
            kd = jnp.where(low, kcat[p], krol[p])
            vd = jnp.where(low, vcat[p], vrol[p])
        else:
            kd = jnp.where(low, krol[p], kcat[p])
            vd = jnp.where(low, vrol[p], vcat[p])
        kd = kd.astype(BF16)
        vd = vd.astype(BF16)
        qs = []
        for hl in range(GQA_GROUP):
            h = g * GQA_GROUP + hl
            pc, hh = divmod(h, HEADS_PER_VREG)
            keep = low if hh == 0 else jnp.logical_not(low)
            qs.append(jnp.where(keep, qn[pc], 0.0).astype(BF16))
        qstack = jnp.concatenate(qs, axis=0)
        s_all = lax.dot_general(qstack, kd, (((1,), (1,)), ((), ())), preferred_element_type=F32)
        ps = []
        for hl in range(GQA_GROUP):
            h = g * GQA_GROUP + hl
            s = s_all[hl * blk:(hl + 1) * blk] - _alibi_slope(h) * distf
            s = jnp.where(valid, s, MASK_VALUE)
            sink = sink_ref[h]
            mx = jnp.maximum(jnp.max(s, axis=-1, keepdims=True), sink)
            e = jnp.exp(s - mx)
            den = jnp.sum(e, axis=-1, keepdims=True) + jnp.exp(sink - mx)
            ps.append((e / den).astype(BF16))
        pstack = jnp.concatenate(ps, axis=0)
        o_all = jnp.dot(pstack, vd, preferred_element_type=F32)
        for pair in range(GQA_GROUP // HEADS_PER_VREG):
            pc = g * (GQA_GROUP // HEADS_PER_VREG) + pair
            even = o_all[(2 * pair) * blk:(2 * pair + 1) * blk]
            odd = o_all[(2 * pair + 1) * blk:(2 * pair + 2) * blk]
            o_ref[:, pc * LANES:(pc + 1) * LANES] = jnp.where(low, even, odd).astype(o_ref.dtype)


def _attn_prompt(proj, qw2, kw2, sinks, *, batch, seq):
    nb = seq // WINDOW
    attn_w = N_HEADS * HEAD_DIM
    kv_w = N_KV_HEADS * HEAD_DIM
    kcol = attn_w // kv_w
    vcol = kcol + 1

    def cur(col):
        return lambda b, c: (b * nb + c, col)

    def prev(col):
        return lambda b, c: (jnp.maximum(b * nb + c - 1, 0), col)

    return pl.pallas_call(
        _attn_prompt_kernel,
        grid=(batch, nb),
        in_specs=[
            pl.BlockSpec(memory_space=pltpu.SMEM),
            pl.BlockSpec((WINDOW, attn_w), cur(0)),
            pl.BlockSpec((WINDOW, kv_w), cur(kcol)),
            pl.BlockSpec((WINDOW, kv_w), prev(kcol)),
            pl.BlockSpec((WINDOW, kv_w), cur(vcol)),
            pl.BlockSpec((WINDOW, kv_w), prev(vcol)),
            pl.BlockSpec((1, LANES), lambda b, c: (0, 0)),
            pl.BlockSpec((1, LANES), lambda b, c: (0, 0)),
        ],
        out_specs=[
            pl.BlockSpec((WINDOW, attn_w), lambda b, c: (b * nb + c, 0)),
            pl.BlockSpec((1, WINDOW, kv_w), lambda b, c: (b, 0, 0)),
            pl.BlockSpec((1, WINDOW, kv_w), lambda b, c: (b, 0, 0)),
        ],
        out_shape=[
            jax.ShapeDtypeStruct((batch * seq, attn_w), BF16),
            jax.ShapeDtypeStruct((batch, WINDOW, kv_w), F32),
            jax.ShapeDtypeStruct((batch, WINDOW, kv_w), F32),
        ],
        compiler_params=_params(("parallel", "arbitrary"), 32),
        name="attn_prompt",
    )(sinks, proj, proj, proj, proj, proj, qw2, kw2)


def _layernorm_rows(x, w):
    mu = jnp.mean(x, axis=-1, keepdims=True)
    xc = x - mu
    return xc * lax.rsqrt(jnp.mean(xc * xc, axis=-1, keepdims=True) + EPS) * w


GELU_C = 0.7978845608028654
GELU_A = 0.044715


def _gelu_tanh(x):
    return x * jax.nn.sigmoid((2.0 * GELU_C) * (x + GELU_A * (x * x * x)))


def _sgu_prompt_kernel(gu0_ref, gu1_ref, gv0_ref, gv1_ref, nw_ref, ws_ref, bs_ref, o_ref, st_ref):
    rows = gu0_ref.shape[0]
    gv = jnp.concatenate([gv0_ref[...], gv1_ref[...]], axis=1).astype(F32)
    vn = _layernorm_rows(_gelu_tanh(gv), nw_ref[...])
    st_ref[0] = vn[rows - CHUNK:]
    vb = vn.astype(BF16)
    r = lax.broadcasted_iota(jnp.int32, (CHUNK, CHUNK), 0)
    c = lax.broadcasted_iota(jnp.int32, (CHUNK, CHUNK), 1)
    causal = r >= c
    gw = vb.shape[1] // GM_GROUPS
    half = gu0_ref.shape[1]
    for g in range(GM_GROUPS):
        w = jnp.where(causal, ws_ref[g], 0.0).astype(BF16)
        src = gu0_ref if g * gw < half else gu1_ref
        off = g * gw - (0 if g * gw < half else half)
        for ch in range(rows // CHUNK):
            rs = slice(ch * CHUNK, (ch + 1) * CHUNK)
            mix = jnp.dot(w, vb[rs, g * gw:(g + 1) * gw], preferred_element_type=F32) + bs_ref[:, g:g + 1]
            u = _gelu_tanh(src[rs, off:off + gw].astype(F32))
            o_ref[rs, g * gw:(g + 1) * gw] = (u * mix).astype(o_ref.dtype)


def _sgu_prompt(proj, nw, ws, bs_t, *, batch, seq, gu_off, gm_w, chunks_per_step):
    rows = chunks_per_step * CHUNK
    nc = seq // rows
    half = gm_w // 2
    b0 = gu_off // half

    def col(k):
        return lambda b, c: (b * nc + c, b0 + k)

    return pl.pallas_call(
        _sgu_prompt_kernel,
        grid=(batch, nc),
        in_specs=[
            pl.BlockSpec((rows, half), col(0)),
            pl.BlockSpec((rows, half), col(1)),
            pl.BlockSpec((rows, half), col(2)),
            pl.BlockSpec((rows, half), col(3)),
            pl.BlockSpec((1, gm_w), lambda b, c: (0, 0)),
            pl.BlockSpec((GM_GROUPS, CHUNK, CHUNK), lambda b, c: (0, 0, 0)),
            pl.BlockSpec((CHUNK, GM_GROUPS), lambda b, c: (0, 0)),
        ],
        out_specs=[
            pl.BlockSpec((rows, gm_w), lambda b, c: (b * nc + c, 0)),
            pl.BlockSpec((1, CHUNK, gm_w), lambda b, c: (b, 0, 0)),
        ],
        out_shape=[
            jax.ShapeDtypeStruct((batch * seq, gm_w), BF16),
            jax.ShapeDtypeStruct((batch, CHUNK, gm_w), F32),
        ],
        compiler_params=_params(("parallel", "arbitrary"), 32),
        name="sgu_prompt",
    )(proj, proj, proj, proj, nw.reshape(1, gm_w), ws, bs_t)


N_ATTN_IN, N_ATTN_OUT, N_SGU_IN, N_SGU_OUT, N_CAST = 8, 3, 7, 2, 3


def _mixers_prompt_kernel(*refs):
    attn_in = refs[:N_ATTN_IN]
    sgu_in = refs[N_ATTN_IN:N_ATTN_IN + N_SGU_IN]
    cast_in = refs[N_ATTN_IN + N_SGU_IN:N_ATTN_IN + N_SGU_IN + N_CAST]
    outs = refs[N_ATTN_IN + N_SGU_IN + N_CAST:]
    attn_out = outs[:N_ATTN_OUT]
    sgu_out = outs[N_ATTN_OUT:N_ATTN_OUT + N_SGU_OUT]
    wup_ref, wdn_ref, wpg_ref = cast_in
    wupb_ref, wdnb_ref, wpgb_ref = outs[N_ATTN_OUT + N_SGU_OUT:]
    tile = wupb_ref.shape[2]
    for c in range(wupb_ref.shape[0]):
        wupb_ref[c] = wup_ref[:, c * tile:(c + 1) * tile].astype(BF16)
    wdnb_ref[...] = wdn_ref[...].astype(BF16)
    wpgb_ref[...] = wpg_ref[...].astype(BF16)
    _sgu_prompt_kernel(*sgu_in, *sgu_out)
    _attn_prompt_kernel(*attn_in, *attn_out)


def _mixers_prompt(proj, qw2, kw2, sinks, nw, ws, bs_t, w_up, w_down, w_pg, layer, *,
                   batch, seq, gu_off, gm_w, tile):
    nb = seq // WINDOW
    steps = batch * nb
    flat = lambda b, c: b * nb + c
    _, k_up, n_up = w_up.shape

    def slab_in(a, every):
        return pl.BlockSpec((None, a.shape[1] * every // steps, a.shape[2]),
                            lambda b, c: (layer, flat(b, c) // every, 0))

    def slab_out(a, every):
        return pl.BlockSpec((a.shape[1] * every // steps, a.shape[2]), lambda b, c: (flat(b, c) // every, 0))
    attn_w = N_HEADS * HEAD_DIM
    kv_w = N_KV_HEADS * HEAD_DIM
    kcol = attn_w // kv_w
    vcol = kcol + 1
    half = gm_w // 2
    b0 = gu_off // half
    cur = lambda col: (lambda b, c: (b * nb + c, col))
    prev = lambda col: (lambda b, c: (jnp.maximum(b * nb + c - 1, 0), col))
    const = lambda shape: pl.BlockSpec(shape, lambda b, c: (0,) * len(shape))
    state = lambda width: pl.BlockSpec((1, WINDOW, width), lambda b, c: (b, 0, 0))
    return pl.pallas_call(
        _mixers_prompt_kernel,
        grid=(batch, nb),
        in_specs=[
            pl.BlockSpec(memory_space=pltpu.SMEM),
            pl.BlockSpec((WINDOW, attn_w), cur(0)),
            pl.BlockSpec((WINDOW, kv_w), cur(kcol)),
            pl.BlockSpec((WINDOW, kv_w), prev(kcol)),
            pl.BlockSpec((WINDOW, kv_w), cur(vcol)),
            pl.BlockSpec((WINDOW, kv_w), prev(vcol)),
            const((1, LANES)), const((1, LANES)),
            pl.BlockSpec((CHUNK, half), cur(b0)),
            pl.BlockSpec((CHUNK, half), cur(b0 + 1)),
            pl.BlockSpec((CHUNK, half), cur(b0 + 2)),
            pl.BlockSpec((CHUNK, half), cur(b0 + 3)),
            const((1, gm_w)), const((GM_GROUPS, CHUNK, CHUNK)), const((CHUNK, GM_GROUPS)),
            slab_in(w_up, 1), slab_in(w_down, 2), slab_in(w_pg, 1),
        ],
        out_specs=[
            pl.BlockSpec((WINDOW, attn_w), cur(0)), state(kv_w), state(kv_w),
            pl.BlockSpec((CHUNK, gm_w), cur(0)), state(gm_w),
            pl.BlockSpec((n_up // tile, k_up // steps, tile), lambda b, c: (0, flat(b, c), 0)),
            slab_out(w_down, 2), slab_out(w_pg, 1),
        ],
        out_shape=[
            jax.ShapeDtypeStruct((batch * seq, attn_w), BF16),
            jax.ShapeDtypeStruct((batch, WINDOW, kv_w), F32),
            jax.ShapeDtypeStruct((batch, WINDOW, kv_w), F32),
            jax.ShapeDtypeStruct((batch * seq, gm_w), BF16),
            jax.ShapeDtypeStruct((batch, CHUNK, gm_w), F32),
            jax.ShapeDtypeStruct((n_up // tile, k_up, tile), BF16),
            jax.ShapeDtypeStruct(w_down.shape[1:], BF16),
            jax.ShapeDtypeStruct(w_pg.shape[1:], BF16),
        ],
        compiler_params=_params(("arbitrary", "arbitrary"), 32),
        name="mixers_prompt",
    )(sinks, proj, proj, proj, proj, proj, qw2, kw2, proj, proj, proj, proj, nw.reshape(1, gm_w), ws, bs_t,
      w_up, w_down, w_pg)


def _split_dot(x, m):
    hi = x.astype(BF16)
    lo = (x - hi.astype(F32)).astype(BF16)
    return jnp.dot(hi, m, preferred_element_type=F32) + jnp.dot(lo, m, preferred_element_type=F32)


def _attn_sample_kernel(q_ref, kn_ref, vn_ref, kb_ref, vb_ref, qw_ref, kw_ref, sink_ref,
                        wa_ref, wg_ref, wo_ref, o_ref, ko_ref, vo_ref, wab_ref, wgb_ref, wob_ref):
    wab_ref[...] = wa_ref[...].astype(BF16)
    wgb_ref[...] = wg_ref[...].astype(BF16)
    wob_ref[...] = wo_ref[...].astype(BF16)
    nsamp = q_ref.shape[0]
    attn_w = q_ref.shape[1]
    kv_w = kn_ref.shape[1]
    gmat = _head_group_matrix()
    q = q_ref[...].astype(F32)
    kn = kn_ref[...].astype(F32)
    qn = jnp.concatenate(
        [_head_rmsnorm(q[:, p * LANES:(p + 1) * LANES], qw_ref[...], gmat)
         for p in range(attn_w // LANES)], axis=1)
    knn = jnp.concatenate(
        [_head_rmsnorm(kn[:, p * LANES:(p + 1) * LANES], kw_ref[...], gmat)
         for p in range(kv_w // LANES)], axis=1)
    vnn = vn_ref[...].astype(F32)

    ec = lax.broadcasted_iota(jnp.int32, (kv_w, attn_w), 0)
    el = lax.broadcasted_iota(jnp.int32, (kv_w, attn_w), 1)
    expand = (((ec >> 6) == (el >> 8)) & ((ec & 63) == (el & 63))).astype(BF16)
    gl = lax.broadcasted_iota(jnp.int32, (attn_w, LANES), 0)
    gh = lax.broadcasted_iota(jnp.int32, (attn_w, LANES), 1)
    hsum = ((gl >> 6) == gh).astype(BF16)
    tl = lax.broadcasted_iota(jnp.int32, (LANES, attn_w), 1)
    th = lax.broadcasted_iota(jnp.int32, (LANES, attn_w), 0)
    hexp = ((tl >> 6) == th).astype(BF16)

    nkeys = WINDOW + SUBLANES
    row = lax.broadcasted_iota(jnp.int32, (nkeys, LANES), 0)
    head = lax.broadcasted_iota(jnp.int32, (nkeys, LANES), 1)
    slope = jnp.exp2(-8.0 * (head + 1).astype(F32) / N_HEADS)
    dist = (WINDOW - row).astype(F32)
    key_ok = row <= WINDOW
    srow = lax.broadcasted_iota(jnp.int32, (WINDOW, kv_w), 0)
    sinks = sink_ref[...]

    for s in range(nsamp):
        kb = kb_ref[s]
        vb = vb_ref[s]
        knew = knn[s:s + 1]
        vnew = vnn[s:s + 1]
        ko_ref[s] = jnp.where(srow == WINDOW - 1, knew, pltpu.roll(kb, WINDOW - 1, 0))
        vo_ref[s] = jnp.where(srow == WINDOW - 1, vnew, pltpu.roll(vb, WINDOW - 1, 0))
        kk = jnp.concatenate([kb, jnp.broadcast_to(knew, (SUBLANES, kv_w))], axis=0)
        vv = jnp.concatenate([vb, jnp.broadcast_to(vnew, (SUBLANES, kv_w))], axis=0)
        kexp = jnp.dot(kk.astype(BF16), expand, preferred_element_type=F32)
        vexp = jnp.dot(vv.astype(BF16), expand, preferred_element_type=F32)
        sc = _split_dot(kexp * qn[s:s + 1], hsum)
        sc = sc - slope * dist
        sc = jnp.where(key_ok, sc, MASK_VALUE)
        mx = jnp.maximum(jnp.max(sc, axis=0, keepdims=True), sinks)
        e = jnp.exp(sc - mx)
        den = jnp.sum(e, axis=0, keepdims=True) + jnp.exp(sinks - mx)
        pexp = jnp.dot((e / den).astype(BF16), hexp, preferred_element_type=F32)
        o_ref[s:s + 1, :] = jnp.sum(pexp * vexp, axis=0, keepdims=True)


def _attn_sample(q, knew, vnew, kbuf, vbuf, qw2, kw2, sinks_row, wa, wg, wo, *, group):
    nb, attn_w = q.shape
    kv_w = knew.shape[1]
    steps = nb // group
    row = lambda width: pl.BlockSpec((group, width), lambda i: (i, 0))
    buf = pl.BlockSpec((group, WINDOW, kv_w), lambda i: (i, 0, 0))
    vec = pl.BlockSpec((1, LANES), lambda i: (0, 0))
    slab = lambda w: pl.BlockSpec((w.shape[0] // steps, w.shape[1]), lambda i: (i, 0))
    return pl.pallas_call(
        _attn_sample_kernel,
        grid=(steps,),
        in_specs=[row(attn_w), row(kv_w), row(kv_w), buf, buf, vec, vec, vec, slab(wa), slab(wg), slab(wo)],
        out_specs=[row(attn_w), buf, buf, slab(wa), slab(wg), slab(wo)],
        out_shape=[
            jax.ShapeDtypeStruct((nb, attn_w), F32),
            jax.ShapeDtypeStruct((nb, WINDOW, kv_w), F32),
            jax.ShapeDtypeStruct((nb, WINDOW, kv_w), F32),
            jax.ShapeDtypeStruct(wa.shape, BF16),
            jax.ShapeDtypeStruct(wg.shape, BF16),
            jax.ShapeDtypeStruct(wo.shape, BF16),
        ],
        compiler_params=_params(("parallel",), 48),
        name="attn_sample",
    )(q, knew, vnew, kbuf, vbuf, qw2, kw2, sinks_row, wa, wg, wo)


def _sgu_sample_kernel(gu_ref, gv_ref, nw_ref, w0_ref, b0_ref, o_ref, st_ref):
    vn = _layernorm_rows(jax.nn.gelu(gv_ref[...].astype(F32)), nw_ref[...])
    st_ref[...] = vn
    mix = w0_ref[...] * vn + b0_ref[...]
    o_ref[...] = jax.nn.gelu(gu_ref[...].astype(F32)) * mix


def _sgu_sample(gu, gv, nw, w0_row, b0_row):
    nb, gm_w = gu.shape
    full = pl.BlockSpec((nb, gm_w), lambda i: (0, 0))
    vec = pl.BlockSpec((1, gm_w), lambda i: (0, 0))
    return pl.pallas_call(
        _sgu_sample_kernel,
        grid=(1,),
        in_specs=[full, full, vec, vec, vec],
        out_specs=[full, full],
        out_shape=[jax.ShapeDtypeStruct((nb, gm_w), F32), jax.ShapeDtypeStruct((nb, gm_w), F32)],
        name="sgu_sample",
    )(gu, gv, nw.reshape(1, gm_w), w0_row, b0_row)


def _merge_out_kernel(a_ref, s_ref, ga_ref, gb_ref, x_ref, wa_ref, wg_ref, wo_ref, nw_ref,
                      x1_ref, xn1_ref):
    a = jnp.dot(a_ref[...].astype(BF16), wa_ref[...], preferred_element_type=F32)
    m = jnp.dot(s_ref[...].astype(BF16), wg_ref[...], preferred_element_type=F32)
    merged = (gb_ref[...].astype(F32) * a + ga_ref[...].astype(F32) * m).astype(BF16)
    x1 = x_ref[...] + jnp.dot(merged, wo_ref[...], preferred_element_type=F32)
    x1_ref[...] = x1
    xn1_ref[...] = _rmsnorm_rows(x1, nw_ref[...]).astype(BF16)


def _merge_out(attn, sgu, gates, x, wa, wg, wo, nw, *, tm):
    m, ka = attn.shape
    kg = sgu.shape[1]
    d = wo.shape[1]
    rows = lambda width, col=0: pl.BlockSpec((tm, width), lambda i: (i, col))
    resident = lambda shape: pl.BlockSpec(shape, lambda i: (0, 0), pipeline_mode=pl.Buffered(1))
    return pl.pallas_call(
        _merge_out_kernel,
        grid=(m // tm,),
        in_specs=[
            rows(ka), rows(kg), rows(d, 0), rows(d, 1), rows(d),
            resident((ka, d)), resident((kg, d)), resident((d, d)),
            pl.BlockSpec((1, d), lambda i: (0, 0)),
        ],
        out_specs=[rows(d), rows(d)],
        out_shape=[jax.ShapeDtypeStruct((m, d), F32), jax.ShapeDtypeStruct((m, d), BF16)],
        compiler_params=_params(("parallel",), 48),
        name="merge_out",
    )(attn, sgu, gates, gates, x, wa, wg, wo, nw.reshape(1, d))


PAD = SUBLANES
FFN_ROW_BLOCKS = 4


def _ffn_step(t, xn_ref, wug_ref, wuu_ref, wd_ref, cwg_ref, cwu_ref, cbg_ref, cbu_ref,
              csg_ref, csu_ref, h_new, h_old, carry_ref, acc_ref, *, nf, tiles_per_seq):
    tm = xn_ref.shape[0]
    tf = wug_ref.shape[1]
    rb = tm // FFN_ROW_BLOCKS
    tp = jnp.maximum(t - 1, 0)
    ip = tp // nf
    jp = tp % nf
    first = (ip % tiles_per_seq) == 0
    h_old[0:PAD, :] = jnp.where(first, 0.0, carry_ref[jp])
    carry_ref[jp] = h_old[tm:tm + PAD, :]
    cw = jnp.concatenate([cwg_ref[...], cwu_ref[...]], axis=1)
    cb = jnp.concatenate([cbg_ref[...], cbu_ref[...]], axis=1)
    wug = wug_ref[...]
    wuu = wuu_ref[...]
    wd = wd_ref[...]
    for r in range(FFN_ROW_BLOCKS):
        lo = r * rb
        xr = xn_ref[lo:lo + rb, :]
        h_new[PAD + lo:PAD + lo + rb, 0:tf] = jnp.dot(xr, wug, preferred_element_type=F32)
        h_new[PAD + lo:PAD + lo + rb, tf:2 * tf] = jnp.dot(xr, wuu, preferred_element_type=F32)
        hp = h_old[lo:lo + rb + PAD, :]
        hc = (cw[0:1] * pltpu.roll(hp, 2, 0)[PAD:] + cw[1:2] * pltpu.roll(hp, 1, 0)[PAD:]
              + cw[2:3] * hp[PAD:] + cb)
        act = (jax.nn.silu(hc[:, 0:tf]) * hc[:, tf:2 * tf]).astype(BF16)
        acc_ref[lo:lo + rb, :] += jnp.dot(act, wd, preferred_element_type=F32)
    csg_ref[0] = h_old[PAD + tm - 2:PAD + tm, 0:tf]
    csu_ref[0] = h_old[PAD + tm - 2:PAD + tm, tf:2 * tf]


def _ffn_kernel(xn_ref, wug_ref, wuu_ref, wd_ref, cwg_ref, cwu_ref, cbg_ref, cbu_ref,
                o_ref, csg_ref, csu_ref, ha_ref, hb_ref, carry_ref, acc_ref, *, nf, tiles_per_seq):
    t = pl.program_id(0)
    step = functools.partial(_ffn_step, t, xn_ref, wug_ref, wuu_ref, wd_ref, cwg_ref, cwu_ref,
                             cbg_ref, cbu_ref, csg_ref, csu_ref, carry_ref=carry_ref, acc_ref=acc_ref,
                             nf=nf, tiles_per_seq=tiles_per_seq)

    @pl.when(t == 0)
    def _():
        hb_ref[...] = jnp.zeros_like(hb_ref)
        carry_ref[...] = jnp.zeros_like(carry_ref)

    @pl.when(jnp.maximum(t - 1, 0) % nf == 0)
    def _():
        acc_ref[...] = jnp.zeros_like(acc_ref)

    @pl.when(t % 2 == 0)
    def _():
        step(h_new=ha_ref, h_old=hb_ref)

    @pl.when(t % 2 == 1)
    def _():
        step(h_new=hb_ref, h_old=ha_ref)

    @pl.when(jnp.maximum(t - 1, 0) % nf == nf - 1)
    def _():
        o_ref[...] = acc_ref[...].astype(o_ref.dtype)


def _ffn_prompt(xn, w_up, w_down, cw, cb, *, batch, seq, tm):
    m, k = xn.shape
    d_ff, d = w_down.shape
    tf = w_up.shape[2]
    nf = d_ff // tf
    nm = m // tm
    tiles_per_seq = seq // tm
    prev = lambda t: jnp.maximum(t - 1, 0)
    up_tile = lambda off: (lambda t: (off + t % nf, 0, 0))
    dn_col = lambda off: (lambda t: (0, off + prev(t) % nf))
    state = lambda t: (prev(t) // nf, 0, prev(t) % nf)
    delta, csg, csu = pl.pallas_call(
        functools.partial(_ffn_kernel, nf=nf, tiles_per_seq=tiles_per_seq),
        grid=(nm * nf + 1,),
        in_specs=[
            pl.BlockSpec((tm, k), lambda t: (jnp.minimum(t // nf, nm - 1), 0)),
            pl.BlockSpec((None, k, tf), up_tile(0)),
            pl.BlockSpec((None, k, tf), up_tile(nf)),
            pl.BlockSpec((tf, d), lambda t: (prev(t) % nf, 0)),
            pl.BlockSpec((3, tf), dn_col(0)),
            pl.BlockSpec((3, tf), dn_col(nf)),
            pl.BlockSpec((1, tf), dn_col(0)),
            pl.BlockSpec((1, tf), dn_col(nf)),
        ],
        out_specs=[
            pl.BlockSpec((tm, d), lambda t: (prev(t) // nf, 0)),
            pl.BlockSpec((1, 2, tf), state),
            pl.BlockSpec((1, 2, tf), state),
        ],
        out_shape=[
            jax.ShapeDtypeStruct((m, d), BF16),
            jax.ShapeDtypeStruct((nm, 2, d_ff), F32),
            jax.ShapeDtypeStruct((nm, 2, d_ff), F32),
        ],
        scratch_shapes=[
            pltpu.VMEM((tm + PAD, 2 * tf), F32),
            pltpu.VMEM((tm + PAD, 2 * tf), F32),
            pltpu.VMEM((nf, PAD, 2 * tf), F32),
            pltpu.VMEM((tm, d), F32),
        ],
        compiler_params=_params(("arbitrary",), 58),
        name="ffn_prompt",
    )(xn, w_up, w_up, w_down, cw, cw, cb.reshape(1, -1), cb.reshape(1, -1))
    tails = jnp.concatenate([csg, csu], axis=-1).reshape(batch, tiles_per_seq, 2, 2 * d_ff)
    return delta, tails[:, -1]


def _ffn_down_sample_kernel(hg_ref, hu_ref, b0g_ref, b0u_ref, b1g_ref, b1u_ref, cwg_ref, cwu_ref,
                            cbg_ref, cbu_ref, wd_ref, x_ref, o_ref):
    @pl.when(pl.program_id(0) == 0)
    def _():
        o_ref[...] = x_ref[...]

    def conv(h_ref, b0_ref, b1_ref, cw_ref, cb_ref):
        cw = cw_ref[...]
        return cw[0:1] * b0_ref[...] + cw[1:2] * b1_ref[...] + cw[2:3] * h_ref[...] + cb_ref[...]

    hcg = conv(hg_ref, b0g_ref, b1g_ref, cwg_ref, cbg_ref)
    hcu = conv(hu_ref, b0u_ref, b1u_ref, cwu_ref, cbu_ref)
    act = (jax.nn.silu(hcg) * hcu).astype(BF16)
    o_ref[...] += jnp.dot(act, wd_ref[...], preferred_element_type=F32)


def _ffn_down_sample(h, state2, cw, cb, wd, x, *, tf):
    nb, two_f = h.shape
    d_ff = two_f // 2
    nf = d_ff // tf
    d = wd.shape[1]
    colblk = lambda off: pl.BlockSpec((nb, tf), lambda j: (0, j + off))
    return pl.pallas_call(
        _ffn_down_sample_kernel,
        grid=(nf,),
        in_specs=[
            colblk(0), colblk(nf),
            colblk(0), colblk(nf), colblk(2 * nf), colblk(3 * nf),
            pl.BlockSpec((3, tf), lambda j: (0, j)),
            pl.BlockSpec((3, tf), lambda j: (0, j + nf)),
            pl.BlockSpec((1, tf), lambda j: (0, j)),
            pl.BlockSpec((1, tf), lambda j: (0, j + nf)),
            pl.BlockSpec((tf, d), lambda j: (j, 0)),
            pl.BlockSpec((nb, d), lambda j: (0, 0)),
        ],
        out_specs=pl.BlockSpec((nb, d), lambda j: (0, 0)),
        out_shape=jax.ShapeDtypeStruct((nb, d), F32),
        compiler_params=_params(("arbitrary",), 32),
        name="ffn_down_sample",
    )(h, h, state2, state2, state2, state2, cw, cw, cb.reshape(1, two_f), cb.reshape(1, two_f), wd, x)


def _ple_sample_kernel(x_ref, nw_ref, w_ref, p_ref, wp_ref, o_ref, wpb_ref):
    x2 = x_ref[...]
    xn = _rmsnorm_rows(x2, nw_ref[...]).astype(BF16)
    wpb = wp_ref[...].astype(BF16)
    wpb_ref[...] = wpb
    gate = jax.nn.sigmoid(jnp.dot(xn, w_ref[...], preferred_element_type=F32))
    emb = jnp.dot(p_ref[...].astype(BF16), wpb, preferred_element_type=F32)
    o_ref[...] = x2 + gate * emb


def _ple_sample(x, nw, w, p, wp):
    m, d = x.shape
    kp = p.shape[1]
    full = lambda shape: pl.BlockSpec(shape, lambda i: (0, 0))
    return pl.pallas_call(
        _ple_sample_kernel,
        grid=(1,),
        in_specs=[full((m, d)), full((1, d)), full((d, d)), full((m, kp)), full((kp, d))],
        out_specs=[full((m, d)), full((kp, d))],
        out_shape=[jax.ShapeDtypeStruct((m, d), F32), jax.ShapeDtypeStruct((kp, d), BF16)],
        compiler_params=_params(("arbitrary",), 32),
        name="ple_sample",
    )(x, nw.reshape(1, d), w, p, wp)


def _ple_rows_kernel(x_ref, d_ref, nw_ref, w_ref, p_ref, wp_ref, o_ref):
    x2 = x_ref[...] + d_ref[...].astype(F32)
    xn = _rmsnorm_rows(x2, nw_ref[...]).astype(BF16)
    gate = jax.nn.sigmoid(jnp.dot(xn, w_ref[...], preferred_element_type=F32))
    emb = jnp.dot(p_ref[...].astype(BF16), wp_ref[...], preferred_element_type=F32)
    o_ref[...] = x2 + gate * emb


def _ple_rows(x, delta, nw, w, p, wp, *, tm):
    m, d = x.shape
    kp = p.shape[1]
    rows = lambda width: pl.BlockSpec((tm, width), lambda i: (i, 0))
    resident = lambda shape: pl.BlockSpec(shape, lambda i: (0, 0), pipeline_mode=pl.Buffered(1))
    return pl.pallas_call(
        _ple_rows_kernel,
        grid=(m // tm,),
        in_specs=[rows(d), rows(d), pl.BlockSpec((1, d), lambda i: (0, 0)), resident((d, d)),
                  rows(kp), resident((kp, d))],
        out_specs=rows(d),
        out_shape=jax.ShapeDtypeStruct((m, d), F32),
        compiler_params=_params(("parallel",), 56),
        name="ple_prompt",
    )(x, delta, nw.reshape(1, d), w, p, wp)


def _layer_weights(i, attn_norm_w, w_in, q_norm_w, k_norm_w, attn_sinks, sgu_norm_w, sgu_w, sgu_b,
                   w_br_attn, w_br_gm, w_gate, b_gate, w_out, ffn_norm_w, w_up, conv_w, conv_b,
                   w_down, ple_norm_w, w_ple_gate, w_ple_proj):
    sinks = attn_sinks[i]
    gm_group_w = w_br_gm.shape[1] // GM_GROUPS
    return dict(
        attn_norm_w=attn_norm_w[i], w_in=w_in[i],
        qw2=jnp.tile(q_norm_w[i] * (HEAD_DIM ** -0.5), HEADS_PER_VREG).reshape(1, LANES),
        kw2=jnp.tile(k_norm_w[i], HEADS_PER_VREG).reshape(1, LANES),
        sinks=sinks,
        sinks_row=jnp.pad(sinks, (0, LANES - N_HEADS)).reshape(1, LANES),
        sgu_norm_w=sgu_norm_w[i], sgu_w=sgu_w[i], sgu_b_t=jnp.transpose(sgu_b[i]),
        sgu_w0_row=jnp.repeat(sgu_w[i][:, 0, 0], gm_group_w).reshape(1, -1),
        sgu_b0_row=jnp.repeat(sgu_b[i][:, 0], gm_group_w).reshape(1, -1),
        w_br_attn=w_br_attn[i], w_br_gm=w_br_gm[i],
        w_gate=w_gate[i], b_gate=b_gate[i], w_out=w_out[i],
        ffn_norm_w=ffn_norm_w[i], conv_w=conv_w[i], conv_b=conv_b[i],
        ple_norm_w=ple_norm_w[i], w_ple_proj=w_ple_proj[i],
        layer=i, w_up_all=w_up, w_down_all=w_down, w_ple_gate_all=w_ple_gate,
    )


W_TILE = 512


def _prompt_mixers(x, lw, wb):
    batch, seq, d = x.shape
    x2d = x.reshape(batch * seq, d)
    attn_w = N_HEADS * HEAD_DIM
    kv_w = N_KV_HEADS * HEAD_DIM
    gm_w = lw["w_br_gm"].shape[0]
    proj, gates = _in_proj(x2d, lw["attn_norm_w"], wb["w_in"], wb["w_gate"], lw["b_gate"], tm=256)
    attn, k_new, v_new, sgu, sgu_state, wb["w_up"], wb["w_down"], wb["w_ple_gate"] = _mixers_prompt(
        proj, lw["qw2"], lw["kw2"], lw["sinks"], lw["sgu_norm_w"], lw["sgu_w"], lw["sgu_b_t"],
        lw["w_up_all"], lw["w_down_all"], lw["w_ple_gate_all"], lw["layer"],
        batch=batch, seq=seq, gu_off=attn_w + 2 * kv_w, gm_w=gm_w, tile=W_TILE)
    states = (k_new.reshape(batch, WINDOW, N_KV_HEADS, HEAD_DIM),
              v_new.reshape(batch, WINDOW, N_KV_HEADS, HEAD_DIM), sgu_state)
    return (attn, sgu, gates), states


def _prompt_tail(x, p, mixed, lw, wb):
    batch, seq, d = x.shape
    x2d = x.reshape(batch * seq, d)
    attn, sgu, gates = mixed
    x1, xn1 = _merge_out(attn, sgu, gates, x2d, wb["w_br_attn"], wb["w_br_gm"], wb["w_out"],
                         lw["ffn_norm_w"], tm=256)
    delta, conv_state = _ffn_prompt(xn1, wb["w_up"], wb["w_down"], lw["conv_w"], lw["conv_b"],
                                    batch=batch, seq=seq, tm=1024)
    x3 = _ple_rows(x1, delta, lw["ple_norm_w"], wb["w_ple_gate"], p.reshape(batch * seq, -1),
                   wb["w_ple_proj"], tm=512)
    return x3.reshape(batch, seq, d), conv_state


def _sample_mixers(x, k_buf, v_buf, lw, wb):
    nb, t, d = x.shape
    x2d = x.reshape(nb * t, d)
    attn_w = N_HEADS * HEAD_DIM
    kv_w = N_KV_HEADS * HEAD_DIM
    gm_w = lw["w_br_gm"].shape[0]
    proj, wb["w_in"] = _norm_mm(x2d, lw["attn_norm_w"], lw["w_in"], None, act=None,
                                tn=W_TILE, tile=W_TILE, name="proj_sample")
    gates, wb["w_gate"] = _norm_mm(x2d, lw["attn_norm_w"], lw["w_gate"], lw["b_gate"], act="sigmoid",
                                   tn=2 * W_TILE, tile=W_TILE, name="gates_sample")
    q = proj[:, :attn_w]
    k = proj[:, attn_w:attn_w + kv_w]
    v = proj[:, attn_w + kv_w:attn_w + 2 * kv_w]
    gu = proj[:, attn_w + 2 * kv_w:attn_w + 2 * kv_w + gm_w]
    gv = proj[:, attn_w + 2 * kv_w + gm_w:]
    attn, k_new, v_new, wb["w_br_attn"], wb["w_br_gm"], wb["w_out"] = _attn_sample(
        q, k, v, k_buf.reshape(nb, WINDOW, kv_w), v_buf.reshape(nb, WINDOW, kv_w),
        lw["qw2"], lw["kw2"], lw["sinks_row"], lw["w_br_attn"], lw["w_br_gm"], lw["w_out"], group=8)
    sgu, sgu_state = _sgu_sample(gu, gv, lw["sgu_norm_w"], lw["sgu_w0_row"], lw["sgu_b0_row"])
    x1, _ = _merge_out(attn, sgu, gates, x2d, wb["w_br_attn"], wb["w_br_gm"], wb["w_out"],
                       lw["ffn_norm_w"], tm=nb)
    states = (k_new.reshape(nb, WINDOW, N_KV_HEADS, HEAD_DIM), v_new.reshape(nb, WINDOW, N_KV_HEADS, HEAD_DIM),
              sgu_state.reshape(nb, t, gm_w))
    return x1, states


def _sample_tail(x1, p, conv_buf, lw, wb):
    nb = x1.shape[0]
    h = _norm_mm_tiles(x1, lw["ffn_norm_w"], wb["w_up"], tiles_per_step=2, name="ffn_up_sample")
    x2 = _ffn_down_sample(h, conv_buf.reshape(nb, -1), lw["conv_w"], lw["conv_b"], wb["w_down"], x1, tf=W_TILE)
    x3, wb["w_ple_proj"] = _ple_sample(x2, lw["ple_norm_w"], wb["w_ple_gate"], p.reshape(nb, -1),
                                       lw["w_ple_proj"])
    conv_state = jnp.stack([conv_buf[:, 1, :], h], axis=1)
    return x3, conv_state


def kernel(x_prompt, x_sample, p_prompt, p_sample, state_attn_k, state_attn_v, state_conv, attn_norm_w, w_in, q_norm_w, k_norm_w, attn_sinks, sgu_norm_w, sgu_w, sgu_b, w_br_attn, w_br_gm, w_gate, b_gate, w_out, ffn_norm_w, w_up, conv_w, conv_b, w_down, ple_norm_w, w_ple_gate, w_ple_proj):
    depth = w_in.shape[0]
    xp, xs = x_prompt, x_sample
    outs = [[] for _ in range(8)]
    for i in range(depth):
        lw = _layer_weights(i, attn_norm_w, w_in, q_norm_w, k_norm_w, attn_sinks, sgu_norm_w, sgu_w, sgu_b,
                            w_br_attn, w_br_gm, w_gate, b_gate, w_out, ffn_norm_w, w_up, conv_w, conv_b,
                            w_down, ple_norm_w, w_ple_gate, w_ple_proj)
        wb = {}
        xs1, (ks, vs, gs) = _sample_mixers(xs, state_attn_k[i], state_attn_v[i], lw, wb)
        mixed, (kp, vp, gp) = _prompt_mixers(xp, lw, wb)
        xs3, cs = _sample_tail(xs1, p_sample[i], state_conv[i], lw, wb)
        xs = xs3.reshape(xs.shape)
        xp, cp = _prompt_tail(xp, p_prompt[i], mixed, lw, wb)
        for lst, val in zip(outs, (kp, vp, ks, vs, gp, gs, cp, cs)):
            lst.append(val)
    return (xp, xs) + tuple(jnp.stack(lst) for lst in outs)
```

```python
import functools

import jax
import jax.numpy as jnp
from jax import lax
from jax.experimental import pallas as pl
from jax.experimental.pallas import tpu as pltpu

F32 = jnp.float32
BF16 = jnp.bfloat16

HEAD_DIM = 64
N_HEADS = 16
N_KV_HEADS = 4
GQA_GROUP = N_HEADS // N_KV_HEADS
WINDOW = 128
CHUNK = 128
GM_GROUPS = 4
EPS = 1e-6
MASK_VALUE = -1e30
LANES = 128
SUBLANES = 8
HEADS_PER_VREG = LANES // HEAD_DIM
MIB = 1 << 20


def _alibi_slope(h):
    return float(2.0 ** (-8.0 * (h + 1) / N_HEADS))


ROW_TILE = 256
PLE_ROW_TILE = 512
FFN_ROW_TILE = 1024
W_TILE = 512
SAMPLE_GROUP = 8
VMEM_MIB = dict(stream=56, in_proj=56, mixers=32, attn_sample=48, merge_out=48, ffn=58,
                ffn_sample=48, ple=56, small=32)


def _params(semantics, vmem_mib):
    return pltpu.CompilerParams(dimension_semantics=semantics, vmem_limit_bytes=vmem_mib * MIB)


def _rmsnorm_rows(x, w):
    ms = jnp.mean(x * x, axis=-1, keepdims=True)
    return x * lax.rsqrt(ms + EPS) * w


def _norm_mm_kernel(*refs, has_bias, act):
    if has_bias:
        x_ref, nw_ref, w_ref, b_ref, o_ref, wb_ref, xn_ref = refs
    else:
        x_ref, nw_ref, w_ref, o_ref, wb_ref, xn_ref = refs

    @pl.when(pl.program_id(0) == 0)
    def _():
        xn_ref[...] = _rmsnorm_rows(x_ref[...], nw_ref[...]).astype(BF16)

    wb = w_ref[...].astype(BF16)
    tile = wb_ref.shape[2]
    for c in range(wb_ref.shape[0]):
        wb_ref[c] = wb[:, c * tile:(c + 1) * tile]
    acc = jnp.dot(xn_ref[...], wb, preferred_element_type=F32)
    if has_bias:
        acc = acc + b_ref[...]
    if act == "sigmoid":
        acc = jax.nn.sigmoid(acc)
    o_ref[...] = acc.astype(o_ref.dtype)


def _norm_mm(x, nw, w, bias, *, act, tn, tile, name):
    m, k = x.shape
    n = w.shape[1]
    per_step = tn // tile
    in_specs = [
        pl.BlockSpec((m, k), lambda j: (0, 0)),
        pl.BlockSpec((1, k), lambda j: (0, 0)),
        pl.BlockSpec((k, tn), lambda j: (0, j)),
    ]
    args = [x, nw.reshape(1, k), w]
    if bias is not None:
        in_specs.append(pl.BlockSpec((1, tn), lambda j: (0, j)))
        args.append(bias.reshape(1, n))
    return pl.pallas_call(
        functools.partial(_norm_mm_kernel, has_bias=bias is not None, act=act),
        grid=(n // tn,),
        in_specs=in_specs,
        out_specs=[pl.BlockSpec((m, tn), lambda j: (0, j)),
                   pl.BlockSpec((per_step, k, tile), lambda j: (j, 0, 0))],
        out_shape=[jax.ShapeDtypeStruct((m, n), F32), jax.ShapeDtypeStruct((n // tile, k, tile), BF16)],
        scratch_shapes=[pltpu.VMEM((m, k), BF16)],
        compiler_params=_params(("arbitrary",), VMEM_MIB["stream"]),
        name=name,
    )(*args)


def _norm_mm_tiles_kernel(x_ref, nw_ref, w_ref, o_ref, xn_ref):
    @pl.when(pl.program_id(0) == 0)
    def _():
        xn_ref[...] = _rmsnorm_rows(x_ref[...], nw_ref[...]).astype(BF16)

    tile = w_ref.shape[2]
    for c in range(w_ref.shape[0]):
        o_ref[:, c * tile:(c + 1) * tile] = jnp.dot(xn_ref[...], w_ref[c], preferred_element_type=F32)


def _norm_mm_tiles(x, nw, wt, *, tiles_per_step, name):
    m, k = x.shape
    n_tiles, _, tile = wt.shape
    tn = tiles_per_step * tile
    return pl.pallas_call(
        _norm_mm_tiles_kernel,
        grid=(n_tiles // tiles_per_step,),
        in_specs=[
            pl.BlockSpec((m, k), lambda j: (0, 0)),
            pl.BlockSpec((1, k), lambda j: (0, 0)),
            pl.BlockSpec((tiles_per_step, k, tile), lambda j: (j, 0, 0)),
        ],
        out_specs=pl.BlockSpec((m, tn), lambda j: (0, j)),
        out_shape=jax.ShapeDtypeStruct((m, n_tiles * tile), F32),
        scratch_shapes=[pltpu.VMEM((m, k), BF16)],
        compiler_params=_params(("arbitrary",), VMEM_MIB["ffn_sample"]),
        name=name,
    )(x, nw.reshape(1, k), wt)


def _in_proj_kernel(x_ref, nw_ref, win_ref, wg_ref, bg_ref, proj_ref, gates_ref, xn_ref):
    xn_ref[...] = _rmsnorm_rows(x_ref[...], nw_ref[...]).astype(BF16)
    tn = win_ref.shape[2]
    for c in range(win_ref.shape[0]):
        acc = jnp.dot(xn_ref[...], win_ref[c], preferred_element_type=F32)
        proj_ref[:, c * tn:(c + 1) * tn] = acc.astype(proj_ref.dtype)
    for c in range(wg_ref.shape[0]):
        cols = slice(c * tn, (c + 1) * tn)
        acc = jnp.dot(xn_ref[...], wg_ref[c], preferred_element_type=F32)
        gates_ref[:, cols] = jax.nn.sigmoid(acc + bg_ref[:, cols]).astype(gates_ref.dtype)


def _in_proj(x, nw, w_in, w_gate, b_gate, *, tm):
    m, k = x.shape
    n_in, _, tn = w_in.shape
    n_g = w_gate.shape[0]
    rows = lambda width: pl.BlockSpec((tm, width), lambda i: (i, 0))
    resident = lambda a: pl.BlockSpec(a.shape, lambda i: (0,) * a.ndim, pipeline_mode=pl.Buffered(1))
    bg = b_gate.reshape(1, -1)
    return pl.pallas_call(
        _in_proj_kernel,
        grid=(m // tm,),
        in_specs=[rows(k), pl.BlockSpec((1, k), lambda i: (0, 0)), resident(w_in), resident(w_gate),
                  pl.BlockSpec(bg.shape, lambda i: (0, 0))],
        out_specs=[rows(n_in * tn), rows(n_g * tn)],
        out_shape=[
            jax.ShapeDtypeStruct((m, n_in * tn), BF16),
            jax.ShapeDtypeStruct((m, n_g * tn), BF16),
        ],
        scratch_shapes=[pltpu.VMEM((tm, k), BF16)],
        compiler_params=_params(("parallel",), VMEM_MIB["in_proj"]),
        name="in_proj",
    )(x, nw.reshape(1, k), w_in, w_gate, bg)


def _head_group_matrix():
    r = lax.broadcasted_iota(jnp.int32, (LANES, LANES), 0)
    c = lax.broadcasted_iota(jnp.int32, (LANES, LANES), 1)
    return jnp.where((r >> 6) == (c >> 6), 1.0 / HEAD_DIM, 0.0).astype(BF16)


def _head_rmsnorm(xcol, w, gmat):
    ms = jnp.dot((xcol * xcol).astype(BF16), gmat, preferred_element_type=F32)
    return xcol * lax.rsqrt(ms + EPS) * w


def _attn_prompt_kernel(sink_ref, q_ref, kc_ref, kp_ref, vc_ref, vp_ref, qw_ref, kw_ref,
                        o_ref, kn_ref, vn_ref):
    c = pl.program_id(1)
    blk = WINDOW
    nblk = q_ref.shape[0] // blk
    gmat = _head_group_matrix()
    lane = lax.broadcasted_iota(jnp.int32, (1, LANES), 1)
    low = lane < HEAD_DIM

    kw = kw_ref[...]
    qw = qw_ref[...]
    kall = jnp.concatenate([kp_ref[...], kc_ref[...]], axis=0).astype(F32)
    vall = jnp.concatenate([vp_ref[...], vc_ref[...]], axis=0).astype(F32)
    ncol = kall.shape[1] // LANES
    kcat = [_head_rmsnorm(kall[:, p * LANES:(p + 1) * LANES], kw, gmat) for p in range(ncol)]
    vcat = [vall[:, p * LANES:(p + 1) * LANES] for p in range(ncol)]
    kn_ref[0] = jnp.concatenate([kc_[nblk * blk:] for kc_ in kcat], axis=1)
    vn_ref[0] = vall[nblk * blk:]
    krol = [pltpu.roll(kcat[p], HEAD_DIM, 1) for p in range(ncol)]
    vrol = [pltpu.roll(vcat[p], HEAD_DIM, 1) for p in range(ncol)]

    qi = lax.broadcasted_iota(jnp.int32, (blk, 2 * blk), 0)
    kj = lax.broadcasted_iota(jnp.int32, (blk, 2 * blk), 1)
    dist = blk + qi - kj
    in_window = (dist >= 0) & (dist <= WINDOW)
    distf = dist.astype(F32)

    q = q_ref[...].astype(F32)
    qn = [_head_rmsnorm(q[:, p * LANES:(p + 1) * LANES], qw, gmat)
          for p in range(N_HEADS // HEADS_PER_VREG)]

    for g in range(N_KV_HEADS):
        p, half = divmod(g, HEADS_PER_VREG)
        if half == 0:
            kd_all = jnp.where(low, kcat[p], krol[p]).astype(BF16)
            vd_all = jnp.where(low, vcat[p], vrol[p]).astype(BF16)
        else:
            kd_all = jnp.where(low, krol[p], kcat[p]).astype(BF16)
            vd_all = jnp.where(low, vrol[p], vcat[p]).astype(BF16)
        for b in range(nblk):
            rows = slice(b * blk, (b + 1) * blk)
            kd = kd_all[b * blk:(b + 2) * blk]
            vd = vd_all[b * blk:(b + 2) * blk]
            valid = in_window if b > 0 else in_window & ((kj >= blk) | (c > 0))
            qs = []
            for hl in range(GQA_GROUP):
                h = g * GQA_GROUP + hl
                pc, hh = divmod(h, HEADS_PER_VREG)
                keep = low if hh == 0 else jnp.logical_not(low)
                qs.append(jnp.where(keep, qn[pc][rows], 0.0).astype(BF16))
            qstack = jnp.concatenate(qs, axis=0)
            s_all = lax.dot_general(qstack, kd, (((1,), (1,)), ((), ())), preferred_element_type=F32)
            ps = []
            for hl in range(GQA_GROUP):
                h = g * GQA_GROUP + hl
                s = s_all[hl * blk:(hl + 1) * blk] - _alibi_slope(h) * distf
                s = jnp.where(valid, s, MASK_VALUE)
                sink = sink_ref[h]
                mx = jnp.maximum(jnp.max(s, axis=-1, keepdims=True), sink)
                e = jnp.exp(s - mx)
                den = jnp.sum(e, axis=-1, keepdims=True) + jnp.exp(sink - mx)
                ps.append((e / den).astype(BF16))
            pstack = jnp.concatenate(ps, axis=0)
            o_all = jnp.dot(pstack, vd, preferred_element_type=F32)
            for pair in range(GQA_GROUP // HEADS_PER_VREG):
                pc = g * (GQA_GROUP // HEADS_PER_VREG) + pair
                even = o_all[(2 * pair) * blk:(2 * pair + 1) * blk]
                odd = o_all[(2 * pair + 1) * blk:(2 * pair + 2) * blk]
                o_ref[rows, pc * LANES:(pc + 1) * LANES] = jnp.where(low, even, odd).astype(o_ref.dtype)


def _layernorm_rows(x, w):
    mu = jnp.mean(x, axis=-1, keepdims=True)
    xc = x - mu
    return xc * lax.rsqrt(jnp.mean(xc * xc, axis=-1, keepdims=True) + EPS) * w


GELU_C = 0.7978845608028654
GELU_A = 0.044715


def _gelu_tanh(x):
    return x * jax.nn.sigmoid((2.0 * GELU_C) * (x + GELU_A * (x * x * x)))


def _sgu_prompt_kernel(gu0_ref, gu1_ref, gv0_ref, gv1_ref, nw_ref, ws_ref, bs_ref, o_ref, st_ref):
    rows = gu0_ref.shape[0]
    gv = jnp.concatenate([gv0_ref[...], gv1_ref[...]], axis=1).astype(F32)
    vn = _layernorm_rows(_gelu_tanh(gv), nw_ref[...])
    st_ref[0] = vn[rows - CHUNK:]
    vb = vn.astype(BF16)
    r = lax.broadcasted_iota(jnp.int32, (CHUNK, CHUNK), 0)
    c = lax.broadcasted_iota(jnp.int32, (CHUNK, CHUNK), 1)
    causal = r >= c
    gw = vb.shape[1] // GM_GROUPS
    half = gu0_ref.shape[1]
    for g in range(GM_GROUPS):
        w = jnp.where(causal, ws_ref[g], 0.0).astype(BF16)
        src = gu0_ref if g * gw < half else gu1_ref
        off = g * gw - (0 if g * gw < half else half)
        for ch in range(rows // CHUNK):
            rs = slice(ch * CHUNK, (ch + 1) * CHUNK)
            mix = jnp.dot(w, vb[rs, g * gw:(g + 1) * gw], preferred_element_type=F32) + bs_ref[:, g:g + 1]
            u = _gelu_tanh(src[rs, off:off + gw].astype(F32))
            o_ref[rs, g * gw:(g + 1) * gw] = (u * mix).astype(o_ref.dtype)


N_ATTN_IN, N_ATTN_OUT, N_SGU_IN, N_SGU_OUT, N_CAST = 8, 3, 7, 2, 3
MIXER_BLOCKS = 2


def _mixers_prompt_kernel(*refs):
    attn_in = refs[:N_ATTN_IN]
    sgu_in = refs[N_ATTN_IN:N_ATTN_IN + N_SGU_IN]
    cast_in = refs[N_ATTN_IN + N_SGU_IN:N_ATTN_IN + N_SGU_IN + N_CAST]
    outs = refs[N_ATTN_IN + N_SGU_IN + N_CAST:]
    attn_out = outs[:N_ATTN_OUT]
    sgu_out = outs[N_ATTN_OUT:N_ATTN_OUT + N_SGU_OUT]
    wup_ref, wdn_ref, wpg_ref = cast_in
    wupb_ref, wdnb_ref, wpgb_ref = outs[N_ATTN_OUT + N_SGU_OUT:]
    tile = wupb_ref.shape[2]
    for c in range(wupb_ref.shape[0]):
        wupb_ref[c] = wup_ref[:, c * tile:(c + 1) * tile].astype(BF16)
    wdnb_ref[...] = wdn_ref[...].astype(BF16)
    wpgb_ref[...] = wpg_ref[...].astype(BF16)
    _sgu_prompt_kernel(*sgu_in, *sgu_out)
    _attn_prompt_kernel(*attn_in, *attn_out)


def _mixers_prompt(proj, qw2, kw2, sinks, nw, ws, bs_t, w_up, w_down, w_pg, layer, *,
                   batch, seq, gu_off, gm_w, tile):
    rows = MIXER_BLOCKS * WINDOW
    nb = seq // rows
    steps = batch * nb
    flat = lambda b, c: b * nb + c
    _, k_up, n_up = w_up.shape
    slab_in = lambda a: pl.BlockSpec((None, a.shape[1] // steps, a.shape[2]), lambda b, c: (layer, flat(b, c), 0))
    slab_out = lambda a: pl.BlockSpec((a.shape[1] // steps, a.shape[2]), lambda b, c: (flat(b, c), 0))
    attn_w = N_HEADS * HEAD_DIM
    kv_w = N_KV_HEADS * HEAD_DIM
    kcol = attn_w // kv_w
    vcol = kcol + 1
    half = gm_w // 2
    b0 = gu_off // half
    cur = lambda col: (lambda b, c: (flat(b, c), col))
    prev = lambda col: (lambda b, c: (jnp.maximum(flat(b, c) * MIXER_BLOCKS - 1, 0), col))
    const = lambda shape: pl.BlockSpec(shape, lambda b, c: (0,) * len(shape))
    state = lambda width: pl.BlockSpec((1, WINDOW, width), lambda b, c: (b, 0, 0))
    return pl.pallas_call(
        _mixers_prompt_kernel,
        grid=(batch, nb),
        in_specs=[
            pl.BlockSpec(memory_space=pltpu.SMEM),
            pl.BlockSpec((rows, attn_w), cur(0)),
            pl.BlockSpec((rows, kv_w), cur(kcol)),
            pl.BlockSpec((WINDOW, kv_w), prev(kcol)),
            pl.BlockSpec((rows, kv_w), cur(vcol)),
            pl.BlockSpec((WINDOW, kv_w), prev(vcol)),
            const((1, LANES)), const((1, LANES)),
            pl.BlockSpec((rows, half), cur(b0)),
            pl.BlockSpec((rows, half), cur(b0 + 1)),
            pl.BlockSpec((rows, half), cur(b0 + 2)),
            pl.BlockSpec((rows, half), cur(b0 + 3)),
            const((1, gm_w)), const((GM_GROUPS, CHUNK, CHUNK)), const((CHUNK, GM_GROUPS)),
            slab_in(w_up), slab_in(w_down), slab_in(w_pg),
        ],
        out_specs=[
            pl.BlockSpec((rows, attn_w), cur(0)), state(kv_w), state(kv_w),
            pl.BlockSpec((rows, gm_w), cur(0)), state(gm_w),
            pl.BlockSpec((n_up // tile, k_up // steps, tile), lambda b, c: (0, flat(b, c), 0)),
            slab_out(w_down), slab_out(w_pg),
        ],
        out_shape=[
            jax.ShapeDtypeStruct((batch * seq, attn_w), BF16),
            jax.ShapeDtypeStruct((batch, WINDOW, kv_w), F32),
            jax.ShapeDtypeStruct((batch, WINDOW, kv_w), F32),
            jax.ShapeDtypeStruct((batch * seq, gm_w), BF16),
            jax.ShapeDtypeStruct((batch, CHUNK, gm_w), F32),
            jax.ShapeDtypeStruct((n_up // tile, k_up, tile), BF16),
            jax.ShapeDtypeStruct(w_down.shape[1:], BF16),
            jax.ShapeDtypeStruct(w_pg.shape[1:], BF16),
        ],
        compiler_params=_params(("arbitrary", "arbitrary"), VMEM_MIB["mixers"]),
        name="mixers_prompt",
    )(sinks, proj, proj, proj, proj, proj, qw2, kw2, proj, proj, proj, proj, nw.reshape(1, gm_w), ws, bs_t,
      w_up, w_down, w_pg)


def _split_dot(x, m):
    hi = x.astype(BF16)
    lo = (x - hi.astype(F32)).astype(BF16)
    return jnp.dot(hi, m, preferred_element_type=F32) + jnp.dot(lo, m, preferred_element_type=F32)


def _attn_sample_kernel(q_ref, kn_ref, vn_ref, kb_ref, vb_ref, qw_ref, kw_ref, sink_ref,
                        wa_ref, wg_ref, wo_ref, o_ref, ko_ref, vo_ref, wab_ref, wgb_ref, wob_ref):
    wab_ref[...] = wa_ref[...].astype(BF16)
    wgb_ref[...] = wg_ref[...].astype(BF16)
    wob_ref[...] = wo_ref[...].astype(BF16)
    nsamp = q_ref.shape[0]
    attn_w = q_ref.shape[1]
    kv_w = kn_ref.shape[1]
    gmat = _head_group_matrix()
    q = q_ref[...].astype(F32)
    kn = kn_ref[...].astype(F32)
    qn = jnp.concatenate(
        [_head_rmsnorm(q[:, p * LANES:(p + 1) * LANES], qw_ref[...], gmat)
         for p in range(attn_w // LANES)], axis=1)
    knn = jnp.concatenate(
        [_head_rmsnorm(kn[:, p * LANES:(p + 1) * LANES], kw_ref[...], gmat)
         for p in range(kv_w // LANES)], axis=1)
    vnn = vn_ref[...].astype(F32)

    ec = lax.broadcasted_iota(jnp.int32, (kv_w, attn_w), 0)
    el = lax.broadcasted_iota(jnp.int32, (kv_w, attn_w), 1)
    expand = (((ec >> 6) == (el >> 8)) & ((ec & 63) == (el & 63))).astype(BF16)
    gl = lax.broadcasted_iota(jnp.int32, (attn_w, LANES), 0)
    gh = lax.broadcasted_iota(jnp.int32, (attn_w, LANES), 1)
    hsum = ((gl >> 6) == gh).astype(BF16)
    tl = lax.broadcasted_iota(jnp.int32, (LANES, attn_w), 1)
    th = lax.broadcasted_iota(jnp.int32, (LANES, attn_w), 0)
    hexp = ((tl >> 6) == th).astype(BF16)

    nkeys = WINDOW + SUBLANES
    row = lax.broadcasted_iota(jnp.int32, (nkeys, LANES), 0)
    head = lax.broadcasted_iota(jnp.int32, (nkeys, LANES), 1)
    slope = jnp.exp2(-8.0 * (head + 1).astype(F32) / N_HEADS)
    dist = (WINDOW - row).astype(F32)
    key_ok = row <= WINDOW
    srow = lax.broadcasted_iota(jnp.int32, (WINDOW, kv_w), 0)
    sinks = sink_ref[...]

    for s in range(nsamp):
        kb = kb_ref[s]
        vb = vb_ref[s]
        knew = knn[s:s + 1]
        vnew = vnn[s:s + 1]
        ko_ref[s] = jnp.where(srow == WINDOW - 1, knew, pltpu.roll(kb, WINDOW - 1, 0))
        vo_ref[s] = jnp.where(srow == WINDOW - 1, vnew, pltpu.roll(vb, WINDOW - 1, 0))
        kk = jnp.concatenate([kb, jnp.broadcast_to(knew, (SUBLANES, kv_w))], axis=0)
        vv = jnp.concatenate([vb, jnp.broadcast_to(vnew, (SUBLANES, kv_w))], axis=0)
        kexp = jnp.dot(kk.astype(BF16), expand, preferred_element_type=F32)
        vexp = jnp.dot(vv.astype(BF16), expand, preferred_element_type=F32)
        sc = _split_dot(kexp * qn[s:s + 1], hsum)
        sc = sc - slope * dist
        sc = jnp.where(key_ok, sc, MASK_VALUE)
        mx = jnp.maximum(jnp.max(sc, axis=0, keepdims=True), sinks)
        e = jnp.exp(sc - mx)
        den = jnp.sum(e, axis=0, keepdims=True) + jnp.exp(sinks - mx)
        pexp = jnp.dot((e / den).astype(BF16), hexp, preferred_element_type=F32)
        o_ref[s:s + 1, :] = jnp.sum(pexp * vexp, axis=0, keepdims=True)


def _attn_sample(q, knew, vnew, kbuf, vbuf, qw2, kw2, sinks_row, wa, wg, wo, *, group):
    nb, attn_w = q.shape
    kv_w = knew.shape[1]
    steps = nb // group
    row = lambda width: pl.BlockSpec((group, width), lambda i: (i, 0))
    buf = pl.BlockSpec((group, WINDOW, kv_w), lambda i: (i, 0, 0))
    vec = pl.BlockSpec((1, LANES), lambda i: (0, 0))
    slab = lambda w: pl.BlockSpec((w.shape[0] // steps, w.shape[1]), lambda i: (i, 0))
    return pl.pallas_call(
        _attn_sample_kernel,
        grid=(steps,),
        in_specs=[row(attn_w), row(kv_w), row(kv_w), buf, buf, vec, vec, vec, slab(wa), slab(wg), slab(wo)],
        out_specs=[row(attn_w), buf, buf, slab(wa), slab(wg), slab(wo)],
        out_shape=[
            jax.ShapeDtypeStruct((nb, attn_w), F32),
            jax.ShapeDtypeStruct((nb, WINDOW, kv_w), F32),
            jax.ShapeDtypeStruct((nb, WINDOW, kv_w), F32),
            jax.ShapeDtypeStruct(wa.shape, BF16),
            jax.ShapeDtypeStruct(wg.shape, BF16),
            jax.ShapeDtypeStruct(wo.shape, BF16),
        ],
        compiler_params=_params(("parallel",), VMEM_MIB["attn_sample"]),
        name="attn_sample",
    )(q, knew, vnew, kbuf, vbuf, qw2, kw2, sinks_row, wa, wg, wo)


def _sgu_sample_kernel(gu_ref, gv_ref, nw_ref, w0_ref, b0_ref, o_ref, st_ref):
    vn = _layernorm_rows(jax.nn.gelu(gv_ref[...].astype(F32)), nw_ref[...])
    st_ref[...] = vn
    mix = w0_ref[...] * vn + b0_ref[...]
    o_ref[...] = jax.nn.gelu(gu_ref[...].astype(F32)) * mix


def _sgu_sample(gu, gv, nw, w0_row, b0_row):
    nb, gm_w = gu.shape
    full = pl.BlockSpec((nb, gm_w), lambda i: (0, 0))
    vec = pl.BlockSpec((1, gm_w), lambda i: (0, 0))
    return pl.pallas_call(
        _sgu_sample_kernel,
        grid=(1,),
        in_specs=[full, full, vec, vec, vec],
        out_specs=[full, full],
        out_shape=[jax.ShapeDtypeStruct((nb, gm_w), F32), jax.ShapeDtypeStruct((nb, gm_w), F32)],
        name="sgu_sample",
    )(gu, gv, nw.reshape(1, gm_w), w0_row, b0_row)


def _merge_out_kernel(a_ref, s_ref, ga_ref, gb_ref, x_ref, wa_ref, wg_ref, wo_ref, nw_ref,
                      x1_ref, xn1_ref):
    a = jnp.dot(a_ref[...].astype(BF16), wa_ref[...], preferred_element_type=F32)
    m = jnp.dot(s_ref[...].astype(BF16), wg_ref[...], preferred_element_type=F32)
    merged = (gb_ref[...].astype(F32) * a + ga_ref[...].astype(F32) * m).astype(BF16)
    x1 = x_ref[...] + jnp.dot(merged, wo_ref[...], preferred_element_type=F32)
    x1_ref[...] = x1
    xn1_ref[...] = _rmsnorm_rows(x1, nw_ref[...]).astype(BF16)


def _merge_out(attn, sgu, gates, x, wa, wg, wo, nw, *, tm):
    m, ka = attn.shape
    kg = sgu.shape[1]
    d = wo.shape[1]
    rows = lambda width, col=0: pl.BlockSpec((tm, width), lambda i: (i, col))
    resident = lambda shape: pl.BlockSpec(shape, lambda i: (0, 0), pipeline_mode=pl.Buffered(1))
    return pl.pallas_call(
        _merge_out_kernel,
        grid=(m // tm,),
        in_specs=[
            rows(ka), rows(kg), rows(d, 0), rows(d, 1), rows(d),
            resident((ka, d)), resident((kg, d)), resident((d, d)),
            pl.BlockSpec((1, d), lambda i: (0, 0)),
        ],
        out_specs=[rows(d), rows(d)],
        out_shape=[jax.ShapeDtypeStruct((m, d), F32), jax.ShapeDtypeStruct((m, d), BF16)],
        compiler_params=_params(("parallel",), VMEM_MIB["merge_out"]),
        name="merge_out",
    )(attn, sgu, gates, gates, x, wa, wg, wo, nw.reshape(1, d))


PAD = SUBLANES
FFN_ROW_BLOCKS = 4


def _ffn_step(t, xn_ref, wug_ref, wuu_ref, wd_ref, cwg_ref, cwu_ref, cbg_ref, cbu_ref,
              csg_ref, csu_ref, h_new, h_old, carry_ref, acc_ref, *, nf, tiles_per_seq):
    tm = xn_ref.shape[0]
    tf = wug_ref.shape[1]
    rb = tm // FFN_ROW_BLOCKS
    tp = jnp.maximum(t - 1, 0)
    ip = tp // nf
    jp = tp % nf
    first = (ip % tiles_per_seq) == 0
    h_old[0:PAD, :] = jnp.where(first, 0.0, carry_ref[jp])
    carry_ref[jp] = h_old[tm:tm + PAD, :]
    cw = jnp.concatenate([cwg_ref[...], cwu_ref[...]], axis=1)
    cb = jnp.concatenate([cbg_ref[...], cbu_ref[...]], axis=1)
    wug = wug_ref[...]
    wuu = wuu_ref[...]
    wd = wd_ref[...]
    for r in range(FFN_ROW_BLOCKS):
        lo = r * rb
        xr = xn_ref[lo:lo + rb, :]
        h_new[PAD + lo:PAD + lo + rb, 0:tf] = jnp.dot(xr, wug, preferred_element_type=F32)
        h_new[PAD + lo:PAD + lo + rb, tf:2 * tf] = jnp.dot(xr, wuu, preferred_element_type=F32)
        hp = h_old[lo:lo + rb + PAD, :]
        hc = (cw[0:1] * pltpu.roll(hp, 2, 0)[PAD:] + cw[1:2] * pltpu.roll(hp, 1, 0)[PAD:]
              + cw[2:3] * hp[PAD:] + cb)
        act = (jax.nn.silu(hc[:, 0:tf]) * hc[:, tf:2 * tf]).astype(BF16)
        acc_ref[lo:lo + rb, :] += jnp.dot(act, wd, preferred_element_type=F32)
    csg_ref[0] = h_old[PAD + tm - 2:PAD + tm, 0:tf]
    csu_ref[0] = h_old[PAD + tm - 2:PAD + tm, tf:2 * tf]


def _ffn_kernel(xn_ref, wug_ref, wuu_ref, wd_ref, cwg_ref, cwu_ref, cbg_ref, cbu_ref,
                o_ref, csg_ref, csu_ref, ha_ref, hb_ref, carry_ref, acc_ref, *, nf, tiles_per_seq):
    t = pl.program_id(0)
    step = functools.partial(_ffn_step, t, xn_ref, wug_ref, wuu_ref, wd_ref, cwg_ref, cwu_ref,
                             cbg_ref, cbu_ref, csg_ref, csu_ref, carry_ref=carry_ref, acc_ref=acc_ref,
                             nf=nf, tiles_per_seq=tiles_per_seq)

    @pl.when(t == 0)
    def _():
        hb_ref[...] = jnp.zeros_like(hb_ref)
        carry_ref[...] = jnp.zeros_like(carry_ref)

    @pl.when(jnp.maximum(t - 1, 0) % nf == 0)
    def _():
        acc_ref[...] = jnp.zeros_like(acc_ref)

    @pl.when(t % 2 == 0)
    def _():
        step(h_new=ha_ref, h_old=hb_ref)

    @pl.when(t % 2 == 1)
    def _():
        step(h_new=hb_ref, h_old=ha_ref)

    @pl.when(jnp.maximum(t - 1, 0) % nf == nf - 1)
    def _():
        o_ref[...] = acc_ref[...].astype(o_ref.dtype)


def _ffn_prompt(xn, w_up, w_down, cw, cb, *, batch, seq, tm):
    m, k = xn.shape
    d_ff, d = w_down.shape
    tf = w_up.shape[2]
    nf = d_ff // tf
    nm = m // tm
    tiles_per_seq = seq // tm
    prev = lambda t: jnp.maximum(t - 1, 0)
    up_tile = lambda off: (lambda t: (off + t % nf, 0, 0))
    dn_col = lambda off: (lambda t: (0, off + prev(t) % nf))
    state = lambda t: (prev(t) // nf, 0, prev(t) % nf)
    delta, csg, csu = pl.pallas_call(
        functools.partial(_ffn_kernel, nf=nf, tiles_per_seq=tiles_per_seq),
        grid=(nm * nf + 1,),
        in_specs=[
            pl.BlockSpec((tm, k), lambda t: (jnp.minimum(t // nf, nm - 1), 0)),
            pl.BlockSpec((None, k, tf), up_tile(0)),
            pl.BlockSpec((None, k, tf), up_tile(nf)),
            pl.BlockSpec((tf, d), lambda t: (prev(t) % nf, 0)),
            pl.BlockSpec((3, tf), dn_col(0)),
            pl.BlockSpec((3, tf), dn_col(nf)),
            pl.BlockSpec((1, tf), dn_col(0)),
            pl.BlockSpec((1, tf), dn_col(nf)),
        ],
        out_specs=[
            pl.BlockSpec((tm, d), lambda t: (prev(t) // nf, 0)),
            pl.BlockSpec((1, 2, tf), state),
            pl.BlockSpec((1, 2, tf), state),
        ],
        out_shape=[
            jax.ShapeDtypeStruct((m, d), BF16),
            jax.ShapeDtypeStruct((nm, 2, d_ff), F32),
            jax.ShapeDtypeStruct((nm, 2, d_ff), F32),
        ],
        scratch_shapes=[
            pltpu.VMEM((tm + PAD, 2 * tf), F32),
            pltpu.VMEM((tm + PAD, 2 * tf), F32),
            pltpu.VMEM((nf, PAD, 2 * tf), F32),
            pltpu.VMEM((tm, d), F32),
        ],
        compiler_params=_params(("arbitrary",), VMEM_MIB["ffn"]),
        name="ffn_prompt",
    )(xn, w_up, w_up, w_down, cw, cw, cb.reshape(1, -1), cb.reshape(1, -1))
    tails = jnp.concatenate([csg, csu], axis=-1).reshape(batch, tiles_per_seq, 2, 2 * d_ff)
    return delta, tails[:, -1]


def _ffn_down_sample_kernel(hg_ref, hu_ref, b0g_ref, b0u_ref, b1g_ref, b1u_ref, cwg_ref, cwu_ref,
                            cbg_ref, cbu_ref, wd_ref, x_ref, o_ref):
    @pl.when(pl.program_id(0) == 0)
    def _():
        o_ref[...] = x_ref[...]

    def conv(h_ref, b0_ref, b1_ref, cw_ref, cb_ref):
        cw = cw_ref[...]
        return cw[0:1] * b0_ref[...] + cw[1:2] * b1_ref[...] + cw[2:3] * h_ref[...] + cb_ref[...]

    hcg = conv(hg_ref, b0g_ref, b1g_ref, cwg_ref, cbg_ref)
    hcu = conv(hu_ref, b0u_ref, b1u_ref, cwu_ref, cbu_ref)
    act = (jax.nn.silu(hcg) * hcu).astype(BF16)
    o_ref[...] += jnp.dot(act, wd_ref[...], preferred_element_type=F32)


def _ffn_down_sample(h, state2, cw, cb, wd, x, *, tf):
    nb, two_f = h.shape
    d_ff = two_f // 2
    nf = d_ff // tf
    d = wd.shape[1]
    colblk = lambda off: pl.BlockSpec((nb, tf), lambda j: (0, j + off))
    return pl.pallas_call(
        _ffn_down_sample_kernel,
        grid=(nf,),
        in_specs=[
            colblk(0), colblk(nf),
            colblk(0), colblk(nf), colblk(2 * nf), colblk(3 * nf),
            pl.BlockSpec((3, tf), lambda j: (0, j)),
            pl.BlockSpec((3, tf), lambda j: (0, j + nf)),
            pl.BlockSpec((1, tf), lambda j: (0, j)),
            pl.BlockSpec((1, tf), lambda j: (0, j + nf)),
            pl.BlockSpec((tf, d), lambda j: (j, 0)),
            pl.BlockSpec((nb, d), lambda j: (0, 0)),
        ],
        out_specs=pl.BlockSpec((nb, d), lambda j: (0, 0)),
        out_shape=jax.ShapeDtypeStruct((nb, d), F32),
        compiler_params=_params(("arbitrary",), VMEM_MIB["small"]),
        name="ffn_down_sample",
    )(h, h, state2, state2, state2, state2, cw, cw, cb.reshape(1, two_f), cb.reshape(1, two_f), wd, x)


def _ple_sample_kernel(x_ref, nw_ref, w_ref, p_ref, wp_ref, o_ref, wpb_ref):
    x2 = x_ref[...]
    xn = _rmsnorm_rows(x2, nw_ref[...]).astype(BF16)
    wpb = wp_ref[...].astype(BF16)
    wpb_ref[...] = wpb
    gate = jax.nn.sigmoid(jnp.dot(xn, w_ref[...], preferred_element_type=F32))
    emb = jnp.dot(p_ref[...].astype(BF16), wpb, preferred_element_type=F32)
    o_ref[...] = x2 + gate * emb


def _ple_sample(x, nw, w, p, wp):
    m, d = x.shape
    kp = p.shape[1]
    full = lambda shape: pl.BlockSpec(shape, lambda i: (0, 0))
    return pl.pallas_call(
        _ple_sample_kernel,
        grid=(1,),
        in_specs=[full((m, d)), full((1, d)), full((d, d)), full((m, kp)), full((kp, d))],
        out_specs=[full((m, d)), full((kp, d))],
        out_shape=[jax.ShapeDtypeStruct((m, d), F32), jax.ShapeDtypeStruct((kp, d), BF16)],
        compiler_params=_params(("arbitrary",), VMEM_MIB["small"]),
        name="ple_sample",
    )(x, nw.reshape(1, d), w, p, wp)


def _ple_rows_kernel(x_ref, d_ref, nw_ref, w_ref, p_ref, wp_ref, o_ref):
    x2 = x_ref[...] + d_ref[...].astype(F32)
    xn = _rmsnorm_rows(x2, nw_ref[...]).astype(BF16)
    gate = jax.nn.sigmoid(jnp.dot(xn, w_ref[...], preferred_element_type=F32))
    emb = jnp.dot(p_ref[...].astype(BF16), wp_ref[...], preferred_element_type=F32)
    o_ref[...] = x2 + gate * emb


def _ple_rows(x, delta, nw, w, p, wp, *, tm):
    m, d = x.shape
    kp = p.shape[1]
    rows = lambda width: pl.BlockSpec((tm, width), lambda i: (i, 0))
    resident = lambda shape: pl.BlockSpec(shape, lambda i: (0, 0), pipeline_mode=pl.Buffered(1))
    return pl.pallas_call(
        _ple_rows_kernel,
        grid=(m // tm,),
        in_specs=[rows(d), rows(d), pl.BlockSpec((1, d), lambda i: (0, 0)), resident((d, d)),
                  rows(kp), resident((kp, d))],
        out_specs=rows(d),
        out_shape=jax.ShapeDtypeStruct((m, d), F32),
        compiler_params=_params(("parallel",), VMEM_MIB["ple"]),
        name="ple_prompt",
    )(x, delta, nw.reshape(1, d), w, p, wp)


def _layer_weights(i, attn_norm_w, w_in, q_norm_w, k_norm_w, attn_sinks, sgu_norm_w, sgu_w, sgu_b,
                   w_br_attn, w_br_gm, w_gate, b_gate, w_out, ffn_norm_w, w_up, conv_w, conv_b,
                   w_down, ple_norm_w, w_ple_gate, w_ple_proj):
    sinks = attn_sinks[i]
    gm_group_w = w_br_gm.shape[1] // GM_GROUPS
    return dict(
        attn_norm_w=attn_norm_w[i], w_in=w_in[i],
        qw2=jnp.tile(q_norm_w[i] * (HEAD_DIM ** -0.5), HEADS_PER_VREG).reshape(1, LANES),
        kw2=jnp.tile(k_norm_w[i], HEADS_PER_VREG).reshape(1, LANES),
        sinks=sinks,
        sinks_row=jnp.pad(sinks, (0, LANES - N_HEADS)).reshape(1, LANES),
        sgu_norm_w=sgu_norm_w[i], sgu_w=sgu_w[i], sgu_b_t=jnp.transpose(sgu_b[i]),
        sgu_w0_row=jnp.repeat(sgu_w[i][:, 0, 0], gm_group_w).reshape(1, -1),
        sgu_b0_row=jnp.repeat(sgu_b[i][:, 0], gm_group_w).reshape(1, -1),
        w_br_attn=w_br_attn[i], w_br_gm=w_br_gm[i],
        w_gate=w_gate[i], b_gate=b_gate[i], w_out=w_out[i],
        ffn_norm_w=ffn_norm_w[i], conv_w=conv_w[i], conv_b=conv_b[i],
        ple_norm_w=ple_norm_w[i], w_ple_proj=w_ple_proj[i],
        layer=i, w_up_all=w_up, w_down_all=w_down, w_ple_gate_all=w_ple_gate,
    )


def _prompt_mixers(x, lw, wb):
    batch, seq, d = x.shape
    x2d = x.reshape(batch * seq, d)
    attn_w = N_HEADS * HEAD_DIM
    kv_w = N_KV_HEADS * HEAD_DIM
    gm_w = lw["w_br_gm"].shape[0]
    proj, gates = _in_proj(x2d, lw["attn_norm_w"], wb["w_in"], wb["w_gate"], lw["b_gate"], tm=ROW_TILE)
    attn, k_new, v_new, sgu, sgu_state, wb["w_up"], wb["w_down"], wb["w_ple_gate"] = _mixers_prompt(
        proj, lw["qw2"], lw["kw2"], lw["sinks"], lw["sgu_norm_w"], lw["sgu_w"], lw["sgu_b_t"],
        lw["w_up_all"], lw["w_down_all"], lw["w_ple_gate_all"], lw["layer"],
        batch=batch, seq=seq, gu_off=attn_w + 2 * kv_w, gm_w=gm_w, tile=W_TILE)
    states = (k_new.reshape(batch, WINDOW, N_KV_HEADS, HEAD_DIM),
              v_new.reshape(batch, WINDOW, N_KV_HEADS, HEAD_DIM), sgu_state)
    return (attn, sgu, gates), states


def _prompt_tail(x, p, mixed, lw, wb):
    batch, seq, d = x.shape
    x2d = x.reshape(batch * seq, d)
    attn, sgu, gates = mixed
    x1, xn1 = _merge_out(attn, sgu, gates, x2d, wb["w_br_attn"], wb["w_br_gm"], wb["w_out"],
                         lw["ffn_norm_w"], tm=ROW_TILE)
    delta, conv_state = _ffn_prompt(xn1, wb["w_up"], wb["w_down"], lw["conv_w"], lw["conv_b"],
                                    batch=batch, seq=seq, tm=FFN_ROW_TILE)
    x3 = _ple_rows(x1, delta, lw["ple_norm_w"], wb["w_ple_gate"], p.reshape(batch * seq, -1),
                   wb["w_ple_proj"], tm=PLE_ROW_TILE)
    return x3.reshape(batch, seq, d), conv_state


def _sample_mixers(x, k_buf, v_buf, lw, wb):
    nb, t, d = x.shape
    x2d = x.reshape(nb * t, d)
    attn_w = N_HEADS * HEAD_DIM
    kv_w = N_KV_HEADS * HEAD_DIM
    gm_w = lw["w_br_gm"].shape[0]
    proj, wb["w_in"] = _norm_mm(x2d, lw["attn_norm_w"], lw["w_in"], None, act=None,
                                tn=W_TILE, tile=W_TILE, name="proj_sample")
    gates, wb["w_gate"] = _norm_mm(x2d, lw["attn_norm_w"], lw["w_gate"], lw["b_gate"], act="sigmoid",
                                   tn=2 * W_TILE, tile=W_TILE, name="gates_sample")
    q = proj[:, :attn_w]
    k = proj[:, attn_w:attn_w + kv_w]
    v = proj[:, attn_w + kv_w:attn_w + 2 * kv_w]
    gu = proj[:, attn_w + 2 * kv_w:attn_w + 2 * kv_w + gm_w]
    gv = proj[:, attn_w + 2 * kv_w + gm_w:]
    attn, k_new, v_new, wb["w_br_attn"], wb["w_br_gm"], wb["w_out"] = _attn_sample(
        q, k, v, k_buf.reshape(nb, WINDOW, kv_w), v_buf.reshape(nb, WINDOW, kv_w),
        lw["qw2"], lw["kw2"], lw["sinks_row"], lw["w_br_attn"], lw["w_br_gm"], lw["w_out"], group=SAMPLE_GROUP)
    sgu, sgu_state = _sgu_sample(gu, gv, lw["sgu_norm_w"], lw["sgu_w0_row"], lw["sgu_b0_row"])
    x1, _ = _merge_out(attn, sgu, gates, x2d, wb["w_br_attn"], wb["w_br_gm"], wb["w_out"],
                       lw["ffn_norm_w"], tm=nb)
    states = (k_new.reshape(nb, WINDOW, N_KV_HEADS, HEAD_DIM), v_new.reshape(nb, WINDOW, N_KV_HEADS, HEAD_DIM),
              sgu_state.reshape(nb, t, gm_w))
    return x1, states


def _sample_tail(x1, p, conv_buf, lw, wb):
    nb = x1.shape[0]
    h = _norm_mm_tiles(x1, lw["ffn_norm_w"], wb["w_up"], tiles_per_step=2, name="ffn_up_sample")
    x2 = _ffn_down_sample(h, conv_buf.reshape(nb, -1), lw["conv_w"], lw["conv_b"], wb["w_down"], x1, tf=W_TILE)
    x3, wb["w_ple_proj"] = _ple_sample(x2, lw["ple_norm_w"], wb["w_ple_gate"], p.reshape(nb, -1),
                                       lw["w_ple_proj"])
    conv_state = jnp.stack([conv_buf[:, 1, :], h], axis=1)
    return x3, conv_state


def kernel(x_prompt, x_sample, p_prompt, p_sample, state_attn_k, state_attn_v, state_conv, attn_norm_w, w_in, q_norm_w, k_norm_w, attn_sinks, sgu_norm_w, sgu_w, sgu_b, w_br_attn, w_br_gm, w_gate, b_gate, w_out, ffn_norm_w, w_up, conv_w, conv_b, w_down, ple_norm_w, w_ple_gate, w_ple_proj):
    depth = w_in.shape[0]
    xp, xs = x_prompt, x_sample
    outs = [[] for _ in range(8)]
    for i in range(depth):
        lw = _layer_weights(i, attn_norm_w, w_in, q_norm_w, k_norm_w, attn_sinks, sgu_norm_w, sgu_w, sgu_b,
                            w_br_attn, w_br_gm, w_gate, b_gate, w_out, ffn_norm_w, w_up, conv_w, conv_b,
                            w_down, ple_norm_w, w_ple_gate, w_ple_proj)
        wb = {}
        xs1, (ks, vs, gs) = _sample_mixers(xs, state_attn_k[i], state_attn_v[i], lw, wb)
        mixed, (kp, vp, gp) = _prompt_mixers(xp, lw, wb)
        xs3, cs = _sample_tail(xs1, p_sample[i], state_conv[i], lw, wb)
        xs = xs3.reshape(xs.shape)
        xp, cp = _prompt_tail(xp, p_prompt[i], mixed, lw, wb)
        for lst, val in zip(outs, (kp, vp, ks, vs, gp, gs, cp, cs)):
            lst.append(val)
    return (xp, xs) + tuple(jnp.stack(lst) for lst in outs)
```

```python
import functools

import jax
import jax.numpy as jnp
from jax import lax
from jax.experimental import pallas as pl
from jax.experimental.pallas import tpu as pltpu

F32 = jnp.float32
BF16 = jnp.bfloat16

HEAD_DIM = 64
N_HEADS = 16
N_KV_HEADS = 4
GQA_GROUP = N_HEADS // N_KV_HEADS
WINDOW = 128
CHUNK = 128
GM_GROUPS = 4
EPS = 1e-6
MASK_VALUE = -1e30
LANES = 128
SUBLANES = 8
HEADS_PER_VREG = LANES // HEAD_DIM
MIB = 1 << 20


def _alibi_slope(h):
    return float(2.0 ** (-8.0 * (h + 1) / N_HEADS))


ROW_TILE = 256
PLE_ROW_TILE = 512
FFN_ROW_TILE = 1024
W_TILE = 512
SAMPLE_GROUP = 8
VMEM_MIB = dict(stream=56, in_proj=56, mixers=56, attn_sample=48, merge_out=48, ffn=58,
                ffn_sample=48, ple=56, small=32)


def _params(semantics, vmem_mib):
    return pltpu.CompilerParams(dimension_semantics=semantics, vmem_limit_bytes=vmem_mib * MIB)


def _rmsnorm_rows(x, w):
    ms = jnp.mean(x * x, axis=-1, keepdims=True)
    return x * lax.rsqrt(ms + EPS) * w


def _norm_mm_kernel(*refs, has_bias, act):
    if has_bias:
        x_ref, nw_ref, w_ref, b_ref, o_ref, wb_ref, xn_ref = refs
    else:
        x_ref, nw_ref, w_ref, o_ref, wb_ref, xn_ref = refs

    @pl.when(pl.program_id(0) == 0)
    def _():
        xn_ref[...] = _rmsnorm_rows(x_ref[...], nw_ref[...]).astype(BF16)

    wb = w_ref[...].astype(BF16)
    tile = wb_ref.shape[2]
    for c in range(wb_ref.shape[0]):
        wb_ref[c] = wb[:, c * tile:(c + 1) * tile]
    acc = jnp.dot(xn_ref[...], wb, preferred_element_type=F32)
    if has_bias:
        acc = acc + b_ref[...]
    if act == "sigmoid":
        acc = jax.nn.sigmoid(acc)
    o_ref[...] = acc.astype(o_ref.dtype)


def _norm_mm(x, nw, w, bias, *, act, tn, tile, name):
    m, k = x.shape
    n = w.shape[1]
    per_step = tn // tile
    in_specs = [
        pl.BlockSpec((m, k), lambda j: (0, 0)),
        pl.BlockSpec((1, k), lambda j: (0, 0)),
        pl.BlockSpec((k, tn), lambda j: (0, j)),
    ]
    args = [x, nw.reshape(1, k), w]
    if bias is not None:
        in_specs.append(pl.BlockSpec((1, tn), lambda j: (0, j)))
        args.append(bias.reshape(1, n))
    return pl.pallas_call(
        functools.partial(_norm_mm_kernel, has_bias=bias is not None, act=act),
        grid=(n // tn,),
        in_specs=in_specs,
        out_specs=[pl.BlockSpec((m, tn), lambda j: (0, j)),
                   pl.BlockSpec((per_step, k, tile), lambda j: (j, 0, 0))],
        out_shape=[jax.ShapeDtypeStruct((m, n), F32), jax.ShapeDtypeStruct((n // tile, k, tile), BF16)],
        scratch_shapes=[pltpu.VMEM((m, k), BF16)],
        compiler_params=_params(("arbitrary",), VMEM_MIB["stream"]),
        name=name,
    )(*args)


def _norm_mm_tiles_kernel(x_ref, nw_ref, w_ref, o_ref, xn_ref):
    @pl.when(pl.program_id(0) == 0)
    def _():
        xn_ref[...] = _rmsnorm_rows(x_ref[...], nw_ref[...]).astype(BF16)

    tile = w_ref.shape[2]
    for c in range(w_ref.shape[0]):
        o_ref[:, c * tile:(c + 1) * tile] = jnp.dot(xn_ref[...], w_ref[c], preferred_element_type=F32)


def _norm_mm_tiles(x, nw, wt, *, tiles_per_step, name):
    m, k = x.shape
    n_tiles, _, tile = wt.shape
    tn = tiles_per_step * tile
    return pl.pallas_call(
        _norm_mm_tiles_kernel,
        grid=(n_tiles // tiles_per_step,),
        in_specs=[
            pl.BlockSpec((m, k), lambda j: (0, 0)),
            pl.BlockSpec((1, k), lambda j: (0, 0)),
            pl.BlockSpec((tiles_per_step, k, tile), lambda j: (j, 0, 0)),
        ],
        out_specs=pl.BlockSpec((m, tn), lambda j: (0, j)),
        out_shape=jax.ShapeDtypeStruct((m, n_tiles * tile), F32),
        scratch_shapes=[pltpu.VMEM((m, k), BF16)],
        compiler_params=_params(("arbitrary",), VMEM_MIB["ffn_sample"]),
        name=name,
    )(x, nw.reshape(1, k), wt)


def _in_proj_kernel(x_ref, nw_ref, win_ref, wg_ref, bg_ref, proj_ref, gates_ref, xn_ref):
    xn_ref[...] = _rmsnorm_rows(x_ref[...], nw_ref[...]).astype(BF16)
    tn = win_ref.shape[2]
    for c in range(win_ref.shape[0]):
        acc = jnp.dot(xn_ref[...], win_ref[c], preferred_element_type=F32)
        proj_ref[:, c * tn:(c + 1) * tn] = acc.astype(proj_ref.dtype)
    for c in range(wg_ref.shape[0]):
        cols = slice(c * tn, (c + 1) * tn)
        acc = jnp.dot(xn_ref[...], wg_ref[c], preferred_element_type=F32)
        gates_ref[:, cols] = jax.nn.sigmoid(acc + bg_ref[:, cols]).astype(gates_ref.dtype)


def _in_proj(x, nw, w_in, w_gate, b_gate, *, tm):
    m, k = x.shape
    n_in, _, tn = w_in.shape
    n_g = w_gate.shape[0]
    rows = lambda width: pl.BlockSpec((tm, width), lambda i: (i, 0))
    resident = lambda a: pl.BlockSpec(a.shape, lambda i: (0,) * a.ndim, pipeline_mode=pl.Buffered(1))
    bg = b_gate.reshape(1, -1)
    return pl.pallas_call(
        _in_proj_kernel,
        grid=(m // tm,),
        in_specs=[rows(k), pl.BlockSpec((1, k), lambda i: (0, 0)), resident(w_in), resident(w_gate),
                  pl.BlockSpec(bg.shape, lambda i: (0, 0))],
        out_specs=[rows(n_in * tn), rows(n_g * tn)],
        out_shape=[
            jax.ShapeDtypeStruct((m, n_in * tn), BF16),
            jax.ShapeDtypeStruct((m, n_g * tn), BF16),
        ],
        scratch_shapes=[pltpu.VMEM((tm, k), BF16)],
        compiler_params=_params(("parallel",), VMEM_MIB["in_proj"]),
        name="in_proj",
    )(x, nw.reshape(1, k), w_in, w_gate, bg)


def _head_group_matrix():
    r = lax.broadcasted_iota(jnp.int32, (LANES, LANES), 0)
    c = lax.broadcasted_iota(jnp.int32, (LANES, LANES), 1)
    return jnp.where((r >> 6) == (c >> 6), 1.0 / HEAD_DIM, 0.0).astype(BF16)


def _head_rmsnorm(xcol, w, gmat):
    ms = jnp.dot((xcol * xcol).astype(BF16), gmat, preferred_element_type=F32)
    return xcol * lax.rsqrt(ms + EPS) * w


def _attn_prompt_kernel(sink_ref, q_ref, kc_ref, kp_ref, vc_ref, vp_ref, qw_ref, kw_ref,
                        o_ref, kn_ref, vn_ref):
    c = pl.program_id(1)
    blk = WINDOW
    nblk = q_ref.shape[0] // blk
    gmat = _head_group_matrix()
    lane = lax.broadcasted_iota(jnp.int32, (1, LANES), 1)
    low = lane < HEAD_DIM

    kw = kw_ref[...]
    qw = qw_ref[...]
    kall = jnp.concatenate([kp_ref[...], kc_ref[...]], axis=0).astype(F32)
    vall = jnp.concatenate([vp_ref[...], vc_ref[...]], axis=0).astype(F32)
    ncol = kall.shape[1] // LANES
    kcat = [_head_rmsnorm(kall[:, p * LANES:(p + 1) * LANES], kw, gmat) for p in range(ncol)]
    vcat = [vall[:, p * LANES:(p + 1) * LANES] for p in range(ncol)]
    kn_ref[0] = jnp.concatenate([kc_[nblk * blk:] for kc_ in kcat], axis=1)
    vn_ref[0] = vall[nblk * blk:]
    krol = [pltpu.roll(kcat[p], HEAD_DIM, 1) for p in range(ncol)]
    vrol = [pltpu.roll(vcat[p], HEAD_DIM, 1) for p in range(ncol)]

    qi = lax.broadcasted_iota(jnp.int32, (blk, 2 * blk), 0)
    kj = lax.broadcasted_iota(jnp.int32, (blk, 2 * blk), 1)
    dist = blk + qi - kj
    in_window = (dist >= 0) & (dist <= WINDOW)
    distf = dist.astype(F32)

    q = q_ref[...].astype(F32)
    qn = [_head_rmsnorm(q[:, p * LANES:(p + 1) * LANES], qw, gmat)
          for p in range(N_HEADS // HEADS_PER_VREG)]

    for g in range(N_KV_HEADS):
        p, half = divmod(g, HEADS_PER_VREG)
        if half == 0:
            kd_all = jnp.where(low, kcat[p], krol[p]).astype(BF16)
            vd_all = jnp.where(low, vcat[p], vrol[p]).astype(BF16)
        else:
            kd_all = jnp.where(low, krol[p], kcat[p]).astype(BF16)
            vd_all = jnp.where(low, vrol[p], vcat[p]).astype(BF16)
        for b in range(nblk):
            rows = slice(b * blk, (b + 1) * blk)
            kd = kd_all[b * blk:(b + 2) * blk]
            vd = vd_all[b * blk:(b + 2) * blk]
            valid = in_window if b > 0 else in_window & ((kj >= blk) | (c > 0))
            qs = []
            for hl in range(GQA_GROUP):
                h = g * GQA_GROUP + hl
                pc, hh = divmod(h, HEADS_PER_VREG)
                keep = low if hh == 0 else jnp.logical_not(low)
                qs.append(jnp.where(keep, qn[pc][rows], 0.0).astype(BF16))
            qstack = jnp.concatenate(qs, axis=0)
            s_all = lax.dot_general(qstack, kd, (((1,), (1,)), ((), ())), preferred_element_type=F32)
            ps = []
            for hl in range(GQA_GROUP):
                h = g * GQA_GROUP + hl
                s = s_all[hl * blk:(hl + 1) * blk] - _alibi_slope(h) * distf
                s = jnp.where(valid, s, MASK_VALUE)
                sink = sink_ref[h]
                mx = jnp.maximum(jnp.max(s, axis=-1, keepdims=True), sink)
                e = jnp.exp(s - mx)
                den = jnp.sum(e, axis=-1, keepdims=True) + jnp.exp(sink - mx)
                ps.append((e / den).astype(BF16))
            pstack = jnp.concatenate(ps, axis=0)
            o_all = jnp.dot(pstack, vd, preferred_element_type=F32)
            for pair in range(GQA_GROUP // HEADS_PER_VREG):
                pc = g * (GQA_GROUP // HEADS_PER_VREG) + pair
                even = o_all[(2 * pair) * blk:(2 * pair + 1) * blk]
                odd = o_all[(2 * pair + 1) * blk:(2 * pair + 2) * blk]
                o_ref[rows, pc * LANES:(pc + 1) * LANES] = jnp.where(low, even, odd).astype(o_ref.dtype)


def _layernorm_rows(x, w):
    mu = jnp.mean(x, axis=-1, keepdims=True)
    xc = x - mu
    return xc * lax.rsqrt(jnp.mean(xc * xc, axis=-1, keepdims=True) + EPS) * w


GELU_C = 0.7978845608028654
GELU_A = 0.044715


def _gelu_tanh(x):
    return x * jax.nn.sigmoid((2.0 * GELU_C) * (x + GELU_A * (x * x * x)))


def _sgu_prompt_kernel(gu0_ref, gu1_ref, gv0_ref, gv1_ref, nw_ref, ws_ref, bs_ref, o_ref, st_ref):
    rows = gu0_ref.shape[0]
    gv = jnp.concatenate([gv0_ref[...], gv1_ref[...]], axis=1).astype(F32)
    vn = _layernorm_rows(_gelu_tanh(gv), nw_ref[...])
    st_ref[0] = vn[rows - CHUNK:]
    vb = vn.astype(BF16)
    r = lax.broadcasted_iota(jnp.int32, (CHUNK, CHUNK), 0)
    c = lax.broadcasted_iota(jnp.int32, (CHUNK, CHUNK), 1)
    causal = r >= c
    gw = vb.shape[1] // GM_GROUPS
    half = gu0_ref.shape[1]
    for g in range(GM_GROUPS):
        w = jnp.where(causal, ws_ref[g], 0.0).astype(BF16)
        src = gu0_ref if g * gw < half else gu1_ref
        off = g * gw - (0 if g * gw < half else half)
        for ch in range(rows // CHUNK):
            rs = slice(ch * CHUNK, (ch + 1) * CHUNK)
            mix = jnp.dot(w, vb[rs, g * gw:(g + 1) * gw], preferred_element_type=F32) + bs_ref[:, g:g + 1]
            u = _gelu_tanh(src[rs, off:off + gw].astype(F32))
            o_ref[rs, g * gw:(g + 1) * gw] = (u * mix).astype(o_ref.dtype)


N_ATTN_IN, N_ATTN_OUT, N_SGU_IN, N_SGU_OUT, N_CAST = 8, 3, 7, 2, 3
MIXER_BLOCKS = 4


def _mixers_prompt_kernel(*refs):
    attn_in = refs[:N_ATTN_IN]
    sgu_in = refs[N_ATTN_IN:N_ATTN_IN + N_SGU_IN]
    cast_in = refs[N_ATTN_IN + N_SGU_IN:N_ATTN_IN + N_SGU_IN + N_CAST]
    outs = refs[N_ATTN_IN + N_SGU_IN + N_CAST:]
    attn_out = outs[:N_ATTN_OUT]
    sgu_out = outs[N_ATTN_OUT:N_ATTN_OUT + N_SGU_OUT]
    wup_ref, wdn_ref, wpg_ref = cast_in
    wupb_ref, wdnb_ref, wpgb_ref = outs[N_ATTN_OUT + N_SGU_OUT:]
    tile = wupb_ref.shape[2]
    for c in range(wupb_ref.shape[0]):
        wupb_ref[c] = wup_ref[:, c * tile:(c + 1) * tile].astype(BF16)
    wdnb_ref[...] = wdn_ref[...].astype(BF16)
    wpgb_ref[...] = wpg_ref[...].astype(BF16)
    _sgu_prompt_kernel(*sgu_in, *sgu_out)
    _attn_prompt_kernel(*attn_in, *attn_out)


def _mixers_prompt(proj, qw2, kw2, sinks, nw, ws, bs_t, w_up, w_down, w_pg, layer, *,
                   batch, seq, gu_off, gm_w, tile):
    rows = MIXER_BLOCKS * WINDOW
    nb = seq // rows
    steps = batch * nb
    flat = lambda b, c: b * nb + c
    _, k_up, n_up = w_up.shape
    slab_in = lambda a: pl.BlockSpec((None, a.shape[1] // steps, a.shape[2]), lambda b, c: (layer, flat(b, c), 0))
    slab_out = lambda a: pl.BlockSpec((a.shape[1] // steps, a.shape[2]), lambda b, c: (flat(b, c), 0))
    attn_w = N_HEADS * HEAD_DIM
    kv_w = N_KV_HEADS * HEAD_DIM
    kcol = attn_w // kv_w
    vcol = kcol + 1
    half = gm_w // 2
    b0 = gu_off // half
    cur = lambda col: (lambda b, c: (flat(b, c), col))
    prev = lambda col: (lambda b, c: (jnp.maximum(flat(b, c) * MIXER_BLOCKS - 1, 0), col))
    const = lambda shape: pl.BlockSpec(shape, lambda b, c: (0,) * len(shape))
    state = lambda width: pl.BlockSpec((1, WINDOW, width), lambda b, c: (b, 0, 0))
    return pl.pallas_call(
        _mixers_prompt_kernel,
        grid=(batch, nb),
        in_specs=[
            pl.BlockSpec(memory_space=pltpu.SMEM),
            pl.BlockSpec((rows, attn_w), cur(0)),
            pl.BlockSpec((rows, kv_w), cur(kcol)),
            pl.BlockSpec((WINDOW, kv_w), prev(kcol)),
            pl.BlockSpec((rows, kv_w), cur(vcol)),
            pl.BlockSpec((WINDOW, kv_w), prev(vcol)),
            const((1, LANES)), const((1, LANES)),
            pl.BlockSpec((rows, half), cur(b0)),
            pl.BlockSpec((rows, half), cur(b0 + 1)),
            pl.BlockSpec((rows, half), cur(b0 + 2)),
            pl.BlockSpec((rows, half), cur(b0 + 3)),
            const((1, gm_w)), const((GM_GROUPS, CHUNK, CHUNK)), const((CHUNK, GM_GROUPS)),
            slab_in(w_up), slab_in(w_down), slab_in(w_pg),
        ],
        out_specs=[
            pl.BlockSpec((rows, attn_w), cur(0)), state(kv_w), state(kv_w),
            pl.BlockSpec((rows, gm_w), cur(0)), state(gm_w),
            pl.BlockSpec((n_up // tile, k_up // steps, tile), lambda b, c: (0, flat(b, c), 0)),
            slab_out(w_down), slab_out(w_pg),
        ],
        out_shape=[
            jax.ShapeDtypeStruct((batch * seq, attn_w), BF16),
            jax.ShapeDtypeStruct((batch, WINDOW, kv_w), F32),
            jax.ShapeDtypeStruct((batch, WINDOW, kv_w), F32),
            jax.ShapeDtypeStruct((batch * seq, gm_w), BF16),
            jax.ShapeDtypeStruct((batch, CHUNK, gm_w), F32),
            jax.ShapeDtypeStruct((n_up // tile, k_up, tile), BF16),
            jax.ShapeDtypeStruct(w_down.shape[1:], BF16),
            jax.ShapeDtypeStruct(w_pg.shape[1:], BF16),
        ],
        compiler_params=_params(("arbitrary", "arbitrary"), VMEM_MIB["mixers"]),
        name="mixers_prompt",
    )(sinks, proj, proj, proj, proj, proj, qw2, kw2, proj, proj, proj, proj, nw.reshape(1, gm_w), ws, bs_t,
      w_up, w_down, w_pg)


def _split_dot(x, m):
    hi = x.astype(BF16)
    lo = (x - hi.astype(F32)).astype(BF16)
    return jnp.dot(hi, m, preferred_element_type=F32) + jnp.dot(lo, m, preferred_element_type=F32)


def _attn_sample_kernel(q_ref, kn_ref, vn_ref, kb_ref, vb_ref, qw_ref, kw_ref, sink_ref,
                        wa_ref, wg_ref, wo_ref, o_ref, ko_ref, vo_ref, wab_ref, wgb_ref, wob_ref):
    wab_ref[...] = wa_ref[...].astype(BF16)
    wgb_ref[...] = wg_ref[...].astype(BF16)
    wob_ref[...] = wo_ref[...].astype(BF16)
    nsamp = q_ref.shape[0]
    attn_w = q_ref.shape[1]
    kv_w = kn_ref.shape[1]
    gmat = _head_group_matrix()
    q = q_ref[...].astype(F32)
    kn = kn_ref[...].astype(F32)
    qn = jnp.concatenate(
        [_head_rmsnorm(q[:, p * LANES:(p + 1) * LANES], qw_ref[...], gmat)
         for p in range(attn_w // LANES)], axis=1)
    knn = jnp.concatenate(
        [_head_rmsnorm(kn[:, p * LANES:(p + 1) * LANES], kw_ref[...], gmat)
         for p in range(kv_w // LANES)], axis=1)
    vnn = vn_ref[...].astype(F32)

    ec = lax.broadcasted_iota(jnp.int32, (kv_w, attn_w), 0)
    el = lax.broadcasted_iota(jnp.int32, (kv_w, attn_w), 1)
    expand = (((ec >> 6) == (el >> 8)) & ((ec & 63) == (el & 63))).astype(BF16)
    gl = lax.broadcasted_iota(jnp.int32, (attn_w, LANES), 0)
    gh = lax.broadcasted_iota(jnp.int32, (attn_w, LANES), 1)
    hsum = ((gl >> 6) == gh).astype(BF16)
    tl = lax.broadcasted_iota(jnp.int32, (LANES, attn_w), 1)
    th = lax.broadcasted_iota(jnp.int32, (LANES, attn_w), 0)
    hexp = ((tl >> 6) == th).astype(BF16)

    nkeys = WINDOW + SUBLANES
    row = lax.broadcasted_iota(jnp.int32, (nkeys, LANES), 0)
    head = lax.broadcasted_iota(jnp.int32, (nkeys, LANES), 1)
    slope = jnp.exp2(-8.0 * (head + 1).astype(F32) / N_HEADS)
    dist = (WINDOW - row).astype(F32)
    key_ok = row <= WINDOW
    srow = lax.broadcasted_iota(jnp.int32, (WINDOW, kv_w), 0)
    sinks = sink_ref[...]

    for s in range(nsamp):
        kb = kb_ref[s]
        vb = vb_ref[s]
        knew = knn[s:s + 1]
        vnew = vnn[s:s + 1]
        ko_ref[s] = jnp.where(srow == WINDOW - 1, knew, pltpu.roll(kb, WINDOW - 1, 0))
        vo_ref[s] = jnp.where(srow == WINDOW - 1, vnew, pltpu.roll(vb, WINDOW - 1, 0))
        kk = jnp.concatenate([kb, jnp.broadcast_to(knew, (SUBLANES, kv_w))], axis=0)
        vv = jnp.concatenate([vb, jnp.broadcast_to(vnew, (SUBLANES, kv_w))], axis=0)
        kexp = jnp.dot(kk.astype(BF16), expand, preferred_element_type=F32)
        vexp = jnp.dot(vv.astype(BF16), expand, preferred_element_type=F32)
        sc = _split_dot(kexp * qn[s:s + 1], hsum)
        sc = sc - slope * dist
        sc = jnp.where(key_ok, sc, MASK_VALUE)
        mx = jnp.maximum(jnp.max(sc, axis=0, keepdims=True), sinks)
        e = jnp.exp(sc - mx)
        den = jnp.sum(e, axis=0, keepdims=True) + jnp.exp(sinks - mx)
        pexp = jnp.dot((e / den).astype(BF16), hexp, preferred_element_type=F32)
        o_ref[s:s + 1, :] = jnp.sum(pexp * vexp, axis=0, keepdims=True)


def _attn_sample(q, knew, vnew, kbuf, vbuf, qw2, kw2, sinks_row, wa, wg, wo, *, group):
    nb, attn_w = q.shape
    kv_w = knew.shape[1]
    steps = nb // group
    row = lambda width: pl.BlockSpec((group, width), lambda i: (i, 0))
    buf = pl.BlockSpec((group, WINDOW, kv_w), lambda i: (i, 0, 0))
    vec = pl.BlockSpec((1, LANES), lambda i: (0, 0))
    slab = lambda w: pl.BlockSpec((w.shape[0] // steps, w.shape[1]), lambda i: (i, 0))
    return pl.pallas_call(
        _attn_sample_kernel,
        grid=(steps,),
        in_specs=[row(attn_w), row(kv_w), row(kv_w), buf, buf, vec, vec, vec, slab(wa), slab(wg), slab(wo)],
        out_specs=[row(attn_w), buf, buf, slab(wa), slab(wg), slab(wo)],
        out_shape=[
            jax.ShapeDtypeStruct((nb, attn_w), F32),
            jax.ShapeDtypeStruct((nb, WINDOW, kv_w), F32),
            jax.ShapeDtypeStruct((nb, WINDOW, kv_w), F32),
            jax.ShapeDtypeStruct(wa.shape, BF16),
            jax.ShapeDtypeStruct(wg.shape, BF16),
            jax.ShapeDtypeStruct(wo.shape, BF16),
        ],
        compiler_params=_params(("parallel",), VMEM_MIB["attn_sample"]),
        name="attn_sample",
    )(q, knew, vnew, kbuf, vbuf, qw2, kw2, sinks_row, wa, wg, wo)


def _sgu_sample_kernel(gu_ref, gv_ref, nw_ref, w0_ref, b0_ref, o_ref, st_ref):
    vn = _layernorm_rows(jax.nn.gelu(gv_ref[...].astype(F32)), nw_ref[...])
    st_ref[...] = vn
    mix = w0_ref[...] * vn + b0_ref[...]
    o_ref[...] = jax.nn.gelu(gu_ref[...].astype(F32)) * mix


def _sgu_sample(gu, gv, nw, w0_row, b0_row):
    nb, gm_w = gu.shape
    full = pl.BlockSpec((nb, gm_w), lambda i: (0, 0))
    vec = pl.BlockSpec((1, gm_w), lambda i: (0, 0))
    return pl.pallas_call(
        _sgu_sample_kernel,
        grid=(1,),
        in_specs=[full, full, vec, vec, vec],
        out_specs=[full, full],
        out_shape=[jax.ShapeDtypeStruct((nb, gm_w), F32), jax.ShapeDtypeStruct((nb, gm_w), F32)],
        name="sgu_sample",
    )(gu, gv, nw.reshape(1, gm_w), w0_row, b0_row)


def _merge_out_kernel(a_ref, s_ref, ga_ref, gb_ref, x_ref, wa_ref, wg_ref, wo_ref, nw_ref,
                      x1_ref, xn1_ref):
    a = jnp.dot(a_ref[...].astype(BF16), wa_ref[...], preferred_element_type=F32)
    m = jnp.dot(s_ref[...].astype(BF16), wg_ref[...], preferred_element_type=F32)
    merged = (gb_ref[...].astype(F32) * a + ga_ref[...].astype(F32) * m).astype(BF16)
    x1 = x_ref[...] + jnp.dot(merged, wo_ref[...], preferred_element_type=F32)
    x1_ref[...] = x1
    xn1_ref[...] = _rmsnorm_rows(x1, nw_ref[...]).astype(BF16)


def _merge_out(attn, sgu, gates, x, wa, wg, wo, nw, *, tm):
    m, ka = attn.shape
    kg = sgu.shape[1]
    d = wo.shape[1]
    rows = lambda width, col=0: pl.BlockSpec((tm, width), lambda i: (i, col))
    resident = lambda shape: pl.BlockSpec(shape, lambda i: (0, 0), pipeline_mode=pl.Buffered(1))
    return pl.pallas_call(
        _merge_out_kernel,
        grid=(m // tm,),
        in_specs=[
            rows(ka), rows(kg), rows(d, 0), rows(d, 1), rows(d),
            resident((ka, d)), resident((kg, d)), resident((d, d)),
            pl.BlockSpec((1, d), lambda i: (0, 0)),
        ],
        out_specs=[rows(d), rows(d)],
        out_shape=[jax.ShapeDtypeStruct((m, d), F32), jax.ShapeDtypeStruct((m, d), BF16)],
        compiler_params=_params(("parallel",), VMEM_MIB["merge_out"]),
        name="merge_out",
    )(attn, sgu, gates, gates, x, wa, wg, wo, nw.reshape(1, d))


PAD = SUBLANES
FFN_ROW_BLOCKS = 4


def _ffn_step(t, xn_ref, wug_ref, wuu_ref, wd_ref, cwg_ref, cwu_ref, cbg_ref, cbu_ref,
              csg_ref, csu_ref, h_new, h_old, carry_ref, acc_ref, *, nf, tiles_per_seq):
    tm = xn_ref.shape[0]
    tf = wug_ref.shape[1]
    rb = tm // FFN_ROW_BLOCKS
    tp = jnp.maximum(t - 1, 0)
    ip = tp // nf
    jp = tp % nf
    first = (ip % tiles_per_seq) == 0
    h_old[0:PAD, :] = jnp.where(first, 0.0, carry_ref[jp])
    carry_ref[jp] = h_old[tm:tm + PAD, :]
    cw = jnp.concatenate([cwg_ref[...], cwu_ref[...]], axis=1)
    cb = jnp.concatenate([cbg_ref[...], cbu_ref[...]], axis=1)
    wug = wug_ref[...]
    wuu = wuu_ref[...]
    wd = wd_ref[...]
    for r in range(FFN_ROW_BLOCKS):
        lo = r * rb
        xr = xn_ref[lo:lo + rb, :]
        h_new[PAD + lo:PAD + lo + rb, 0:tf] = jnp.dot(xr, wug, preferred_element_type=F32)
        h_new[PAD + lo:PAD + lo + rb, tf:2 * tf] = jnp.dot(xr, wuu, preferred_element_type=F32)
        hp = h_old[lo:lo + rb + PAD, :]
        hc = (cw[0:1] * pltpu.roll(hp, 2, 0)[PAD:] + cw[1:2] * pltpu.roll(hp, 1, 0)[PAD:]
              + cw[2:3] * hp[PAD:] + cb)
        act = (jax.nn.silu(hc[:, 0:tf]) * hc[:, tf:2 * tf]).astype(BF16)
        acc_ref[lo:lo + rb, :] += jnp.dot(act, wd, preferred_element_type=F32)
    csg_ref[0] = h_old[PAD + tm - 2:PAD + tm, 0:tf]
    csu_ref[0] = h_old[PAD + tm - 2:PAD + tm, tf:2 * tf]


def _ffn_kernel(xn_ref, wug_ref, wuu_ref, wd_ref, cwg_ref, cwu_ref, cbg_ref, cbu_ref,
                o_ref, csg_ref, csu_ref, ha_ref, hb_ref, carry_ref, acc_ref, *, nf, tiles_per_seq):
    t = pl.program_id(0)
    step = functools.partial(_ffn_step, t, xn_ref, wug_ref, wuu_ref, wd_ref, cwg_ref, cwu_ref,
                             cbg_ref, cbu_ref, csg_ref, csu_ref, carry_ref=carry_ref, acc_ref=acc_ref,
                             nf=nf, tiles_per_seq=tiles_per_seq)

    @pl.when(t == 0)
    def _():
        hb_ref[...] = jnp.zeros_like(hb_ref)
        carry_ref[...] = jnp.zeros_like(carry_ref)

    @pl.when(jnp.maximum(t - 1, 0) % nf == 0)
    def _():
        acc_ref[...] = jnp.zeros_like(acc_ref)

    @pl.when(t % 2 == 0)
    def _():
        step(h_new=ha_ref, h_old=hb_ref)

    @pl.when(t % 2 == 1)
    def _():
        step(h_new=hb_ref, h_old=ha_ref)

    @pl.when(jnp.maximum(t - 1, 0) % nf == nf - 1)
    def _():
        o_ref[...] = acc_ref[...].astype(o_ref.dtype)


def _ffn_prompt(xn, w_up, w_down, cw, cb, *, batch, seq, tm):
    m, k = xn.shape
    d_ff, d = w_down.shape
    tf = w_up.shape[2]
    nf = d_ff // tf
    nm = m // tm
    tiles_per_seq = seq // tm
    prev = lambda t: jnp.maximum(t - 1, 0)
    up_tile = lambda off: (lambda t: (off + t % nf, 0, 0))
    dn_col = lambda off: (lambda t: (0, off + prev(t) % nf))
    state = lambda t: (prev(t) // nf, 0, prev(t) % nf)
    delta, csg, csu = pl.pallas_call(
        functools.partial(_ffn_kernel, nf=nf, tiles_per_seq=tiles_per_seq),
        grid=(nm * nf + 1,),
        in_specs=[
            pl.BlockSpec((tm, k), lambda t: (jnp.minimum(t // nf, nm - 1), 0)),
            pl.BlockSpec((None, k, tf), up_tile(0)),
            pl.BlockSpec((None, k, tf), up_tile(nf)),
            pl.BlockSpec((tf, d), lambda t: (prev(t) % nf, 0)),
            pl.BlockSpec((3, tf), dn_col(0)),
            pl.BlockSpec((3, tf), dn_col(nf)),
            pl.BlockSpec((1, tf), dn_col(0)),
            pl.BlockSpec((1, tf), dn_col(nf)),
        ],
        out_specs=[
            pl.BlockSpec((tm, d), lambda t: (prev(t) // nf, 0)),
            pl.BlockSpec((1, 2, tf), state),
            pl.BlockSpec((1, 2, tf), state),
        ],
        out_shape=[
            jax.ShapeDtypeStruct((m, d), BF16),
            jax.ShapeDtypeStruct((nm, 2, d_ff), F32),
            jax.ShapeDtypeStruct((nm, 2, d_ff), F32),
        ],
        scratch_shapes=[
            pltpu.VMEM((tm + PAD, 2 * tf), F32),
            pltpu.VMEM((tm + PAD, 2 * tf), F32),
            pltpu.VMEM((nf, PAD, 2 * tf), F32),
            pltpu.VMEM((tm, d), F32),
        ],
        compiler_params=_params(("arbitrary",), VMEM_MIB["ffn"]),
        name="ffn_prompt",
    )(xn, w_up, w_up, w_down, cw, cw, cb.reshape(1, -1), cb.reshape(1, -1))
    tails = jnp.concatenate([csg, csu], axis=-1).reshape(batch, tiles_per_seq, 2, 2 * d_ff)
    return delta, tails[:, -1]


def _ffn_down_sample_kernel(hg_ref, hu_ref, b0g_ref, b0u_ref, b1g_ref, b1u_ref, cwg_ref, cwu_ref,
                            cbg_ref, cbu_ref, wd_ref, x_ref, o_ref):
    @pl.when(pl.program_id(0) == 0)
    def _():
        o_ref[...] = x_ref[...]

    def conv(h_ref, b0_ref, b1_ref, cw_ref, cb_ref):
        cw = cw_ref[...]
        return cw[0:1] * b0_ref[...] + cw[1:2] * b1_ref[...] + cw[2:3] * h_ref[...] + cb_ref[...]

    hcg = conv(hg_ref, b0g_ref, b1g_ref, cwg_ref, cbg_ref)
    hcu = conv(hu_ref, b0u_ref, b1u_ref, cwu_ref, cbu_ref)
    act = (jax.nn.silu(hcg) * hcu).astype(BF16)
    o_ref[...] += jnp.dot(act, wd_ref[...], preferred_element_type=F32)


def _ffn_down_sample(h, state2, cw, cb, wd, x, *, tf):
    nb, two_f = h.shape
    d_ff = two_f // 2
    nf = d_ff // tf
    d = wd.shape[1]
    colblk = lambda off: pl.BlockSpec((nb, tf), lambda j: (0, j + off))
    return pl.pallas_call(
        _ffn_down_sample_kernel,
        grid=(nf,),
        in_specs=[
            colblk(0), colblk(nf),
            colblk(0), colblk(nf), colblk(2 * nf), colblk(3 * nf),
            pl.BlockSpec((3, tf), lambda j: (0, j)),
            pl.BlockSpec((3, tf), lambda j: (0, j + nf)),
            pl.BlockSpec((1, tf), lambda j: (0, j)),
            pl.BlockSpec((1, tf), lambda j: (0, j + nf)),
            pl.BlockSpec((tf, d), lambda j: (j, 0)),
            pl.BlockSpec((nb, d), lambda j: (0, 0)),
        ],
        out_specs=pl.BlockSpec((nb, d), lambda j: (0, 0)),
        out_shape=jax.ShapeDtypeStruct((nb, d), F32),
        compiler_params=_params(("arbitrary",), VMEM_MIB["small"]),
        name="ffn_down_sample",
    )(h, h, state2, state2, state2, state2, cw, cw, cb.reshape(1, two_f), cb.reshape(1, two_f), wd, x)


def _ple_sample_kernel(x_ref, nw_ref, w_ref, p_ref, wp_ref, o_ref, wpb_ref):
    x2 = x_ref[...]
    xn = _rmsnorm_rows(x2, nw_ref[...]).astype(BF16)
    wpb = wp_ref[...].astype(BF16)
    wpb_ref[...] = wpb
    gate = jax.nn.sigmoid(jnp.dot(xn, w_ref[...], preferred_element_type=F32))
    emb = jnp.dot(p_ref[...].astype(BF16), wpb, preferred_element_type=F32)
    o_ref[...] = x2 + gate * emb


def _ple_sample(x, nw, w, p, wp):
    m, d = x.shape
    kp = p.shape[1]
    full = lambda shape: pl.BlockSpec(shape, lambda i: (0, 0))
    return pl.pallas_call(
        _ple_sample_kernel,
        grid=(1,),
        in_specs=[full((m, d)), full((1, d)), full((d, d)), full((m, kp)), full((kp, d))],
        out_specs=[full((m, d)), full((kp, d))],
        out_shape=[jax.ShapeDtypeStruct((m, d), F32), jax.ShapeDtypeStruct((kp, d), BF16)],
        compiler_params=_params(("arbitrary",), VMEM_MIB["small"]),
        name="ple_sample",
    )(x, nw.reshape(1, d), w, p, wp)


def _ple_rows_kernel(x_ref, d_ref, nw_ref, w_ref, p_ref, wp_ref, o_ref):
    x2 = x_ref[...] + d_ref[...].astype(F32)
    xn = _rmsnorm_rows(x2, nw_ref[...]).astype(BF16)
    gate = jax.nn.sigmoid(jnp.dot(xn, w_ref[...], preferred_element_type=F32))
    emb = jnp.dot(p_ref[...].astype(BF16), wp_ref[...], preferred_element_type=F32)
    o_ref[...] = x2 + gate * emb


def _ple_rows(x, delta, nw, w, p, wp, *, tm):
    m, d = x.shape
    kp = p.shape[1]
    rows = lambda width: pl.BlockSpec((tm, width), lambda i: (i, 0))
    resident = lambda shape: pl.BlockSpec(shape, lambda i: (0, 0), pipeline_mode=pl.Buffered(1))
    return pl.pallas_call(
        _ple_rows_kernel,
        grid=(m // tm,),
        in_specs=[rows(d), rows(d), pl.BlockSpec((1, d), lambda i: (0, 0)), resident((d, d)),
                  rows(kp), resident((kp, d))],
        out_specs=rows(d),
        out_shape=jax.ShapeDtypeStruct((m, d), F32),
        compiler_params=_params(("parallel",), VMEM_MIB["ple"]),
        name="ple_prompt",
    )(x, delta, nw.reshape(1, d), w, p, wp)


def _layer_weights(i, attn_norm_w, w_in, q_norm_w, k_norm_w, attn_sinks, sgu_norm_w, sgu_w, sgu_b,
                   w_br_attn, w_br_gm, w_gate, b_gate, w_out, ffn_norm_w, w_up, conv_w, conv_b,
                   w_down, ple_norm_w, w_ple_gate, w_ple_proj):
    sinks = attn_sinks[i]
    gm_group_w = w_br_gm.shape[1] // GM_GROUPS
    return dict(
        attn_norm_w=attn_norm_w[i], w_in=w_in[i],
        qw2=jnp.tile(q_norm_w[i] * (HEAD_DIM ** -0.5), HEADS_PER_VREG).reshape(1, LANES),
        kw2=jnp.tile(k_norm_w[i], HEADS_PER_VREG).reshape(1, LANES),
        sinks=sinks,
        sinks_row=jnp.pad(sinks, (0, LANES - N_HEADS)).reshape(1, LANES),
        sgu_norm_w=sgu_norm_w[i], sgu_w=sgu_w[i], sgu_b_t=jnp.transpose(sgu_b[i]),
        sgu_w0_row=jnp.repeat(sgu_w[i][:, 0, 0], gm_group_w).reshape(1, -1),
        sgu_b0_row=jnp.repeat(sgu_b[i][:, 0], gm_group_w).reshape(1, -1),
        w_br_attn=w_br_attn[i], w_br_gm=w_br_gm[i],
        w_gate=w_gate[i], b_gate=b_gate[i], w_out=w_out[i],
        ffn_norm_w=ffn_norm_w[i], conv_w=conv_w[i], conv_b=conv_b[i],
        ple_norm_w=ple_norm_w[i], w_ple_proj=w_ple_proj[i],
        layer=i, w_up_all=w_up, w_down_all=w_down, w_ple_gate_all=w_ple_gate,
    )


def _prompt_mixers(x, lw, wb):
    batch, seq, d = x.shape
    x2d = x.reshape(batch * seq, d)
    attn_w = N_HEADS * HEAD_DIM
    kv_w = N_KV_HEADS * HEAD_DIM
    gm_w = lw["w_br_gm"].shape[0]
    proj, gates = _in_proj(x2d, lw["attn_norm_w"], wb["w_in"], wb["w_gate"], lw["b_gate"], tm=ROW_TILE)
    attn, k_new, v_new, sgu, sgu_state, wb["w_up"], wb["w_down"], wb["w_ple_gate"] = _mixers_prompt(
        proj, lw["qw2"], lw["kw2"], lw["sinks"], lw["sgu_norm_w"], lw["sgu_w"], lw["sgu_b_t"],
        lw["w_up_all"], lw["w_down_all"], lw["w_ple_gate_all"], lw["layer"],
        batch=batch, seq=seq, gu_off=attn_w + 2 * kv_w, gm_w=gm_w, tile=W_TILE)
    states = (k_new.reshape(batch, WINDOW, N_KV_HEADS, HEAD_DIM),
              v_new.reshape(batch, WINDOW, N_KV_HEADS, HEAD_DIM), sgu_state)
    return (attn, sgu, gates), states


def _prompt_tail(x, p, mixed, lw, wb):
    batch, seq, d = x.shape
    x2d = x.reshape(batch * seq, d)
    attn, sgu, gates = mixed
    x1, xn1 = _merge_out(attn, sgu, gates, x2d, wb["w_br_attn"], wb["w_br_gm"], wb["w_out"],
                         lw["ffn_norm_w"], tm=ROW_TILE)
    delta, conv_state = _ffn_prompt(xn1, wb["w_up"], wb["w_down"], lw["conv_w"], lw["conv_b"],
                                    batch=batch, seq=seq, tm=FFN_ROW_TILE)
    x3 = _ple_rows(x1, delta, lw["ple_norm_w"], wb["w_ple_gate"], p.reshape(batch * seq, -1),
                   wb["w_ple_proj"], tm=PLE_ROW_TILE)
    return x3.reshape(batch, seq, d), conv_state


def _sample_mixers(x, k_buf, v_buf, lw, wb):
    nb, t, d = x.shape
    x2d = x.reshape(nb * t, d)
    attn_w = N_HEADS * HEAD_DIM
    kv_w = N_KV_HEADS * HEAD_DIM
    gm_w = lw["w_br_gm"].shape[0]
    proj, wb["w_in"] = _norm_mm(x2d, lw["attn_norm_w"], lw["w_in"], None, act=None,
                                tn=W_TILE, tile=W_TILE, name="proj_sample")
    gates, wb["w_gate"] = _norm_mm(x2d, lw["attn_norm_w"], lw["w_gate"], lw["b_gate"], act="sigmoid",
                                   tn=2 * W_TILE, tile=W_TILE, name="gates_sample")
    q = proj[:, :attn_w]
    k = proj[:, attn_w:attn_w + kv_w]
    v = proj[:, attn_w + kv_w:attn_w + 2 * kv_w]
    gu = proj[:, attn_w + 2 * kv_w:attn_w + 2 * kv_w + gm_w]
    gv = proj[:, attn_w + 2 * kv_w + gm_w:]
    attn, k_new, v_new, wb["w_br_attn"], wb["w_br_gm"], wb["w_out"] = _attn_sample(
        q, k, v, k_buf.reshape(nb, WINDOW, kv_w), v_buf.reshape(nb, WINDOW, kv_w),
        lw["qw2"], lw["kw2"], lw["sinks_row"], lw["w_br_attn"], lw["w_br_gm"], lw["w_out"], group=SAMPLE_GROUP)
    sgu, sgu_state = _sgu_sample(gu, gv, lw["sgu_norm_w"], lw["sgu_w0_row"], lw["sgu_b0_row"])
    x1, _ = _merge_out(attn, sgu, gates, x2d, wb["w_br_attn"], wb["w_br_gm"], wb["w_out"],
                       lw["ffn_norm_w"], tm=nb)
    states = (k_new.reshape(nb, WINDOW, N_KV_HEADS, HEAD_DIM), v_new.reshape(nb, WINDOW, N_KV_HEADS, HEAD_DIM),
              sgu_state.reshape(nb, t, gm_w))
    return x1, states


def _sample_tail(x1, p, conv_buf, lw, wb):
    nb = x1.shape[0]
    h = _norm_mm_tiles(x1, lw["ffn_norm_w"], wb["w_up"], tiles_per_step=2, name="ffn_up_sample")
    x2 = _ffn_down_sample(h, conv_buf.reshape(nb, -1), lw["conv_w"], lw["conv_b"], wb["w_down"], x1, tf=W_TILE)
    x3, wb["w_ple_proj"] = _ple_sample(x2, lw["ple_norm_w"], wb["w_ple_gate"], p.reshape(nb, -1),
                                       lw["w_ple_proj"])
    conv_state = jnp.stack([conv_buf[:, 1, :], h], axis=1)
    return x3, conv_state


def kernel(x_prompt, x_sample, p_prompt, p_sample, state_attn_k, state_attn_v, state_conv, attn_norm_w, w_in, q_norm_w, k_norm_w, attn_sinks, sgu_norm_w, sgu_w, sgu_b, w_br_attn, w_br_gm, w_gate, b_gate, w_out, ffn_norm_w, w_up, conv_w, conv_b, w_down, ple_norm_w, w_ple_gate, w_ple_proj):
    depth = w_in.shape[0]
    xp, xs = x_prompt, x_sample
    outs = [[] for _ in range(8)]
    for i in range(depth):
        lw = _layer_weights(i, attn_norm_w, w_in, q_norm_w, k_norm_w, attn_sinks, sgu_norm_w, sgu_w, sgu_b,
                            w_br_attn, w_br_gm, w_gate, b_gate, w_out, ffn_norm_w, w_up, conv_w, conv_b,
                            w_down, ple_norm_w, w_ple_gate, w_ple_proj)
        wb = {}
        xs1, (ks, vs, gs) = _sample_mixers(xs, state_attn_k[i], state_attn_v[i], lw, wb)
        mixed, (kp, vp, gp) = _prompt_mixers(xp, lw, wb)
        xs3, cs = _sample_tail(xs1, p_sample[i], state_conv[i], lw, wb)
        xs = xs3.reshape(xs.shape)
        xp, cp = _prompt_tail(xp, p_prompt[i], mixed, lw, wb)
        for lst, val in zip(outs, (kp, vp, ks, vs, gp, gs, cp, cs)):
            lst.append(val)
    return (xp, xs) + tuple(jnp.stack(lst) for lst in outs)
```

```python
import functools

import jax
import jax.numpy as jnp
from jax import lax
from jax.experimental import pallas as pl
from jax.experimental.pallas import tpu as pltpu

F32 = jnp.float32
BF16 = jnp.bfloat16

HEAD_DIM = 64
N_HEADS = 16
N_KV_HEADS = 4
GQA_GROUP = N_HEADS // N_KV_HEADS
WINDOW = 128
CHUNK = 128
GM_GROUPS = 4
EPS = 1e-6
MASK_VALUE = -1e30
LANES = 128
SUBLANES = 8
HEADS_PER_VREG = LANES // HEAD_DIM
MIB = 1 << 20


def _alibi_slope(h):
    return float(2.0 ** (-8.0 * (h + 1) / N_HEADS))


ROW_TILE = 256
PLE_ROW_TILE = 512
FFN_ROW_TILE = 1024
W_TILE = 512
SAMPLE_GROUP = 8
VMEM_MIB = dict(stream=56, in_proj=56, mixers=56, attn_sample=48, merge_out=48, ffn=58,
                ffn_sample=48, ple=56, small=32)


def _params(semantics, vmem_mib):
    return pltpu.CompilerParams(dimension_semantics=semantics, vmem_limit_bytes=vmem_mib * MIB)


def _rmsnorm_rows(x, w):
    ms = jnp.mean(x * x, axis=-1, keepdims=True)
    return x * lax.rsqrt(ms + EPS) * w


def _norm_mm_kernel(*refs, has_bias, act):
    if has_bias:
        x_ref, nw_ref, w_ref, b_ref, o_ref, wb_ref, xn_ref = refs
    else:
        x_ref, nw_ref, w_ref, o_ref, wb_ref, xn_ref = refs

    @pl.when(pl.program_id(0) == 0)
    def _():
        xn_ref[...] = _rmsnorm_rows(x_ref[...], nw_ref[...]).astype(BF16)

    wb = w_ref[...].astype(BF16)
    tile = wb_ref.shape[2]
    for c in range(wb_ref.shape[0]):
        wb_ref[c] = wb[:, c * tile:(c + 1) * tile]
    acc = jnp.dot(xn_ref[...], wb, preferred_element_type=F32)
    if has_bias:
        acc = acc + b_ref[...]
    if act == "sigmoid":
        acc = jax.nn.sigmoid(acc)
    o_ref[...] = acc.astype(o_ref.dtype)


def _norm_mm(x, nw, w, bias, *, act, tn, tile, name):
    m, k = x.shape
    n = w.shape[1]
    per_step = tn // tile
    in_specs = [
        pl.BlockSpec((m, k), lambda j: (0, 0)),
        pl.BlockSpec((1, k), lambda j: (0, 0)),
        pl.BlockSpec((k, tn), lambda j: (0, j)),
    ]
    args = [x, nw.reshape(1, k), w]
    if bias is not None:
        in_specs.append(pl.BlockSpec((1, tn), lambda j: (0, j)))
        args.append(bias.reshape(1, n))
    return pl.pallas_call(
        functools.partial(_norm_mm_kernel, has_bias=bias is not None, act=act),
        grid=(n // tn,),
        in_specs=in_specs,
        out_specs=[pl.BlockSpec((m, tn), lambda j: (0, j)),
                   pl.BlockSpec((per_step, k, tile), lambda j: (j, 0, 0))],
        out_shape=[jax.ShapeDtypeStruct((m, n), F32), jax.ShapeDtypeStruct((n // tile, k, tile), BF16)],
        scratch_shapes=[pltpu.VMEM((m, k), BF16)],
        compiler_params=_params(("arbitrary",), VMEM_MIB["stream"]),
        name=name,
    )(*args)


def _norm_mm_tiles_kernel(x_ref, nw_ref, w_ref, o_ref, xn_ref):
    @pl.when(pl.program_id(0) == 0)
    def _():
        xn_ref[...] = _rmsnorm_rows(x_ref[...], nw_ref[...]).astype(BF16)

    tile = w_ref.shape[2]
    for c in range(w_ref.shape[0]):
        o_ref[:, c * tile:(c + 1) * tile] = jnp.dot(xn_ref[...], w_ref[c], preferred_element_type=F32)


def _norm_mm_tiles(x, nw, wt, *, tiles_per_step, name):
    m, k = x.shape
    n_tiles, _, tile = wt.shape
    tn = tiles_per_step * tile
    return pl.pallas_call(
        _norm_mm_tiles_kernel,
        grid=(n_tiles // tiles_per_step,),
        in_specs=[
            pl.BlockSpec((m, k), lambda j: (0, 0)),
            pl.BlockSpec((1, k), lambda j: (0, 0)),
            pl.BlockSpec((tiles_per_step, k, tile), lambda j: (j, 0, 0)),
        ],
        out_specs=pl.BlockSpec((m, tn), lambda j: (0, j)),
        out_shape=jax.ShapeDtypeStruct((m, n_tiles * tile), F32),
        scratch_shapes=[pltpu.VMEM((m, k), BF16)],
        compiler_params=_params(("arbitrary",), VMEM_MIB["ffn_sample"]),
        name=name,
    )(x, nw.reshape(1, k), wt)


def _in_proj_kernel(x_ref, nw_ref, win_ref, wg_ref, bg_ref, proj_ref, gates_ref, xn_ref):
    xn_ref[...] = _rmsnorm_rows(x_ref[...], nw_ref[...]).astype(BF16)
    tn = win_ref.shape[2]
    for c in range(win_ref.shape[0]):
        acc = jnp.dot(xn_ref[...], win_ref[c], preferred_element_type=F32)
        proj_ref[:, c * tn:(c + 1) * tn] = acc.astype(proj_ref.dtype)
    for c in range(wg_ref.shape[0]):
        cols = slice(c * tn, (c + 1) * tn)
        acc = jnp.dot(xn_ref[...], wg_ref[c], preferred_element_type=F32)
        gates_ref[:, cols] = jax.nn.sigmoid(acc + bg_ref[:, cols]).astype(gates_ref.dtype)


def _in_proj(x, nw, w_in, w_gate, b_gate, *, tm):
    m, k = x.shape
    n_in, _, tn = w_in.shape
    n_g = w_gate.shape[0]
    rows = lambda width: pl.BlockSpec((tm, width), lambda i: (i, 0))
    resident = lambda a: pl.BlockSpec(a.shape, lambda i: (0,) * a.ndim, pipeline_mode=pl.Buffered(1))
    bg = b_gate.reshape(1, -1)
    return pl.pallas_call(
        _in_proj_kernel,
        grid=(m // tm,),
        in_specs=[rows(k), pl.BlockSpec((1, k), lambda i: (0, 0)), resident(w_in), resident(w_gate),
                  pl.BlockSpec(bg.shape, lambda i: (0, 0))],
        out_specs=[rows(n_in * tn), rows(n_g * tn)],
        out_shape=[
            jax.ShapeDtypeStruct((m, n_in * tn), BF16),
            jax.ShapeDtypeStruct((m, n_g * tn), BF16),
        ],
        scratch_shapes=[pltpu.VMEM((tm, k), BF16)],
        compiler_params=_params(("parallel",), VMEM_MIB["in_proj"]),
        name="in_proj",
    )(x, nw.reshape(1, k), w_in, w_gate, bg)


def _head_group_matrix():
    r = lax.broadcasted_iota(jnp.int32, (LANES, LANES), 0)
    c = lax.broadcasted_iota(jnp.int32, (LANES, LANES), 1)
    return jnp.where((r >> 6) == (c >> 6), 1.0 / HEAD_DIM, 0.0).astype(BF16)


def _head_rmsnorm(xcol, w, gmat):
    ms = jnp.dot((xcol * xcol).astype(BF16), gmat, preferred_element_type=F32)
    return xcol * lax.rsqrt(ms + EPS) * w


def _attn_prompt_kernel(sink_ref, q_ref, kc_ref, kp_ref, vc_ref, vp_ref, qw_ref, kw_ref,
                        o_ref, kn_ref, vn_ref):
    c = pl.program_id(1)
    blk = WINDOW
    nblk = q_ref.shape[0] // blk
    gmat = _head_group_matrix()
    lane = lax.broadcasted_iota(jnp.int32, (1, LANES), 1)
    low = lane < HEAD_DIM

    kw = kw_ref[...]
    qw = qw_ref[...]
    kall = jnp.concatenate([kp_ref[...], kc_ref[...]], axis=0).astype(F32)
    vall = jnp.concatenate([vp_ref[...], vc_ref[...]], axis=0).astype(F32)
    ncol = kall.shape[1] // LANES
    kcat = [_head_rmsnorm(kall[:, p * LANES:(p + 1) * LANES], kw, gmat) for p in range(ncol)]
    vcat = [vall[:, p * LANES:(p + 1) * LANES] for p in range(ncol)]
    kn_ref[0] = jnp.concatenate([kc_[nblk * blk:] for kc_ in kcat], axis=1)
    vn_ref[0] = vall[nblk * blk:]
    krol = [pltpu.roll(kcat[p], HEAD_DIM, 1) for p in range(ncol)]
    vrol = [pltpu.roll(vcat[p], HEAD_DIM, 1) for p in range(ncol)]

    qi = lax.broadcasted_iota(jnp.int32, (blk, 2 * blk), 0)
    kj = lax.broadcasted_iota(jnp.int32, (blk, 2 * blk), 1)
    dist = blk + qi - kj
    in_window = (dist >= 0) & (dist <= WINDOW)
    distf = dist.astype(F32)

    q = q_ref[...].astype(F32)
    qn = [_head_rmsnorm(q[:, p * LANES:(p + 1) * LANES], qw, gmat)
          for p in range(N_HEADS // HEADS_PER_VREG)]

    for g in range(N_KV_HEADS):
        p, half = divmod(g, HEADS_PER_VREG)
        if half == 0:
            kd_all = jnp.where(low, kcat[p], krol[p]).astype(BF16)
            vd_all = jnp.where(low, vcat[p], vrol[p]).astype(BF16)
        else:
            kd_all = jnp.where(low, krol[p], kcat[p]).astype(BF16)
            vd_all = jnp.where(low, vrol[p], vcat[p]).astype(BF16)
        for b in range(nblk):
            rows = slice(b * blk, (b + 1) * blk)
            kd = kd_all[b * blk:(b + 2) * blk]
            vd = vd_all[b * blk:(b + 2) * blk]
            valid = in_window if b > 0 else in_window & ((kj >= blk) | (c > 0))
            qs = []
            for hl in range(GQA_GROUP):
                h = g * GQA_GROUP + hl
                pc, hh = divmod(h, HEADS_PER_VREG)
                keep = low if hh == 0 else jnp.logical_not(low)
                qs.append(jnp.where(keep, qn[pc][rows], 0.0).astype(BF16))
            qstack = jnp.concatenate(qs, axis=0)
            s_all = lax.dot_general(qstack, kd, (((1,), (1,)), ((), ())), preferred_element_type=F32)
            ps = []
            for hl in range(GQA_GROUP):
                h = g * GQA_GROUP + hl
                s = s_all[hl * blk:(hl + 1) * blk] - _alibi_slope(h) * distf
                s = jnp.where(valid, s, MASK_VALUE)
                sink = sink_ref[h]
                mx = jnp.maximum(jnp.max(s, axis=-1, keepdims=True), sink)
                e = jnp.exp(s - mx)
                den = jnp.sum(e, axis=-1, keepdims=True) + jnp.exp(sink - mx)
                ps.append((e / den).astype(BF16))
            pstack = jnp.concatenate(ps, axis=0)
            o_all = jnp.dot(pstack, vd, preferred_element_type=F32)
            for pair in range(GQA_GROUP // HEADS_PER_VREG):
                pc = g * (GQA_GROUP // HEADS_PER_VREG) + pair
                even = o_all[(2 * pair) * blk:(2 * pair + 1) * blk]
                odd = o_all[(2 * pair + 1) * blk:(2 * pair + 2) * blk]
                o_ref[rows, pc * LANES:(pc + 1) * LANES] = jnp.where(low, even, odd).astype(o_ref.dtype)


def _layernorm_rows(x, w):
    mu = jnp.mean(x, axis=-1, keepdims=True)
    xc = x - mu
    return xc * lax.rsqrt(jnp.mean(xc * xc, axis=-1, keepdims=True) + EPS) * w


GELU_C = 0.7978845608028654
GELU_A = 0.044715


def _gelu_tanh(x):
    return x * jax.nn.sigmoid((2.0 * GELU_C) * (x + GELU_A * (x * x * x)))


def _sgu_prompt_kernel(gu0_ref, gu1_ref, gv0_ref, gv1_ref, nw_ref, ws_ref, bs_ref, o_ref, st_ref):
    rows = gu0_ref.shape[0]
    gv = jnp.concatenate([gv0_ref[...], gv1_ref[...]], axis=1).astype(F32)
    vn = _layernorm_rows(_gelu_tanh(gv), nw_ref[...])
    st_ref[0] = vn[rows - CHUNK:]
    vb = vn.astype(BF16)
    r = lax.broadcasted_iota(jnp.int32, (CHUNK, CHUNK), 0)
    c = lax.broadcasted_iota(jnp.int32, (CHUNK, CHUNK), 1)
    causal = r >= c
    gw = vb.shape[1] // GM_GROUPS
    half = gu0_ref.shape[1]
    for g in range(GM_GROUPS):
        w = jnp.where(causal, ws_ref[g], 0.0).astype(BF16)
        src = gu0_ref if g * gw < half else gu1_ref
        off = g * gw - (0 if g * gw < half else half)
        for ch in range(rows // CHUNK):
            rs = slice(ch * CHUNK, (ch + 1) * CHUNK)
            mix = jnp.dot(w, vb[rs, g * gw:(g + 1) * gw], preferred_element_type=F32) + bs_ref[:, g:g + 1]
            u = _gelu_tanh(src[rs, off:off + gw].astype(F32))
            o_ref[rs, g * gw:(g + 1) * gw] = (u * mix).astype(o_ref.dtype)


N_ATTN_IN, N_ATTN_OUT, N_SGU_IN, N_SGU_OUT, N_CAST = 8, 3, 7, 2, 3
MIXER_BLOCKS = 4


def _mixers_prompt_kernel(*refs):
    attn_in = refs[:N_ATTN_IN]
    sgu_in = refs[N_ATTN_IN:N_ATTN_IN + N_SGU_IN]
    cast_in = refs[N_ATTN_IN + N_SGU_IN:N_ATTN_IN + N_SGU_IN + N_CAST]
    outs = refs[N_ATTN_IN + N_SGU_IN + N_CAST:]
    attn_out = outs[:N_ATTN_OUT]
    sgu_out = outs[N_ATTN_OUT:N_ATTN_OUT + N_SGU_OUT]
    wup_ref, wdn_ref, wpg_ref = cast_in
    wupb_ref, wdnb_ref, wpgb_ref = outs[N_ATTN_OUT + N_SGU_OUT:]
    tile = wupb_ref.shape[2]
    for c in range(wupb_ref.shape[0]):
        wupb_ref[c] = wup_ref[:, c * tile:(c + 1) * tile].astype(BF16)
    wdnb_ref[...] = wdn_ref[...].astype(BF16)
    wpgb_ref[...] = wpg_ref[...].astype(BF16)
    _sgu_prompt_kernel(*sgu_in, *sgu_out)
    _attn_prompt_kernel(*attn_in, *attn_out)


def _mixers_prompt(proj, qw2, kw2, sinks, nw, ws, bs_t, w_up, w_down, w_pg, layer, *,
                   batch, seq, gu_off, gm_w, tile):
    rows = MIXER_BLOCKS * WINDOW
    nb = seq // rows
    steps = batch * nb
    flat = lambda b, c: b * nb + c
    _, k_up, n_up = w_up.shape
    slab_in = lambda a: pl.BlockSpec((None, a.shape[1] // steps, a.shape[2]), lambda b, c: (layer, flat(b, c), 0))
    slab_out = lambda a: pl.BlockSpec((a.shape[1] // steps, a.shape[2]), lambda b, c: (flat(b, c), 0))
    attn_w = N_HEADS * HEAD_DIM
    kv_w = N_KV_HEADS * HEAD_DIM
    kcol = attn_w // kv_w
    vcol = kcol + 1
    half = gm_w // 2
    b0 = gu_off // half
    cur = lambda col: (lambda b, c: (flat(b, c), col))
    prev = lambda col: (lambda b, c: (jnp.maximum(flat(b, c) * MIXER_BLOCKS - 1, 0), col))
    const = lambda shape: pl.BlockSpec(shape, lambda b, c: (0,) * len(shape))
    state = lambda width: pl.BlockSpec((1, WINDOW, width), lambda b, c: (b, 0, 0))
    return pl.pallas_call(
        _mixers_prompt_kernel,
        grid=(batch, nb),
        in_specs=[
            pl.BlockSpec(memory_space=pltpu.SMEM),
            pl.BlockSpec((rows, attn_w), cur(0)),
            pl.BlockSpec((rows, kv_w), cur(kcol)),
            pl.BlockSpec((WINDOW, kv_w), prev(kcol)),
            pl.BlockSpec((rows, kv_w), cur(vcol)),
            pl.BlockSpec((WINDOW, kv_w), prev(vcol)),
            const((1, LANES)), const((1, LANES)),
            pl.BlockSpec((rows, half), cur(b0)),
            pl.BlockSpec((rows, half), cur(b0 + 1)),
            pl.BlockSpec((rows, half), cur(b0 + 2)),
            pl.BlockSpec((rows, half), cur(b0 + 3)),
            const((1, gm_w)), const((GM_GROUPS, CHUNK, CHUNK)), const((CHUNK, GM_GROUPS)),
            slab_in(w_up), slab_in(w_down), slab_in(w_pg),
        ],
        out_specs=[
            pl.BlockSpec((rows, attn_w), cur(0)), state(kv_w), state(kv_w),
            pl.BlockSpec((rows, gm_w), cur(0)), state(gm_w),
            pl.BlockSpec((n_up // tile, k_up // steps, tile), lambda b, c: (0, flat(b, c), 0)),
            slab_out(w_down), slab_out(w_pg),
        ],
        out_shape=[
            jax.ShapeDtypeStruct((batch * seq, attn_w), BF16),
            jax.ShapeDtypeStruct((batch, WINDOW, kv_w), F32),
            jax.ShapeDtypeStruct((batch, WINDOW, kv_w), F32),
            jax.ShapeDtypeStruct((batch * seq, gm_w), BF16),
            jax.ShapeDtypeStruct((batch, CHUNK, gm_w), F32),
            jax.ShapeDtypeStruct((n_up // tile, k_up, tile), BF16),
            jax.ShapeDtypeStruct(w_down.shape[1:], BF16),
            jax.ShapeDtypeStruct(w_pg.shape[1:], BF16),
        ],
        compiler_params=_params(("arbitrary", "arbitrary"), VMEM_MIB["mixers"]),
        name="mixers_prompt",
    )(sinks, proj, proj, proj, proj, proj, qw2, kw2, proj, proj, proj, proj, nw.reshape(1, gm_w), ws, bs_t,
      w_up, w_down, w_pg)


def _split_dot(x, m):
    hi = x.astype(BF16)
    lo = (x - hi.astype(F32)).astype(BF16)
    return jnp.dot(hi, m, preferred_element_type=F32) + jnp.dot(lo, m, preferred_element_type=F32)


def _attn_sample_kernel(q_ref, kn_ref, vn_ref, kb_ref, vb_ref, qw_ref, kw_ref, sink_ref,
                        wa_ref, wg_ref, wo_ref, o_ref, ko_ref, vo_ref, wab_ref, wgb_ref, wob_ref):
    wab_ref[...] = wa_ref[...].astype(BF16)
    wgb_ref[...] = wg_ref[...].astype(BF16)
    wob_ref[...] = wo_ref[...].astype(BF16)
    nsamp = q_ref.shape[0]
    attn_w = q_ref.shape[1]
    kv_w = kn_ref.shape[1]
    gmat = _head_group_matrix()
    q = q_ref[...].astype(F32)
    kn = kn_ref[...].astype(F32)
    qn = jnp.concatenate(
        [_head_rmsnorm(q[:, p * LANES:(p + 1) * LANES], qw_ref[...], gmat)
         for p in range(attn_w // LANES)], axis=1)
    knn = jnp.concatenate(
        [_head_rmsnorm(kn[:, p * LANES:(p + 1) * LANES], kw_ref[...], gmat)
         for p in range(kv_w // LANES)], axis=1)
    vnn = vn_ref[...].astype(F32)

    ec = lax.broadcasted_iota(jnp.int32, (kv_w, attn_w), 0)
    el = lax.broadcasted_iota(jnp.int32, (kv_w, attn_w), 1)
    expand = (((ec >> 6) == (el >> 8)) & ((ec & 63) == (el & 63))).astype(BF16)
    gl = lax.broadcasted_iota(jnp.int32, (attn_w, LANES), 0)
    gh = lax.broadcasted_iota(jnp.int32, (attn_w, LANES), 1)
    hsum = ((gl >> 6) == gh).astype(BF16)
    tl = lax.broadcasted_iota(jnp.int32, (LANES, attn_w), 1)
    th = lax.broadcasted_iota(jnp.int32, (LANES, attn_w), 0)
    hexp = ((tl >> 6) == th).astype(BF16)

    nkeys = WINDOW + SUBLANES
    row = lax.broadcasted_iota(jnp.int32, (nkeys, LANES), 0)
    head = lax.broadcasted_iota(jnp.int32, (nkeys, LANES), 1)
    slope = jnp.exp2(-8.0 * (head + 1).astype(F32) / N_HEADS)
    dist = (WINDOW - row).astype(F32)
    key_ok = row <= WINDOW
    srow = lax.broadcasted_iota(jnp.int32, (WINDOW, kv_w), 0)
    sinks = sink_ref[...]

    for s in range(nsamp):
        kb = kb_ref[s]
        vb = vb_ref[s]
        knew = knn[s:s + 1]
        vnew = vnn[s:s + 1]
        ko_ref[s] = jnp.where(srow == WINDOW - 1, knew, pltpu.roll(kb, WINDOW - 1, 0))
        vo_ref[s] = jnp.where(srow == WINDOW - 1, vnew, pltpu.roll(vb, WINDOW - 1, 0))
        kk = jnp.concatenate([kb, jnp.broadcast_to(knew, (SUBLANES, kv_w))], axis=0)
        vv = jnp.concatenate([vb, jnp.broadcast_to(vnew, (SUBLANES, kv_w))], axis=0)
        kexp = jnp.dot(kk.astype(BF16), expand, preferred_element_type=F32)
        vexp = jnp.dot(vv.astype(BF16), expand, preferred_element_type=F32)
        sc = _split_dot(kexp * qn[s:s + 1], hsum)
        sc = sc - slope * dist
        sc = jnp.where(key_ok, sc, MASK_VALUE)
        mx = jnp.maximum(jnp.max(sc, axis=0, keepdims=True), sinks)
        e = jnp.exp(sc - mx)
        den = jnp.sum(e, axis=0, keepdims=True) + jnp.exp(sinks - mx)
        pexp = jnp.dot((e / den).astype(BF16), hexp, preferred_element_type=F32)
        o_ref[s:s + 1, :] = jnp.sum(pexp * vexp, axis=0, keepdims=True)


def _attn_sample(proj, kbuf, vbuf, qw2, kw2, sinks_row, wa, wg, wo, *, group):
    nb = proj.shape[0]
    attn_w = N_HEADS * HEAD_DIM
    kv_w = N_KV_HEADS * HEAD_DIM
    steps = nb // group
    row = lambda width, col=0: pl.BlockSpec((group, width), lambda i: (i, col))
    buf = pl.BlockSpec((group, WINDOW, kv_w), lambda i: (i, 0, 0))
    vec = pl.BlockSpec((1, LANES), lambda i: (0, 0))
    slab = lambda w: pl.BlockSpec((w.shape[0] // steps, w.shape[1]), lambda i: (i, 0))
    return pl.pallas_call(
        _attn_sample_kernel,
        grid=(steps,),
        in_specs=[row(attn_w), row(kv_w, attn_w // kv_w), row(kv_w, attn_w // kv_w + 1), buf, buf,
                  vec, vec, vec, slab(wa), slab(wg), slab(wo)],
        out_specs=[row(attn_w), buf, buf, slab(wa), slab(wg), slab(wo)],
        out_shape=[
            jax.ShapeDtypeStruct((nb, attn_w), F32),
            jax.ShapeDtypeStruct((nb, WINDOW, kv_w), F32),
            jax.ShapeDtypeStruct((nb, WINDOW, kv_w), F32),
            jax.ShapeDtypeStruct(wa.shape, BF16),
            jax.ShapeDtypeStruct(wg.shape, BF16),
            jax.ShapeDtypeStruct(wo.shape, BF16),
        ],
        compiler_params=_params(("parallel",), VMEM_MIB["attn_sample"]),
        name="attn_sample",
    )(proj, proj, proj, kbuf, vbuf, qw2, kw2, sinks_row, wa, wg, wo)


def _sgu_sample_kernel(proj_ref, nw_ref, w0_ref, b0_ref, o_ref, st_ref, *, gu_off):
    gm_w = o_ref.shape[1]
    gu = proj_ref[:, gu_off:gu_off + gm_w]
    gv = proj_ref[:, gu_off + gm_w:gu_off + 2 * gm_w]
    vn = _layernorm_rows(_gelu_tanh(gv), nw_ref[...])
    st_ref[...] = vn
    mix = w0_ref[...] * vn + b0_ref[...]
    o_ref[...] = _gelu_tanh(gu) * mix


def _sgu_sample(proj, nw, w0_row, b0_row, *, gu_off, gm_w):
    nb = proj.shape[0]
    full = pl.BlockSpec((nb, gm_w), lambda i: (0, 0))
    vec = pl.BlockSpec((1, gm_w), lambda i: (0, 0))
    return pl.pallas_call(
        functools.partial(_sgu_sample_kernel, gu_off=gu_off),
        grid=(1,),
        in_specs=[pl.BlockSpec(proj.shape, lambda i: (0, 0)), vec, vec, vec],
        out_specs=[full, full],
        out_shape=[jax.ShapeDtypeStruct((nb, gm_w), F32), jax.ShapeDtypeStruct((nb, gm_w), F32)],
        name="sgu_sample",
    )(proj, nw.reshape(1, gm_w), w0_row, b0_row)


def _merge_out_kernel(a_ref, s_ref, ga_ref, gb_ref, x_ref, wa_ref, wg_ref, wo_ref, nw_ref,
                      x1_ref, xn1_ref):
    a = jnp.dot(a_ref[...].astype(BF16), wa_ref[...], preferred_element_type=F32)
    m = jnp.dot(s_ref[...].astype(BF16), wg_ref[...], preferred_element_type=F32)
    merged = (gb_ref[...].astype(F32) * a + ga_ref[...].astype(F32) * m).astype(BF16)
    x1 = x_ref[...] + jnp.dot(merged, wo_ref[...], preferred_element_type=F32)
    x1_ref[...] = x1
    xn1_ref[...] = _rmsnorm_rows(x1, nw_ref[...]).astype(BF16)


def _merge_out(attn, sgu, gates, x, wa, wg, wo, nw, *, tm):
    m, ka = attn.shape
    kg = sgu.shape[1]
    d = wo.shape[1]
    rows = lambda width, col=0: pl.BlockSpec((tm, width), lambda i: (i, col))
    resident = lambda shape: pl.BlockSpec(shape, lambda i: (0, 0), pipeline_mode=pl.Buffered(1))
    return pl.pallas_call(
        _merge_out_kernel,
        grid=(m // tm,),
        in_specs=[
            rows(ka), rows(kg), rows(d, 0), rows(d, 1), rows(d),
            resident((ka, d)), resident((kg, d)), resident((d, d)),
            pl.BlockSpec((1, d), lambda i: (0, 0)),
        ],
        out_specs=[rows(d), rows(d)],
        out_shape=[jax.ShapeDtypeStruct((m, d), F32), jax.ShapeDtypeStruct((m, d), BF16)],
        compiler_params=_params(("parallel",), VMEM_MIB["merge_out"]),
        name="merge_out",
    )(attn, sgu, gates, gates, x, wa, wg, wo, nw.reshape(1, d))


PAD = SUBLANES
FFN_ROW_BLOCKS = 4


def _ffn_step(t, xn_ref, wug_ref, wuu_ref, wd_ref, cwg_ref, cwu_ref, cbg_ref, cbu_ref,
              csg_ref, csu_ref, h_new, h_old, carry_ref, acc_ref, *, nf, tiles_per_seq):
    tm = xn_ref.shape[0]
    tf = wug_ref.shape[1]
    rb = tm // FFN_ROW_BLOCKS
    tp = jnp.maximum(t - 1, 0)
    ip = tp // nf
    jp = tp % nf
    first = (ip % tiles_per_seq) == 0
    h_old[0:PAD, :] = jnp.where(first, 0.0, carry_ref[jp])
    carry_ref[jp] = h_old[tm:tm + PAD, :]
    cw = jnp.concatenate([cwg_ref[...], cwu_ref[...]], axis=1)
    cb = jnp.concatenate([cbg_ref[...], cbu_ref[...]], axis=1)
    wug = wug_ref[...]
    wuu = wuu_ref[...]
    wd = wd_ref[...]
    for r in range(FFN_ROW_BLOCKS):
        lo = r * rb
        xr = xn_ref[lo:lo + rb, :]
        h_new[PAD + lo:PAD + lo + rb, 0:tf] = jnp.dot(xr, wug, preferred_element_type=F32)
        h_new[PAD + lo:PAD + lo + rb, tf:2 * tf] = jnp.dot(xr, wuu, preferred_element_type=F32)
        hp = h_old[lo:lo + rb + PAD, :]
        hc = (cw[0:1] * pltpu.roll(hp, 2, 0)[PAD:] + cw[1:2] * pltpu.roll(hp, 1, 0)[PAD:]
              + cw[2:3] * hp[PAD:] + cb)
        act = (jax.nn.silu(hc[:, 0:tf]) * hc[:, tf:2 * tf]).astype(BF16)
        acc_ref[lo:lo + rb, :] += jnp.dot(act, wd, preferred_element_type=F32)
    csg_ref[0] = h_old[PAD + tm - 2:PAD + tm, 0:tf]
    csu_ref[0] = h_old[PAD + tm - 2:PAD + tm, tf:2 * tf]


def _ffn_kernel(xn_ref, wug_ref, wuu_ref, wd_ref, cwg_ref, cwu_ref, cbg_ref, cbu_ref,
                o_ref, csg_ref, csu_ref, ha_ref, hb_ref, carry_ref, acc_ref, *, nf, tiles_per_seq):
    t = pl.program_id(0)
    step = functools.partial(_ffn_step, t, xn_ref, wug_ref, wuu_ref, wd_ref, cwg_ref, cwu_ref,
                             cbg_ref, cbu_ref, csg_ref, csu_ref, carry_ref=carry_ref, acc_ref=acc_ref,
                             nf=nf, tiles_per_seq=tiles_per_seq)

    @pl.when(t == 0)
    def _():
        hb_ref[...] = jnp.zeros_like(hb_ref)
        carry_ref[...] = jnp.zeros_like(carry_ref)

    @pl.when(jnp.maximum(t - 1, 0) % nf == 0)
    def _():
        acc_ref[...] = jnp.zeros_like(acc_ref)

    @pl.when(t % 2 == 0)
    def _():
        step(h_new=ha_ref, h_old=hb_ref)

    @pl.when(t % 2 == 1)
    def _():
        step(h_new=hb_ref, h_old=ha_ref)

    @pl.when(jnp.maximum(t - 1, 0) % nf == nf - 1)
    def _():
        o_ref[...] = acc_ref[...].astype(o_ref.dtype)


def _ffn_prompt(xn, w_up, w_down, cw, cb, *, batch, seq, tm):
    m, k = xn.shape
    d_ff, d = w_down.shape
    tf = w_up.shape[2]
    nf = d_ff // tf
    nm = m // tm
    tiles_per_seq = seq // tm
    prev = lambda t: jnp.maximum(t - 1, 0)
    up_tile = lambda off: (lambda t: (off + t % nf, 0, 0))
    dn_col = lambda off: (lambda t: (0, off + prev(t) % nf))
    state = lambda t: (prev(t) // nf, 0, prev(t) % nf)
    delta, csg, csu = pl.pallas_call(
        functools.partial(_ffn_kernel, nf=nf, tiles_per_seq=tiles_per_seq),
        grid=(nm * nf + 1,),
        in_specs=[
            pl.BlockSpec((tm, k), lambda t: (jnp.minimum(t // nf, nm - 1), 0)),
            pl.BlockSpec((None, k, tf), up_tile(0)),
            pl.BlockSpec((None, k, tf), up_tile(nf)),
            pl.BlockSpec((tf, d), lambda t: (prev(t) % nf, 0)),
            pl.BlockSpec((3, tf), dn_col(0)),
            pl.BlockSpec((3, tf), dn_col(nf)),
            pl.BlockSpec((1, tf), dn_col(0)),
            pl.BlockSpec((1, tf), dn_col(nf)),
        ],
        out_specs=[
            pl.BlockSpec((tm, d), lambda t: (prev(t) // nf, 0)),
            pl.BlockSpec((1, 2, tf), state),
            pl.BlockSpec((1, 2, tf), state),
        ],
        out_shape=[
            jax.ShapeDtypeStruct((m, d), BF16),
            jax.ShapeDtypeStruct((nm, 2, d_ff), F32),
            jax.ShapeDtypeStruct((nm, 2, d_ff), F32),
        ],
        scratch_shapes=[
            pltpu.VMEM((tm + PAD, 2 * tf), F32),
            pltpu.VMEM((tm + PAD, 2 * tf), F32),
            pltpu.VMEM((nf, PAD, 2 * tf), F32),
            pltpu.VMEM((tm, d), F32),
        ],
        compiler_params=_params(("arbitrary",), VMEM_MIB["ffn"]),
        name="ffn_prompt",
    )(xn, w_up, w_up, w_down, cw, cw, cb.reshape(1, -1), cb.reshape(1, -1))
    tails = jnp.concatenate([csg, csu], axis=-1).reshape(batch, tiles_per_seq, 2, 2 * d_ff)
    return delta, tails[:, -1]


def _ffn_down_sample_kernel(hg_ref, hu_ref, b0g_ref, b0u_ref, b1g_ref, b1u_ref, cwg_ref, cwu_ref,
                            cbg_ref, cbu_ref, wd_ref, x_ref, o_ref):
    @pl.when(pl.program_id(0) == 0)
    def _():
        o_ref[...] = x_ref[...]

    def conv(h_ref, b0_ref, b1_ref, cw_ref, cb_ref):
        cw = cw_ref[...]
        return cw[0:1] * b0_ref[...] + cw[1:2] * b1_ref[...] + cw[2:3] * h_ref[...] + cb_ref[...]

    hcg = conv(hg_ref, b0g_ref, b1g_ref, cwg_ref, cbg_ref)
    hcu = conv(hu_ref, b0u_ref, b1u_ref, cwu_ref, cbu_ref)
    act = (jax.nn.silu(hcg) * hcu).astype(BF16)
    o_ref[...] += jnp.dot(act, wd_ref[...], preferred_element_type=F32)


def _ffn_down_sample(h, state2, cw, cb, wd, x, *, tf):
    nb, two_f = h.shape
    d_ff = two_f // 2
    nf = d_ff // tf
    d = wd.shape[1]
    colblk = lambda off: pl.BlockSpec((nb, tf), lambda j: (0, j + off))
    return pl.pallas_call(
        _ffn_down_sample_kernel,
        grid=(nf,),
        in_specs=[
            colblk(0), colblk(nf),
            colblk(0), colblk(nf), colblk(2 * nf), colblk(3 * nf),
            pl.BlockSpec((3, tf), lambda j: (0, j)),
            pl.BlockSpec((3, tf), lambda j: (0, j + nf)),
            pl.BlockSpec((1, tf), lambda j: (0, j)),
            pl.BlockSpec((1, tf), lambda j: (0, j + nf)),
            pl.BlockSpec((tf, d), lambda j: (j, 0)),
            pl.BlockSpec((nb, d), lambda j: (0, 0)),
        ],
        out_specs=pl.BlockSpec((nb, d), lambda j: (0, 0)),
        out_shape=jax.ShapeDtypeStruct((nb, d), F32),
        compiler_params=_params(("arbitrary",), VMEM_MIB["small"]),
        name="ffn_down_sample",
    )(h, h, state2, state2, state2, state2, cw, cw, cb.reshape(1, two_f), cb.reshape(1, two_f), wd, x)


def _ple_sample_kernel(x_ref, nw_ref, w_ref, p_ref, wp_ref, o_ref, wpb_ref):
    x2 = x_ref[...]
    xn = _rmsnorm_rows(x2, nw_ref[...]).astype(BF16)
    wpb = wp_ref[...].astype(BF16)
    wpb_ref[...] = wpb
    gate = jax.nn.sigmoid(jnp.dot(xn, w_ref[...], preferred_element_type=F32))
    emb = jnp.dot(p_ref[...].astype(BF16), wpb, preferred_element_type=F32)
    o_ref[...] = x2 + gate * emb


def _ple_sample(x, nw, w, p, wp):
    m, d = x.shape
    kp = p.shape[1]
    full = lambda shape: pl.BlockSpec(shape, lambda i: (0, 0))
    return pl.pallas_call(
        _ple_sample_kernel,
        grid=(1,),
        in_specs=[full((m, d)), full((1, d)), full((d, d)), full((m, kp)), full((kp, d))],
        out_specs=[full((m, d)), full((kp, d))],
        out_shape=[jax.ShapeDtypeStruct((m, d), F32), jax.ShapeDtypeStruct((kp, d), BF16)],
        compiler_params=_params(("arbitrary",), VMEM_MIB["small"]),
        name="ple_sample",
    )(x, nw.reshape(1, d), w, p, wp)


def _ple_rows_kernel(x_ref, d_ref, nw_ref, w_ref, p_ref, wp_ref, o_ref):
    x2 = x_ref[...] + d_ref[...].astype(F32)
    xn = _rmsnorm_rows(x2, nw_ref[...]).astype(BF16)
    gate = jax.nn.sigmoid(jnp.dot(xn, w_ref[...], preferred_element_type=F32))
    emb = jnp.dot(p_ref[...].astype(BF16), wp_ref[...], preferred_element_type=F32)
    o_ref[...] = x2 + gate * emb


def _ple_rows(x, delta, nw, w, p, wp, *, tm):
    m, d = x.shape
    kp = p.shape[1]
    rows = lambda width: pl.BlockSpec((tm, width), lambda i: (i, 0))
    resident = lambda shape: pl.BlockSpec(shape, lambda i: (0, 0), pipeline_mode=pl.Buffered(1))
    return pl.pallas_call(
        _ple_rows_kernel,
        grid=(m // tm,),
        in_specs=[rows(d), rows(d), pl.BlockSpec((1, d), lambda i: (0, 0)), resident((d, d)),
                  rows(kp), resident((kp, d))],
        out_specs=rows(d),
        out_shape=jax.ShapeDtypeStruct((m, d), F32),
        compiler_params=_params(("parallel",), VMEM_MIB["ple"]),
        name="ple_prompt",
    )(x, delta, nw.reshape(1, d), w, p, wp)


def _layer_weights(i, attn_norm_w, w_in, q_norm_w, k_norm_w, attn_sinks, sgu_norm_w, sgu_w, sgu_b,
                   w_br_attn, w_br_gm, w_gate, b_gate, w_out, ffn_norm_w, w_up, conv_w, conv_b,
                   w_down, ple_norm_w, w_ple_gate, w_ple_proj):
    sinks = attn_sinks[i]
    gm_group_w = w_br_gm.shape[1] // GM_GROUPS
    return dict(
        attn_norm_w=attn_norm_w[i], w_in=w_in[i],
        qw2=jnp.tile(q_norm_w[i] * (HEAD_DIM ** -0.5), HEADS_PER_VREG).reshape(1, LANES),
        kw2=jnp.tile(k_norm_w[i], HEADS_PER_VREG).reshape(1, LANES),
        sinks=sinks,
        sinks_row=jnp.pad(sinks, (0, LANES - N_HEADS)).reshape(1, LANES),
        sgu_norm_w=sgu_norm_w[i], sgu_w=sgu_w[i], sgu_b_t=jnp.transpose(sgu_b[i]),
        sgu_w0_row=jnp.repeat(sgu_w[i][:, 0, 0], gm_group_w).reshape(1, -1),
        sgu_b0_row=jnp.repeat(sgu_b[i][:, 0], gm_group_w).reshape(1, -1),
        w_br_attn=w_br_attn[i], w_br_gm=w_br_gm[i],
        w_gate=w_gate[i], b_gate=b_gate[i], w_out=w_out[i],
        ffn_norm_w=ffn_norm_w[i], conv_w=conv_w[i], conv_b=conv_b[i],
        ple_norm_w=ple_norm_w[i], w_ple_proj=w_ple_proj[i],
        layer=i, w_up_all=w_up, w_down_all=w_down, w_ple_gate_all=w_ple_gate,
    )


def _prompt_mixers(x, lw, wb):
    batch, seq, d = x.shape
    x2d = x.reshape(batch * seq, d)
    attn_w = N_HEADS * HEAD_DIM
    kv_w = N_KV_HEADS * HEAD_DIM
    gm_w = lw["w_br_gm"].shape[0]
    proj, gates = _in_proj(x2d, lw["attn_norm_w"], wb["w_in"], wb["w_gate"], lw["b_gate"], tm=ROW_TILE)
    attn, k_new, v_new, sgu, sgu_state, wb["w_up"], wb["w_down"], wb["w_ple_gate"] = _mixers_prompt(
        proj, lw["qw2"], lw["kw2"], lw["sinks"], lw["sgu_norm_w"], lw["sgu_w"], lw["sgu_b_t"],
        lw["w_up_all"], lw["w_down_all"], lw["w_ple_gate_all"], lw["layer"],
        batch=batch, seq=seq, gu_off=attn_w + 2 * kv_w, gm_w=gm_w, tile=W_TILE)
    states = (k_new.reshape(batch, WINDOW, N_KV_HEADS, HEAD_DIM),
              v_new.reshape(batch, WINDOW, N_KV_HEADS, HEAD_DIM), sgu_state)
    return (attn, sgu, gates), states


def _prompt_tail(x, p, mixed, lw, wb):
    batch, seq, d = x.shape
    x2d = x.reshape(batch * seq, d)
    attn, sgu, gates = mixed
    x1, xn1 = _merge_out(attn, sgu, gates, x2d, wb["w_br_attn"], wb["w_br_gm"], wb["w_out"],
                         lw["ffn_norm_w"], tm=ROW_TILE)
    delta, conv_state = _ffn_prompt(xn1, wb["w_up"], wb["w_down"], lw["conv_w"], lw["conv_b"],
                                    batch=batch, seq=seq, tm=FFN_ROW_TILE)
    x3 = _ple_rows(x1, delta, lw["ple_norm_w"], wb["w_ple_gate"], p.reshape(batch * seq, -1),
                   wb["w_ple_proj"], tm=PLE_ROW_TILE)
    return x3.reshape(batch, seq, d), conv_state


def _sample_mixers(x, k_buf, v_buf, lw, wb):
    nb, t, d = x.shape
    x2d = x.reshape(nb * t, d)
    attn_w = N_HEADS * HEAD_DIM
    kv_w = N_KV_HEADS * HEAD_DIM
    gm_w = lw["w_br_gm"].shape[0]
    proj, wb["w_in"] = _norm_mm(x2d, lw["attn_norm_w"], lw["w_in"], None, act=None,
                                tn=W_TILE, tile=W_TILE, name="proj_sample")
    gates, wb["w_gate"] = _norm_mm(x2d, lw["attn_norm_w"], lw["w_gate"], lw["b_gate"], act="sigmoid",
                                   tn=2 * W_TILE, tile=W_TILE, name="gates_sample")
    attn, k_new, v_new, wb["w_br_attn"], wb["w_br_gm"], wb["w_out"] = _attn_sample(
        proj, k_buf.reshape(nb, WINDOW, kv_w), v_buf.reshape(nb, WINDOW, kv_w),
        lw["qw2"], lw["kw2"], lw["sinks_row"], lw["w_br_attn"], lw["w_br_gm"], lw["w_out"], group=SAMPLE_GROUP)
    sgu, sgu_state = _sgu_sample(proj, lw["sgu_norm_w"], lw["sgu_w0_row"], lw["sgu_b0_row"],
                                 gu_off=attn_w + 2 * kv_w, gm_w=gm_w)
    x1, _ = _merge_out(attn, sgu, gates, x2d, wb["w_br_attn"], wb["w_br_gm"], wb["w_out"],
                       lw["ffn_norm_w"], tm=nb)
    states = (k_new.reshape(nb, WINDOW, N_KV_HEADS, HEAD_DIM), v_new.reshape(nb, WINDOW, N_KV_HEADS, HEAD_DIM),
              sgu_state.reshape(nb, t, gm_w))
    return x1, states


def _sample_tail(x1, p, conv_buf, lw, wb):
    nb = x1.shape[0]
    h = _norm_mm_tiles(x1, lw["ffn_norm_w"], wb["w_up"], tiles_per_step=2, name="ffn_up_sample")
    x2 = _ffn_down_sample(h, conv_buf.reshape(nb, -1), lw["conv_w"], lw["conv_b"], wb["w_down"], x1, tf=W_TILE)
    x3, wb["w_ple_proj"] = _ple_sample(x2, lw["ple_norm_w"], wb["w_ple_gate"], p.reshape(nb, -1),
                                       lw["w_ple_proj"])
    conv_state = jnp.stack([conv_buf[:, 1, :], h], axis=1)
    return x3, conv_state


def kernel(x_prompt, x_sample, p_prompt, p_sample, state_attn_k, state_attn_v, state_conv, attn_norm_w, w_in, q_norm_w, k_norm_w, attn_sinks, sgu_norm_w, sgu_w, sgu_b, w_br_attn, w_br_gm, w_gate, b_gate, w_out, ffn_norm_w, w_up, conv_w, conv_b, w_down, ple_norm_w, w_ple_gate, w_ple_proj):
    depth = w_in.shape[0]
    xp, xs = x_prompt, x_sample
    outs = [[] for _ in range(8)]
    for i in range(depth):
        lw = _layer_weights(i, attn_norm_w, w_in, q_norm_w, k_norm_w, attn_sinks, sgu_norm_w, sgu_w, sgu_b,
                            w_br_attn, w_br_gm, w_gate, b_gate, w_out, ffn_norm_w, w_up, conv_w, conv_b,
                            w_down, ple_norm_w, w_ple_gate, w_ple_proj)
        wb = {}
        xs1, (ks, vs, gs) = _sample_mixers(xs, state_attn_k[i], state_attn_v[i], lw, wb)
        mixed, (kp, vp, gp) = _prompt_mixers(xp, lw, wb)
        xs3, cs = _sample_tail(xs1, p_sample[i], state_conv[i], lw, wb)
        xs = xs3.reshape(xs.shape)
        xp, cp = _prompt_tail(xp, p_prompt[i], mixed, lw, wb)
        for lst, val in zip(outs, (kp, vp, ks, vs, gp, gs, cp, cs)):
            lst.append(val)
    return (xp, xs) + tuple(jnp.stack(lst) for lst in outs)
```

```python
import functools

import jax
import jax.numpy as jnp
from jax import lax
from jax.experimental import pallas as pl
from jax.experimental.pallas import tpu as pltpu

F32 = jnp.float32
BF16 = jnp.bfloat16

HEAD_DIM = 64
N_HEADS = 16
N_KV_HEADS = 4
GQA_GROUP = N_HEADS // N_KV_HEADS
WINDOW = 128
CHUNK = 128
GM_GROUPS = 4
EPS = 1e-6
MASK_VALUE = -1e30
LANES = 128
SUBLANES = 8
HEADS_PER_VREG = LANES // HEAD_DIM
MIB = 1 << 20


def _alibi_slope(h):
    return float(2.0 ** (-8.0 * (h + 1) / N_HEADS))


ROW_TILE = 256
PLE_ROW_TILE = 512
FFN_ROW_TILE = 1024
W_TILE = 512
SAMPLE_GROUP = 8
VMEM_MIB = dict(stream=56, in_proj=56, mixers=56, attn_sample=48, merge_out=48, ffn=58,
                ffn_sample=48, ple=56, small=32)


def _params(semantics, vmem_mib):
    return pltpu.CompilerParams(dimension_semantics=semantics, vmem_limit_bytes=vmem_mib * MIB)


def _rmsnorm_rows(x, w):
    ms = jnp.mean(x * x, axis=-1, keepdims=True)
    return x * lax.rsqrt(ms + EPS) * w


def _norm_mm_kernel(*refs, has_bias, act):
    if has_bias:
        x_ref, nw_ref, w_ref, b_ref, o_ref, wb_ref, xn_ref = refs
    else:
        x_ref, nw_ref, w_ref, o_ref, wb_ref, xn_ref = refs

    @pl.when(pl.program_id(0) == 0)
    def _():
        xn_ref[...] = _rmsnorm_rows(x_ref[...], nw_ref[...]).astype(BF16)

    wb = w_ref[...].astype(BF16)
    tile = wb_ref.shape[2]
    for c in range(wb_ref.shape[0]):
        wb_ref[c] = wb[:, c * tile:(c + 1) * tile]
    acc = jnp.dot(xn_ref[...], wb, preferred_element_type=F32)
    if has_bias:
        acc = acc + b_ref[...]
    if act == "sigmoid":
        acc = jax.nn.sigmoid(acc)
    o_ref[...] = acc.astype(o_ref.dtype)


def _norm_mm(x, nw, w, bias, *, act, tn, tile, name):
    m, k = x.shape
    n = w.shape[1]
    per_step = tn // tile
    in_specs = [
        pl.BlockSpec((m, k), lambda j: (0, 0)),
        pl.BlockSpec((1, k), lambda j: (0, 0)),
        pl.BlockSpec((k, tn), lambda j: (0, j)),
    ]
    args = [x, nw.reshape(1, k), w]
    if bias is not None:
        in_specs.append(pl.BlockSpec((1, tn), lambda j: (0, j)))
        args.append(bias.reshape(1, n))
    return pl.pallas_call(
        functools.partial(_norm_mm_kernel, has_bias=bias is not None, act=act),
        grid=(n // tn,),
        in_specs=in_specs,
        out_specs=[pl.BlockSpec((m, tn), lambda j: (0, j)),
                   pl.BlockSpec((per_step, k, tile), lambda j: (j, 0, 0))],
        out_shape=[jax.ShapeDtypeStruct((m, n), F32), jax.ShapeDtypeStruct((n // tile, k, tile), BF16)],
        scratch_shapes=[pltpu.VMEM((m, k), BF16)],
        compiler_params=_params(("arbitrary",), VMEM_MIB["stream"]),
        name=name,
    )(*args)


def _norm_mm_tiles_kernel(x_ref, nw_ref, w_ref, o_ref, xn_ref):
    @pl.when(pl.program_id(0) == 0)
    def _():
        xn_ref[...] = _rmsnorm_rows(x_ref[...], nw_ref[...]).astype(BF16)

    tile = w_ref.shape[2]
    for c in range(w_ref.shape[0]):
        o_ref[:, c * tile:(c + 1) * tile] = jnp.dot(xn_ref[...], w_ref[c], preferred_element_type=F32)


def _norm_mm_tiles(x, nw, wt, *, tiles_per_step, name):
    m, k = x.shape
    n_tiles, _, tile = wt.shape
    tn = tiles_per_step * tile
    return pl.pallas_call(
        _norm_mm_tiles_kernel,
        grid=(n_tiles // tiles_per_step,),
        in_specs=[
            pl.BlockSpec((m, k), lambda j: (0, 0)),
            pl.BlockSpec((1, k), lambda j: (0, 0)),
            pl.BlockSpec((tiles_per_step, k, tile), lambda j: (j, 0, 0)),
        ],
        out_specs=pl.BlockSpec((m, tn), lambda j: (0, j)),
        out_shape=jax.ShapeDtypeStruct((m, n_tiles * tile), F32),
        scratch_shapes=[pltpu.VMEM((m, k), BF16)],
        compiler_params=_params(("arbitrary",), VMEM_MIB["ffn_sample"]),
        name=name,
    )(x, nw.reshape(1, k), wt)


def _in_proj_kernel(x_ref, nw_ref, win_ref, wg_ref, bg_ref, proj_ref, gates_ref, xn_ref):
    xn_ref[...] = _rmsnorm_rows(x_ref[...], nw_ref[...]).astype(BF16)
    tn = win_ref.shape[2]
    for c in range(win_ref.shape[0]):
        acc = jnp.dot(xn_ref[...], win_ref[c], preferred_element_type=F32)
        proj_ref[:, c * tn:(c + 1) * tn] = acc.astype(proj_ref.dtype)
    for c in range(wg_ref.shape[0]):
        cols = slice(c * tn, (c + 1) * tn)
        acc = jnp.dot(xn_ref[...], wg_ref[c], preferred_element_type=F32)
        gates_ref[:, cols] = jax.nn.sigmoid(acc + bg_ref[:, cols]).astype(gates_ref.dtype)


def _in_proj(x, nw, w_in, w_gate, b_gate, *, tm):
    m, k = x.shape
    n_in, _, tn = w_in.shape
    n_g = w_gate.shape[0]
    rows = lambda width: pl.BlockSpec((tm, width), lambda i: (i, 0))
    resident = lambda a: pl.BlockSpec(a.shape, lambda i: (0,) * a.ndim, pipeline_mode=pl.Buffered(1))
    bg = b_gate.reshape(1, -1)
    return pl.pallas_call(
        _in_proj_kernel,
        grid=(m // tm,),
        in_specs=[rows(k), pl.BlockSpec((1, k), lambda i: (0, 0)), resident(w_in), resident(w_gate),
                  pl.BlockSpec(bg.shape, lambda i: (0, 0))],
        out_specs=[rows(n_in * tn), rows(n_g * tn)],
        out_shape=[
            jax.ShapeDtypeStruct((m, n_in * tn), BF16),
            jax.ShapeDtypeStruct((m, n_g * tn), BF16),
        ],
        scratch_shapes=[pltpu.VMEM((tm, k), BF16)],
        compiler_params=_params(("parallel",), VMEM_MIB["in_proj"]),
        name="in_proj",
    )(x, nw.reshape(1, k), w_in, w_gate, bg)


def _head_group_matrix():
    r = lax.broadcasted_iota(jnp.int32, (LANES, LANES), 0)
    c = lax.broadcasted_iota(jnp.int32, (LANES, LANES), 1)
    return jnp.where((r >> 6) == (c >> 6), 1.0 / HEAD_DIM, 0.0).astype(BF16)


def _head_rmsnorm(xcol, w, gmat):
    ms = jnp.dot((xcol * xcol).astype(BF16), gmat, preferred_element_type=F32)
    return xcol * lax.rsqrt(ms + EPS) * w


def _attn_prompt_kernel(sink_ref, q_ref, kc_ref, kp_ref, vc_ref, vp_ref, qw_ref, kw_ref,
                        o_ref, kn_ref, vn_ref):
    c = pl.program_id(1)
    blk = WINDOW
    nblk = q_ref.shape[0] // blk
    gmat = _head_group_matrix()
    lane = lax.broadcasted_iota(jnp.int32, (1, LANES), 1)
    low = lane < HEAD_DIM

    kw = kw_ref[...]
    qw = qw_ref[...]
    kall = jnp.concatenate([kp_ref[...], kc_ref[...]], axis=0).astype(F32)
    vall = jnp.concatenate([vp_ref[...], vc_ref[...]], axis=0).astype(F32)
    ncol = kall.shape[1] // LANES
    kcat = [_head_rmsnorm(kall[:, p * LANES:(p + 1) * LANES], kw, gmat) for p in range(ncol)]
    vcat = [vall[:, p * LANES:(p + 1) * LANES] for p in range(ncol)]
    kn_ref[0] = jnp.concatenate([kc_[nblk * blk:] for kc_ in kcat], axis=1)
    vn_ref[0] = vall[nblk * blk:]
    krol = [pltpu.roll(kcat[p], HEAD_DIM, 1) for p in range(ncol)]
    vrol = [pltpu.roll(vcat[p], HEAD_DIM, 1) for p in range(ncol)]

    qi = lax.broadcasted_iota(jnp.int32, (blk, 2 * blk), 0)
    kj = lax.broadcasted_iota(jnp.int32, (blk, 2 * blk), 1)
    dist = blk + qi - kj
    in_window = (dist >= 0) & (dist <= WINDOW)
    distf = dist.astype(F32)

    q = q_ref[...].astype(F32)
    qn = [_head_rmsnorm(q[:, p * LANES:(p + 1) * LANES], qw, gmat)
          for p in range(N_HEADS // HEADS_PER_VREG)]

    for g in range(N_KV_HEADS):
        p, half = divmod(g, HEADS_PER_VREG)
        if half == 0:
            kd_all = jnp.where(low, kcat[p], krol[p]).astype(BF16)
            vd_all = jnp.where(low, vcat[p], vrol[p]).astype(BF16)
        else:
            kd_all = jnp.where(low, krol[p], kcat[p]).astype(BF16)
            vd_all = jnp.where(low, vrol[p], vcat[p]).astype(BF16)
        for b in range(nblk):
            rows = slice(b * blk, (b + 1) * blk)
            kd = kd_all[b * blk:(b + 2) * blk]
            vd = vd_all[b * blk:(b + 2) * blk]
            valid = in_window if b > 0 else in_window & ((kj >= blk) | (c > 0))
            qs = []
            for hl in range(GQA_GROUP):
                h = g * GQA_GROUP + hl
                pc, hh = divmod(h, HEADS_PER_VREG)
                keep = low if hh == 0 else jnp.logical_not(low)
                qs.append(jnp.where(keep, qn[pc][rows], 0.0).astype(BF16))
            qstack = jnp.concatenate(qs, axis=0)
            s_all = lax.dot_general(qstack, kd, (((1,), (1,)), ((), ())), preferred_element_type=F32)
            ps = []
            for hl in range(GQA_GROUP):
                h = g * GQA_GROUP + hl
                s = s_all[hl * blk:(hl + 1) * blk] - _alibi_slope(h) * distf
                s = jnp.where(valid, s, MASK_VALUE)
                sink = sink_ref[h]
                mx = jnp.maximum(jnp.max(s, axis=-1, keepdims=True), sink)
                e = jnp.exp(s - mx)
                den = jnp.sum(e, axis=-1, keepdims=True) + jnp.exp(sink - mx)
                ps.append((e / den).astype(BF16))
            pstack = jnp.concatenate(ps, axis=0)
            o_all = jnp.dot(pstack, vd, preferred_element_type=F32)
            for pair in range(GQA_GROUP // HEADS_PER_VREG):
                pc = g * (GQA_GROUP // HEADS_PER_VREG) + pair
                even = o_all[(2 * pair) * blk:(2 * pair + 1) * blk]
                odd = o_all[(2 * pair + 1) * blk:(2 * pair + 2) * blk]
                o_ref[rows, pc * LANES:(pc + 1) * LANES] = jnp.where(low, even, odd).astype(o_ref.dtype)


def _layernorm_rows(x, w):
    mu = jnp.mean(x, axis=-1, keepdims=True)
    xc = x - mu
    return xc * lax.rsqrt(jnp.mean(xc * xc, axis=-1, keepdims=True) + EPS) * w


GELU_C = 0.7978845608028654
GELU_A = 0.044715


def _gelu_tanh(x):
    return x * jax.nn.sigmoid((2.0 * GELU_C) * (x + GELU_A * (x * x * x)))


def _sgu_prompt_kernel(gu0_ref, gu1_ref, gv0_ref, gv1_ref, nw_ref, ws_ref, bs_ref, o_ref, st_ref):
    rows = gu0_ref.shape[0]
    gv = jnp.concatenate([gv0_ref[...], gv1_ref[...]], axis=1).astype(F32)
    vn = _layernorm_rows(_gelu_tanh(gv), nw_ref[...])
    st_ref[0] = vn[rows - CHUNK:]
    vb = vn.astype(BF16)
    r = lax.broadcasted_iota(jnp.int32, (CHUNK, CHUNK), 0)
    c = lax.broadcasted_iota(jnp.int32, (CHUNK, CHUNK), 1)
    causal = r >= c
    gw = vb.shape[1] // GM_GROUPS
    half = gu0_ref.shape[1]
    for g in range(GM_GROUPS):
        w = jnp.where(causal, ws_ref[g], 0.0).astype(BF16)
        src = gu0_ref if g * gw < half else gu1_ref
        off = g * gw - (0 if g * gw < half else half)
        for ch in range(rows // CHUNK):
            rs = slice(ch * CHUNK, (ch + 1) * CHUNK)
            mix = jnp.dot(w, vb[rs, g * gw:(g + 1) * gw], preferred_element_type=F32) + bs_ref[:, g:g + 1]
            u = _gelu_tanh(src[rs, off:off + gw].astype(F32))
            o_ref[rs, g * gw:(g + 1) * gw] = (u * mix).astype(o_ref.dtype)


N_ATTN_IN, N_ATTN_OUT, N_SGU_IN, N_SGU_OUT, N_CAST = 8, 3, 7, 2, 3
MIXER_BLOCKS = 4


def _mixers_prompt_kernel(*refs):
    attn_in = refs[:N_ATTN_IN]
    sgu_in = refs[N_ATTN_IN:N_ATTN_IN + N_SGU_IN]
    cast_in = refs[N_ATTN_IN + N_SGU_IN:N_ATTN_IN + N_SGU_IN + N_CAST]
    outs = refs[N_ATTN_IN + N_SGU_IN + N_CAST:]
    attn_out = outs[:N_ATTN_OUT]
    sgu_out = outs[N_ATTN_OUT:N_ATTN_OUT + N_SGU_OUT]
    wup_ref, wdn_ref, wpg_ref = cast_in
    wupb_ref, wdnb_ref, wpgb_ref = outs[N_ATTN_OUT + N_SGU_OUT:]
    tile = wupb_ref.shape[2]
    for c in range(wupb_ref.shape[0]):
        wupb_ref[c] = wup_ref[:, c * tile:(c + 1) * tile].astype(BF16)
    wdnb_ref[...] = wdn_ref[...].astype(BF16)
    wpgb_ref[...] = wpg_ref[...].astype(BF16)
    _sgu_prompt_kernel(*sgu_in, *sgu_out)
    _attn_prompt_kernel(*attn_in, *attn_out)


def _mixers_prompt(proj, qw2, kw2, sinks, nw, ws, bs_t, w_up, w_down, w_pg, layer, *,
                   batch, seq, gu_off, gm_w, tile):
    rows = MIXER_BLOCKS * WINDOW
    nb = seq // rows
    steps = batch * nb
    flat = lambda b, c: b * nb + c
    _, k_up, n_up = w_up.shape
    slab_in = lambda a: pl.BlockSpec((None, a.shape[1] // steps, a.shape[2]), lambda b, c: (layer, flat(b, c), 0))
    slab_out = lambda a: pl.BlockSpec((a.shape[1] // steps, a.shape[2]), lambda b, c: (flat(b, c), 0))
    attn_w = N_HEADS * HEAD_DIM
    kv_w = N_KV_HEADS * HEAD_DIM
    kcol = attn_w // kv_w
    vcol = kcol + 1
    half = gm_w // 2
    b0 = gu_off // half
    cur = lambda col: (lambda b, c: (flat(b, c), col))
    prev = lambda col: (lambda b, c: (jnp.maximum(flat(b, c) * MIXER_BLOCKS - 1, 0), col))
    const = lambda shape: pl.BlockSpec(shape, lambda b, c: (0,) * len(shape))
    state = lambda width: pl.BlockSpec((1, WINDOW, width), lambda b, c: (b, 0, 0))
    return pl.pallas_call(
        _mixers_prompt_kernel,
        grid=(batch, nb),
        in_specs=[
            pl.BlockSpec(memory_space=pltpu.SMEM),
            pl.BlockSpec((rows, attn_w), cur(0)),
            pl.BlockSpec((rows, kv_w), cur(kcol)),
            pl.BlockSpec((WINDOW, kv_w), prev(kcol)),
            pl.BlockSpec((rows, kv_w), cur(vcol)),
            pl.BlockSpec((WINDOW, kv_w), prev(vcol)),
            const((1, LANES)), const((1, LANES)),
            pl.BlockSpec((rows, half), cur(b0)),
            pl.BlockSpec((rows, half), cur(b0 + 1)),
            pl.BlockSpec((rows, half), cur(b0 + 2)),
            pl.BlockSpec((rows, half), cur(b0 + 3)),
            const((1, gm_w)), const((GM_GROUPS, CHUNK, CHUNK)), const((CHUNK, GM_GROUPS)),
            slab_in(w_up), slab_in(w_down), slab_in(w_pg),
        ],
        out_specs=[
            pl.BlockSpec((rows, attn_w), cur(0)), state(kv_w), state(kv_w),
            pl.BlockSpec((rows, gm_w), cur(0)), state(gm_w),
            pl.BlockSpec((n_up // tile, k_up // steps, tile), lambda b, c: (0, flat(b, c), 0)),
            slab_out(w_down), slab_out(w_pg),
        ],
        out_shape=[
            jax.ShapeDtypeStruct((batch * seq, attn_w), BF16),
            jax.ShapeDtypeStruct((batch, WINDOW, kv_w), F32),
            jax.ShapeDtypeStruct((batch, WINDOW, kv_w), F32),
            jax.ShapeDtypeStruct((batch * seq, gm_w), BF16),
            jax.ShapeDtypeStruct((batch, CHUNK, gm_w), F32),
            jax.ShapeDtypeStruct((n_up // tile, k_up, tile), BF16),
            jax.ShapeDtypeStruct(w_down.shape[1:], BF16),
            jax.ShapeDtypeStruct(w_pg.shape[1:], BF16),
        ],
        compiler_params=_params(("arbitrary", "arbitrary"), VMEM_MIB["mixers"]),
        name="mixers_prompt",
    )(sinks, proj, proj, proj, proj, proj, qw2, kw2, proj, proj, proj, proj, nw.reshape(1, gm_w), ws, bs_t,
      w_up, w_down, w_pg)


def _split_dot(x, m):
    hi = x.astype(BF16)
    lo = (x - hi.astype(F32)).astype(BF16)
    return jnp.dot(hi, m, preferred_element_type=F32) + jnp.dot(lo, m, preferred_element_type=F32)


def _attn_sample_kernel(q_ref, kn_ref, vn_ref, kb_ref, vb_ref, qw_ref, kw_ref, sink_ref,
                        wa_ref, wg_ref, wo_ref, o_ref, ko_ref, vo_ref, wab_ref, wgb_ref, wob_ref):
    wab_ref[...] = wa_ref[...].astype(BF16)
    wgb_ref[...] = wg_ref[...].astype(BF16)
    wob_ref[...] = wo_ref[...].astype(BF16)
    nsamp = q_ref.shape[0]
    attn_w = q_ref.shape[1]
    kv_w = kn_ref.shape[1]
    gmat = _head_group_matrix()
    q = q_ref[...].astype(F32)
    kn = kn_ref[...].astype(F32)
    qn = jnp.concatenate(
        [_head_rmsnorm(q[:, p * LANES:(p + 1) * LANES], qw_ref[...], gmat)
         for p in range(attn_w // LANES)], axis=1)
    knn = jnp.concatenate(
        [_head_rmsnorm(kn[:, p * LANES:(p + 1) * LANES], kw_ref[...], gmat)
         for p in range(kv_w // LANES)], axis=1)
    vnn = vn_ref[...].astype(F32)

    ec = lax.broadcasted_iota(jnp.int32, (kv_w, attn_w), 0)
    el = lax.broadcasted_iota(jnp.int32, (kv_w, attn_w), 1)
    expand = (((ec >> 6) == (el >> 8)) & ((ec & 63) == (el & 63))).astype(BF16)
    gl = lax.broadcasted_iota(jnp.int32, (attn_w, LANES), 0)
    gh = lax.broadcasted_iota(jnp.int32, (attn_w, LANES), 1)
    hsum = ((gl >> 6) == gh).astype(BF16)
    tl = lax.broadcasted_iota(jnp.int32, (LANES, attn_w), 1)
    th = lax.broadcasted_iota(jnp.int32, (LANES, attn_w), 0)
    hexp = ((tl >> 6) == th).astype(BF16)

    nkeys = WINDOW + SUBLANES
    row = lax.broadcasted_iota(jnp.int32, (nkeys, LANES), 0)
    head = lax.broadcasted_iota(jnp.int32, (nkeys, LANES), 1)
    slope = jnp.exp2(-8.0 * (head + 1).astype(F32) / N_HEADS)
    dist = (WINDOW - row).astype(F32)
    key_ok = row <= WINDOW
    srow = lax.broadcasted_iota(jnp.int32, (WINDOW, kv_w), 0)
    sinks = sink_ref[...]

    for s in range(nsamp):
        kb = kb_ref[s]
        vb = vb_ref[s]
        knew = knn[s:s + 1]
        vnew = vnn[s:s + 1]
        ko_ref[s] = jnp.where(srow == WINDOW - 1, knew, pltpu.roll(kb, WINDOW - 1, 0))
        vo_ref[s] = jnp.where(srow == WINDOW - 1, vnew, pltpu.roll(vb, WINDOW - 1, 0))
        kk = jnp.concatenate([kb, jnp.broadcast_to(knew, (SUBLANES, kv_w))], axis=0)
        vv = jnp.concatenate([vb, jnp.broadcast_to(vnew, (SUBLANES, kv_w))], axis=0)
        kexp = jnp.dot(kk.astype(BF16), expand, preferred_element_type=F32)
        vexp = jnp.dot(vv.astype(BF16), expand, preferred_element_type=F32)
        sc = _split_dot(kexp * qn[s:s + 1], hsum)
        sc = sc - slope * dist
        sc = jnp.where(key_ok, sc, MASK_VALUE)
        mx = jnp.maximum(jnp.max(sc, axis=0, keepdims=True), sinks)
        e = jnp.exp(sc - mx)
        den = jnp.sum(e, axis=0, keepdims=True) + jnp.exp(sinks - mx)
        pexp = jnp.dot((e / den).astype(BF16), hexp, preferred_element_type=F32)
        o_ref[s:s + 1, :] = jnp.sum(pexp * vexp, axis=0, keepdims=True)


def _attn_sample(proj, kbuf, vbuf, qw2, kw2, sinks_row, wa, wg, wo, *, group):
    nb = proj.shape[0]
    attn_w = N_HEADS * HEAD_DIM
    kv_w = N_KV_HEADS * HEAD_DIM
    steps = nb // group
    row = lambda width, col=0: pl.BlockSpec((group, width), lambda i: (i, col))
    buf = pl.BlockSpec((group, WINDOW, kv_w), lambda i: (i, 0, 0))
    vec = pl.BlockSpec((1, LANES), lambda i: (0, 0))
    slab = lambda w: pl.BlockSpec((w.shape[0] // steps, w.shape[1]), lambda i: (i, 0))
    return pl.pallas_call(
        _attn_sample_kernel,
        grid=(steps,),
        in_specs=[row(attn_w), row(kv_w, attn_w // kv_w), row(kv_w, attn_w // kv_w + 1), buf, buf,
                  vec, vec, vec, slab(wa), slab(wg), slab(wo)],
        out_specs=[row(attn_w), buf, buf, slab(wa), slab(wg), slab(wo)],
        out_shape=[
            jax.ShapeDtypeStruct((nb, attn_w), F32),
            jax.ShapeDtypeStruct((nb, WINDOW, kv_w), F32),
            jax.ShapeDtypeStruct((nb, WINDOW, kv_w), F32),
            jax.ShapeDtypeStruct(wa.shape, BF16),
            jax.ShapeDtypeStruct(wg.shape, BF16),
            jax.ShapeDtypeStruct(wo.shape, BF16),
        ],
        compiler_params=_params(("parallel",), VMEM_MIB["attn_sample"]),
        name="attn_sample",
    )(proj, proj, proj, kbuf, vbuf, qw2, kw2, sinks_row, wa, wg, wo)


def _sgu_sample_kernel(proj_ref, nw_ref, w0_ref, b0_ref, o_ref, st_ref, *, gu_off):
    gm_w = o_ref.shape[1]
    gu = proj_ref[:, gu_off:gu_off + gm_w]
    gv = proj_ref[:, gu_off + gm_w:gu_off + 2 * gm_w]
    vn = _layernorm_rows(_gelu_tanh(gv), nw_ref[...])
    st_ref[...] = vn
    mix = w0_ref[...] * vn + b0_ref[...]
    o_ref[...] = _gelu_tanh(gu) * mix


def _sgu_sample(proj, nw, w0_row, b0_row, *, gu_off, gm_w):
    nb = proj.shape[0]
    full = pl.BlockSpec((nb, gm_w), lambda i: (0, 0))
    vec = pl.BlockSpec((1, gm_w), lambda i: (0, 0))
    return pl.pallas_call(
        functools.partial(_sgu_sample_kernel, gu_off=gu_off),
        grid=(1,),
        in_specs=[pl.BlockSpec(proj.shape, lambda i: (0, 0)), vec, vec, vec],
        out_specs=[full, full],
        out_shape=[jax.ShapeDtypeStruct((nb, gm_w), F32), jax.ShapeDtypeStruct((nb, gm_w), F32)],
        name="sgu_sample",
    )(proj, nw.reshape(1, gm_w), w0_row, b0_row)


def _merge_out_kernel(a_ref, s_ref, ga_ref, gb_ref, x_ref, wa_ref, wg_ref, wo_ref, nw_ref,
                      x1_ref, xn1_ref):
    a = jnp.dot(a_ref[...].astype(BF16), wa_ref[...], preferred_element_type=F32)
    m = jnp.dot(s_ref[...].astype(BF16), wg_ref[...], preferred_element_type=F32)
    merged = (gb_ref[...].astype(F32) * a + ga_ref[...].astype(F32) * m).astype(BF16)
    x1 = x_ref[...] + jnp.dot(merged, wo_ref[...], preferred_element_type=F32)
    x1_ref[...] = x1
    xn1_ref[...] = _rmsnorm_rows(x1, nw_ref[...]).astype(BF16)


def _merge_out(attn, sgu, gates, x, wa, wg, wo, nw, *, tm):
    m, ka = attn.shape
    kg = sgu.shape[1]
    d = wo.shape[1]
    rows = lambda width, col=0: pl.BlockSpec((tm, width), lambda i: (i, col))
    resident = lambda shape: pl.BlockSpec(shape, lambda i: (0, 0), pipeline_mode=pl.Buffered(1))
    return pl.pallas_call(
        _merge_out_kernel,
        grid=(m // tm,),
        in_specs=[
            rows(ka), rows(kg), rows(d, 0), rows(d, 1), rows(d),
            resident((ka, d)), resident((kg, d)), resident((d, d)),
            pl.BlockSpec((1, d), lambda i: (0, 0)),
        ],
        out_specs=[rows(d), rows(d)],
        out_shape=[jax.ShapeDtypeStruct((m, d), F32), jax.ShapeDtypeStruct((m, d), BF16)],
        compiler_params=_params(("parallel",), VMEM_MIB["merge_out"]),
        name="merge_out",
    )(attn, sgu, gates, gates, x, wa, wg, wo, nw.reshape(1, d))


PAD = SUBLANES
FFN_ROW_BLOCKS = 1


def _ffn_step(t, xn_ref, wug_ref, wuu_ref, wd_ref, cwg_ref, cwu_ref, cbg_ref, cbu_ref,
              csg_ref, csu_ref, h_new, h_old, carry_ref, acc_ref, *, nf, tiles_per_seq):
    tm = xn_ref.shape[0]
    tf = wug_ref.shape[1]
    rb = tm // FFN_ROW_BLOCKS
    tp = jnp.maximum(t - 1, 0)
    ip = tp // nf
    jp = tp % nf
    first = (ip % tiles_per_seq) == 0
    h_old[0:PAD, :] = jnp.where(first, 0.0, carry_ref[jp])
    carry_ref[jp] = h_old[tm:tm + PAD, :]
    cw = jnp.concatenate([cwg_ref[...], cwu_ref[...]], axis=1)
    cb = jnp.concatenate([cbg_ref[...], cbu_ref[...]], axis=1)
    wug = wug_ref[...]
    wuu = wuu_ref[...]
    wd = wd_ref[...]
    for r in range(FFN_ROW_BLOCKS):
        lo = r * rb
        xr = xn_ref[lo:lo + rb, :]
        h_new[PAD + lo:PAD + lo + rb, 0:tf] = jnp.dot(xr, wug, preferred_element_type=F32)
        h_new[PAD + lo:PAD + lo + rb, tf:2 * tf] = jnp.dot(xr, wuu, preferred_element_type=F32)
        hp = h_old[lo:lo + rb + PAD, :]
        hc = (cw[0:1] * pltpu.roll(hp, 2, 0)[PAD:] + cw[1:2] * pltpu.roll(hp, 1, 0)[PAD:]
              + cw[2:3] * hp[PAD:] + cb)
        act = (jax.nn.silu(hc[:, 0:tf]) * hc[:, tf:2 * tf]).astype(BF16)
        acc_ref[lo:lo + rb, :] += jnp.dot(act, wd, preferred_element_type=F32)
    csg_ref[0] = h_old[PAD + tm - 2:PAD + tm, 0:tf]
    csu_ref[0] = h_old[PAD + tm - 2:PAD + tm, tf:2 * tf]


def _ffn_kernel(xn_ref, wug_ref, wuu_ref, wd_ref, cwg_ref, cwu_ref, cbg_ref, cbu_ref,
                o_ref, csg_ref, csu_ref, ha_ref, hb_ref, carry_ref, acc_ref, *, nf, tiles_per_seq):
    t = pl.program_id(0)
    step = functools.partial(_ffn_step, t, xn_ref, wug_ref, wuu_ref, wd_ref, cwg_ref, cwu_ref,
                             cbg_ref, cbu_ref, csg_ref, csu_ref, carry_ref=carry_ref, acc_ref=acc_ref,
                             nf=nf, tiles_per_seq=tiles_per_seq)

    @pl.when(t == 0)
    def _():
        hb_ref[...] = jnp.zeros_like(hb_ref)
        carry_ref[...] = jnp.zeros_like(carry_ref)

    @pl.when(jnp.maximum(t - 1, 0) % nf == 0)
    def _():
        acc_ref[...] = jnp.zeros_like(acc_ref)

    @pl.when(t % 2 == 0)
    def _():
        step(h_new=ha_ref, h_old=hb_ref)

    @pl.when(t % 2 == 1)
    def _():
        step(h_new=hb_ref, h_old=ha_ref)

    @pl.when(jnp.maximum(t - 1, 0) % nf == nf - 1)
    def _():
        o_ref[...] = acc_ref[...].astype(o_ref.dtype)


def _ffn_prompt(xn, w_up, w_down, cw, cb, *, batch, seq, tm):
    m, k = xn.shape
    d_ff, d = w_down.shape
    tf = w_up.shape[2]
    nf = d_ff // tf
    nm = m // tm
    tiles_per_seq = seq // tm
    prev = lambda t: jnp.maximum(t - 1, 0)
    up_tile = lambda off: (lambda t: (off + t % nf, 0, 0))
    dn_col = lambda off: (lambda t: (0, off + prev(t) % nf))
    state = lambda t: (prev(t) // nf, 0, prev(t) % nf)
    delta, csg, csu = pl.pallas_call(
        functools.partial(_ffn_kernel, nf=nf, tiles_per_seq=tiles_per_seq),
        grid=(nm * nf + 1,),
        in_specs=[
            pl.BlockSpec((tm, k), lambda t: (jnp.minimum(t // nf, nm - 1), 0)),
            pl.BlockSpec((None, k, tf), up_tile(0)),
            pl.BlockSpec((None, k, tf), up_tile(nf)),
            pl.BlockSpec((tf, d), lambda t: (prev(t) % nf, 0)),
            pl.BlockSpec((3, tf), dn_col(0)),
            pl.BlockSpec((3, tf), dn_col(nf)),
            pl.BlockSpec((1, tf), dn_col(0)),
            pl.BlockSpec((1, tf), dn_col(nf)),
        ],
        out_specs=[
            pl.BlockSpec((tm, d), lambda t: (prev(t) // nf, 0)),
            pl.BlockSpec((1, 2, tf), state),
            pl.BlockSpec((1, 2, tf), state),
        ],
        out_shape=[
            jax.ShapeDtypeStruct((m, d), BF16),
            jax.ShapeDtypeStruct((nm, 2, d_ff), F32),
            jax.ShapeDtypeStruct((nm, 2, d_ff), F32),
        ],
        scratch_shapes=[
            pltpu.VMEM((tm + PAD, 2 * tf), F32),
            pltpu.VMEM((tm + PAD, 2 * tf), F32),
            pltpu.VMEM((nf, PAD, 2 * tf), F32),
            pltpu.VMEM((tm, d), F32),
        ],
        compiler_params=_params(("arbitrary",), VMEM_MIB["ffn"]),
        name="ffn_prompt",
    )(xn, w_up, w_up, w_down, cw, cw, cb.reshape(1, -1), cb.reshape(1, -1))
    tails = jnp.concatenate([csg, csu], axis=-1).reshape(batch, tiles_per_seq, 2, 2 * d_ff)
    return delta, tails[:, -1]


def _ffn_down_sample_kernel(hg_ref, hu_ref, b0g_ref, b0u_ref, b1g_ref, b1u_ref, cwg_ref, cwu_ref,
                            cbg_ref, cbu_ref, wd_ref, x_ref, o_ref):
    @pl.when(pl.program_id(0) == 0)
    def _():
        o_ref[...] = x_ref[...]

    def conv(h_ref, b0_ref, b1_ref, cw_ref, cb_ref):
        cw = cw_ref[...]
        return cw[0:1] * b0_ref[...] + cw[1:2] * b1_ref[...] + cw[2:3] * h_ref[...] + cb_ref[...]

    hcg = conv(hg_ref, b0g_ref, b1g_ref, cwg_ref, cbg_ref)
    hcu = conv(hu_ref, b0u_ref, b1u_ref, cwu_ref, cbu_ref)
    act = (jax.nn.silu(hcg) * hcu).astype(BF16)
    o_ref[...] += jnp.dot(act, wd_ref[...], preferred_element_type=F32)


def _ffn_down_sample(h, state2, cw, cb, wd, x, *, tf):
    nb, two_f = h.shape
    d_ff = two_f // 2
    nf = d_ff // tf
    d = wd.shape[1]
    colblk = lambda off: pl.BlockSpec((nb, tf), lambda j: (0, j + off))
    return pl.pallas_call(
        _ffn_down_sample_kernel,
        grid=(nf,),
        in_specs=[
            colblk(0), colblk(nf),
            colblk(0), colblk(nf), colblk(2 * nf), colblk(3 * nf),
            pl.BlockSpec((3, tf), lambda j: (0, j)),
            pl.BlockSpec((3, tf), lambda j: (0, j + nf)),
            pl.BlockSpec((1, tf), lambda j: (0, j)),
            pl.BlockSpec((1, tf), lambda j: (0, j + nf)),
            pl.BlockSpec((tf, d), lambda j: (j, 0)),
            pl.BlockSpec((nb, d), lambda j: (0, 0)),
        ],
        out_specs=pl.BlockSpec((nb, d), lambda j: (0, 0)),
        out_shape=jax.ShapeDtypeStruct((nb, d), F32),
        compiler_params=_params(("arbitrary",), VMEM_MIB["small"]),
        name="ffn_down_sample",
    )(h, h, state2, state2, state2, state2, cw, cw, cb.reshape(1, two_f), cb.reshape(1, two_f), wd, x)


def _ple_sample_kernel(x_ref, nw_ref, w_ref, p_ref, wp_ref, o_ref, wpb_ref):
    x2 = x_ref[...]
    xn = _rmsnorm_rows(x2, nw_ref[...]).astype(BF16)
    wpb = wp_ref[...].astype(BF16)
    wpb_ref[...] = wpb
    gate = jax.nn.sigmoid(jnp.dot(xn, w_ref[...], preferred_element_type=F32))
    emb = jnp.dot(p_ref[...].astype(BF16), wpb, preferred_element_type=F32)
    o_ref[...] = x2 + gate * emb


def _ple_sample(x, nw, w, p, wp):
    m, d = x.shape
    kp = p.shape[1]
    full = lambda shape: pl.BlockSpec(shape, lambda i: (0, 0))
    return pl.pallas_call(
        _ple_sample_kernel,
        grid=(1,),
        in_specs=[full((m, d)), full((1, d)), full((d, d)), full((m, kp)), full((kp, d))],
        out_specs=[full((m, d)), full((kp, d))],
        out_shape=[jax.ShapeDtypeStruct((m, d), F32), jax.ShapeDtypeStruct((kp, d), BF16)],
        compiler_params=_params(("arbitrary",), VMEM_MIB["small"]),
        name="ple_sample",
    )(x, nw.reshape(1, d), w, p, wp)


def _ple_rows_kernel(x_ref, d_ref, nw_ref, w_ref, p_ref, wp_ref, o_ref):
    x2 = x_ref[...] + d_ref[...].astype(F32)
    xn = _rmsnorm_rows(x2, nw_ref[...]).astype(BF16)
    gate = jax.nn.sigmoid(jnp.dot(xn, w_ref[...], preferred_element_type=F32))
    emb = jnp.dot(p_ref[...].astype(BF16), wp_ref[...], preferred_element_type=F32)
    o_ref[...] = x2 + gate * emb


def _ple_rows(x, delta, nw, w, p, wp, *, tm):
    m, d = x.shape
    kp = p.shape[1]
    rows = lambda width: pl.BlockSpec((tm, width), lambda i: (i, 0))
    resident = lambda shape: pl.BlockSpec(shape, lambda i: (0, 0), pipeline_mode=pl.Buffered(1))
    return pl.pallas_call(
        _ple_rows_kernel,
        grid=(m // tm,),
        in_specs=[rows(d), rows(d), pl.BlockSpec((1, d), lambda i: (0, 0)), resident((d, d)),
                  rows(kp), resident((kp, d))],
        out_specs=rows(d),
        out_shape=jax.ShapeDtypeStruct((m, d), F32),
        compiler_params=_params(("parallel",), VMEM_MIB["ple"]),
        name="ple_prompt",
    )(x, delta, nw.reshape(1, d), w, p, wp)


def _layer_weights(i, attn_norm_w, w_in, q_norm_w, k_norm_w, attn_sinks, sgu_norm_w, sgu_w, sgu_b,
                   w_br_attn, w_br_gm, w_gate, b_gate, w_out, ffn_norm_w, w_up, conv_w, conv_b,
                   w_down, ple_norm_w, w_ple_gate, w_ple_proj):
    sinks = attn_sinks[i]
    gm_group_w = w_br_gm.shape[1] // GM_GROUPS
    return dict(
        attn_norm_w=attn_norm_w[i], w_in=w_in[i],
        qw2=jnp.tile(q_norm_w[i] * (HEAD_DIM ** -0.5), HEADS_PER_VREG).reshape(1, LANES),
        kw2=jnp.tile(k_norm_w[i], HEADS_PER_VREG).reshape(1, LANES),
        sinks=sinks,
        sinks_row=jnp.pad(sinks, (0, LANES - N_HEADS)).reshape(1, LANES),
        sgu_norm_w=sgu_norm_w[i], sgu_w=sgu_w[i], sgu_b_t=jnp.transpose(sgu_b[i]),
        sgu_w0_row=jnp.repeat(sgu_w[i][:, 0, 0], gm_group_w).reshape(1, -1),
        sgu_b0_row=jnp.repeat(sgu_b[i][:, 0], gm_group_w).reshape(1, -1),
        w_br_attn=w_br_attn[i], w_br_gm=w_br_gm[i],
        w_gate=w_gate[i], b_gate=b_gate[i], w_out=w_out[i],
        ffn_norm_w=ffn_norm_w[i], conv_w=conv_w[i], conv_b=conv_b[i],
        ple_norm_w=ple_norm_w[i], w_ple_proj=w_ple_proj[i],
        layer=i, w_up_all=w_up, w_down_all=w_down, w_ple_gate_all=w_ple_gate,
    )


def _prompt_mixers(x, lw, wb):
    batch, seq, d = x.shape
    x2d = x.reshape(batch * seq, d)
    attn_w = N_HEADS * HEAD_DIM
    kv_w = N_KV_HEADS * HEAD_DIM
    gm_w = lw["w_br_gm"].shape[0]
    proj, gates = _in_proj(x2d, lw["attn_norm_w"], wb["w_in"], wb["w_gate"], lw["b_gate"], tm=ROW_TILE)
    attn, k_new, v_new, sgu, sgu_state, wb["w_up"], wb["w_down"], wb["w_ple_gate"] = _mixers_prompt(
        proj, lw["qw2"], lw["kw2"], lw["sinks"], lw["sgu_norm_w"], lw["sgu_w"], lw["sgu_b_t"],
        lw["w_up_all"], lw["w_down_all"], lw["w_ple_gate_all"], lw["layer"],
        batch=batch, seq=seq, gu_off=attn_w + 2 * kv_w, gm_w=gm_w, tile=W_TILE)
    states = (k_new.reshape(batch, WINDOW, N_KV_HEADS, HEAD_DIM),
              v_new.reshape(batch, WINDOW, N_KV_HEADS, HEAD_DIM), sgu_state)
    return (attn, sgu, gates), states


def _prompt_tail(x, p, mixed, lw, wb):
    batch, seq, d = x.shape
    x2d = x.reshape(batch * seq, d)
    attn, sgu, gates = mixed
    x1, xn1 = _merge_out(attn, sgu, gates, x2d, wb["w_br_attn"], wb["w_br_gm"], wb["w_out"],
                         lw["ffn_norm_w"], tm=ROW_TILE)
    delta, conv_state = _ffn_prompt(xn1, wb["w_up"], wb["w_down"], lw["conv_w"], lw["conv_b"],
                                    batch=batch, seq=seq, tm=FFN_ROW_TILE)
    x3 = _ple_rows(x1, delta, lw["ple_norm_w"], wb["w_ple_gate"], p.reshape(batch * seq, -1),
                   wb["w_ple_proj"], tm=PLE_ROW_TILE)
    return x3.reshape(batch, seq, d), conv_state


def _sample_mixers(x, k_buf, v_buf, lw, wb):
    nb, t, d = x.shape
    x2d = x.reshape(nb * t, d)
    attn_w = N_HEADS * HEAD_DIM
    kv_w = N_KV_HEADS * HEAD_DIM
    gm_w = lw["w_br_gm"].shape[0]
    proj, wb["w_in"] = _norm_mm(x2d, lw["attn_norm_w"], lw["w_in"], None, act=None,
                                tn=W_TILE, tile=W_TILE, name="proj_sample")
    gates, wb["w_gate"] = _norm_mm(x2d, lw["attn_norm_w"], lw["w_gate"], lw["b_gate"], act="sigmoid",
                                   tn=2 * W_TILE, tile=W_TILE, name="gates_sample")
    attn, k_new, v_new, wb["w_br_attn"], wb["w_br_gm"], wb["w_out"] = _attn_sample(
        proj, k_buf.reshape(nb, WINDOW, kv_w), v_buf.reshape(nb, WINDOW, kv_w),
        lw["qw2"], lw["kw2"], lw["sinks_row"], lw["w_br_attn"], lw["w_br_gm"], lw["w_out"], group=SAMPLE_GROUP)
    sgu, sgu_state = _sgu_sample(proj, lw["sgu_norm_w"], lw["sgu_w0_row"], lw["sgu_b0_row"],
                                 gu_off=attn_w + 2 * kv_w, gm_w=gm_w)
    x1, _ = _merge_out(attn, sgu, gates, x2d, wb["w_br_attn"], wb["w_br_gm"], wb["w_out"],
                       lw["ffn_norm_w"], tm=nb)
    states = (k_new.reshape(nb, WINDOW, N_KV_HEADS, HEAD_DIM), v_new.reshape(nb, WINDOW, N_KV_HEADS, HEAD_DIM),
              sgu_state.reshape(nb, t, gm_w))
    return x1, states


def _sample_tail(x1, p, conv_buf, lw, wb):
    nb = x1.shape[0]
    h = _norm_mm_tiles(x1, lw["ffn_norm_w"], wb["w_up"], tiles_per_step=2, name="ffn_up_sample")
    x2 = _ffn_down_sample(h, conv_buf.reshape(nb, -1), lw["conv_w"], lw["conv_b"], wb["w_down"], x1, tf=W_TILE)
    x3, wb["w_ple_proj"] = _ple_sample(x2, lw["ple_norm_w"], wb["w_ple_gate"], p.reshape(nb, -1),
                                       lw["w_ple_proj"])
    conv_state = jnp.stack([conv_buf[:, 1, :], h], axis=1)
    return x3, conv_state


def kernel(x_prompt, x_sample, p_prompt, p_sample, state_attn_k, state_attn_v, state_conv, attn_norm_w, w_in, q_norm_w, k_norm_w, attn_sinks, sgu_norm_w, sgu_w, sgu_b, w_br_attn, w_br_gm, w_gate, b_gate, w_out, ffn_norm_w, w_up, conv_w, conv_b, w_down, ple_norm_w, w_ple_gate, w_ple_proj):
    depth = w_in.shape[0]
    xp, xs = x_prompt, x_sample
    outs = [[] for _ in range(8)]
    for i in range(depth):
        lw = _layer_weights(i, attn_norm_w, w_in, q_norm_w, k_norm_w, attn_sinks, sgu_norm_w, sgu_w, sgu_b,
                            w_br_attn, w_br_gm, w_gate, b_gate, w_out, ffn_norm_w, w_up, conv_w, conv_b,
                            w_down, ple_norm_w, w_ple_gate, w_ple_proj)
        wb = {}
        xs1, (ks, vs, gs) = _sample_mixers(xs, state_attn_k[i], state_attn_v[i], lw, wb)
        mixed, (kp, vp, gp) = _prompt_mixers(xp, lw, wb)
        xs3, cs = _sample_tail(xs1, p_sample[i], state_conv[i], lw, wb)
        xs = xs3.reshape(xs.shape)
        xp, cp = _prompt_tail(xp, p_prompt[i], mixed, lw, wb)
        for lst, val in zip(outs, (kp, vp, ks, vs, gp, gs, cp, cs)):
            lst.append(val)
    return (xp, xs) + tuple(jnp.stack(lst) for lst in outs)
```
